```python
import math
import jax
import jax.numpy as jnp
from jax import lax
import numpy as np

D_MODEL = 2048
BATCH = 2
SEQ = 4096
DEPTH = 2

GRID_W = 64
CTX_LEN = 256
HEAD_DIM = 128
NA_W = D_MODEL // 2
NA_HEADS = NA_W // HEAD_DIM
WIN_H_MAX = 8
WIN_W = 16
LRU_W = D_MODEL // 4
LRU_BLOCKS = 4
LRU_BW = LRU_W // LRU_BLOCKS
CONV_W = 4
LRU_C = 8.0
FNET_W = D_MODEL // 4
FNET_GROUPS = 4
FNET_GW = FNET_W // FNET_GROUPS
MIX_W = NA_W + LRU_W + FNET_W
Q0 = 0
K0 = NA_W
V0 = 2 * NA_W
X0 = 3 * NA_W
G0 = X0 + LRU_W
F0 = G0 + LRU_W
IN_W = F0 + FNET_W
D_FF = 4 * D_MODEL
ROPE_THETA = 10000.0
LN_EPS = 1e-5
NEG_INF = -1e30

kernel_name = "hybrid_na_rglru_fnet_deepnorm_dit"


def layer_norm(x, g=None, b=None):
    xf = x.astype(jnp.float32)
    mu = jnp.mean(xf, -1, keepdims=True)
    var = jnp.mean(jnp.square(xf - mu), -1, keepdims=True)
    y = (xf - mu) * lax.rsqrt(var + LN_EPS)
    if g is not None:
        y = y * g.astype(jnp.float32) + b.astype(jnp.float32)
    return y.astype(x.dtype)


def axial_rope(x, rows, cols):
    half = HEAD_DIM // 2
    quarter = half // 2
    inv = ROPE_THETA ** (-jnp.arange(quarter, dtype=jnp.float32) / quarter)

    def rot(xa, pos):
        ang = pos.astype(jnp.float32)[:, None] * inv
        cos = jnp.cos(ang)[None, :, None, :]
        sin = jnp.sin(ang)[None, :, None, :]
        x1, x2 = xa[..., :quarter], xa[..., quarter:]
        return jnp.concatenate([x1 * cos - x2 * sin, x1 * sin + x2 * cos], -1)

    xf = x.astype(jnp.float32)
    out = jnp.concatenate([rot(xf[..., :half], rows), rot(xf[..., half:], cols)], -1)
    return out.astype(x.dtype)


def neighbourhood_attention(q, k, v, kc, vc, rpb):
    B, L, H, d = q.shape
    rows = L // GRID_W
    kh = min(WIN_H_MAX, rows)
    scale = d ** -0.5
    t = jnp.arange(L)
    qr = axial_rope(q, t // GRID_W, t % GRID_W)
    kr = axial_rope(k, t // GRID_W, t % GRID_W)
    grid = lambda a: a.reshape(B, rows, GRID_W, H, d)
    qg, qrg, krg, vg = grid(q), grid(qr), grid(kr), grid(v)
    r = jnp.arange(rows)
    row_start = jnp.clip(r - kh // 2, 0, rows - kh)
    row_idx = row_start[:, None] + jnp.arange(kh)
    k_band = krg[:, row_idx]
    v_band = vg[:, row_idx]
    s_band = jnp.einsum('brqhd,brkchd->bhrqkc', qrg, k_band).astype(jnp.float32) * scale
    col = jnp.arange(GRID_W)
    col_start = jnp.clip(col - WIN_W // 2, 0, GRID_W - WIN_W)
    in_win = (col[None, :] >= col_start[:, None]) & (col[None, :] < col_start[:, None] + WIN_W)
    dr = row_idx - r[:, None] + (WIN_H_MAX - 1)
    dc = jnp.clip(col[None, :] - col[:, None] + (WIN_W - 1), 0, 2 * WIN_W - 2)
    bias = rpb[:, dr[:, None, :, None], dc[None, :, None, :]].astype(jnp.float32)
    s_band = jnp.where(in_win[:, None, :], s_band + bias[None], NEG_INF)
    s_ctx = jnp.einsum('brqhd,bchd->bhrqc', qg, kc).astype(jnp.float32) * scale
    n_band = kh * GRID_W
    s = jnp.concatenate([s_band.reshape(B, H, rows, GRID_W, n_band), s_ctx], -1)
    p = jax.nn.softmax(s, axis=-1).astype(v.dtype)
    p_band = p[..., :n_band].reshape(B, H, rows, GRID_W, kh, GRID_W)
    p_ctx = p[..., n_band:]
    out = (jnp.einsum('bhrqkc,brkchd->brqhd', p_band, v_band)
           + jnp.einsum('bhrqc,bchd->brqhd', p_ctx, vc))
    return out.reshape(B, L, H * d)


def context_attention(qc, kc, vc):
    B, C, H, d = qc.shape
    s = jnp.einsum('bqhd,bkhd->bhqk', qc, kc).astype(jnp.float32) * (d ** -0.5)
    p = jax.nn.softmax(s, axis=-1).astype(vc.dtype)
    return jnp.einsum('bhqk,bkhd->bqhd', p, vc).reshape(B, C, H * d)


def centred_conv(x, w, b):
    L = x.shape[1]
    left = CONV_W // 2
    xp = jnp.pad(x, ((0, 0), (left, CONV_W - 1 - left), (0, 0)))
    out = xp[:, 0:L] * w[0]
    for j in range(1, CONV_W):
        out = out + xp[:, j:j + L] * w[j]
    return out + b


def block_diag(x, w, b):
    xb = x.reshape(*x.shape[:-1], LRU_BLOCKS, LRU_BW)
    return jnp.einsum('blnc,ncd->blnd', xb, w).reshape(x.shape) + b


def rglru_coeffs(x, wa, ba, wx, bx, lam):
    xf = x.astype(jnp.float32)
    r = jax.nn.sigmoid(block_diag(xf, wa, ba).astype(jnp.float32))
    i = jax.nn.sigmoid(block_diag(xf, wx, bx).astype(jnp.float32))
    log_a = -LRU_C * r * jax.nn.softplus(-lam.astype(jnp.float32))
    a = jnp.exp(log_a)
    u = jnp.sqrt(-jnp.expm1(2.0 * log_a)) * (i * xf)
    return a, u


def _scan_combine(e1, e2):
    a1, b1 = e1
    a2, b2 = e2
    return a1 * a2, a2 * b1 + b2


def linear_scan(a, u, h0, reverse):
    if reverse:
        a, u = jnp.flip(a, 1), jnp.flip(u, 1)
    u = u.at[:, 0].add(a[:, 0] * h0)
    _, h = lax.associative_scan(_scan_combine, (a, u), axis=1)
    h_last = h[:, -1]
    if reverse:
        h = jnp.flip(h, 1)
    return h, h_last


def bidirectional_rglru(xl, xc, conv_w, conv_b, wa, ba, wx, bx, lam):
    xl = centred_conv(xl, conv_w, conv_b)
    xc = centred_conv(xc, conv_w, conv_b)
    h0 = jnp.zeros((xc.shape[0], LRU_W), jnp.float32)
    y_lat, y_ctx = None, None
    for d, rev in enumerate((False, True)):
        a_c, u_c = rglru_coeffs(xc, wa[d], ba[d], wx[d], bx[d], lam[d])
        h_c, h_last = linear_scan(a_c, u_c, h0, rev)
        a_l, u_l = rglru_coeffs(xl, wa[d], ba[d], wx[d], bx[d], lam[d])
        h_l, _ = linear_scan(a_l, u_l, h_last, rev)
        y_lat = h_l if y_lat is None else y_lat + h_l
        y_ctx = h_c if y_ctx is None else y_ctx + h_c
    return y_lat.astype(xl.dtype), y_ctx.astype(xc.dtype)


def fourier_mix(x, w, b):
    B, L, _ = x.shape
    xg = x.astype(jnp.float32).reshape(B, L, FNET_GROUPS, FNET_GW)
    y = jnp.real(jnp.fft.fft2(xg, axes=(1, 3), norm='ortho')).reshape(B, L, FNET_W)
    return y.astype(x.dtype) @ w + b


def token_mixers(u_lat, u_ctx, w_in, rpb, conv_w, conv_b, wa, ba, wx, bx, lam, fno_w, fno_b, w_out, ctx_out):
    heads = lambda t: t.reshape(*t.shape[:2], NA_HEADS, HEAD_DIM)
    p_lat = u_lat @ w_in
    q, k, v = heads(p_lat[..., Q0:K0]), heads(p_lat[..., K0:V0]), heads(p_lat[..., V0:X0])
    if ctx_out:
        p_ctx = u_ctx @ w_in
    else:
        p_ctx = u_ctx @ w_in[:, K0:G0]
        p_ctx = jnp.pad(p_ctx, ((0, 0), (0, 0), (K0, IN_W - G0)))[..., :G0] if False else None
    if ctx_out:
        kc, vc, xrc = heads(p_ctx[..., K0:V0]), heads(p_ctx[..., V0:X0]), p_ctx[..., X0:G0]
    else:
        pc = u_ctx @ w_in[:, K0:G0]
        kc, vc, xrc = heads(pc[..., :NA_W]), heads(pc[..., NA_W:2 * NA_W]), pc[..., 2 * NA_W:]
    na_lat = neighbourhood_attention(q, k, v, kc, vc, rpb)
    h_lat, h_ctx = bidirectional_rglru(p_lat[..., X0:G0], xrc, conv_w, conv_b, wa, ba, wx, bx, lam)
    lru_lat = h_lat * jax.nn.gelu(p_lat[..., G0:F0])
    f_lat = fourier_mix(p_lat[..., F0:IN_W], fno_w, fno_b)
    m_lat = jnp.concatenate([na_lat, lru_lat, f_lat], -1) @ w_out
    if not ctx_out:
        return m_lat, None
    na_ctx = context_attention(heads(p_ctx[..., Q0:K0]), kc, vc)
    lru_ctx = h_ctx * jax.nn.gelu(p_ctx[..., G0:F0])
    f_ctx = fourier_mix(p_ctx[..., F0:IN_W], fno_w, fno_b)
    m_ctx = jnp.concatenate([na_ctx, lru_ctx, f_ctx], -1) @ w_out
    return m_lat, m_ctx


def sq_relu_mlp(u, w1, b1, w2, b2):
    return jnp.square(jax.nn.relu(u @ w1 + b1)) @ w2 + b2


def setup_inputs(seed: int = 0) -> dict:
    key = jax.random.key(seed)
    ks = jax.random.split(key, 32)
    f32 = jnp.float32
    nrm = lambda k, shape, s: jax.random.normal(k, shape, f32) * s
    beta = (8.0 * DEPTH) ** -0.25
    u = jax.random.uniform(ks[14], (DEPTH, 2, LRU_W), f32, 0.9, 0.999)
    a0 = u ** (1.0 / LRU_C)
    lam = jnp.log(a0) - jnp.log1p(-a0)
    return {
        "x": nrm(ks[0], (BATCH, SEQ, D_MODEL), 1.0),
        "c": nrm(ks[1], (BATCH, D_MODEL), 1.0),
        "ctx": nrm(ks[2], (BATCH, CTX_LEN, D_MODEL), 1.0),
        "c_ctx": nrm(ks[3], (D_MODEL,), 1.0),
        "w_mod": nrm(ks[4], (DEPTH, D_MODEL, 6 * D_MODEL), 0.5 * D_MODEL ** -0.5),
        "b_mod": nrm(ks[5], (DEPTH, 6 * D_MODEL), 0.02),
        "w_in": nrm(ks[6], (DEPTH, D_MODEL, IN_W), D_MODEL ** -0.5),
        "rpb": nrm(ks[7], (DEPTH, NA_HEADS, 2 * WIN_H_MAX - 1, 2 * WIN_W - 1), 0.1),
        "conv_w": nrm(ks[8], (DEPTH, CONV_W, LRU_W), CONV_W ** -0.5),
        "conv_b": nrm(ks[9], (DEPTH, LRU_W), 0.02),
        "lru_wa": nrm(ks[10], (DEPTH, 2, LRU_BLOCKS, LRU_BW, LRU_BW), LRU_BW ** -0.5),
        "lru_ba": nrm(ks[11], (DEPTH, 2, LRU_W), 0.02),
        "lru_wx": nrm(ks[12], (DEPTH, 2, LRU_BLOCKS, LRU_BW, LRU_BW), LRU_BW ** -0.5),
        "lru_bx": nrm(ks[13], (DEPTH, 2, LRU_W), 0.02),
        "lru_lambda": lam,
        "fno_w": nrm(ks[15], (DEPTH, FNET_W, FNET_W), FNET_W ** -0.5),
        "fno_b": nrm(ks[16], (DEPTH, FNET_W), 0.02),
        "w_out": nrm(ks[17], (DEPTH, MIX_W, D_MODEL), beta * MIX_W ** -0.5),
        "ln1_g": 1.0 + nrm(ks[18], (DEPTH, D_MODEL), 0.02),
        "ln1_b": nrm(ks[19], (DEPTH, D_MODEL), 0.02),
        "w_fc1": nrm(ks[20], (DEPTH, D_MODEL, D_FF), D_MODEL ** -0.5),
        "b_fc1": nrm(ks[21], (DEPTH, D_FF), 0.02),
        "w_fc2": nrm(ks[22], (DEPTH, D_FF, D_MODEL), beta * D_FF ** -0.5),
        "b_fc2": nrm(ks[23], (DEPTH, D_MODEL), 0.02),
        "ln2_g": 1.0 + nrm(ks[24], (DEPTH, D_MODEL), 0.02),
        "ln2_b": nrm(ks[25], (DEPTH, D_MODEL), 0.02),
    }


def reference(x, c, ctx, c_ctx, w_mod, b_mod, w_in, rpb, conv_w, conv_b, lru_wa, lru_ba, lru_wx, lru_bx,
              lru_lambda, fno_w, fno_b, w_out, ln1_g, ln1_b, w_fc1, b_fc1, w_fc2, b_fc2, ln2_g, ln2_b):
    alpha = (2.0 * DEPTH) ** 0.25
    for l in range(DEPTH):
        ctx_out = l < DEPTH - 1
        mod_lat = jax.nn.silu(c) @ w_mod[l] + b_mod[l]
        mod_ctx = jax.nn.silu(c_ctx[None]) @ w_mod[l] + b_mod[l]
        sh1, sc1, g1, sh2, sc2, g2 = jnp.split(mod_lat[:, None], 6, axis=-1)
        csh1, csc1, cg1, csh2, csc2, cg2 = jnp.split(mod_ctx[:, None], 6, axis=-1)
        u_lat = layer_norm(x) * (1.0 + sc1) + sh1
        u_ctx = layer_norm(ctx) * (1.0 + csc1) + csh1
        m_lat, m_ctx = token_mixers(u_lat, u_ctx, w_in[l], rpb[l], conv_w[l], conv_b[l], lru_wa[l], lru_ba[l],
                                    lru_wx[l], lru_bx[l], lru_lambda[l], fno_w[l], fno_b[l], w_out[l], ctx_out)
        x = layer_norm(alpha * x + g1 * m_lat, ln1_g[l], ln1_b[l])
        v_lat = layer_norm(x) * (1.0 + sc2) + sh2
        x = layer_norm(alpha * x + g2 * sq_relu_mlp(v_lat, w_fc1[l], b_fc1[l], w_fc2[l], b_fc2[l]), ln2_g[l], ln2_b[l])
        if ctx_out:
            ctx = layer_norm(alpha * ctx + cg1 * m_ctx, ln1_g[l], ln1_b[l])
            v_ctx = layer_norm(ctx) * (1.0 + csc2) + csh2
            ctx = layer_norm(alpha * ctx + cg2 * sq_relu_mlp(v_ctx, w_fc1[l], b_fc1[l], w_fc2[l], b_fc2[l]),
                             ln2_g[l], ln2_b[l])
    return x
```

```python
import functools
import math

import jax
import jax.numpy as jnp
import numpy as np
from jax import lax
from jax.experimental import pallas as pl
from jax.experimental.pallas import tpu as pltpu

F32 = jnp.float32
BF16 = jnp.bfloat16

D_MODEL = 2048
BATCH = 2
SEQ = 4096
DEPTH = 2
GRID_W = 64
GRID_H = SEQ // GRID_W
CTX_LEN = 256
HEAD_DIM = 128
NA_W = D_MODEL // 2
NA_HEADS = NA_W // HEAD_DIM
WIN_H = 8
WIN_W = 16
LRU_W = D_MODEL // 4
LRU_BLOCKS = 4
LRU_BW = LRU_W // LRU_BLOCKS
CONV_W = 4
LRU_C = 8.0
FNET_W = D_MODEL // 4
FNET_GROUPS = 4
FNET_GW = FNET_W // FNET_GROUPS
IN_W = 3 * NA_W + 2 * LRU_W + FNET_W
D_FF = 4 * D_MODEL
ROPE_THETA = 10000.0
LN_EPS = 1e-5
NEG_INF = -1e30
ALPHA = (2.0 * DEPTH) ** 0.25
ATTN_SCALE = HEAD_DIM ** -0.5

V7X_LANES = 128
V7X_SUBLANES = 8
V7X_VMEM_BYTES = 64 * 1024 * 1024
VMEM_CEILING = V7X_VMEM_BYTES - 6 * 1024 * 1024

COL_TILE = 512
N_COL_TILES = IN_W // COL_TILE
ROW_CHUNK = 128
MOD_ROWS = 8
CTX_MOD_ROW = BATCH


def _vmem_limit(nbytes):
    return int(min(VMEM_CEILING, nbytes * 5 // 4 + (4 << 20)))


def _params(semantics, nbytes):
    return pltpu.CompilerParams(dimension_semantics=semantics, vmem_limit_bytes=_vmem_limit(nbytes))


def _ln(x):
    mu = jnp.mean(x, axis=-1, keepdims=True)
    xc = x - mu
    var = jnp.mean(xc * xc, axis=-1, keepdims=True)
    return xc * lax.rsqrt(var + LN_EPS)


def _sigmoid(x):
    return 1.0 / (1.0 + jnp.exp(-x))


def _gelu_tanh(x):
    return 0.5 * x * (1.0 + jnp.tanh(math.sqrt(2.0 / math.pi) * (x + 0.044715 * (x * x * x))))


def _dot(a, b):
    return jnp.dot(a, b, preferred_element_type=F32)


def _dot_nt(a, b):
    return lax.dot_general(a, b, (((1,), (1,)), ((), ())), preferred_element_type=F32)


MOD_TN = 1024


def _mod_kernel(s_ref, w_ref, b_ref, o_ref):
    s = s_ref[...]
    s = s * _sigmoid(s)
    o_ref[...] = _dot(s.astype(BF16), w_ref[...].astype(BF16)) + b_ref[...]


def _mod_call(s_in, w_mod, b_mod):
    n_out = w_mod.shape[-1]
    nbytes = 2 * (D_MODEL * MOD_TN * 4) + D_MODEL * MOD_TN * 2 + 4 * MOD_ROWS * n_out
    return pl.pallas_call(
        _mod_kernel,
        grid=(DEPTH, n_out // MOD_TN),
        in_specs=[
            pl.BlockSpec((MOD_ROWS, D_MODEL), lambda l, n: (0, 0)),
            pl.BlockSpec((None, D_MODEL, MOD_TN), lambda l, n: (l, 0, n)),
            pl.BlockSpec((None, 1, MOD_TN), lambda l, n: (l, 0, n)),
        ],
        out_specs=pl.BlockSpec((None, MOD_ROWS, MOD_TN), lambda l, n: (l, 0, n)),
        out_shape=jax.ShapeDtypeStruct((DEPTH, MOD_ROWS, n_out), F32),
        compiler_params=_params(("arbitrary", "arbitrary"), nbytes),
        name="modulation",
    )(s_in, w_mod, b_mod.reshape(DEPTH, 1, n_out))


def _ln_mod_to(x_ref, mod_ref, shift_idx, scale_idx, dst_ref):
    shift = mod_ref[shift_idx:shift_idx + 1, :]
    scale = 1.0 + mod_ref[scale_idx:scale_idx + 1, :]
    rows = x_ref.shape[0]
    chunk = min(ROW_CHUNK, rows)

    def body(i, carry):
        r0 = pl.multiple_of(i * chunk, chunk)
        xn = _ln(x_ref[pl.ds(r0, chunk), :])
        dst_ref[pl.ds(r0, chunk), :] = (xn * scale + shift).astype(dst_ref.dtype)
        return carry

    lax.fori_loop(0, rows // chunk, body, 0)


def _rope(a, cos, sin):
    lane = lax.broadcasted_iota(jnp.int32, a.shape, 1)
    first = (lane % (HEAD_DIM // 2)) < (HEAD_DIM // 4)
    partner = jnp.where(first, pltpu.roll(a, HEAD_DIM - HEAD_DIM // 4, 1), pltpu.roll(a, HEAD_DIM // 4, 1))
    return a * cos + partner * sin


def _inproj_lat_kernel(x_ref, mod_ref, w_ref, cos_ref, sin_ref,
                       q_ref, qr_ref, k_ref, v_ref, xo_ref, go_ref, f_ref, xn_ref):
    n = pl.program_id(1)

    @pl.when(n == 0)
    def _():
        _ln_mod_to(x_ref, mod_ref, 0, 1, xn_ref)

    acc = _dot(xn_ref[...], w_ref[...].astype(BF16))

    @pl.when(n < 2)
    def _():
        q_ref[...] = (acc * ATTN_SCALE).astype(q_ref.dtype)
        for h in range(COL_TILE // HEAD_DIM):
            sl = slice(h * HEAD_DIM, (h + 1) * HEAD_DIM)
            qr_ref[:, sl] = (_rope(acc[:, sl], cos_ref[...], sin_ref[...]) * ATTN_SCALE).astype(qr_ref.dtype)

    @pl.when((n >= 2) & (n < 4))
    def _():
        for h in range(COL_TILE // HEAD_DIM):
            sl = slice(h * HEAD_DIM, (h + 1) * HEAD_DIM)
            k_ref[:, sl] = _rope(acc[:, sl], cos_ref[...], sin_ref[...]).astype(k_ref.dtype)

    @pl.when((n >= 4) & (n < 6))
    def _():
        v_ref[...] = acc.astype(v_ref.dtype)

    @pl.when(n == 6)
    def _():
        xo_ref[...] = acc

    @pl.when(n == 7)
    def _():
        go_ref[...] = acc

    @pl.when(n == 8)
    def _():
        f_ref[...] = acc.astype(f_ref.dtype)


def _inproj_lat_call(x2d, mod4, w_in, layer, cos_t, sin_t):
    m_rows = x2d.shape[0]
    tm = 1024
    tiles_per_seq = SEQ // tm

    def col(lo):
        return lambda m, n: (m, jnp.clip(n - lo, 0, 1))

    def one(lo):
        return lambda m, n: (m, 0)

    nbytes = (tm * D_MODEL * 4 + tm * D_MODEL * 2 + 2 * D_MODEL * COL_TILE * 4 + D_MODEL * COL_TILE * 2
              + 4 * tm * HEAD_DIM * 4 + 2 * 5 * tm * COL_TILE * 2 + 2 * 2 * tm * COL_TILE * 4 + 3 * tm * COL_TILE * 4
              + 4 * ROW_CHUNK * D_MODEL * 4)
    bf = lambda w: jax.ShapeDtypeStruct((m_rows, w), BF16)
    ff = lambda w: jax.ShapeDtypeStruct((m_rows, w), F32)
    return pl.pallas_call(
        _inproj_lat_kernel,
        grid=(m_rows // tm, N_COL_TILES),
        in_specs=[
            pl.BlockSpec((tm, D_MODEL), lambda m, n: (m, 0), pipeline_mode=pl.Buffered(1)),
            pl.BlockSpec((None, None, 6, D_MODEL), lambda m, n: (layer, m // tiles_per_seq, 0, 0)),
            pl.BlockSpec((None, D_MODEL, COL_TILE), lambda m, n: (layer, 0, n)),
            pl.BlockSpec((tm, HEAD_DIM), lambda m, n: (m % tiles_per_seq, 0)),
            pl.BlockSpec((tm, HEAD_DIM), lambda m, n: (m % tiles_per_seq, 0)),
        ],
        out_specs=[
            pl.BlockSpec((tm, COL_TILE), col(0)),
            pl.BlockSpec((tm, COL_TILE), col(0)),
            pl.BlockSpec((tm, COL_TILE), col(2)),
            pl.BlockSpec((tm, COL_TILE), col(4)),
            pl.BlockSpec((tm, COL_TILE), one(6)),
            pl.BlockSpec((tm, COL_TILE), one(7)),
            pl.BlockSpec((tm, COL_TILE), one(8)),
        ],
        out_shape=[bf(NA_W), bf(NA_W), bf(NA_W), bf(NA_W), ff(LRU_W), ff(LRU_W), bf(FNET_W)],
        scratch_shapes=[pltpu.VMEM((tm, D_MODEL), BF16)],
        compiler_params=_params(("arbitrary", "arbitrary"), nbytes),
        name="inproj_latent",
    )(x2d, mod4, w_in, cos_t, sin_t)


def _inproj_ctx_kernel(tile_lo, with_q, with_gf, x_ref, mod_ref, w_ref, *refs):
    refs = list(refs)
    xn_ref = refs.pop()
    q_ref = refs.pop(0) if with_q else None
    k_ref, v_ref, xo_ref = refs[0], refs[1], refs[2]
    go_ref, f_ref = (refs[3], refs[4]) if with_gf else (None, None)
    n = pl.program_id(1) + tile_lo

    @pl.when(pl.program_id(1) == 0)
    def _():
        _ln_mod_to(x_ref, mod_ref, 0, 1, xn_ref)

    acc = _dot(xn_ref[...], w_ref[...].astype(BF16))

    if with_q:
        @pl.when(n < 2)
        def _():
            q_ref[...] = (acc * ATTN_SCALE).astype(q_ref.dtype)

    @pl.when((n >= 2) & (n < 4))
    def _():
        k_ref[...] = acc.astype(k_ref.dtype)

    @pl.when((n >= 4) & (n < 6))
    def _():
        v_ref[...] = acc.astype(v_ref.dtype)

    @pl.when(n == 6)
    def _():
        xo_ref[...] = acc

    if with_gf:
        @pl.when(n == 7)
        def _():
            go_ref[...] = acc

        @pl.when(n == 8)
        def _():
            f_ref[...] = acc.astype(f_ref.dtype)


def _inproj_ctx_call(c2d, mod4, w_in, layer, full):
    m_rows = c2d.shape[0]
    tm = m_rows
    tile_lo, tile_hi = (0, N_COL_TILES) if full else (2, 7)

    def col(lo):
        return lambda m, n: (m, jnp.clip(n + tile_lo - lo, 0, 1))

    one = lambda m, n: (m, 0)
    bf = lambda w: jax.ShapeDtypeStruct((m_rows, w), BF16)
    ff = lambda w: jax.ShapeDtypeStruct((m_rows, w), F32)
    out_specs, out_shape = [], []
    if full:
        out_specs.append(pl.BlockSpec((tm, COL_TILE), col(0)))
        out_shape.append(bf(NA_W))
    out_specs += [pl.BlockSpec((tm, COL_TILE), col(2)), pl.BlockSpec((tm, COL_TILE), col(4)),
                  pl.BlockSpec((tm, COL_TILE), one)]
    out_shape += [bf(NA_W), bf(NA_W), ff(LRU_W)]
    if full:
        out_specs += [pl.BlockSpec((tm, COL_TILE), one), pl.BlockSpec((tm, COL_TILE), one)]
        out_shape += [ff(LRU_W), bf(FNET_W)]
    nbytes = (2 * tm * D_MODEL * 4 + tm * D_MODEL * 2 + 2 * D_MODEL * COL_TILE * 4 + D_MODEL * COL_TILE * 2
              + 2 * 6 * tm * COL_TILE * 4 + 3 * tm * COL_TILE * 4)
    return pl.pallas_call(
        functools.partial(_inproj_ctx_kernel, tile_lo, full, full),
        grid=(1, tile_hi - tile_lo),
        in_specs=[
            pl.BlockSpec((tm, D_MODEL), lambda m, n: (m, 0)),
            pl.BlockSpec((None, None, 6, D_MODEL), lambda m, n: (layer, CTX_MOD_ROW, 0, 0)),
            pl.BlockSpec((None, D_MODEL, COL_TILE), lambda m, n: (layer, 0, n + tile_lo)),
        ],
        out_specs=out_specs,
        out_shape=out_shape,
        scratch_shapes=[pltpu.VMEM((tm, D_MODEL), BF16)],
        compiler_params=_params(("arbitrary", "arbitrary"), nbytes),
        name="inproj_context",
    )(c2d, mod4, w_in)


BAND = WIN_H * GRID_W


def _attn_kernel(q_ref, qr_ref, k_ref, v_ref, kc_ref, vc_ref, bias_ref, o_ref):
    kc = kc_ref[...]
    vc = vc_ref[...]

    def body(r, carry):
        row_start = jnp.clip(r - WIN_H // 2, 0, GRID_H - WIN_H)
        variant = row_start - r + (WIN_H - 1)
        q0 = pl.multiple_of(r * GRID_W, GRID_W)
        k0 = pl.multiple_of(row_start * GRID_W, GRID_W)
        s = _dot_nt(qr_ref[pl.ds(q0, GRID_W), :], k_ref[pl.ds(k0, BAND), :]) + bias_ref[variant]
        sc = _dot_nt(q_ref[pl.ds(q0, GRID_W), :], kc)
        m = jnp.maximum(jnp.max(s, axis=-1, keepdims=True), jnp.max(sc, axis=-1, keepdims=True))
        p = jnp.exp(s - m)
        pc = jnp.exp(sc - m)
        denom = jnp.sum(p, axis=-1, keepdims=True) + jnp.sum(pc, axis=-1, keepdims=True)
        o = _dot(p.astype(BF16), v_ref[pl.ds(k0, BAND), :]) + _dot(pc.astype(BF16), vc)
        o_ref[pl.ds(q0, GRID_W), :] = (o / denom).astype(o_ref.dtype)
        return carry

    lax.fori_loop(0, GRID_H, body, 0)


def _attn_call(q, qr, k, v, kc, vc, bias):
    seq_blk = lambda b, h: (b, h)
    nbytes = 2 * (5 * SEQ * HEAD_DIM * 2 + 2 * CTX_LEN * HEAD_DIM * 2 + WIN_H * GRID_W * BAND * 4) + (4 << 20)
    return pl.pallas_call(
        _attn_kernel,
        grid=(BATCH, NA_HEADS),
        in_specs=[
            pl.BlockSpec((SEQ, HEAD_DIM), seq_blk),
            pl.BlockSpec((SEQ, HEAD_DIM), seq_blk),
            pl.BlockSpec((SEQ, HEAD_DIM), seq_blk),
            pl.BlockSpec((SEQ, HEAD_DIM), seq_blk),
            pl.BlockSpec((CTX_LEN, HEAD_DIM), seq_blk),
            pl.BlockSpec((CTX_LEN, HEAD_DIM), seq_blk),
            pl.BlockSpec((None, WIN_H, GRID_W, BAND), lambda b, h: (h, 0, 0, 0)),
        ],
        out_specs=pl.BlockSpec((SEQ, HEAD_DIM), seq_blk),
        out_shape=jax.ShapeDtypeStruct((BATCH * SEQ, NA_W), BF16),
        compiler_params=_params(("arbitrary", "arbitrary"), nbytes),
        name="neighbourhood_attention",
    )(q, qr, k, v, kc, vc, bias)


def _ctx_attn_kernel(q_ref, k_ref, v_ref, o_ref):
    s = _dot_nt(q_ref[...], k_ref[...])
    m = jnp.max(s, axis=-1, keepdims=True)
    p = jnp.exp(s - m)
    denom = jnp.sum(p, axis=-1, keepdims=True)
    o_ref[...] = (_dot(p.astype(BF16), v_ref[...]) / denom).astype(o_ref.dtype)


def _ctx_attn_call(q, k, v):
    blk = pl.BlockSpec((CTX_LEN, HEAD_DIM), lambda b, h: (b, h))
    return pl.pallas_call(
        _ctx_attn_kernel,
        grid=(BATCH, NA_HEADS),
        in_specs=[blk, blk, blk],
        out_specs=blk,
        out_shape=jax.ShapeDtypeStruct((BATCH * CTX_LEN, NA_W), BF16),
        compiler_params=_params(("arbitrary", "arbitrary"), 16 << 20),
        name="context_attention",
    )(q, k, v)


def _attn_bias_table(rpb_l):
    col = jnp.arange(GRID_W)
    col_start = jnp.clip(col - WIN_W // 2, 0, GRID_W - WIN_W)
    in_win = (col[None, :] >= col_start[:, None]) & (col[None, :] < col_start[:, None] + WIN_W)
    dc = jnp.clip(col[None, :] - col[:, None] + (WIN_W - 1), 0, 2 * WIN_W - 2)
    t = jnp.where(in_win[None, None], rpb_l[:, :, dc], NEG_INF)
    dr = jnp.arange(WIN_H)[:, None] + jnp.arange(WIN_H)[None, :]
    tv = t[:, dr]
    return tv.transpose(0, 1, 3, 2, 4).reshape(NA_HEADS, WIN_H, GRID_W, BAND).astype(F32)


def _rope_tables():
    quarter = HEAD_DIM // 4
    inv = ROPE_THETA ** (-jnp.arange(quarter, dtype=F32) / quarter)
    t = jnp.arange(SEQ)
    ang_r = (t // GRID_W).astype(F32)[:, None] * inv
    ang_c = (t % GRID_W).astype(F32)[:, None] * inv
    cos = jnp.concatenate([jnp.cos(ang_r), jnp.cos(ang_r), jnp.cos(ang_c), jnp.cos(ang_c)], -1)
    sin = jnp.concatenate([-jnp.sin(ang_r), jnp.sin(ang_r), -jnp.sin(ang_c), jnp.sin(ang_c)], -1)
    return cos, sin


HALO = V7X_SUBLANES
LRU_CHUNK = 512


def _lru_coeffs(xp_ref, n_rows, cw_ref, cb_ref, w4, b4_ref, sp, a_refs, u_refs):
    chunk = min(LRU_CHUNK, n_rows)
    for c in range(n_rows // chunk):
        base = HALO + c * chunk
        xc = cb_ref[...] + xp_ref[base - CONV_W // 2:base - CONV_W // 2 + chunk, :] * cw_ref[0:1, :]
        for j in range(1, CONV_W):
            off = base - CONV_W // 2 + j
            xc = xc + xp_ref[off:off + chunk, :] * cw_ref[j:j + 1, :]
        z = _dot(xc.astype(BF16), w4) + b4_ref[...]
        for d in range(2):
            r = _sigmoid(z[:, (2 * d) * LRU_BW:(2 * d + 1) * LRU_BW])
            i = _sigmoid(z[:, (2 * d + 1) * LRU_BW:(2 * d + 2) * LRU_BW])
            log_a = (-LRU_C) * r * sp[d:d + 1, :]
            a = jnp.exp(log_a)
            a_refs[d][c * chunk:(c + 1) * chunk, :] = a
            one_minus_a2 = -jnp.tanh(log_a) * (a * a + 1.0)
            u_refs[d][c * chunk:(c + 1) * chunk, :] = jnp.sqrt(one_minus_a2) * (i * xc)


def _lru_scan(n_rows, hf, hb, af, uf, ab, ub, yf, yb):
    unroll = V7X_SUBLANES

    def body(i, carry):
        hf, hb = carry
        f0 = pl.multiple_of(i * unroll, unroll)
        b0 = pl.multiple_of(n_rows - unroll - i * unroll, unroll)
        for j in range(unroll):
            tf = f0 + j
            hf = af[pl.ds(tf, 1), :] * hf + uf[pl.ds(tf, 1), :]
            yf[pl.ds(tf, 1), :] = hf
            tb = b0 + (unroll - 1 - j)
            hb = ab[pl.ds(tb, 1), :] * hb + ub[pl.ds(tb, 1), :]
            yb[pl.ds(tb, 1), :] = hb
        return hf, hb

    return lax.fori_loop(0, n_rows // unroll, body, (hf, hb))


def _lru_kernel(ctx_out, x_ref, g_ref, xc_ref, *refs):
    refs = list(refs)
    gc_ref = refs.pop(0) if ctx_out else None
    cw_ref, cb_ref, w4_ref, b4_ref, lam_ref, o_ref = refs[:6]
    refs = refs[6:]
    oc_ref = refs.pop(0) if ctx_out else None
    xp_ref, af, uf, ab, ub, yf, yb = refs

    lam = lam_ref[...]
    z = -lam
    sp = jnp.maximum(z, 0.0) + jnp.log1p(jnp.exp(-jnp.abs(z)))
    w4 = w4_ref[...].astype(BF16)
    zeros_halo = jnp.zeros((HALO, LRU_BW), F32)
    h0 = jnp.zeros((1, LRU_BW), F32)

    xp_ref[0:HALO, :] = zeros_halo
    xp_ref[HALO:HALO + CTX_LEN, :] = xc_ref[...]
    xp_ref[HALO + CTX_LEN:2 * HALO + CTX_LEN, :] = zeros_halo
    _lru_coeffs(xp_ref, CTX_LEN, cw_ref, cb_ref, w4, b4_ref, sp, (af, ab), (uf, ub))
    hf, hb = _lru_scan(CTX_LEN, h0, h0, af, uf, ab, ub, yf, yb)
    if ctx_out:
        y = yf[0:CTX_LEN, :] + yb[0:CTX_LEN, :]
        oc_ref[...] = (y * _gelu_tanh(gc_ref[...])).astype(oc_ref.dtype)

    xp_ref[HALO:HALO + SEQ, :] = x_ref[...]
    xp_ref[HALO + SEQ:2 * HALO + SEQ, :] = zeros_halo
    _lru_coeffs(xp_ref, SEQ, cw_ref, cb_ref, w4, b4_ref, sp, (af, ab), (uf, ub))
    _lru_scan(SEQ, hf, hb, af, uf, ab, ub, yf, yb)
    for c in range(SEQ // LRU_CHUNK):
        sl = slice(c * LRU_CHUNK, (c + 1) * LRU_CHUNK)
        o_ref[sl, :] = ((yf[sl, :] + yb[sl, :]) * _gelu_tanh(g_ref[sl, :])).astype(o_ref.dtype)


def _lru_call(ctx_out, xl, gl, xc, gc, conv_w_l, conv_b_l, w4, b4, lam_l):
    lat = pl.BlockSpec((SEQ, LRU_BW), lambda b, j: (b, j))
    cx = pl.BlockSpec((CTX_LEN, LRU_BW), lambda b, j: (b, j))
    in_specs = [lat, lat, cx] + ([cx] if ctx_out else []) + [
        pl.BlockSpec((CONV_W, LRU_BW), lambda b, j: (0, j)),
        pl.BlockSpec((1, LRU_BW), lambda b, j: (0, j)),
        pl.BlockSpec((None, LRU_BW, 4 * LRU_BW), lambda b, j: (j, 0, 0)),
        pl.BlockSpec((None, 1, 4 * LRU_BW), lambda b, j: (j, 0, 0)),
        pl.BlockSpec((2, LRU_BW), lambda b, j: (0, j)),
    ]
    out_specs = [lat] + ([cx] if ctx_out else [])
    out_shape = [jax.ShapeDtypeStruct((BATCH * SEQ, LRU_W), BF16)]
    if ctx_out:
        out_shape.append(jax.ShapeDtypeStruct((BATCH * CTX_LEN, LRU_W), BF16))
    seq_bytes = SEQ * LRU_BW * 4
    args = [xl, gl, xc] + ([gc] if ctx_out else []) + [conv_w_l, conv_b_l, w4, b4, lam_l]
    res = pl.pallas_call(
        functools.partial(_lru_kernel, ctx_out),
        grid=(BATCH, LRU_BLOCKS),
        in_specs=in_specs,
        out_specs=out_specs,
        out_shape=out_shape,
        scratch_shapes=[pltpu.VMEM((SEQ + 2 * HALO, LRU_BW), F32)] + [pltpu.VMEM((SEQ, LRU_BW), F32)] * 6,
        compiler_params=_params(("arbitrary", "arbitrary"), 13 * seq_bytes + (8 << 20)),
        name="rglru",
    )(*args)
    return (res[0], res[1]) if ctx_out else (res[0], None)


def _fourier_kernel(norm, f_ref, cl_ref, sl_ref, cc_ref, sc_ref, w_ref, b_ref, o_ref, xc_ref, xs_ref):
    @pl.when(pl.program_id(1) == 0)
    def _():
        for g in range(FNET_GROUPS):
            sl = slice(g * FNET_GW, (g + 1) * FNET_GW)
            xg = f_ref[:, sl]
            xc_ref[:, sl] = _dot(xg, cc_ref[...]).astype(BF16)
            xs_ref[:, sl] = _dot(xg, sc_ref[...]).astype(BF16)

    y = (_dot(cl_ref[...], xc_ref[...]) - _dot(sl_ref[...], xs_ref[...])) * norm
    o_ref[...] = (_dot(y.astype(BF16), w_ref[...].astype(BF16)) + b_ref[...]).astype(o_ref.dtype)


DFT_SPLIT = 64


def _dft_matrices(n):
    t = np.arange(n, dtype=np.int64)

    def table(k):
        ang = (2.0 * np.pi / n) * ((k[:, None] * t[None, :]) % n).astype(np.float64)
        return jnp.asarray(np.cos(ang), F32), jnp.asarray(np.sin(ang), F32)

    if n <= DFT_SPLIT:
        c, s = table(t)
        return c.astype(BF16), s.astype(BF16)
    c1, s1 = table(DFT_SPLIT * np.arange(n // DFT_SPLIT, dtype=np.int64))
    c2, s2 = table(np.arange(DFT_SPLIT, dtype=np.int64))
    c = c1[:, None, :] * c2[None, :, :] - s1[:, None, :] * s2[None, :, :]
    s = s1[:, None, :] * c2[None, :, :] + c1[:, None, :] * s2[None, :, :]
    return c.reshape(n, n).astype(BF16), s.reshape(n, n).astype(BF16)


def _fourier_call(f2d, n_pos, fno_w_l, fno_b_l):
    tk = min(512, n_pos)
    steps = n_pos // tk
    cl, sl = _dft_matrices(n_pos)
    cc, sc = _dft_matrices(FNET_GW)
    norm = 1.0 / math.sqrt(n_pos * FNET_GW)
    nbytes = (2 * n_pos * FNET_W * 2 + 2 * 2 * tk * n_pos * 2 + 2 * n_pos * FNET_W * 2 + 2 * FNET_W * FNET_W * 4
              + 4 * tk * FNET_W * 4)
    return pl.pallas_call(
        functools.partial(_fourier_kernel, norm),
        grid=(BATCH, steps),
        in_specs=[
            pl.BlockSpec((n_pos, FNET_W), lambda b, k: (b, 0)),
            pl.BlockSpec((tk, n_pos), lambda b, k: (k, 0)),
            pl.BlockSpec((tk, n_pos), lambda b, k: (k, 0)),
            pl.BlockSpec((FNET_GW, FNET_GW), lambda b, k: (0, 0)),
            pl.BlockSpec((FNET_GW, FNET_GW), lambda b, k: (0, 0)),
            pl.BlockSpec((FNET_W, FNET_W), lambda b, k: (0, 0)),
            pl.BlockSpec((1, FNET_W), lambda b, k: (0, 0)),
        ],
        out_specs=pl.BlockSpec((tk, FNET_W), lambda b, k: (b * steps + k, 0)),
        out_shape=jax.ShapeDtypeStruct((BATCH * n_pos, FNET_W), BF16),
        scratch_shapes=[pltpu.VMEM((n_pos, FNET_W), BF16), pltpu.VMEM((n_pos, FNET_W), BF16)],
        compiler_params=_params(("arbitrary", "arbitrary"), nbytes),
        name="fourier_mix",
    )(f2d, cl, sl, cc, sc, fno_w_l, fno_b_l)


def _residual_ln_rows(res_ref, acc_ref, gate, gain, bias, extra_bias, o_ref):
    rows = res_ref.shape[0]
    chunk = min(ROW_CHUNK, rows)

    def body(i, carry):
        r0 = pl.multiple_of(i * chunk, chunk)
        y = acc_ref[pl.ds(r0, chunk), :]
        if extra_bias is not None:
            y = y + extra_bias
        z = ALPHA * res_ref[pl.ds(r0, chunk), :] + gate * y
        o_ref[pl.ds(r0, chunk), :] = _ln(z) * gain + bias
        return carry

    lax.fori_loop(0, rows // chunk, body, 0)


def _outproj_kernel(na_ref, lru_ref, f_ref, res_ref, mod_ref, w_ref, g_ref, b_ref, o_ref, wb_ref, cat_ref):
    @pl.when(pl.program_id(0) == 0)
    def _():
        for c in range(D_MODEL // COL_TILE):
            sl = slice(c * COL_TILE, (c + 1) * COL_TILE)
            wb_ref[sl, :] = w_ref[sl, :].astype(BF16)

    cat_ref[:, 0:NA_W] = na_ref[...]
    cat_ref[:, NA_W:NA_W + LRU_W] = lru_ref[...]
    cat_ref[:, NA_W + LRU_W:D_MODEL] = f_ref[...]
    for c in range(D_MODEL // COL_TILE):
        sl = slice(c * COL_TILE, (c + 1) * COL_TILE)
        o_ref[:, sl] = _dot(cat_ref[...], wb_ref[:, sl])
    _residual_ln_rows(res_ref, o_ref, mod_ref[2:3, :], g_ref[...], b_ref[...], None, o_ref)


def _outproj_call(na, lru, f, res, mod4, mod_row, w_out, layer, ln_g, ln_b):
    m_rows = res.shape[0]
    tm = 512
    row = lambda m: (m, 0)
    nbytes = (2 * tm * D_MODEL * 2 + 2 * 2 * tm * D_MODEL * 4 + D_MODEL * D_MODEL * (4 + 2) + tm * D_MODEL * 2
              + tm * COL_TILE * 4 + 4 * ROW_CHUNK * D_MODEL * 4)
    return pl.pallas_call(
        _outproj_kernel,
        grid=(m_rows // tm,),
        in_specs=[
            pl.BlockSpec((tm, NA_W), row),
            pl.BlockSpec((tm, LRU_W), row),
            pl.BlockSpec((tm, FNET_W), row),
            pl.BlockSpec((tm, D_MODEL), row),
            pl.BlockSpec((None, None, 6, D_MODEL), lambda m: (layer, mod_row(m * tm), 0, 0)),
            pl.BlockSpec((None, D_MODEL, D_MODEL), lambda m: (layer, 0, 0), pipeline_mode=pl.Buffered(1)),
            pl.BlockSpec((None, 1, D_MODEL), lambda m: (layer, 0, 0)),
            pl.BlockSpec((None, 1, D_MODEL), lambda m: (layer, 0, 0)),
        ],
        out_specs=pl.BlockSpec((tm, D_MODEL), row),
        out_shape=jax.ShapeDtypeStruct((m_rows, D_MODEL), F32),
        scratch_shapes=[pltpu.VMEM((D_MODEL, D_MODEL), BF16), pltpu.VMEM((tm, D_MODEL), BF16)],
        compiler_params=_params(("arbitrary",), nbytes),
        name="outproj_residual",
    )(na, lru, f, res, mod4, w_out, ln_g, ln_b)


MLP_TF = 256


def _mlp_kernel(x_ref, mod_ref, w1_ref, b1_ref, w2_ref, b2_ref, g_ref, b_ref, o_ref, v_ref):
    j = pl.program_id(1)

    @pl.when(j == 0)
    def _():
        _ln_mod_to(x_ref, mod_ref, 3, 4, v_ref)
        o_ref[...] = jnp.zeros(o_ref.shape, o_ref.dtype)

    h = _dot(v_ref[...], w1_ref[...].astype(BF16)) + b1_ref[...]
    h = jnp.square(jnp.maximum(h, 0.0)).astype(BF16)
    for c in range(D_MODEL // COL_TILE):
        sl = slice(c * COL_TILE, (c + 1) * COL_TILE)
        o_ref[:, sl] += _dot(h, w2_ref[:, sl].astype(BF16))

    @pl.when(j == pl.num_programs(1) - 1)
    def _():
        _residual_ln_rows(x_ref, o_ref, mod_ref[5:6, :], g_ref[...], b_ref[...], b2_ref[...], o_ref)


def _mlp_call(x1, mod4, mod_row, tm, w1, b1, w2, b2, layer, ln_g, ln_b):
    m_rows = x1.shape[0]
    row = lambda m, j: (m, 0)
    vec = lambda m, j: (layer, 0, 0)
    nbytes = (3 * tm * D_MODEL * 4 + tm * D_MODEL * 2 + 2 * 2 * D_MODEL * MLP_TF * 4 + 2 * D_MODEL * MLP_TF * 2
              + tm * MLP_TF * 6 + tm * COL_TILE * 4 + 4 * ROW_CHUNK * D_MODEL * 4)
    return pl.pallas_call(
        _mlp_kernel,
        grid=(m_rows // tm, D_FF // MLP_TF),
        in_specs=[
            pl.BlockSpec((tm, D_MODEL), row, pipeline_mode=pl.Buffered(1)),
            pl.BlockSpec((None, None, 6, D_MODEL), lambda m, j: (layer, mod_row(m * tm), 0, 0)),
            pl.BlockSpec((None, D_MODEL, MLP_TF), lambda m, j: (layer, 0, j)),
            pl.BlockSpec((None, 1, MLP_TF), lambda m, j: (layer, 0, j)),
            pl.BlockSpec((None, MLP_TF, D_MODEL), lambda m, j: (layer, j, 0)),
            pl.BlockSpec((None, 1, D_MODEL), vec),
            pl.BlockSpec((None, 1, D_MODEL), vec),
            pl.BlockSpec((None, 1, D_MODEL), vec),
        ],
        out_specs=pl.BlockSpec((tm, D_MODEL), row),
        out_shape=jax.ShapeDtypeStruct((m_rows, D_MODEL), F32),
        scratch_shapes=[pltpu.VMEM((tm, D_MODEL), BF16)],
        compiler_params=_params(("arbitrary", "arbitrary"), nbytes),
        name="mlp_residual",
    )(x1, mod4, w1, b1, w2, b2, ln_g, ln_b)


def kernel(x, c, ctx, c_ctx, w_mod, b_mod, w_in, rpb, conv_w, conv_b, lru_wa, lru_ba, lru_wx, lru_bx, lru_lambda,
           fno_w, fno_b, w_out, ln1_g, ln1_b, w_fc1, b_fc1, w_fc2, b_fc2, ln2_g, ln2_b):
    xl = x.reshape(BATCH * SEQ, D_MODEL)
    xc = ctx.reshape(BATCH * CTX_LEN, D_MODEL)
    s_in = jnp.concatenate([c, c_ctx[None], jnp.zeros((MOD_ROWS - BATCH - 1, D_MODEL), F32)], 0)
    mod4 = _mod_call(s_in, w_mod, b_mod).reshape(DEPTH, MOD_ROWS, 6, D_MODEL)
    cos_t, sin_t = _rope_tables()
    vec3 = lambda a: a.reshape(DEPTH, 1, a.shape[-1])
    ln1_g3, ln1_b3, ln2_g3, ln2_b3 = vec3(ln1_g), vec3(ln1_b), vec3(ln2_g), vec3(ln2_b)
    b_fc1_3, b_fc2_3 = vec3(b_fc1), vec3(b_fc2)
    lat_row = lambda r0: r0 // SEQ
    ctx_row = lambda r0: CTX_MOD_ROW

    for layer in range(DEPTH):
        ctx_out = layer < DEPTH - 1
        q, qr, k, v, xo, go, f = _inproj_lat_call(xl, mod4, w_in, layer, cos_t, sin_t)
        if ctx_out:
            qc, kc, vc, xoc, goc, fc = _inproj_ctx_call(xc, mod4, w_in, layer, True)
        else:
            kc, vc, xoc = _inproj_ctx_call(xc, mod4, w_in, layer, False)
            goc = None

        na = _attn_call(q, qr, k, v, kc, vc, _attn_bias_table(rpb[layer]))

        w4 = jnp.concatenate([lru_wa[layer, 0], lru_wx[layer, 0], lru_wa[layer, 1], lru_wx[layer, 1]], -1)
        blk = lambda a: a.reshape(LRU_BLOCKS, 1, LRU_BW)
        b4 = jnp.concatenate([blk(lru_ba[layer, 0]), blk(lru_bx[layer, 0]),
                              blk(lru_ba[layer, 1]), blk(lru_bx[layer, 1])], -1)
        lru, lru_c = _lru_call(ctx_out, xo, go, xoc, goc, conv_w[layer], conv_b[layer][None], w4, b4,
                               lru_lambda[layer])

        fm = _fourier_call(f, SEQ, fno_w[layer], fno_b[layer][None])
        x1 = _outproj_call(na, lru, fm, xl, mod4, lat_row, w_out, layer, ln1_g3, ln1_b3)
        xl = _mlp_call(x1, mod4, lat_row, 1024, w_fc1, b_fc1_3, w_fc2, b_fc2_3, layer, ln2_g3, ln2_b3)

        if ctx_out:
            na_c = _ctx_attn_call(qc, kc, vc)
            fm_c = _fourier_call(fc, CTX_LEN, fno_w[layer], fno_b[layer][None])
            c1 = _outproj_call(na_c, lru_c, fm_c, xc, mod4, ctx_row, w_out, layer, ln1_g3, ln1_b3)
            xc = _mlp_call(c1, mod4, ctx_row, 512, w_fc1, b_fc1_3, w_fc2, b_fc2_3, layer, ln2_g3, ln2_b3)

    return xl.reshape(BATCH, SEQ, D_MODEL)
```

```python
import functools
import math

import jax
import jax.numpy as jnp
import numpy as np
from jax import lax
from jax.experimental import pallas as pl
from jax.experimental.pallas import tpu as pltpu

F32 = jnp.float32
BF16 = jnp.bfloat16

D_MODEL = 2048
BATCH = 2
SEQ = 4096
DEPTH = 2
GRID_W = 64
GRID_H = SEQ // GRID_W
CTX_LEN = 256
HEAD_DIM = 128
NA_W = D_MODEL // 2
NA_HEADS = NA_W // HEAD_DIM
WIN_H = 8
WIN_W = 16
LRU_W = D_MODEL // 4
LRU_BLOCKS = 4
LRU_BW = LRU_W // LRU_BLOCKS
CONV_W = 4
LRU_C = 8.0
FNET_W = D_MODEL // 4
FNET_GROUPS = 4
FNET_GW = FNET_W // FNET_GROUPS
IN_W = 3 * NA_W + 2 * LRU_W + FNET_W
D_FF = 4 * D_MODEL
ROPE_THETA = 10000.0
LN_EPS = 1e-5
NEG_INF = -1e30
ALPHA = (2.0 * DEPTH) ** 0.25
ATTN_SCALE = HEAD_DIM ** -0.5

V7X_LANES = 128
V7X_SUBLANES = 8
V7X_VMEM_BYTES = 64 * 1024 * 1024
VMEM_CEILING = V7X_VMEM_BYTES - 6 * 1024 * 1024

COL_TILE = 512
N_COL_TILES = IN_W // COL_TILE
ROW_CHUNK = 128
MOD_ROWS = 8
CTX_MOD_ROW = BATCH


def _vmem_limit(nbytes):
    return int(min(VMEM_CEILING, nbytes * 5 // 4 + (4 << 20)))


def _params(semantics, nbytes):
    return pltpu.CompilerParams(dimension_semantics=semantics, vmem_limit_bytes=_vmem_limit(nbytes))


def _ln(x):
    mu = jnp.mean(x, axis=-1, keepdims=True)
    xc = x - mu
    var = jnp.mean(xc * xc, axis=-1, keepdims=True)
    return xc * lax.rsqrt(var + LN_EPS)


def _sigmoid(x):
    return 1.0 / (1.0 + jnp.exp(-x))


def _gelu_tanh(x):
    return 0.5 * x * (1.0 + jnp.tanh(math.sqrt(2.0 / math.pi) * (x + 0.044715 * (x * x * x))))


def _dot(a, b):
    return jnp.dot(a, b, preferred_element_type=F32)


def _dot_nt(a, b):
    return lax.dot_general(a, b, (((1,), (1,)), ((), ())), preferred_element_type=F32)


MOD_TN = 1024


def _mod_kernel(s_ref, w_ref, b_ref, o_ref):
    s = s_ref[...]
    s = s * _sigmoid(s)
    o_ref[...] = _dot(s.astype(BF16), w_ref[...].astype(BF16)) + b_ref[...]


def _mod_call(s_in, w_mod, b_mod):
    n_out = w_mod.shape[-1]
    nbytes = 2 * (D_MODEL * MOD_TN * 4) + D_MODEL * MOD_TN * 2 + 4 * MOD_ROWS * n_out
    return pl.pallas_call(
        _mod_kernel,
        grid=(DEPTH, n_out // MOD_TN),
        in_specs=[
            pl.BlockSpec((MOD_ROWS, D_MODEL), lambda l, n: (0, 0)),
            pl.BlockSpec((None, D_MODEL, MOD_TN), lambda l, n: (l, 0, n)),
            pl.BlockSpec((None, 1, MOD_TN), lambda l, n: (l, 0, n)),
        ],
        out_specs=pl.BlockSpec((None, MOD_ROWS, MOD_TN), lambda l, n: (l, 0, n)),
        out_shape=jax.ShapeDtypeStruct((DEPTH, MOD_ROWS, n_out), F32),
        compiler_params=_params(("arbitrary", "arbitrary"), nbytes),
        name="modulation",
    )(s_in, w_mod, b_mod.reshape(DEPTH, 1, n_out))


MM_SUB = 256


def _row_tiles(n_rows, size):
    size = min(size, n_rows)
    return [slice(r, r + size) for r in range(0, n_rows, size)]


def _ln_mod_rows(x_ref, shift, scale1, dst_ref, rows):
    for piece in _row_tiles(rows.stop - rows.start, ROW_CHUNK):
        sl = slice(rows.start + piece.start, rows.start + piece.stop)
        dst_ref[sl, :] = (_ln(x_ref[sl, :]) * scale1 + shift).astype(dst_ref.dtype)


def _residual_ln_store(res, y, gate, gain, bias, o_ref, rows):
    for piece in _row_tiles(rows.stop - rows.start, ROW_CHUNK):
        sl = slice(rows.start + piece.start, rows.start + piece.stop)
        z = ALPHA * res[sl, :] + gate * y[piece, :]
        o_ref[sl, :] = _ln(z) * gain + bias


def _rope(a, cos, sin):
    lane = lax.broadcasted_iota(jnp.int32, a.shape, 1)
    first = (lane % (HEAD_DIM // 2)) < (HEAD_DIM // 4)
    partner = jnp.where(first, pltpu.roll(a, HEAD_DIM - HEAD_DIM // 4, 1), pltpu.roll(a, HEAD_DIM // 4, 1))
    return a * cos + partner * sin


def _inproj_lat_kernel(x_ref, mod_ref, w_ref, cos_ref, sin_ref,
                       q_ref, qr_ref, k_ref, v_ref, xo_ref, go_ref, f_ref, xn_ref):
    n = pl.program_id(1)
    shift = mod_ref[0:1, :]
    scale1 = 1.0 + mod_ref[1:2, :]

    def sweep(emit, with_ln=False):
        for rows in _row_tiles(x_ref.shape[0], MM_SUB):
            if with_ln:
                _ln_mod_rows(x_ref, shift, scale1, xn_ref, rows)
            emit(rows, _dot(xn_ref[rows, :], w_ref[...]))

    def emit_q(rows, acc):
        q_ref[rows, :] = (acc * ATTN_SCALE).astype(q_ref.dtype)
        for h in range(COL_TILE // HEAD_DIM):
            sl = slice(h * HEAD_DIM, (h + 1) * HEAD_DIM)
            rot = _rope(acc[:, sl], cos_ref[rows, :], sin_ref[rows, :])
            qr_ref[rows, sl] = (rot * ATTN_SCALE).astype(qr_ref.dtype)

    def emit_k(rows, acc):
        for h in range(COL_TILE // HEAD_DIM):
            sl = slice(h * HEAD_DIM, (h + 1) * HEAD_DIM)
            k_ref[rows, sl] = _rope(acc[:, sl], cos_ref[rows, :], sin_ref[rows, :]).astype(k_ref.dtype)

    def emit_to(ref):
        def emit(rows, acc):
            ref[rows, :] = acc.astype(ref.dtype)
        return emit

    pl.when(n == 0)(lambda: sweep(emit_q, with_ln=True))
    pl.when(n == 1)(lambda: sweep(emit_q))
    pl.when((n >= 2) & (n < 4))(lambda: sweep(emit_k))
    pl.when((n >= 4) & (n < 6))(lambda: sweep(emit_to(v_ref)))
    pl.when(n == 6)(lambda: sweep(emit_to(xo_ref)))
    pl.when(n == 7)(lambda: sweep(emit_to(go_ref)))
    pl.when(n == 8)(lambda: sweep(emit_to(f_ref)))


def _inproj_lat_call(x2d, mod4, w_in, layer, cos_t, sin_t):
    m_rows = x2d.shape[0]
    tm = 1024
    tiles_per_seq = SEQ // tm

    def col(lo):
        return lambda m, n: (m, jnp.clip(n - lo, 0, 1))

    def one(lo):
        return lambda m, n: (m, 0)

    nbytes = (tm * D_MODEL * 4 + tm * D_MODEL * 2 + 2 * D_MODEL * COL_TILE * 2
              + 4 * tm * HEAD_DIM * 4 + 2 * 5 * tm * COL_TILE * 2 + 2 * 2 * tm * COL_TILE * 4 + 6 * MM_SUB * COL_TILE * 4
              + 4 * ROW_CHUNK * D_MODEL * 4)
    bf = lambda w: jax.ShapeDtypeStruct((m_rows, w), BF16)
    ff = lambda w: jax.ShapeDtypeStruct((m_rows, w), F32)
    return pl.pallas_call(
        _inproj_lat_kernel,
        grid=(m_rows // tm, N_COL_TILES),
        in_specs=[
            pl.BlockSpec((tm, D_MODEL), lambda m, n: (m, 0), pipeline_mode=pl.Buffered(1)),
            pl.BlockSpec((None, None, 6, D_MODEL), lambda m, n: (layer, m // tiles_per_seq, 0, 0)),
            pl.BlockSpec((None, D_MODEL, COL_TILE), lambda m, n: (layer, 0, n)),
            pl.BlockSpec((tm, HEAD_DIM), lambda m, n: (m % tiles_per_seq, 0)),
            pl.BlockSpec((tm, HEAD_DIM), lambda m, n: (m % tiles_per_seq, 0)),
        ],
        out_specs=[
            pl.BlockSpec((tm, COL_TILE), col(0)),
            pl.BlockSpec((tm, COL_TILE), col(0)),
            pl.BlockSpec((tm, COL_TILE), col(2)),
            pl.BlockSpec((tm, COL_TILE), col(4)),
            pl.BlockSpec((tm, COL_TILE), one(6)),
            pl.BlockSpec((tm, COL_TILE), one(7)),
            pl.BlockSpec((tm, COL_TILE), one(8)),
        ],
        out_shape=[bf(NA_W), bf(NA_W), bf(NA_W), bf(NA_W), ff(LRU_W), ff(LRU_W), bf(FNET_W)],
        scratch_shapes=[pltpu.VMEM((tm, D_MODEL), BF16)],
        compiler_params=_params(("arbitrary", "arbitrary"), nbytes),
        name="inproj_latent",
    )(x2d, mod4, w_in, cos_t, sin_t)


def _inproj_ctx_kernel(tile_lo, with_q, with_gf, x_ref, mod_ref, w_ref, *refs):
    refs = list(refs)
    xn_ref = refs.pop()
    q_ref = refs.pop(0) if with_q else None
    k_ref, v_ref, xo_ref = refs[0], refs[1], refs[2]
    go_ref, f_ref = (refs[3], refs[4]) if with_gf else (None, None)
    n = pl.program_id(1) + tile_lo

    @pl.when(pl.program_id(1) == 0)
    def _():
        for rows in _row_tiles(x_ref.shape[0], MM_SUB):
            _ln_mod_rows(x_ref, mod_ref[0:1, :], 1.0 + mod_ref[1:2, :], xn_ref, rows)

    acc = _dot(xn_ref[...], w_ref[...])

    if with_q:
        @pl.when(n < 2)
        def _():
            q_ref[...] = (acc * ATTN_SCALE).astype(q_ref.dtype)

    @pl.when((n >= 2) & (n < 4))
    def _():
        k_ref[...] = acc.astype(k_ref.dtype)

    @pl.when((n >= 4) & (n < 6))
    def _():
        v_ref[...] = acc.astype(v_ref.dtype)

    @pl.when(n == 6)
    def _():
        xo_ref[...] = acc

    if with_gf:
        @pl.when(n == 7)
        def _():
            go_ref[...] = acc

        @pl.when(n == 8)
        def _():
            f_ref[...] = acc.astype(f_ref.dtype)


def _inproj_ctx_call(c2d, mod4, w_in, layer, full):
    m_rows = c2d.shape[0]
    tm = m_rows
    tile_lo, tile_hi = (0, N_COL_TILES) if full else (2, 7)

    def col(lo):
        return lambda m, n: (m, jnp.clip(n + tile_lo - lo, 0, 1))

    one = lambda m, n: (m, 0)
    bf = lambda w: jax.ShapeDtypeStruct((m_rows, w), BF16)
    ff = lambda w: jax.ShapeDtypeStruct((m_rows, w), F32)
    out_specs, out_shape = [], []
    if full:
        out_specs.append(pl.BlockSpec((tm, COL_TILE), col(0)))
        out_shape.append(bf(NA_W))
    out_specs += [pl.BlockSpec((tm, COL_TILE), col(2)), pl.BlockSpec((tm, COL_TILE), col(4)),
                  pl.BlockSpec((tm, COL_TILE), one)]
    out_shape += [bf(NA_W), bf(NA_W), ff(LRU_W)]
    if full:
        out_specs += [pl.BlockSpec((tm, COL_TILE), one), pl.BlockSpec((tm, COL_TILE), one)]
        out_shape += [ff(LRU_W), bf(FNET_W)]
    nbytes = (2 * tm * D_MODEL * 4 + tm * D_MODEL * 2 + 2 * D_MODEL * COL_TILE * 2
              + 2 * 6 * tm * COL_TILE * 4 + 3 * tm * COL_TILE * 4 + 4 * ROW_CHUNK * D_MODEL * 4)
    return pl.pallas_call(
        functools.partial(_inproj_ctx_kernel, tile_lo, full, full),
        grid=(1, tile_hi - tile_lo),
        in_specs=[
            pl.BlockSpec((tm, D_MODEL), lambda m, n: (m, 0)),
            pl.BlockSpec((None, None, 6, D_MODEL), lambda m, n: (layer, CTX_MOD_ROW, 0, 0)),
            pl.BlockSpec((None, D_MODEL, COL_TILE), lambda m, n: (layer, 0, n + tile_lo)),
        ],
        out_specs=out_specs,
        out_shape=out_shape,
        scratch_shapes=[pltpu.VMEM((tm, D_MODEL), BF16)],
        compiler_params=_params(("arbitrary", "arbitrary"), nbytes),
        name="inproj_context",
    )(c2d, mod4, w_in)


BAND = WIN_H * GRID_W
ATTN_ROWS_PER_STEP = 4


def _attn_kernel(q_ref, qr_ref, k_ref, v_ref, kc_ref, vc_ref, bias_ref, o_ref):
    kc = kc_ref[...]
    vc = vc_ref[...]

    def one_row(r):
        row_start = jnp.clip(r - WIN_H // 2, 0, GRID_H - WIN_H)
        variant = row_start - r + (WIN_H - 1)
        q0 = pl.multiple_of(r * GRID_W, GRID_W)
        k0 = pl.multiple_of(row_start * GRID_W, GRID_W)
        s = _dot_nt(qr_ref[pl.ds(q0, GRID_W), :], k_ref[pl.ds(k0, BAND), :]) + bias_ref[variant]
        sc = _dot_nt(q_ref[pl.ds(q0, GRID_W), :], kc)
        m = jnp.maximum(jnp.max(s, axis=-1, keepdims=True), jnp.max(sc, axis=-1, keepdims=True))
        p = jnp.exp(s - m)
        pc = jnp.exp(sc - m)
        denom = jnp.sum(p, axis=-1, keepdims=True) + jnp.sum(pc, axis=-1, keepdims=True)
        o = _dot(p.astype(BF16), v_ref[pl.ds(k0, BAND), :]) + _dot(pc.astype(BF16), vc)
        o_ref[pl.ds(q0, GRID_W), :] = (o / denom).astype(o_ref.dtype)

    def body(i, carry):
        for u in range(ATTN_ROWS_PER_STEP):
            one_row(i * ATTN_ROWS_PER_STEP + u)
        return carry

    lax.fori_loop(0, GRID_H // ATTN_ROWS_PER_STEP, body, 0)


def _attn_call(q, qr, k, v, kc, vc, bias):
    seq_blk = lambda b, h: (b, h)
    nbytes = 2 * (5 * SEQ * HEAD_DIM * 2 + 2 * CTX_LEN * HEAD_DIM * 2 + WIN_H * GRID_W * BAND * 4) + (4 << 20)
    return pl.pallas_call(
        _attn_kernel,
        grid=(BATCH, NA_HEADS),
        in_specs=[
            pl.BlockSpec((SEQ, HEAD_DIM), seq_blk),
            pl.BlockSpec((SEQ, HEAD_DIM), seq_blk),
            pl.BlockSpec((SEQ, HEAD_DIM), seq_blk),
            pl.BlockSpec((SEQ, HEAD_DIM), seq_blk),
            pl.BlockSpec((CTX_LEN, HEAD_DIM), seq_blk),
            pl.BlockSpec((CTX_LEN, HEAD_DIM), seq_blk),
            pl.BlockSpec((None, WIN_H, GRID_W, BAND), lambda b, h: (h, 0, 0, 0)),
        ],
        out_specs=pl.BlockSpec((SEQ, HEAD_DIM), seq_blk),
        out_shape=jax.ShapeDtypeStruct((BATCH * SEQ, NA_W), BF16),
        compiler_params=_params(("arbitrary", "arbitrary"), nbytes),
        name="neighbourhood_attention",
    )(q, qr, k, v, kc, vc, bias)


def _ctx_attn_kernel(q_ref, k_ref, v_ref, o_ref):
    s = _dot_nt(q_ref[...], k_ref[...])
    m = jnp.max(s, axis=-1, keepdims=True)
    p = jnp.exp(s - m)
    denom = jnp.sum(p, axis=-1, keepdims=True)
    o_ref[...] = (_dot(p.astype(BF16), v_ref[...]) / denom).astype(o_ref.dtype)


def _ctx_attn_call(q, k, v):
    blk = pl.BlockSpec((CTX_LEN, HEAD_DIM), lambda b, h: (b, h))
    return pl.pallas_call(
        _ctx_attn_kernel,
        grid=(BATCH, NA_HEADS),
        in_specs=[blk, blk, blk],
        out_specs=blk,
        out_shape=jax.ShapeDtypeStruct((BATCH * CTX_LEN, NA_W), BF16),
        compiler_params=_params(("arbitrary", "arbitrary"), 16 << 20),
        name="context_attention",
    )(q, k, v)


def _attn_bias_table(rpb_l):
    col = jnp.arange(GRID_W)
    col_start = jnp.clip(col - WIN_W // 2, 0, GRID_W - WIN_W)
    in_win = (col[None, :] >= col_start[:, None]) & (col[None, :] < col_start[:, None] + WIN_W)
    dc = jnp.clip(col[None, :] - col[:, None] + (WIN_W - 1), 0, 2 * WIN_W - 2)
    t = jnp.where(in_win[None, None], rpb_l[:, :, dc], NEG_INF)
    dr = jnp.arange(WIN_H)[:, None] + jnp.arange(WIN_H)[None, :]
    tv = t[:, dr]
    return tv.transpose(0, 1, 3, 2, 4).reshape(NA_HEADS, WIN_H, GRID_W, BAND).astype(F32)


def _rope_tables():
    quarter = HEAD_DIM // 4
    inv = ROPE_THETA ** (-jnp.arange(quarter, dtype=F32) / quarter)
    t = jnp.arange(SEQ)
    ang_r = (t // GRID_W).astype(F32)[:, None] * inv
    ang_c = (t % GRID_W).astype(F32)[:, None] * inv
    cos = jnp.concatenate([jnp.cos(ang_r), jnp.cos(ang_r), jnp.cos(ang_c), jnp.cos(ang_c)], -1)
    sin = jnp.concatenate([-jnp.sin(ang_r), jnp.sin(ang_r), -jnp.sin(ang_c), jnp.sin(ang_c)], -1)
    return cos, sin


HALO = V7X_SUBLANES
LRU_CHUNK = 512


def _lru_coeffs(xp_ref, n_rows, cw_ref, cb_ref, w4, b4_ref, sp, a_refs, u_refs):
    chunk = min(LRU_CHUNK, n_rows)
    for c in range(n_rows // chunk):
        base = HALO + c * chunk
        xc = cb_ref[...] + xp_ref[base - CONV_W // 2:base - CONV_W // 2 + chunk, :] * cw_ref[0:1, :]
        for j in range(1, CONV_W):
            off = base - CONV_W // 2 + j
            xc = xc + xp_ref[off:off + chunk, :] * cw_ref[j:j + 1, :]
        z = _dot(xc.astype(BF16), w4) + b4_ref[...]
        for d in range(2):
            r = _sigmoid(z[:, (2 * d) * LRU_BW:(2 * d + 1) * LRU_BW])
            i = _sigmoid(z[:, (2 * d + 1) * LRU_BW:(2 * d + 2) * LRU_BW])
            log_a = (-LRU_C) * r * sp[d:d + 1, :]
            a = jnp.exp(log_a)
            a_refs[d][c * chunk:(c + 1) * chunk, :] = a
            one_minus_a2 = -jnp.tanh(log_a) * (a * a + 1.0)
            u_refs[d][c * chunk:(c + 1) * chunk, :] = jnp.sqrt(one_minus_a2) * (i * xc)


def _lru_scan(n_rows, hf, hb, af, uf, ab, ub, yf, yb):
    unroll = V7X_SUBLANES

    def body(i, carry):
        hf, hb = carry
        f0 = pl.multiple_of(i * unroll, unroll)
        b0 = pl.multiple_of(n_rows - unroll - i * unroll, unroll)
        for j in range(unroll):
            tf = f0 + j
            hf = af[pl.ds(tf, 1), :] * hf + uf[pl.ds(tf, 1), :]
            yf[pl.ds(tf, 1), :] = hf
            tb = b0 + (unroll - 1 - j)
            hb = ab[pl.ds(tb, 1), :] * hb + ub[pl.ds(tb, 1), :]
            yb[pl.ds(tb, 1), :] = hb
        return hf, hb

    return lax.fori_loop(0, n_rows // unroll, body, (hf, hb))


def _lru_kernel(ctx_out, x_ref, g_ref, xc_ref, *refs):
    refs = list(refs)
    gc_ref = refs.pop(0) if ctx_out else None
    cw_ref, cb_ref, w4_ref, b4_ref, lam_ref, o_ref = refs[:6]
    refs = refs[6:]
    oc_ref = refs.pop(0) if ctx_out else None
    xp_ref, af, uf, ab, ub, yf, yb = refs

    lam = lam_ref[...]
    z = -lam
    sp = jnp.maximum(z, 0.0) + jnp.log1p(jnp.exp(-jnp.abs(z)))
    w4 = w4_ref[...].astype(BF16)
    zeros_halo = jnp.zeros((HALO, LRU_BW), F32)
    h0 = jnp.zeros((1, LRU_BW), F32)

    xp_ref[0:HALO, :] = zeros_halo
    xp_ref[HALO:HALO + CTX_LEN, :] = xc_ref[...]
    xp_ref[HALO + CTX_LEN:2 * HALO + CTX_LEN, :] = zeros_halo
    _lru_coeffs(xp_ref, CTX_LEN, cw_ref, cb_ref, w4, b4_ref, sp, (af, ab), (uf, ub))
    hf, hb = _lru_scan(CTX_LEN, h0, h0, af, uf, ab, ub, yf, yb)
    if ctx_out:
        y = yf[0:CTX_LEN, :] + yb[0:CTX_LEN, :]
        oc_ref[...] = (y * _gelu_tanh(gc_ref[...])).astype(oc_ref.dtype)

    xp_ref[HALO:HALO + SEQ, :] = x_ref[...]
    xp_ref[HALO + SEQ:2 * HALO + SEQ, :] = zeros_halo
    _lru_coeffs(xp_ref, SEQ, cw_ref, cb_ref, w4, b4_ref, sp, (af, ab), (uf, ub))
    _lru_scan(SEQ, hf, hb, af, uf, ab, ub, yf, yb)
    for c in range(SEQ // LRU_CHUNK):
        sl = slice(c * LRU_CHUNK, (c + 1) * LRU_CHUNK)
        o_ref[sl, :] = ((yf[sl, :] + yb[sl, :]) * _gelu_tanh(g_ref[sl, :])).astype(o_ref.dtype)


def _lru_call(ctx_out, xl, gl, xc, gc, conv_w_l, conv_b_l, w4, b4, lam_l):
    lat = pl.BlockSpec((SEQ, LRU_BW), lambda b, j: (b, j))
    cx = pl.BlockSpec((CTX_LEN, LRU_BW), lambda b, j: (b, j))
    in_specs = [lat, lat, cx] + ([cx] if ctx_out else []) + [
        pl.BlockSpec((CONV_W, LRU_BW), lambda b, j: (0, j)),
        pl.BlockSpec((1, LRU_BW), lambda b, j: (0, j)),
        pl.BlockSpec((None, LRU_BW, 4 * LRU_BW), lambda b, j: (j, 0, 0)),
        pl.BlockSpec((None, 1, 4 * LRU_BW), lambda b, j: (j, 0, 0)),
        pl.BlockSpec((2, LRU_BW), lambda b, j: (0, j)),
    ]
    out_specs = [lat] + ([cx] if ctx_out else [])
    out_shape = [jax.ShapeDtypeStruct((BATCH * SEQ, LRU_W), BF16)]
    if ctx_out:
        out_shape.append(jax.ShapeDtypeStruct((BATCH * CTX_LEN, LRU_W), BF16))
    seq_bytes = SEQ * LRU_BW * 4
    args = [xl, gl, xc] + ([gc] if ctx_out else []) + [conv_w_l, conv_b_l, w4, b4, lam_l]
    res = pl.pallas_call(
        functools.partial(_lru_kernel, ctx_out),
        grid=(BATCH, LRU_BLOCKS),
        in_specs=in_specs,
        out_specs=out_specs,
        out_shape=out_shape,
        scratch_shapes=[pltpu.VMEM((SEQ + 2 * HALO, LRU_BW), F32)] + [pltpu.VMEM((SEQ, LRU_BW), F32)] * 6,
        compiler_params=_params(("arbitrary", "arbitrary"), 13 * seq_bytes + (8 << 20)),
        name="rglru",
    )(*args)
    return (res[0], res[1]) if ctx_out else (res[0], None)


def _fourier_kernel(norm, f_ref, cl_ref, sl_ref, cc_ref, sc_ref, w_ref, b_ref, o_ref, xc_ref, xs_ref):
    @pl.when(pl.program_id(1) == 0)
    def _():
        for g in range(FNET_GROUPS):
            sl = slice(g * FNET_GW, (g + 1) * FNET_GW)
            xg = f_ref[:, sl]
            xc_ref[:, sl] = _dot(xg, cc_ref[...]).astype(BF16)
            xs_ref[:, sl] = _dot(xg, sc_ref[...]).astype(BF16)

    y = (_dot(cl_ref[...], xc_ref[...]) - _dot(sl_ref[...], xs_ref[...])) * norm
    o_ref[...] = (_dot(y.astype(BF16), w_ref[...].astype(BF16)) + b_ref[...]).astype(o_ref.dtype)


DFT_SPLIT = 64


def _dft_matrices(n):
    t = np.arange(n, dtype=np.int64)

    def table(k):
        ang = (2.0 * np.pi / n) * ((k[:, None] * t[None, :]) % n).astype(np.float64)
        return jnp.asarray(np.cos(ang), F32), jnp.asarray(np.sin(ang), F32)

    if n <= DFT_SPLIT:
        c, s = table(t)
        return c.astype(BF16), s.astype(BF16)
    c1, s1 = table(DFT_SPLIT * np.arange(n // DFT_SPLIT, dtype=np.int64))
    c2, s2 = table(np.arange(DFT_SPLIT, dtype=np.int64))
    c = c1[:, None, :] * c2[None, :, :] - s1[:, None, :] * s2[None, :, :]
    s = s1[:, None, :] * c2[None, :, :] + c1[:, None, :] * s2[None, :, :]
    return c.reshape(n, n).astype(BF16), s.reshape(n, n).astype(BF16)


def _fourier_call(f2d, n_pos, fno_w_l, fno_b_l):
    tk = min(512, n_pos)
    steps = n_pos // tk
    cl, sl = _dft_matrices(n_pos)
    cc, sc = _dft_matrices(FNET_GW)
    norm = 1.0 / math.sqrt(n_pos * FNET_GW)
    nbytes = (2 * n_pos * FNET_W * 2 + 2 * 2 * tk * n_pos * 2 + 2 * n_pos * FNET_W * 2 + 2 * FNET_W * FNET_W * 4
              + 4 * tk * FNET_W * 4)
    return pl.pallas_call(
        functools.partial(_fourier_kernel, norm),
        grid=(BATCH, steps),
        in_specs=[
            pl.BlockSpec((n_pos, FNET_W), lambda b, k: (b, 0)),
            pl.BlockSpec((tk, n_pos), lambda b, k: (k, 0)),
            pl.BlockSpec((tk, n_pos), lambda b, k: (k, 0)),
            pl.BlockSpec((FNET_GW, FNET_GW), lambda b, k: (0, 0)),
            pl.BlockSpec((FNET_GW, FNET_GW), lambda b, k: (0, 0)),
            pl.BlockSpec((FNET_W, FNET_W), lambda b, k: (0, 0)),
            pl.BlockSpec((1, FNET_W), lambda b, k: (0, 0)),
        ],
        out_specs=pl.BlockSpec((tk, FNET_W), lambda b, k: (b * steps + k, 0)),
        out_shape=jax.ShapeDtypeStruct((BATCH * n_pos, FNET_W), BF16),
        scratch_shapes=[pltpu.VMEM((n_pos, FNET_W), BF16), pltpu.VMEM((n_pos, FNET_W), BF16)],
        compiler_params=_params(("arbitrary", "arbitrary"), nbytes),
        name="fourier_mix",
    )(f2d, cl, sl, cc, sc, fno_w_l, fno_b_l)


def _outproj_kernel(na_ref, lru_ref, f_ref, res_ref, mod_ref, w_ref, g_ref, b_ref, o_ref):
    gate = mod_ref[2:3, :]
    for rows in _row_tiles(res_ref.shape[0], MM_SUB):
        y = (_dot(na_ref[rows, :], w_ref[0:NA_W, :])
             + _dot(lru_ref[rows, :], w_ref[NA_W:NA_W + LRU_W, :])
             + _dot(f_ref[rows, :], w_ref[NA_W + LRU_W:D_MODEL, :]))
        _residual_ln_store(res_ref, y, gate, g_ref[...], b_ref[...], o_ref, rows)


def _outproj_call(na, lru, f, res, mod4, mod_row, w_out, layer, ln_g, ln_b):
    m_rows = res.shape[0]
    tm = 512
    row = lambda m: (m, 0)
    nbytes = (2 * tm * D_MODEL * 2 + 2 * 2 * tm * D_MODEL * 4 + D_MODEL * D_MODEL * 2 + 2 * MM_SUB * D_MODEL * 4
              + 4 * ROW_CHUNK * D_MODEL * 4)
    return pl.pallas_call(
        _outproj_kernel,
        grid=(m_rows // tm,),
        in_specs=[
            pl.BlockSpec((tm, NA_W), row),
            pl.BlockSpec((tm, LRU_W), row),
            pl.BlockSpec((tm, FNET_W), row),
            pl.BlockSpec((tm, D_MODEL), row),
            pl.BlockSpec((None, None, 6, D_MODEL), lambda m: (layer, mod_row(m * tm), 0, 0)),
            pl.BlockSpec((None, D_MODEL, D_MODEL), lambda m: (layer, 0, 0), pipeline_mode=pl.Buffered(1)),
            pl.BlockSpec((None, 1, D_MODEL), lambda m: (layer, 0, 0)),
            pl.BlockSpec((None, 1, D_MODEL), lambda m: (layer, 0, 0)),
        ],
        out_specs=pl.BlockSpec((tm, D_MODEL), row),
        out_shape=jax.ShapeDtypeStruct((m_rows, D_MODEL), F32),
        compiler_params=_params(("arbitrary",), nbytes),
        name="outproj_residual",
    )(na, lru, f, res, mod4, w_out, ln_g, ln_b)


MLP_TF = 512
MLP_TILES = D_FF // MLP_TF


def _mlp_kernel(x_ref, mod_ref, w1_ref, b1_ref, w2_ref, b2_ref, g_ref, b_ref, o_ref, v_ref, h0_ref, h1_ref):
    j = pl.program_id(1)
    tm = x_ref.shape[0]

    def up(h_out, rows=slice(None)):
        h = _dot(v_ref[rows, :], w1_ref[...]) + b1_ref[...]
        h_out[rows, :] = jnp.square(jnp.maximum(h, 0.0)).astype(BF16)

    def down(h_in, first):
        for c in range(D_MODEL // COL_TILE):
            sl = slice(c * COL_TILE, (c + 1) * COL_TILE)
            part = _dot(h_in[...], w2_ref[:, sl])
            if first:
                o_ref[:, sl] = part
            else:
                o_ref[:, sl] += part

    @pl.when(j == 0)
    def _():
        shift = mod_ref[3:4, :]
        scale1 = 1.0 + mod_ref[4:5, :]
        for rows in _row_tiles(tm, MM_SUB):
            _ln_mod_rows(x_ref, shift, scale1, v_ref, rows)
            up(h0_ref, rows)

    @pl.when(j == 1)
    def _():
        down(h0_ref, True)
        up(h1_ref)

    @pl.when((j > 1) & (j < MLP_TILES) & (j % 2 == 0))
    def _():
        down(h1_ref, False)
        up(h0_ref)

    @pl.when((j > 1) & (j < MLP_TILES) & (j % 2 == 1))
    def _():
        down(h0_ref, False)
        up(h1_ref)

    @pl.when(j == MLP_TILES)
    def _():
        h_last = h1_ref if (MLP_TILES - 1) % 2 else h0_ref
        gate = mod_ref[5:6, :]
        for rows in _row_tiles(tm, MM_SUB):
            y = o_ref[rows, :] + _dot(h_last[rows, :], w2_ref[...]) + b2_ref[...]
            _residual_ln_store(x_ref, y, gate, g_ref[...], b_ref[...], o_ref, rows)


def _mlp_call(x1, mod4, mod_row, tm, w1, b1, w2, b2, layer, ln_g, ln_b):
    m_rows = x1.shape[0]
    row = lambda m, j: (m, 0)
    vec = lambda m, j: (layer, 0, 0)
    nbytes = (3 * tm * D_MODEL * 4 + tm * D_MODEL * 2 + 2 * 2 * D_MODEL * MLP_TF * 2 + 2 * tm * MLP_TF * 2
              + tm * MLP_TF * 4 + tm * COL_TILE * 4 + 2 * MM_SUB * D_MODEL * 4 + 4 * ROW_CHUNK * D_MODEL * 4)
    return pl.pallas_call(
        _mlp_kernel,
        grid=(m_rows // tm, MLP_TILES + 1),
        in_specs=[
            pl.BlockSpec((tm, D_MODEL), row, pipeline_mode=pl.Buffered(1)),
            pl.BlockSpec((None, None, 6, D_MODEL), lambda m, j: (layer, mod_row(m * tm), 0, 0)),
            pl.BlockSpec((None, D_MODEL, MLP_TF), lambda m, j: (layer, 0, jnp.minimum(j, MLP_TILES - 1))),
            pl.BlockSpec((None, 1, MLP_TF), lambda m, j: (layer, 0, jnp.minimum(j, MLP_TILES - 1))),
            pl.BlockSpec((None, MLP_TF, D_MODEL), lambda m, j: (layer, jnp.maximum(j - 1, 0), 0)),
            pl.BlockSpec((None, 1, D_MODEL), vec),
            pl.BlockSpec((None, 1, D_MODEL), vec),
            pl.BlockSpec((None, 1, D_MODEL), vec),
        ],
        out_specs=pl.BlockSpec((tm, D_MODEL), row),
        out_shape=jax.ShapeDtypeStruct((m_rows, D_MODEL), F32),
        scratch_shapes=[pltpu.VMEM((tm, D_MODEL), BF16), pltpu.VMEM((tm, MLP_TF), BF16),
                        pltpu.VMEM((tm, MLP_TF), BF16)],
        compiler_params=_params(("arbitrary", "arbitrary"), nbytes),
        name="mlp_residual",
    )(x1, mod4, w1, b1, w2, b2, ln_g, ln_b)


def kernel(x, c, ctx, c_ctx, w_mod, b_mod, w_in, rpb, conv_w, conv_b, lru_wa, lru_ba, lru_wx, lru_bx, lru_lambda,
           fno_w, fno_b, w_out, ln1_g, ln1_b, w_fc1, b_fc1, w_fc2, b_fc2, ln2_g, ln2_b):
    xl = x.reshape(BATCH * SEQ, D_MODEL)
    xc = ctx.reshape(BATCH * CTX_LEN, D_MODEL)
    s_in = jnp.concatenate([c, c_ctx[None], jnp.zeros((MOD_ROWS - BATCH - 1, D_MODEL), F32)], 0)
    mod4 = _mod_call(s_in, w_mod, b_mod).reshape(DEPTH, MOD_ROWS, 6, D_MODEL)
    cos_t, sin_t = _rope_tables()
    w_in, w_out, w_fc1, w_fc2 = (w.astype(BF16) for w in (w_in, w_out, w_fc1, w_fc2))
    vec3 = lambda a: a.reshape(DEPTH, 1, a.shape[-1])
    ln1_g3, ln1_b3, ln2_g3, ln2_b3 = vec3(ln1_g), vec3(ln1_b), vec3(ln2_g), vec3(ln2_b)
    b_fc1_3, b_fc2_3 = vec3(b_fc1), vec3(b_fc2)
    lat_row = lambda r0: r0 // SEQ
    ctx_row = lambda r0: CTX_MOD_ROW

    for layer in range(DEPTH):
        ctx_out = layer < DEPTH - 1
        q, qr, k, v, xo, go, f = _inproj_lat_call(xl, mod4, w_in, layer, cos_t, sin_t)
        if ctx_out:
            qc, kc, vc, xoc, goc, fc = _inproj_ctx_call(xc, mod4, w_in, layer, True)
        else:
            kc, vc, xoc = _inproj_ctx_call(xc, mod4, w_in, layer, False)
            goc = None

        na = _attn_call(q, qr, k, v, kc, vc, _attn_bias_table(rpb[layer]))

        w4 = jnp.concatenate([lru_wa[layer, 0], lru_wx[layer, 0], lru_wa[layer, 1], lru_wx[layer, 1]], -1)
        blk = lambda a: a.reshape(LRU_BLOCKS, 1, LRU_BW)
        b4 = jnp.concatenate([blk(lru_ba[layer, 0]), blk(lru_bx[layer, 0]),
                              blk(lru_ba[layer, 1]), blk(lru_bx[layer, 1])], -1)
        lru, lru_c = _lru_call(ctx_out, xo, go, xoc, goc, conv_w[layer], conv_b[layer][None], w4, b4,
                               lru_lambda[layer])

        fm = _fourier_call(f, SEQ, fno_w[layer], fno_b[layer][None])
        x1 = _outproj_call(na, lru, fm, xl, mod4, lat_row, w_out, layer, ln1_g3, ln1_b3)
        xl = _mlp_call(x1, mod4, lat_row, 1024, w_fc1, b_fc1_3, w_fc2, b_fc2_3, layer, ln2_g3, ln2_b3)

        if ctx_out:
            na_c = _ctx_attn_call(qc, kc, vc)
            fm_c = _fourier_call(fc, CTX_LEN, fno_w[layer], fno_b[layer][None])
            c1 = _outproj_call(na_c, lru_c, fm_c, xc, mod4, ctx_row, w_out, layer, ln1_g3, ln1_b3)
            xc = _mlp_call(c1, mod4, ctx_row, 512, w_fc1, b_fc1_3, w_fc2, b_fc2_3, layer, ln2_g3, ln2_b3)

    return xl.reshape(BATCH, SEQ, D_MODEL)
```

```python
import functools
import math

import jax
import jax.numpy as jnp
import numpy as np
from jax import lax
from jax.experimental import pallas as pl
from jax.experimental.pallas import tpu as pltpu

F32 = jnp.float32
BF16 = jnp.bfloat16

D_MODEL = 2048
BATCH = 2
SEQ = 4096
DEPTH = 2
GRID_W = 64
GRID_H = SEQ // GRID_W
CTX_LEN = 256
HEAD_DIM = 128
NA_W = D_MODEL // 2
NA_HEADS = NA_W // HEAD_DIM
WIN_H = 8
WIN_W = 16
LRU_W = D_MODEL // 4
LRU_BLOCKS = 4
LRU_BW = LRU_W // LRU_BLOCKS
CONV_W = 4
LRU_C = 8.0
FNET_W = D_MODEL // 4
FNET_GROUPS = 4
FNET_GW = FNET_W // FNET_GROUPS
IN_W = 3 * NA_W + 2 * LRU_W + FNET_W
D_FF = 4 * D_MODEL
ROPE_THETA = 10000.0
LN_EPS = 1e-5
NEG_INF = -1e30
ALPHA = (2.0 * DEPTH) ** 0.25
ATTN_SCALE = HEAD_DIM ** -0.5

V7X_LANES = 128
V7X_SUBLANES = 8
V7X_VMEM_BYTES = 64 * 1024 * 1024
VMEM_CEILING = V7X_VMEM_BYTES - 6 * 1024 * 1024

COL_TILE = 512
N_COL_TILES = IN_W // COL_TILE
ROW_CHUNK = 128
MOD_ROWS = 8
CTX_MOD_ROW = BATCH


def _vmem_limit(nbytes):
    return int(min(VMEM_CEILING, nbytes * 5 // 4 + (4 << 20)))


def _params(semantics, nbytes):
    return pltpu.CompilerParams(dimension_semantics=semantics, vmem_limit_bytes=_vmem_limit(nbytes))


def _ln(x):
    mu = jnp.mean(x, axis=-1, keepdims=True)
    xc = x - mu
    var = jnp.mean(xc * xc, axis=-1, keepdims=True)
    return xc * lax.rsqrt(var + LN_EPS)


def _sigmoid(x):
    return 1.0 / (1.0 + jnp.exp(-x))


def _gelu_tanh(x):
    return 0.5 * x * (1.0 + jnp.tanh(math.sqrt(2.0 / math.pi) * (x + 0.044715 * (x * x * x))))


def _dot(a, b):
    return jnp.dot(a, b, preferred_element_type=F32)


def _dot_nt(a, b):
    return lax.dot_general(a, b, (((1,), (1,)), ((), ())), preferred_element_type=F32)


MOD_TN = 1024


def _mod_kernel(s_ref, w_ref, b_ref, o_ref):
    s = s_ref[...]
    s = s * _sigmoid(s)
    o_ref[...] = _dot(s.astype(BF16), w_ref[...].astype(BF16)) + b_ref[...]


def _mod_call(s_in, w_mod, b_mod):
    n_out = w_mod.shape[-1]
    nbytes = 2 * (D_MODEL * MOD_TN * 4) + D_MODEL * MOD_TN * 2 + 4 * MOD_ROWS * n_out
    return pl.pallas_call(
        _mod_kernel,
        grid=(DEPTH, n_out // MOD_TN),
        in_specs=[
            pl.BlockSpec((MOD_ROWS, D_MODEL), lambda l, n: (0, 0)),
            pl.BlockSpec((None, D_MODEL, MOD_TN), lambda l, n: (l, 0, n)),
            pl.BlockSpec((None, 1, MOD_TN), lambda l, n: (l, 0, n)),
        ],
        out_specs=pl.BlockSpec((None, MOD_ROWS, MOD_TN), lambda l, n: (l, 0, n)),
        out_shape=jax.ShapeDtypeStruct((DEPTH, MOD_ROWS, n_out), F32),
        compiler_params=_params(("arbitrary", "arbitrary"), nbytes),
        name="modulation",
    )(s_in, w_mod, b_mod.reshape(DEPTH, 1, n_out))


MM_SUB = 256


def _row_tiles(n_rows, size):
    size = min(size, n_rows)
    return [slice(r, r + size) for r in range(0, n_rows, size)]


def _ln_mod_rows(x_ref, shift, scale1, dst_ref, rows):
    for piece in _row_tiles(rows.stop - rows.start, ROW_CHUNK):
        sl = slice(rows.start + piece.start, rows.start + piece.stop)
        dst_ref[sl, :] = (_ln(x_ref[sl, :]) * scale1 + shift).astype(dst_ref.dtype)


def _residual_ln_store(res, y, gate, gain, bias, o_ref, rows):
    for piece in _row_tiles(rows.stop - rows.start, ROW_CHUNK):
        sl = slice(rows.start + piece.start, rows.start + piece.stop)
        z = ALPHA * res[sl, :] + gate * y[piece, :]
        o_ref[sl, :] = _ln(z) * gain + bias


def _rope(a, cos, sin):
    lane = lax.broadcasted_iota(jnp.int32, a.shape, 1)
    first = (lane % (HEAD_DIM // 2)) < (HEAD_DIM // 4)
    partner = jnp.where(first, pltpu.roll(a, HEAD_DIM - HEAD_DIM // 4, 1), pltpu.roll(a, HEAD_DIM // 4, 1))
    return a * cos + partner * sin


def _inproj_lat_kernel(x_ref, mod_ref, w_ref, cos_ref, sin_ref,
                       q_ref, qr_ref, k_ref, v_ref, xo_ref, go_ref, f_ref, xn_ref):
    shift = mod_ref[0:1, :]
    scale1 = 1.0 + mod_ref[1:2, :]

    def emit_q(rows, cols, acc):
        q_ref[rows, cols] = (acc * ATTN_SCALE).astype(q_ref.dtype)
        for h in range(COL_TILE // HEAD_DIM):
            sl = slice(h * HEAD_DIM, (h + 1) * HEAD_DIM)
            dst = slice(cols.start + sl.start, cols.start + sl.stop)
            rot = _rope(acc[:, sl], cos_ref[rows, :], sin_ref[rows, :])
            qr_ref[rows, dst] = (rot * ATTN_SCALE).astype(qr_ref.dtype)

    def emit_k(rows, cols, acc):
        for h in range(COL_TILE // HEAD_DIM):
            sl = slice(h * HEAD_DIM, (h + 1) * HEAD_DIM)
            dst = slice(cols.start + sl.start, cols.start + sl.stop)
            k_ref[rows, dst] = _rope(acc[:, sl], cos_ref[rows, :], sin_ref[rows, :]).astype(k_ref.dtype)

    def emit_to(ref):
        def emit(rows, cols, acc):
            ref[rows, cols] = acc.astype(ref.dtype)
        return emit

    half = [slice(0, COL_TILE), slice(COL_TILE, 2 * COL_TILE)]
    plan = ([(emit_q, c) for c in half] + [(emit_k, c) for c in half] + [(emit_to(v_ref), c) for c in half]
            + [(emit_to(xo_ref), half[0]), (emit_to(go_ref), half[0]), (emit_to(f_ref), half[0])])
    for rows in _row_tiles(x_ref.shape[0], MM_SUB):
        _ln_mod_rows(x_ref, shift, scale1, xn_ref, rows)
        for n, (emit, cols) in enumerate(plan):
            emit(rows, cols, _dot(xn_ref[rows, :], w_ref[:, n * COL_TILE:(n + 1) * COL_TILE]))


def _inproj_lat_call(x2d, mod4, w_in, layer, cos_t, sin_t):
    m_rows = x2d.shape[0]
    tm = 512
    tiles_per_seq = SEQ // tm
    row = lambda m: (m, 0)
    nbytes = (2 * tm * D_MODEL * 4 + tm * D_MODEL * 2 + D_MODEL * IN_W * 2 + 4 * tm * HEAD_DIM * 4
              + 2 * tm * (4 * NA_W * 2 + 2 * LRU_W * 4 + FNET_W * 2) + 6 * MM_SUB * COL_TILE * 4
              + 4 * ROW_CHUNK * D_MODEL * 4)
    bf = lambda w: jax.ShapeDtypeStruct((m_rows, w), BF16)
    ff = lambda w: jax.ShapeDtypeStruct((m_rows, w), F32)
    widths = [NA_W, NA_W, NA_W, NA_W, LRU_W, LRU_W, FNET_W]
    return pl.pallas_call(
        _inproj_lat_kernel,
        grid=(m_rows // tm,),
        in_specs=[
            pl.BlockSpec((tm, D_MODEL), row),
            pl.BlockSpec((None, None, 6, D_MODEL), lambda m: (layer, m // tiles_per_seq, 0, 0)),
            pl.BlockSpec((None, D_MODEL, IN_W), lambda m: (layer, 0, 0), pipeline_mode=pl.Buffered(1)),
            pl.BlockSpec((tm, HEAD_DIM), lambda m: (m % tiles_per_seq, 0)),
            pl.BlockSpec((tm, HEAD_DIM), lambda m: (m % tiles_per_seq, 0)),
        ],
        out_specs=[pl.BlockSpec((tm, w), row) for w in widths],
        out_shape=[bf(NA_W), bf(NA_W), bf(NA_W), bf(NA_W), ff(LRU_W), ff(LRU_W), bf(FNET_W)],
        scratch_shapes=[pltpu.VMEM((tm, D_MODEL), BF16)],
        compiler_params=_params(("arbitrary",), nbytes),
        name="inproj_latent",
    )(x2d, mod4, w_in, cos_t, sin_t)


def _inproj_ctx_kernel(tile_lo, with_q, with_gf, x_ref, mod_ref, w_ref, *refs):
    refs = list(refs)
    xn_ref = refs.pop()
    q_ref = refs.pop(0) if with_q else None
    k_ref, v_ref, xo_ref = refs[0], refs[1], refs[2]
    go_ref, f_ref = (refs[3], refs[4]) if with_gf else (None, None)
    n = pl.program_id(1) + tile_lo

    @pl.when(pl.program_id(1) == 0)
    def _():
        for rows in _row_tiles(x_ref.shape[0], MM_SUB):
            _ln_mod_rows(x_ref, mod_ref[0:1, :], 1.0 + mod_ref[1:2, :], xn_ref, rows)

    acc = _dot(xn_ref[...], w_ref[...])

    if with_q:
        @pl.when(n < 2)
        def _():
            q_ref[...] = (acc * ATTN_SCALE).astype(q_ref.dtype)

    @pl.when((n >= 2) & (n < 4))
    def _():
        k_ref[...] = acc.astype(k_ref.dtype)

    @pl.when((n >= 4) & (n < 6))
    def _():
        v_ref[...] = acc.astype(v_ref.dtype)

    @pl.when(n == 6)
    def _():
        xo_ref[...] = acc

    if with_gf:
        @pl.when(n == 7)
        def _():
            go_ref[...] = acc

        @pl.when(n == 8)
        def _():
            f_ref[...] = acc.astype(f_ref.dtype)


def _inproj_ctx_call(c2d, mod4, w_in, layer, full):
    m_rows = c2d.shape[0]
    tm = m_rows
    tile_lo, tile_hi = (0, N_COL_TILES) if full else (2, 7)

    def col(lo):
        return lambda m, n: (m, jnp.clip(n + tile_lo - lo, 0, 1))

    one = lambda m, n: (m, 0)
    bf = lambda w: jax.ShapeDtypeStruct((m_rows, w), BF16)
    ff = lambda w: jax.ShapeDtypeStruct((m_rows, w), F32)
    out_specs, out_shape = [], []
    if full:
        out_specs.append(pl.BlockSpec((tm, COL_TILE), col(0)))
        out_shape.append(bf(NA_W))
    out_specs += [pl.BlockSpec((tm, COL_TILE), col(2)), pl.BlockSpec((tm, COL_TILE), col(4)),
                  pl.BlockSpec((tm, COL_TILE), one)]
    out_shape += [bf(NA_W), bf(NA_W), ff(LRU_W)]
    if full:
        out_specs += [pl.BlockSpec((tm, COL_TILE), one), pl.BlockSpec((tm, COL_TILE), one)]
        out_shape += [ff(LRU_W), bf(FNET_W)]
    nbytes = (2 * tm * D_MODEL * 4 + tm * D_MODEL * 2 + 2 * D_MODEL * COL_TILE * 2
              + 2 * 6 * tm * COL_TILE * 4 + 3 * tm * COL_TILE * 4 + 4 * ROW_CHUNK * D_MODEL * 4)
    return pl.pallas_call(
        functools.partial(_inproj_ctx_kernel, tile_lo, full, full),
        grid=(1, tile_hi - tile_lo),
        in_specs=[
            pl.BlockSpec((tm, D_MODEL), lambda m, n: (m, 0)),
            pl.BlockSpec((None, None, 6, D_MODEL), lambda m, n: (layer, CTX_MOD_ROW, 0, 0)),
            pl.BlockSpec((None, D_MODEL, COL_TILE), lambda m, n: (layer, 0, n + tile_lo)),
        ],
        out_specs=out_specs,
        out_shape=out_shape,
        scratch_shapes=[pltpu.VMEM((tm, D_MODEL), BF16)],
        compiler_params=_params(("arbitrary", "arbitrary"), nbytes),
        name="inproj_context",
    )(c2d, mod4, w_in)


QBLK_ROWS = 4
KBLK_ROWS = 12
N_QBLK = GRID_H // QBLK_ROWS
QBLK = QBLK_ROWS * GRID_W
KBLK = KBLK_ROWS * GRID_W
BIAS_VARIANT_FIRST_ROW = (0, QBLK_ROWS, GRID_H - QBLK_ROWS)


def _kblk_start(first_query_row):
    return np.clip(first_query_row - WIN_H // 2, 0, GRID_H - KBLK_ROWS)


def _attn_kernel(q_ref, qr_ref, k_ref, v_ref, kc_ref, vc_ref, bias_ref, o_ref):
    kc = kc_ref[...]
    vc = vc_ref[...]

    def one_block(b):
        win_start = jnp.clip(b * QBLK_ROWS - WIN_H // 2, 0, GRID_H - KBLK_ROWS)
        variant = jnp.where(b == 0, 0, jnp.where(b == N_QBLK - 1, 2, 1))
        q0 = pl.multiple_of(b * QBLK, QBLK)
        k0 = pl.multiple_of(win_start * GRID_W, GRID_W)
        s = _dot_nt(qr_ref[pl.ds(q0, QBLK), :], k_ref[pl.ds(k0, KBLK), :]) + bias_ref[variant]
        sc = _dot_nt(q_ref[pl.ds(q0, QBLK), :], kc)
        m = jnp.maximum(jnp.max(s, axis=-1, keepdims=True), jnp.max(sc, axis=-1, keepdims=True))
        p = jnp.exp(s - m)
        pc = jnp.exp(sc - m)
        denom = jnp.sum(p, axis=-1, keepdims=True) + jnp.sum(pc, axis=-1, keepdims=True)
        o = _dot(p.astype(BF16), v_ref[pl.ds(k0, KBLK), :]) + _dot(pc.astype(BF16), vc)
        o_ref[pl.ds(q0, QBLK), :] = (o / denom).astype(o_ref.dtype)

    def body(i, carry):
        one_block(2 * i)
        one_block(2 * i + 1)
        return carry

    lax.fori_loop(0, N_QBLK // 2, body, 0)


def _attn_call(q, qr, k, v, kc, vc, bias):
    seq_blk = lambda b, h: (b, h)
    nbytes = (2 * (5 * SEQ * HEAD_DIM * 2 + 2 * CTX_LEN * HEAD_DIM * 2 + 3 * QBLK * KBLK * 4)
              + 6 * QBLK * (KBLK + CTX_LEN) * 4)
    return pl.pallas_call(
        _attn_kernel,
        grid=(BATCH, NA_HEADS),
        in_specs=[
            pl.BlockSpec((SEQ, HEAD_DIM), seq_blk),
            pl.BlockSpec((SEQ, HEAD_DIM), seq_blk),
            pl.BlockSpec((SEQ, HEAD_DIM), seq_blk),
            pl.BlockSpec((SEQ, HEAD_DIM), seq_blk),
            pl.BlockSpec((CTX_LEN, HEAD_DIM), seq_blk),
            pl.BlockSpec((CTX_LEN, HEAD_DIM), seq_blk),
            pl.BlockSpec((None, len(BIAS_VARIANT_FIRST_ROW), QBLK, KBLK), lambda b, h: (h, 0, 0, 0)),
        ],
        out_specs=pl.BlockSpec((SEQ, HEAD_DIM), seq_blk),
        out_shape=jax.ShapeDtypeStruct((BATCH * SEQ, NA_W), BF16),
        compiler_params=_params(("arbitrary", "arbitrary"), nbytes),
        name="neighbourhood_attention",
    )(q, qr, k, v, kc, vc, bias)


def _ctx_attn_kernel(q_ref, k_ref, v_ref, o_ref):
    s = _dot_nt(q_ref[...], k_ref[...])
    m = jnp.max(s, axis=-1, keepdims=True)
    p = jnp.exp(s - m)
    denom = jnp.sum(p, axis=-1, keepdims=True)
    o_ref[...] = (_dot(p.astype(BF16), v_ref[...]) / denom).astype(o_ref.dtype)


def _ctx_attn_call(q, k, v):
    blk = pl.BlockSpec((CTX_LEN, HEAD_DIM), lambda b, h: (b, h))
    return pl.pallas_call(
        _ctx_attn_kernel,
        grid=(BATCH, NA_HEADS),
        in_specs=[blk, blk, blk],
        out_specs=blk,
        out_shape=jax.ShapeDtypeStruct((BATCH * CTX_LEN, NA_W), BF16),
        compiler_params=_params(("arbitrary", "arbitrary"), 16 << 20),
        name="context_attention",
    )(q, k, v)


def _attn_bias_table(rpb_l):
    col = jnp.arange(GRID_W)
    col_start = jnp.clip(col - WIN_W // 2, 0, GRID_W - WIN_W)
    in_win = (col[None, :] >= col_start[:, None]) & (col[None, :] < col_start[:, None] + WIN_W)
    dc = jnp.clip(col[None, :] - col[:, None] + (WIN_W - 1), 0, 2 * WIN_W - 2)
    t = jnp.where(in_win[None, None], rpb_l[:, :, dc], NEG_INF)
    first = np.asarray(BIAS_VARIANT_FIRST_ROW)
    q_row = first[:, None] + np.arange(QBLK_ROWS)[None, :]
    k_row = _kblk_start(first)[:, None] + np.arange(KBLK_ROWS)[None, :]
    row_start = np.clip(q_row - WIN_H // 2, 0, GRID_H - WIN_H)
    valid = (k_row[:, None, :] >= row_start[:, :, None]) & (k_row[:, None, :] < row_start[:, :, None] + WIN_H)
    dr = np.clip(k_row[:, None, :] - q_row[:, :, None] + (WIN_H - 1), 0, 2 * WIN_H - 2)
    tv = jnp.where(valid[None, :, :, :, None, None], t[:, dr], NEG_INF)
    return tv.transpose(0, 1, 2, 4, 3, 5).reshape(NA_HEADS, len(first), QBLK, KBLK).astype(F32)


def _rope_tables():
    quarter = HEAD_DIM // 4
    inv = ROPE_THETA ** (-jnp.arange(quarter, dtype=F32) / quarter)
    t = jnp.arange(SEQ)
    ang_r = (t // GRID_W).astype(F32)[:, None] * inv
    ang_c = (t % GRID_W).astype(F32)[:, None] * inv
    cos = jnp.concatenate([jnp.cos(ang_r), jnp.cos(ang_r), jnp.cos(ang_c), jnp.cos(ang_c)], -1)
    sin = jnp.concatenate([-jnp.sin(ang_r), jnp.sin(ang_r), -jnp.sin(ang_c), jnp.sin(ang_c)], -1)
    return cos, sin


HALO = V7X_SUBLANES
LRU_CHUNK = 512


def _lru_coeffs(xp_ref, n_rows, cw_ref, cb_ref, w4, b4_ref, sp, a_refs, u_refs):
    chunk = min(LRU_CHUNK, n_rows)
    for c in range(n_rows // chunk):
        base = HALO + c * chunk
        xc = cb_ref[...] + xp_ref[base - CONV_W // 2:base - CONV_W // 2 + chunk, :] * cw_ref[0:1, :]
        for j in range(1, CONV_W):
            off = base - CONV_W // 2 + j
            xc = xc + xp_ref[off:off + chunk, :] * cw_ref[j:j + 1, :]
        z = _dot(xc.astype(BF16), w4) + b4_ref[...]
        for d in range(2):
            r = _sigmoid(z[:, (2 * d) * LRU_BW:(2 * d + 1) * LRU_BW])
            i = _sigmoid(z[:, (2 * d + 1) * LRU_BW:(2 * d + 2) * LRU_BW])
            log_a = (-LRU_C) * r * sp[d:d + 1, :]
            a = jnp.exp(log_a)
            a_refs[d][c * chunk:(c + 1) * chunk, :] = a
            one_minus_a2 = -jnp.tanh(log_a) * (a * a + 1.0)
            u_refs[d][c * chunk:(c + 1) * chunk, :] = jnp.sqrt(one_minus_a2) * (i * xc)


def _lru_scan(n_rows, hf, hb, af, uf, ab, ub, yf, yb):
    unroll = V7X_SUBLANES

    def body(i, carry):
        hf, hb = carry
        f0 = pl.multiple_of(i * unroll, unroll)
        b0 = pl.multiple_of(n_rows - unroll - i * unroll, unroll)
        for j in range(unroll):
            tf = f0 + j
            hf = af[pl.ds(tf, 1), :] * hf + uf[pl.ds(tf, 1), :]
            yf[pl.ds(tf, 1), :] = hf
            tb = b0 + (unroll - 1 - j)
            hb = ab[pl.ds(tb, 1), :] * hb + ub[pl.ds(tb, 1), :]
            yb[pl.ds(tb, 1), :] = hb
        return hf, hb

    return lax.fori_loop(0, n_rows // unroll, body, (hf, hb))


def _lru_kernel(ctx_out, x_ref, g_ref, xc_ref, *refs):
    refs = list(refs)
    gc_ref = refs.pop(0) if ctx_out else None
    cw_ref, cb_ref, w4_ref, b4_ref, lam_ref, o_ref = refs[:6]
    refs = refs[6:]
    oc_ref = refs.pop(0) if ctx_out else None
    xp_ref, af, uf, ab, ub, yf, yb = refs

    lam = lam_ref[...]
    z = -lam
    sp = jnp.maximum(z, 0.0) + jnp.log1p(jnp.exp(-jnp.abs(z)))
    w4 = w4_ref[...].astype(BF16)
    zeros_halo = jnp.zeros((HALO, LRU_BW), F32)
    h0 = jnp.zeros((1, LRU_BW), F32)

    xp_ref[0:HALO, :] = zeros_halo
    xp_ref[HALO:HALO + CTX_LEN, :] = xc_ref[...]
    xp_ref[HALO + CTX_LEN:2 * HALO + CTX_LEN, :] = zeros_halo
    _lru_coeffs(xp_ref, CTX_LEN, cw_ref, cb_ref, w4, b4_ref, sp, (af, ab), (uf, ub))
    hf, hb = _lru_scan(CTX_LEN, h0, h0, af, uf, ab, ub, yf, yb)
    if ctx_out:
        y = yf[0:CTX_LEN, :] + yb[0:CTX_LEN, :]
        oc_ref[...] = (y * _gelu_tanh(gc_ref[...])).astype(oc_ref.dtype)

    xp_ref[HALO:HALO + SEQ, :] = x_ref[...]
    xp_ref[HALO + SEQ:2 * HALO + SEQ, :] = zeros_halo
    _lru_coeffs(xp_ref, SEQ, cw_ref, cb_ref, w4, b4_ref, sp, (af, ab), (uf, ub))
    _lru_scan(SEQ, hf, hb, af, uf, ab, ub, yf, yb)
    for c in range(SEQ // LRU_CHUNK):
        sl = slice(c * LRU_CHUNK, (c + 1) * LRU_CHUNK)
        o_ref[sl, :] = ((yf[sl, :] + yb[sl, :]) * _gelu_tanh(g_ref[sl, :])).astype(o_ref.dtype)


def _lru_call(ctx_out, xl, gl, xc, gc, conv_w_l, conv_b_l, w4, b4, lam_l):
    lat = pl.BlockSpec((SEQ, LRU_BW), lambda b, j: (b, j))
    cx = pl.BlockSpec((CTX_LEN, LRU_BW), lambda b, j: (b, j))
    in_specs = [lat, lat, cx] + ([cx] if ctx_out else []) + [
        pl.BlockSpec((CONV_W, LRU_BW), lambda b, j: (0, j)),
        pl.BlockSpec((1, LRU_BW), lambda b, j: (0, j)),
        pl.BlockSpec((None, LRU_BW, 4 * LRU_BW), lambda b, j: (j, 0, 0)),
        pl.BlockSpec((None, 1, 4 * LRU_BW), lambda b, j: (j, 0, 0)),
        pl.BlockSpec((2, LRU_BW), lambda b, j: (0, j)),
    ]
    out_specs = [lat] + ([cx] if ctx_out else [])
    out_shape = [jax.ShapeDtypeStruct((BATCH * SEQ, LRU_W), BF16)]
    if ctx_out:
        out_shape.append(jax.ShapeDtypeStruct((BATCH * CTX_LEN, LRU_W), BF16))
    seq_bytes = SEQ * LRU_BW * 4
    args = [xl, gl, xc] + ([gc] if ctx_out else []) + [conv_w_l, conv_b_l, w4, b4, lam_l]
    res = pl.pallas_call(
        functools.partial(_lru_kernel, ctx_out),
        grid=(BATCH, LRU_BLOCKS),
        in_specs=in_specs,
        out_specs=out_specs,
        out_shape=out_shape,
        scratch_shapes=[pltpu.VMEM((SEQ + 2 * HALO, LRU_BW), F32)] + [pltpu.VMEM((SEQ, LRU_BW), F32)] * 6,
        compiler_params=_params(("arbitrary", "arbitrary"), 13 * seq_bytes + (8 << 20)),
        name="rglru",
    )(*args)
    return (res[0], res[1]) if ctx_out else (res[0], None)


def _fourier_kernel(norm, f_ref, cl_ref, sl_ref, cc_ref, sc_ref, w_ref, b_ref, o_ref, xc_ref, xs_ref):
    @pl.when(pl.program_id(1) == 0)
    def _():
        for g in range(FNET_GROUPS):
            sl = slice(g * FNET_GW, (g + 1) * FNET_GW)
            xg = f_ref[:, sl]
            xc_ref[:, sl] = _dot(xg, cc_ref[...]).astype(BF16)
            xs_ref[:, sl] = _dot(xg, sc_ref[...]).astype(BF16)

    y = (_dot(cl_ref[...], xc_ref[...]) - _dot(sl_ref[...], xs_ref[...])) * norm
    o_ref[...] = (_dot(y.astype(BF16), w_ref[...].astype(BF16)) + b_ref[...]).astype(o_ref.dtype)


DFT_SPLIT = 64


def _dft_matrices(n):
    t = np.arange(n, dtype=np.int64)

    def table(k):
        ang = (2.0 * np.pi / n) * ((k[:, None] * t[None, :]) % n).astype(np.float64)
        return jnp.asarray(np.cos(ang), F32), jnp.asarray(np.sin(ang), F32)

    if n <= DFT_SPLIT:
        c, s = table(t)
        return c.astype(BF16), s.astype(BF16)
    c1, s1 = table(DFT_SPLIT * np.arange(n // DFT_SPLIT, dtype=np.int64))
    c2, s2 = table(np.arange(DFT_SPLIT, dtype=np.int64))
    c = c1[:, None, :] * c2[None, :, :] - s1[:, None, :] * s2[None, :, :]
    s = s1[:, None, :] * c2[None, :, :] + c1[:, None, :] * s2[None, :, :]
    return c.reshape(n, n).astype(BF16), s.reshape(n, n).astype(BF16)


def _fourier_call(f2d, n_pos, fno_w_l, fno_b_l):
    tk = min(512, n_pos)
    steps = n_pos // tk
    cl, sl = _dft_matrices(n_pos)
    cc, sc = _dft_matrices(FNET_GW)
    norm = 1.0 / math.sqrt(n_pos * FNET_GW)
    nbytes = (2 * n_pos * FNET_W * 2 + 2 * 2 * tk * n_pos * 2 + 2 * n_pos * FNET_W * 2 + 2 * FNET_W * FNET_W * 4
              + 4 * tk * FNET_W * 4)
    return pl.pallas_call(
        functools.partial(_fourier_kernel, norm),
        grid=(BATCH, steps),
        in_specs=[
            pl.BlockSpec((n_pos, FNET_W), lambda b, k: (b, 0)),
            pl.BlockSpec((tk, n_pos), lambda b, k: (k, 0)),
            pl.BlockSpec((tk, n_pos), lambda b, k: (k, 0)),
            pl.BlockSpec((FNET_GW, FNET_GW), lambda b, k: (0, 0)),
            pl.BlockSpec((FNET_GW, FNET_GW), lambda b, k: (0, 0)),
            pl.BlockSpec((FNET_W, FNET_W), lambda b, k: (0, 0)),
            pl.BlockSpec((1, FNET_W), lambda b, k: (0, 0)),
        ],
        out_specs=pl.BlockSpec((tk, FNET_W), lambda b, k: (b * steps + k, 0)),
        out_shape=jax.ShapeDtypeStruct((BATCH * n_pos, FNET_W), BF16),
        scratch_shapes=[pltpu.VMEM((n_pos, FNET_W), BF16), pltpu.VMEM((n_pos, FNET_W), BF16)],
        compiler_params=_params(("arbitrary", "arbitrary"), nbytes),
        name="fourier_mix",
    )(f2d, cl, sl, cc, sc, fno_w_l, fno_b_l)


def _outproj_kernel(na_ref, lru_ref, f_ref, res_ref, mod_ref, w_ref, g_ref, b_ref, o_ref):
    gate = mod_ref[2:3, :]
    for rows in _row_tiles(res_ref.shape[0], MM_SUB):
        y = (_dot(na_ref[rows, :], w_ref[0:NA_W, :])
             + _dot(lru_ref[rows, :], w_ref[NA_W:NA_W + LRU_W, :])
             + _dot(f_ref[rows, :], w_ref[NA_W + LRU_W:D_MODEL, :]))
        _residual_ln_store(res_ref, y, gate, g_ref[...], b_ref[...], o_ref, rows)


def _outproj_call(na, lru, f, res, mod4, mod_row, w_out, layer, ln_g, ln_b):
    m_rows = res.shape[0]
    tm = 512
    row = lambda m: (m, 0)
    nbytes = (2 * tm * D_MODEL * 2 + 2 * 2 * tm * D_MODEL * 4 + D_MODEL * D_MODEL * 2 + 2 * MM_SUB * D_MODEL * 4
              + 4 * ROW_CHUNK * D_MODEL * 4)
    return pl.pallas_call(
        _outproj_kernel,
        grid=(m_rows // tm,),
        in_specs=[
            pl.BlockSpec((tm, NA_W), row),
            pl.BlockSpec((tm, LRU_W), row),
            pl.BlockSpec((tm, FNET_W), row),
            pl.BlockSpec((tm, D_MODEL), row),
            pl.BlockSpec((None, None, 6, D_MODEL), lambda m: (layer, mod_row(m * tm), 0, 0)),
            pl.BlockSpec((None, D_MODEL, D_MODEL), lambda m: (layer, 0, 0), pipeline_mode=pl.Buffered(1)),
            pl.BlockSpec((None, 1, D_MODEL), lambda m: (layer, 0, 0)),
            pl.BlockSpec((None, 1, D_MODEL), lambda m: (layer, 0, 0)),
        ],
        out_specs=pl.BlockSpec((tm, D_MODEL), row),
        out_shape=jax.ShapeDtypeStruct((m_rows, D_MODEL), F32),
        compiler_params=_params(("arbitrary",), nbytes),
        name="outproj_residual",
    )(na, lru, f, res, mod4, w_out, ln_g, ln_b)


MLP_TF = 512
MLP_TILES = D_FF // MLP_TF


def _mlp_kernel(x_ref, mod_ref, w1_ref, b1_ref, w2_ref, b2_ref, g_ref, b_ref, o_ref, v_ref, h0_ref, h1_ref):
    j = pl.program_id(1)
    tm = x_ref.shape[0]

    def up(h_out, rows=slice(None)):
        h = _dot(v_ref[rows, :], w1_ref[...]) + b1_ref[...]
        h_out[rows, :] = jnp.square(jnp.maximum(h, 0.0)).astype(BF16)

    def down(h_in, first):
        for c in range(D_MODEL // COL_TILE):
            sl = slice(c * COL_TILE, (c + 1) * COL_TILE)
            part = _dot(h_in[...], w2_ref[:, sl])
            if first:
                o_ref[:, sl] = part
            else:
                o_ref[:, sl] += part

    @pl.when(j == 0)
    def _():
        shift = mod_ref[3:4, :]
        scale1 = 1.0 + mod_ref[4:5, :]
        for rows in _row_tiles(tm, MM_SUB):
            _ln_mod_rows(x_ref, shift, scale1, v_ref, rows)
            up(h0_ref, rows)

    @pl.when(j == 1)
    def _():
        down(h0_ref, True)
        up(h1_ref)

    @pl.when((j > 1) & (j < MLP_TILES) & (j % 2 == 0))
    def _():
        down(h1_ref, False)
        up(h0_ref)

    @pl.when((j > 1) & (j < MLP_TILES) & (j % 2 == 1))
    def _():
        down(h0_ref, False)
        up(h1_ref)

    @pl.when(j == MLP_TILES)
    def _():
        h_last = h1_ref if (MLP_TILES - 1) % 2 else h0_ref
        gate = mod_ref[5:6, :]
        for rows in _row_tiles(tm, MM_SUB):
            y = o_ref[rows, :] + _dot(h_last[rows, :], w2_ref[...]) + b2_ref[...]
            _residual_ln_store(x_ref, y, gate, g_ref[...], b_ref[...], o_ref, rows)


def _mlp_call(x1, mod4, mod_row, tm, w1, b1, w2, b2, layer, ln_g, ln_b):
    m_rows = x1.shape[0]
    row = lambda m, j: (m, 0)
    vec = lambda m, j: (layer, 0, 0)
    nbytes = (3 * tm * D_MODEL * 4 + tm * D_MODEL * 2 + 2 * 2 * D_MODEL * MLP_TF * 2 + 2 * tm * MLP_TF * 2
              + tm * MLP_TF * 4 + tm * COL_TILE * 4 + 2 * MM_SUB * D_MODEL * 4 + 4 * ROW_CHUNK * D_MODEL * 4)
    return pl.pallas_call(
        _mlp_kernel,
        grid=(m_rows // tm, MLP_TILES + 1),
        in_specs=[
            pl.BlockSpec((tm, D_MODEL), row, pipeline_mode=pl.Buffered(1)),
            pl.BlockSpec((None, None, 6, D_MODEL), lambda m, j: (layer, mod_row(m * tm), 0, 0)),
            pl.BlockSpec((None, D_MODEL, MLP_TF), lambda m, j: (layer, 0, jnp.minimum(j, MLP_TILES - 1))),
            pl.BlockSpec((None, 1, MLP_TF), lambda m, j: (layer, 0, jnp.minimum(j, MLP_TILES - 1))),
            pl.BlockSpec((None, MLP_TF, D_MODEL), lambda m, j: (layer, jnp.maximum(j - 1, 0), 0)),
            pl.BlockSpec((None, 1, D_MODEL), vec),
            pl.BlockSpec((None, 1, D_MODEL), vec),
            pl.BlockSpec((None, 1, D_MODEL), vec),
        ],
        out_specs=pl.BlockSpec((tm, D_MODEL), row),
        out_shape=jax.ShapeDtypeStruct((m_rows, D_MODEL), F32),
        scratch_shapes=[pltpu.VMEM((tm, D_MODEL), BF16), pltpu.VMEM((tm, MLP_TF), BF16),
                        pltpu.VMEM((tm, MLP_TF), BF16)],
        compiler_params=_params(("arbitrary", "arbitrary"), nbytes),
        name="mlp_residual",
    )(x1, mod4, w1, b1, w2, b2, ln_g, ln_b)


def kernel(x, c, ctx, c_ctx, w_mod, b_mod, w_in, rpb, conv_w, conv_b, lru_wa, lru_ba, lru_wx, lru_bx, lru_lambda,
           fno_w, fno_b, w_out, ln1_g, ln1_b, w_fc1, b_fc1, w_fc2, b_fc2, ln2_g, ln2_b):
    xl = x.reshape(BATCH * SEQ, D_MODEL)
    xc = ctx.reshape(BATCH * CTX_LEN, D_MODEL)
    s_in = jnp.concatenate([c, c_ctx[None], jnp.zeros((MOD_ROWS - BATCH - 1, D_MODEL), F32)], 0)
    mod4 = _mod_call(s_in, w_mod, b_mod).reshape(DEPTH, MOD_ROWS, 6, D_MODEL)
    cos_t, sin_t = _rope_tables()
    w_in, w_out, w_fc1, w_fc2 = (w.astype(BF16) for w in (w_in, w_out, w_fc1, w_fc2))
    vec3 = lambda a: a.reshape(DEPTH, 1, a.shape[-1])
    ln1_g3, ln1_b3, ln2_g3, ln2_b3 = vec3(ln1_g), vec3(ln1_b), vec3(ln2_g), vec3(ln2_b)
    b_fc1_3, b_fc2_3 = vec3(b_fc1), vec3(b_fc2)
    lat_row = lambda r0: r0 // SEQ
    ctx_row = lambda r0: CTX_MOD_ROW

    for layer in range(DEPTH):
        ctx_out = layer < DEPTH - 1
        q, qr, k, v, xo, go, f = _inproj_lat_call(xl, mod4, w_in, layer, cos_t, sin_t)
        if ctx_out:
            qc, kc, vc, xoc, goc, fc = _inproj_ctx_call(xc, mod4, w_in, layer, True)
        else:
            kc, vc, xoc = _inproj_ctx_call(xc, mod4, w_in, layer, False)
            goc = None

        na = _attn_call(q, qr, k, v, kc, vc, _attn_bias_table(rpb[layer]))

        w4 = jnp.concatenate([lru_wa[layer, 0], lru_wx[layer, 0], lru_wa[layer, 1], lru_wx[layer, 1]], -1)
        blk = lambda a: a.reshape(LRU_BLOCKS, 1, LRU_BW)
        b4 = jnp.concatenate([blk(lru_ba[layer, 0]), blk(lru_bx[layer, 0]),
                              blk(lru_ba[layer, 1]), blk(lru_bx[layer, 1])], -1)
        lru, lru_c = _lru_call(ctx_out, xo, go, xoc, goc, conv_w[layer], conv_b[layer][None], w4, b4,
                               lru_lambda[layer])

        fm = _fourier_call(f, SEQ, fno_w[layer], fno_b[layer][None])
        x1 = _outproj_call(na, lru, fm, xl, mod4, lat_row, w_out, layer, ln1_g3, ln1_b3)
        xl = _mlp_call(x1, mod4, lat_row, 1024, w_fc1, b_fc1_3, w_fc2, b_fc2_3, layer, ln2_g3, ln2_b3)

        if ctx_out:
            na_c = _ctx_attn_call(qc, kc, vc)
            fm_c = _fourier_call(fc, CTX_LEN, fno_w[layer], fno_b[layer][None])
            c1 = _outproj_call(na_c, lru_c, fm_c, xc, mod4, ctx_row, w_out, layer, ln1_g3, ln1_b3)
            xc = _mlp_call(c1, mod4, ctx_row, 512, w_fc1, b_fc1_3, w_fc2, b_fc2_3, layer, ln2_g3, ln2_b3)

    return xl.reshape(BATCH, SEQ, D_MODEL)
```

```python
import functools
import math

import jax
import jax.numpy as jnp
import numpy as np
from jax import lax
from jax.experimental import pallas as pl
from jax.experimental.pallas import tpu as pltpu

F32 = jnp.float32
BF16 = jnp.bfloat16

D_MODEL = 2048
BATCH = 2
SEQ = 4096
DEPTH = 2
GRID_W = 64
GRID_H = SEQ // GRID_W
CTX_LEN = 256
HEAD_DIM = 128
NA_W = D_MODEL // 2
NA_HEADS = NA_W // HEAD_DIM
WIN_H = 8
WIN_W = 16
LRU_W = D_MODEL // 4
LRU_BLOCKS = 4
LRU_BW = LRU_W // LRU_BLOCKS
CONV_W = 4
LRU_C = 8.0
FNET_W = D_MODEL // 4
FNET_GROUPS = 4
FNET_GW = FNET_W // FNET_GROUPS
IN_W = 3 * NA_W + 2 * LRU_W + FNET_W
D_FF = 4 * D_MODEL
ROPE_THETA = 10000.0
LN_EPS = 1e-5
NEG_INF = -1e30
ALPHA = (2.0 * DEPTH) ** 0.25
ATTN_SCALE = HEAD_DIM ** -0.5

V7X_LANES = 128
V7X_SUBLANES = 8
V7X_VMEM_BYTES = 64 * 1024 * 1024
VMEM_CEILING = V7X_VMEM_BYTES - 6 * 1024 * 1024

COL_TILE = 512
N_COL_TILES = IN_W // COL_TILE
ROW_CHUNK = 128
MOD_ROWS = 8
CTX_MOD_ROW = BATCH


def _vmem_limit(nbytes):
    return int(min(VMEM_CEILING, nbytes * 5 // 4 + (4 << 20)))


def _params(semantics, nbytes):
    return pltpu.CompilerParams(dimension_semantics=semantics, vmem_limit_bytes=_vmem_limit(nbytes))


def _ln(x):
    mu = jnp.mean(x, axis=-1, keepdims=True)
    xc = x - mu
    var = jnp.mean(xc * xc, axis=-1, keepdims=True)
    return xc * lax.rsqrt(var + LN_EPS)


def _sigmoid(x):
    return 1.0 / (1.0 + jnp.exp(-x))


def _gelu_tanh(x):
    return 0.5 * x * (1.0 + jnp.tanh(math.sqrt(2.0 / math.pi) * (x + 0.044715 * (x * x * x))))


def _dot(a, b):
    return jnp.dot(a, b, preferred_element_type=F32)


def _dot_nt(a, b):
    return lax.dot_general(a, b, (((1,), (1,)), ((), ())), preferred_element_type=F32)


MOD_TN = 1024


def _mod_kernel(s_ref, w_ref, b_ref, o_ref):
    s = s_ref[...]
    s = s * _sigmoid(s)
    o_ref[...] = _dot(s.astype(BF16), w_ref[...].astype(BF16)) + b_ref[...]


def _mod_call(s_in, w_mod, b_mod):
    n_out = w_mod.shape[-1]
    nbytes = 2 * (D_MODEL * MOD_TN * 4) + D_MODEL * MOD_TN * 2 + 4 * MOD_ROWS * n_out
    return pl.pallas_call(
        _mod_kernel,
        grid=(DEPTH, n_out // MOD_TN),
        in_specs=[
            pl.BlockSpec((MOD_ROWS, D_MODEL), lambda l, n: (0, 0)),
            pl.BlockSpec((None, D_MODEL, MOD_TN), lambda l, n: (l, 0, n)),
            pl.BlockSpec((None, 1, MOD_TN), lambda l, n: (l, 0, n)),
        ],
        out_specs=pl.BlockSpec((None, MOD_ROWS, MOD_TN), lambda l, n: (l, 0, n)),
        out_shape=jax.ShapeDtypeStruct((DEPTH, MOD_ROWS, n_out), F32),
        compiler_params=_params(("arbitrary", "arbitrary"), nbytes),
        name="modulation",
    )(s_in, w_mod, b_mod.reshape(DEPTH, 1, n_out))


MM_SUB = 256


def _row_tiles(n_rows, size):
    size = min(size, n_rows)
    return [slice(r, r + size) for r in range(0, n_rows, size)]


def _ln_mod_rows(x_ref, shift, scale1, dst_ref, rows):
    for piece in _row_tiles(rows.stop - rows.start, ROW_CHUNK):
        sl = slice(rows.start + piece.start, rows.start + piece.stop)
        dst_ref[sl, :] = (_ln(x_ref[sl, :]) * scale1 + shift).astype(dst_ref.dtype)


def _residual_ln_store(res, y, gate, gain, bias, o_ref, rows):
    for piece in _row_tiles(rows.stop - rows.start, ROW_CHUNK):
        sl = slice(rows.start + piece.start, rows.start + piece.stop)
        z = ALPHA * res[sl, :] + gate * y[piece, :]
        o_ref[sl, :] = _ln(z) * gain + bias


def _rope(a, cos, sin):
    lane = lax.broadcasted_iota(jnp.int32, a.shape, 1)
    first = (lane % (HEAD_DIM // 2)) < (HEAD_DIM // 4)
    partner = jnp.where(first, pltpu.roll(a, HEAD_DIM - HEAD_DIM // 4, 1), pltpu.roll(a, HEAD_DIM // 4, 1))
    return a * cos + partner * sin


def _inproj_lat_kernel(x_ref, mod_ref, w_ref, cos_ref, sin_ref,
                       q_ref, qr_ref, k_ref, v_ref, xo_ref, go_ref, f_ref, xn_ref):
    shift = mod_ref[0:1, :]
    scale1 = 1.0 + mod_ref[1:2, :]

    def emit_q(rows, cols, acc):
        q_ref[rows, cols] = (acc * ATTN_SCALE).astype(q_ref.dtype)
        for h in range(COL_TILE // HEAD_DIM):
            sl = slice(h * HEAD_DIM, (h + 1) * HEAD_DIM)
            dst = slice(cols.start + sl.start, cols.start + sl.stop)
            rot = _rope(acc[:, sl], cos_ref[rows, :], sin_ref[rows, :])
            qr_ref[rows, dst] = (rot * ATTN_SCALE).astype(qr_ref.dtype)

    def emit_k(rows, cols, acc):
        for h in range(COL_TILE // HEAD_DIM):
            sl = slice(h * HEAD_DIM, (h + 1) * HEAD_DIM)
            dst = slice(cols.start + sl.start, cols.start + sl.stop)
            k_ref[rows, dst] = _rope(acc[:, sl], cos_ref[rows, :], sin_ref[rows, :]).astype(k_ref.dtype)

    def emit_to(ref):
        def emit(rows, cols, acc):
            ref[rows, cols] = acc.astype(ref.dtype)
        return emit

    half = [slice(0, COL_TILE), slice(COL_TILE, 2 * COL_TILE)]
    plan = ([(emit_q, c) for c in half] + [(emit_k, c) for c in half] + [(emit_to(v_ref), c) for c in half]
            + [(emit_to(xo_ref), half[0]), (emit_to(go_ref), half[0]), (emit_to(f_ref), half[0])])
    for rows in _row_tiles(x_ref.shape[0], MM_SUB):
        _ln_mod_rows(x_ref, shift, scale1, xn_ref, rows)
        for n, (emit, cols) in enumerate(plan):
            emit(rows, cols, _dot(xn_ref[rows, :], w_ref[:, n * COL_TILE:(n + 1) * COL_TILE]))


def _inproj_lat_call(x2d, mod4, w_in, layer, cos_t, sin_t):
    m_rows = x2d.shape[0]
    tm = 512
    tiles_per_seq = SEQ // tm
    row = lambda m: (m, 0)
    nbytes = (2 * tm * D_MODEL * 4 + tm * D_MODEL * 2 + D_MODEL * IN_W * 2 + 4 * tm * HEAD_DIM * 4
              + 2 * tm * (4 * NA_W * 2 + 2 * LRU_W * 4 + FNET_W * 2) + 6 * MM_SUB * COL_TILE * 4
              + 4 * ROW_CHUNK * D_MODEL * 4)
    bf = lambda w: jax.ShapeDtypeStruct((m_rows, w), BF16)
    ff = lambda w: jax.ShapeDtypeStruct((m_rows, w), F32)
    widths = [NA_W, NA_W, NA_W, NA_W, LRU_W, LRU_W, FNET_W]
    return pl.pallas_call(
        _inproj_lat_kernel,
        grid=(m_rows // tm,),
        in_specs=[
            pl.BlockSpec((tm, D_MODEL), row),
            pl.BlockSpec((None, None, 6, D_MODEL), lambda m: (layer, m // tiles_per_seq, 0, 0)),
            pl.BlockSpec((None, D_MODEL, IN_W), lambda m: (layer, 0, 0), pipeline_mode=pl.Buffered(1)),
            pl.BlockSpec((tm, HEAD_DIM), lambda m: (m % tiles_per_seq, 0)),
            pl.BlockSpec((tm, HEAD_DIM), lambda m: (m % tiles_per_seq, 0)),
        ],
        out_specs=[pl.BlockSpec((tm, w), row) for w in widths],
        out_shape=[bf(NA_W), bf(NA_W), bf(NA_W), bf(NA_W), ff(LRU_W), ff(LRU_W), bf(FNET_W)],
        scratch_shapes=[pltpu.VMEM((tm, D_MODEL), BF16)],
        compiler_params=_params(("arbitrary",), nbytes),
        name="inproj_latent",
    )(x2d, mod4, w_in, cos_t, sin_t)


def _inproj_ctx_kernel(tile_lo, with_q, with_gf, x_ref, mod_ref, w_ref, *refs):
    refs = list(refs)
    xn_ref = refs.pop()
    q_ref = refs.pop(0) if with_q else None
    k_ref, v_ref, xo_ref = refs[0], refs[1], refs[2]
    go_ref, f_ref = (refs[3], refs[4]) if with_gf else (None, None)
    n = pl.program_id(1) + tile_lo

    @pl.when(pl.program_id(1) == 0)
    def _():
        for rows in _row_tiles(x_ref.shape[0], MM_SUB):
            _ln_mod_rows(x_ref, mod_ref[0:1, :], 1.0 + mod_ref[1:2, :], xn_ref, rows)

    acc = _dot(xn_ref[...], w_ref[...])

    if with_q:
        @pl.when(n < 2)
        def _():
            q_ref[...] = (acc * ATTN_SCALE).astype(q_ref.dtype)

    @pl.when((n >= 2) & (n < 4))
    def _():
        k_ref[...] = acc.astype(k_ref.dtype)

    @pl.when((n >= 4) & (n < 6))
    def _():
        v_ref[...] = acc.astype(v_ref.dtype)

    @pl.when(n == 6)
    def _():
        xo_ref[...] = acc

    if with_gf:
        @pl.when(n == 7)
        def _():
            go_ref[...] = acc

        @pl.when(n == 8)
        def _():
            f_ref[...] = acc.astype(f_ref.dtype)


def _inproj_ctx_call(c2d, mod4, w_in, layer, full):
    m_rows = c2d.shape[0]
    tm = m_rows
    tile_lo, tile_hi = (0, N_COL_TILES) if full else (2, 7)

    def col(lo):
        return lambda m, n: (m, jnp.clip(n + tile_lo - lo, 0, 1))

    one = lambda m, n: (m, 0)
    bf = lambda w: jax.ShapeDtypeStruct((m_rows, w), BF16)
    ff = lambda w: jax.ShapeDtypeStruct((m_rows, w), F32)
    out_specs, out_shape = [], []
    if full:
        out_specs.append(pl.BlockSpec((tm, COL_TILE), col(0)))
        out_shape.append(bf(NA_W))
    out_specs += [pl.BlockSpec((tm, COL_TILE), col(2)), pl.BlockSpec((tm, COL_TILE), col(4)),
                  pl.BlockSpec((tm, COL_TILE), one)]
    out_shape += [bf(NA_W), bf(NA_W), ff(LRU_W)]
    if full:
        out_specs += [pl.BlockSpec((tm, COL_TILE), one), pl.BlockSpec((tm, COL_TILE), one)]
        out_shape += [ff(LRU_W), bf(FNET_W)]
    nbytes = (2 * tm * D_MODEL * 4 + tm * D_MODEL * 2 + 2 * D_MODEL * COL_TILE * 2
              + 2 * 6 * tm * COL_TILE * 4 + 3 * tm * COL_TILE * 4 + 4 * ROW_CHUNK * D_MODEL * 4)
    return pl.pallas_call(
        functools.partial(_inproj_ctx_kernel, tile_lo, full, full),
        grid=(1, tile_hi - tile_lo),
        in_specs=[
            pl.BlockSpec((tm, D_MODEL), lambda m, n: (m, 0)),
            pl.BlockSpec((None, None, 6, D_MODEL), lambda m, n: (layer, CTX_MOD_ROW, 0, 0)),
            pl.BlockSpec((None, D_MODEL, COL_TILE), lambda m, n: (layer, 0, n + tile_lo)),
        ],
        out_specs=out_specs,
        out_shape=out_shape,
        scratch_shapes=[pltpu.VMEM((tm, D_MODEL), BF16)],
        compiler_params=_params(("arbitrary", "arbitrary"), nbytes),
        name="inproj_context",
    )(c2d, mod4, w_in)


QBLK_ROWS = 4
KBLK_ROWS = 12
N_QBLK = GRID_H // QBLK_ROWS
QBLK = QBLK_ROWS * GRID_W
KBLK = KBLK_ROWS * GRID_W
BIAS_VARIANT_FIRST_ROW = (0, QBLK_ROWS, GRID_H - QBLK_ROWS)


def _kblk_start(first_query_row):
    return np.clip(first_query_row - WIN_H // 2, 0, GRID_H - KBLK_ROWS)


def _attn_kernel(q_ref, qr_ref, k_ref, v_ref, kc_ref, vc_ref, bias_ref, o_ref):
    kc = kc_ref[...]
    vc = vc_ref[...]

    def one_block(b):
        win_start = jnp.clip(b * QBLK_ROWS - WIN_H // 2, 0, GRID_H - KBLK_ROWS)
        variant = jnp.where(b == 0, 0, jnp.where(b == N_QBLK - 1, 2, 1))
        q0 = pl.multiple_of(b * QBLK, QBLK)
        k0 = pl.multiple_of(win_start * GRID_W, GRID_W)
        s = _dot_nt(qr_ref[pl.ds(q0, QBLK), :], k_ref[pl.ds(k0, KBLK), :]) + bias_ref[variant]
        sc = _dot_nt(q_ref[pl.ds(q0, QBLK), :], kc)
        m = jnp.maximum(jnp.max(s, axis=-1, keepdims=True), jnp.max(sc, axis=-1, keepdims=True))
        p = jnp.exp(s - m)
        pc = jnp.exp(sc - m)
        denom = jnp.sum(p, axis=-1, keepdims=True) + jnp.sum(pc, axis=-1, keepdims=True)
        o = _dot(p.astype(BF16), v_ref[pl.ds(k0, KBLK), :]) + _dot(pc.astype(BF16), vc)
        o_ref[pl.ds(q0, QBLK), :] = (o / denom).astype(o_ref.dtype)

    def body(i, carry):
        one_block(2 * i)
        one_block(2 * i + 1)
        return carry

    lax.fori_loop(0, N_QBLK // 2, body, 0)


def _attn_call(q, qr, k, v, kc, vc, bias):
    seq_blk = lambda b, h: (b, h)
    nbytes = (2 * (5 * SEQ * HEAD_DIM * 2 + 2 * CTX_LEN * HEAD_DIM * 2 + 3 * QBLK * KBLK * 4)
              + 6 * QBLK * (KBLK + CTX_LEN) * 4)
    return pl.pallas_call(
        _attn_kernel,
        grid=(BATCH, NA_HEADS),
        in_specs=[
            pl.BlockSpec((SEQ, HEAD_DIM), seq_blk),
            pl.BlockSpec((SEQ, HEAD_DIM), seq_blk),
            pl.BlockSpec((SEQ, HEAD_DIM), seq_blk),
            pl.BlockSpec((SEQ, HEAD_DIM), seq_blk),
            pl.BlockSpec((CTX_LEN, HEAD_DIM), seq_blk),
            pl.BlockSpec((CTX_LEN, HEAD_DIM), seq_blk),
            pl.BlockSpec((None, len(BIAS_VARIANT_FIRST_ROW), QBLK, KBLK), lambda b, h: (h, 0, 0, 0)),
        ],
        out_specs=pl.BlockSpec((SEQ, HEAD_DIM), seq_blk),
        out_shape=jax.ShapeDtypeStruct((BATCH * SEQ, NA_W), BF16),
        compiler_params=_params(("arbitrary", "arbitrary"), nbytes),
        name="neighbourhood_attention",
    )(q, qr, k, v, kc, vc, bias)


def _ctx_attn_kernel(q_ref, k_ref, v_ref, o_ref):
    s = _dot_nt(q_ref[...], k_ref[...])
    m = jnp.max(s, axis=-1, keepdims=True)
    p = jnp.exp(s - m)
    denom = jnp.sum(p, axis=-1, keepdims=True)
    o_ref[...] = (_dot(p.astype(BF16), v_ref[...]) / denom).astype(o_ref.dtype)


def _ctx_attn_call(q, k, v):
    blk = pl.BlockSpec((CTX_LEN, HEAD_DIM), lambda b, h: (b, h))
    return pl.pallas_call(
        _ctx_attn_kernel,
        grid=(BATCH, NA_HEADS),
        in_specs=[blk, blk, blk],
        out_specs=blk,
        out_shape=jax.ShapeDtypeStruct((BATCH * CTX_LEN, NA_W), BF16),
        compiler_params=_params(("arbitrary", "arbitrary"), 16 << 20),
        name="context_attention",
    )(q, k, v)


def _attn_bias_table(rpb_l):
    col = np.arange(GRID_W)
    col_start = np.clip(col - WIN_W // 2, 0, GRID_W - WIN_W)
    in_win = (col[None, :] >= col_start[:, None]) & (col[None, :] < col_start[:, None] + WIN_W)
    dc = np.clip(col[None, :] - col[:, None] + (WIN_W - 1), 0, 2 * WIN_W - 2)
    t = jnp.full((NA_HEADS, 2 * WIN_H - 1, GRID_W, GRID_W), NEG_INF, F32)
    for j in range(2 * WIN_W - 1):
        t = jnp.where((in_win & (dc == j))[None, None], rpb_l[:, :, j][:, :, None, None], t)
    neg = jnp.full((NA_HEADS, GRID_W, GRID_W), NEG_INF, F32)
    variants = []
    for first in BIAS_VARIANT_FIRST_ROW:
        rows = []
        for u in range(QBLK_ROWS):
            q_row = first + u
            row_start = int(np.clip(q_row - WIN_H // 2, 0, GRID_H - WIN_H))
            tiles = []
            for j in range(KBLK_ROWS):
                k_row = int(_kblk_start(first)) + j
                inside = row_start <= k_row < row_start + WIN_H
                tiles.append(t[:, k_row - q_row + (WIN_H - 1)] if inside else neg)
            rows.append(jnp.concatenate(tiles, axis=-1))
        variants.append(jnp.concatenate(rows, axis=-2))
    return jnp.stack(variants, axis=1)


def _rope_tables():
    quarter = HEAD_DIM // 4
    inv = ROPE_THETA ** (-jnp.arange(quarter, dtype=F32) / quarter)
    t = jnp.arange(SEQ)
    ang_r = (t // GRID_W).astype(F32)[:, None] * inv
    ang_c = (t % GRID_W).astype(F32)[:, None] * inv
    cos = jnp.concatenate([jnp.cos(ang_r), jnp.cos(ang_r), jnp.cos(ang_c), jnp.cos(ang_c)], -1)
    sin = jnp.concatenate([-jnp.sin(ang_r), jnp.sin(ang_r), -jnp.sin(ang_c), jnp.sin(ang_c)], -1)
    return cos, sin


HALO = V7X_SUBLANES
N_SEG = V7X_SUBLANES
SEG_PAD = V7X_SUBLANES
LRU_SCAN_UNROLL = 8


def _sigmoid_tanh(x):
    return 0.5 * jnp.tanh(0.5 * x) + 0.5


def _lru_coeffs(xp_ref, n_rows, cw_ref, cb_ref, w4, b4_ref, sp, a_refs, u_refs):
    seg = n_rows // N_SEG
    pitch = seg + SEG_PAD
    for s in range(N_SEG):
        base = HALO + s * seg
        xc = cb_ref[...] + xp_ref[base - CONV_W // 2:base - CONV_W // 2 + seg, :] * cw_ref[0:1, :]
        for j in range(1, CONV_W):
            off = base - CONV_W // 2 + j
            xc = xc + xp_ref[off:off + seg, :] * cw_ref[j:j + 1, :]
        z = _dot(xc.astype(BF16), w4) + b4_ref[...]
        for d in range(2):
            r = _sigmoid_tanh(z[:, (2 * d) * LRU_BW:(2 * d + 1) * LRU_BW])
            i = _sigmoid_tanh(z[:, (2 * d + 1) * LRU_BW:(2 * d + 2) * LRU_BW])
            log_a = (-LRU_C) * r * sp[d:d + 1, :]
            a = jnp.exp(log_a)
            a_refs[d][s * pitch:s * pitch + seg, :] = a
            one_minus_a2 = -jnp.tanh(log_a) * (a * a + 1.0)
            u_refs[d][s * pitch:s * pitch + seg, :] = jnp.sqrt(one_minus_a2) * (i * xc)


def _lru_local_scan(n_rows, coef_f, coef_b, state_f, state_b):
    seg = n_rows // N_SEG
    pitch = seg + SEG_PAD
    zero = jnp.zeros((N_SEG, LRU_BW), F32)
    one = jnp.ones((N_SEG, LRU_BW), F32)

    def step(coef, state, row, h, p):
        rows = pl.ds(row, N_SEG, stride=pitch)
        a = coef[0][rows, :]
        h = a * h + coef[1][rows, :]
        p = p * a
        state[0][rows, :] = p
        state[1][rows, :] = h
        return h, p

    def body(i, carry):
        hf, pf, hb, pb = carry
        for j in range(LRU_SCAN_UNROLL):
            t = i * LRU_SCAN_UNROLL + j
            hf, pf = step(coef_f, state_f, t, hf, pf)
            hb, pb = step(coef_b, state_b, seg - 1 - t, hb, pb)
        return hf, pf, hb, pb

    lax.fori_loop(0, seg // LRU_SCAN_UNROLL, body, (zero, one, zero, one))


def _lru_carries(n_rows, h_in_f, h_in_b, af, uf, ab, ub):
    seg = n_rows // N_SEG
    pitch = seg + SEG_PAD
    cf, cb = [h_in_f], [h_in_b]
    for s in range(N_SEG):
        last = s * pitch + seg - 1
        cf.append(uf[last:last + 1, :] + af[last:last + 1, :] * cf[-1])
        first = (N_SEG - 1 - s) * pitch
        cb.append(ub[first:first + 1, :] + ab[first:first + 1, :] * cb[-1])
    return cf[:N_SEG], cb[:N_SEG][::-1], cf[N_SEG], cb[N_SEG]


def _lru_emit(n_rows, cf, cb, af, uf, ab, ub, g_ref, o_ref):
    seg = n_rows // N_SEG
    pitch = seg + SEG_PAD
    for s in range(N_SEG):
        src = slice(s * pitch, s * pitch + seg)
        dst = slice(s * seg, (s + 1) * seg)
        y = (uf[src, :] + af[src, :] * cf[s]) + (ub[src, :] + ab[src, :] * cb[s])
        o_ref[dst, :] = (y * _gelu_tanh(g_ref[dst, :])).astype(o_ref.dtype)


def _lru_kernel(ctx_out, x_ref, g_ref, xc_ref, *refs):
    refs = list(refs)
    gc_ref = refs.pop(0) if ctx_out else None
    cw_ref, cb_ref, w4_ref, b4_ref, lam_ref, o_ref = refs[:6]
    refs = refs[6:]
    oc_ref = refs.pop(0) if ctx_out else None
    xp_ref, af, uf, ab, ub, pf, sf, pb, sb = refs

    lam = lam_ref[...]
    z = -lam
    sp = jnp.maximum(z, 0.0) + jnp.log1p(jnp.exp(-jnp.abs(z)))
    w4 = w4_ref[...].astype(BF16)
    zeros_halo = jnp.zeros((HALO, LRU_BW), F32)
    h0 = jnp.zeros((1, LRU_BW), F32)

    xp_ref[0:HALO, :] = zeros_halo
    xp_ref[HALO:HALO + CTX_LEN, :] = xc_ref[...]
    xp_ref[HALO + CTX_LEN:2 * HALO + CTX_LEN, :] = zeros_halo
    _lru_coeffs(xp_ref, CTX_LEN, cw_ref, cb_ref, w4, b4_ref, sp, (af, ab), (uf, ub))
    _lru_local_scan(CTX_LEN, (af, uf), (ab, ub), (pf, sf), (pb, sb))
    cf, cb, hf, hb = _lru_carries(CTX_LEN, h0, h0, pf, sf, pb, sb)
    if ctx_out:
        _lru_emit(CTX_LEN, cf, cb, pf, sf, pb, sb, gc_ref, oc_ref)

    xp_ref[HALO:HALO + SEQ, :] = x_ref[...]
    xp_ref[HALO + SEQ:2 * HALO + SEQ, :] = zeros_halo
    _lru_coeffs(xp_ref, SEQ, cw_ref, cb_ref, w4, b4_ref, sp, (af, ab), (uf, ub))
    _lru_local_scan(SEQ, (af, uf), (ab, ub), (pf, sf), (pb, sb))
    cf, cb, _, _ = _lru_carries(SEQ, hf, hb, pf, sf, pb, sb)
    _lru_emit(SEQ, cf, cb, pf, sf, pb, sb, g_ref, o_ref)


def _lru_call(ctx_out, xl, gl, xc, gc, conv_w_l, conv_b_l, w4, b4, lam_l):
    lat = pl.BlockSpec((SEQ, LRU_BW), lambda b, j: (b, j))
    cx = pl.BlockSpec((CTX_LEN, LRU_BW), lambda b, j: (b, j))
    in_specs = [lat, lat, cx] + ([cx] if ctx_out else []) + [
        pl.BlockSpec((CONV_W, LRU_BW), lambda b, j: (0, j)),
        pl.BlockSpec((1, LRU_BW), lambda b, j: (0, j)),
        pl.BlockSpec((None, LRU_BW, 4 * LRU_BW), lambda b, j: (j, 0, 0)),
        pl.BlockSpec((None, 1, 4 * LRU_BW), lambda b, j: (j, 0, 0)),
        pl.BlockSpec((2, LRU_BW), lambda b, j: (0, j)),
    ]
    out_specs = [lat] + ([cx] if ctx_out else [])
    out_shape = [jax.ShapeDtypeStruct((BATCH * SEQ, LRU_W), BF16)]
    if ctx_out:
        out_shape.append(jax.ShapeDtypeStruct((BATCH * CTX_LEN, LRU_W), BF16))
    seq_bytes = SEQ * LRU_BW * 4
    args = [xl, gl, xc] + ([gc] if ctx_out else []) + [conv_w_l, conv_b_l, w4, b4, lam_l]
    res = pl.pallas_call(
        functools.partial(_lru_kernel, ctx_out),
        grid=(BATCH, LRU_BLOCKS),
        in_specs=in_specs,
        out_specs=out_specs,
        out_shape=out_shape,
        scratch_shapes=([pltpu.VMEM((SEQ + 2 * HALO, LRU_BW), F32)]
                        + [pltpu.VMEM((SEQ + N_SEG * SEG_PAD, LRU_BW), F32)] * 8),
        compiler_params=_params(("arbitrary", "arbitrary"), 15 * seq_bytes + (8 << 20)),
        name="rglru",
    )(*args)
    return (res[0], res[1]) if ctx_out else (res[0], None)


def _fourier_kernel(norm, f_ref, cl_ref, sl_ref, cc_ref, sc_ref, w_ref, b_ref, o_ref, xc_ref, xs_ref):
    @pl.when(pl.program_id(1) == 0)
    def _():
        for g in range(FNET_GROUPS):
            sl = slice(g * FNET_GW, (g + 1) * FNET_GW)
            xg = f_ref[:, sl]
            xc_ref[:, sl] = _dot(xg, cc_ref[...]).astype(BF16)
            xs_ref[:, sl] = _dot(xg, sc_ref[...]).astype(BF16)

    y = (_dot(cl_ref[...], xc_ref[...]) - _dot(sl_ref[...], xs_ref[...])) * norm
    o_ref[...] = (_dot(y.astype(BF16), w_ref[...].astype(BF16)) + b_ref[...]).astype(o_ref.dtype)


DFT_SPLIT = 64


def _dft_matrices(n):
    t = np.arange(n, dtype=np.int64)

    def table(k):
        ang = (2.0 * np.pi / n) * ((k[:, None] * t[None, :]) % n).astype(np.float64)
        return jnp.asarray(np.cos(ang), F32), jnp.asarray(np.sin(ang), F32)

    if n <= DFT_SPLIT:
        c, s = table(t)
        return c.astype(BF16), s.astype(BF16)
    c1, s1 = table(DFT_SPLIT * np.arange(n // DFT_SPLIT, dtype=np.int64))
    c2, s2 = table(np.arange(DFT_SPLIT, dtype=np.int64))
    c = c1[:, None, :] * c2[None, :, :] - s1[:, None, :] * s2[None, :, :]
    s = s1[:, None, :] * c2[None, :, :] + c1[:, None, :] * s2[None, :, :]
    return c.reshape(n, n).astype(BF16), s.reshape(n, n).astype(BF16)


def _fourier_call(f2d, n_pos, fno_w_l, fno_b_l):
    tk = min(512, n_pos)
    steps = n_pos // tk
    cl, sl = _dft_matrices(n_pos)
    cc, sc = _dft_matrices(FNET_GW)
    norm = 1.0 / math.sqrt(n_pos * FNET_GW)
    nbytes = (2 * n_pos * FNET_W * 2 + 2 * 2 * tk * n_pos * 2 + 2 * n_pos * FNET_W * 2 + 2 * FNET_W * FNET_W * 4
              + 4 * tk * FNET_W * 4)
    return pl.pallas_call(
        functools.partial(_fourier_kernel, norm),
        grid=(BATCH, steps),
        in_specs=[
            pl.BlockSpec((n_pos, FNET_W), lambda b, k: (b, 0)),
            pl.BlockSpec((tk, n_pos), lambda b, k: (k, 0)),
            pl.BlockSpec((tk, n_pos), lambda b, k: (k, 0)),
            pl.BlockSpec((FNET_GW, FNET_GW), lambda b, k: (0, 0)),
            pl.BlockSpec((FNET_GW, FNET_GW), lambda b, k: (0, 0)),
            pl.BlockSpec((FNET_W, FNET_W), lambda b, k: (0, 0)),
            pl.BlockSpec((1, FNET_W), lambda b, k: (0, 0)),
        ],
        out_specs=pl.BlockSpec((tk, FNET_W), lambda b, k: (b * steps + k, 0)),
        out_shape=jax.ShapeDtypeStruct((BATCH * n_pos, FNET_W), BF16),
        scratch_shapes=[pltpu.VMEM((n_pos, FNET_W), BF16), pltpu.VMEM((n_pos, FNET_W), BF16)],
        compiler_params=_params(("arbitrary", "arbitrary"), nbytes),
        name="fourier_mix",
    )(f2d, cl, sl, cc, sc, fno_w_l, fno_b_l)


def _outproj_kernel(na_ref, lru_ref, f_ref, res_ref, mod_ref, w_ref, g_ref, b_ref, o_ref):
    gate = mod_ref[2:3, :]
    for rows in _row_tiles(res_ref.shape[0], MM_SUB):
        y = (_dot(na_ref[rows, :], w_ref[0:NA_W, :])
             + _dot(lru_ref[rows, :], w_ref[NA_W:NA_W + LRU_W, :])
             + _dot(f_ref[rows, :], w_ref[NA_W + LRU_W:D_MODEL, :]))
        _residual_ln_store(res_ref, y, gate, g_ref[...], b_ref[...], o_ref, rows)


def _outproj_call(na, lru, f, res, mod4, mod_row, w_out, layer, ln_g, ln_b):
    m_rows = res.shape[0]
    tm = 512
    row = lambda m: (m, 0)
    nbytes = (2 * tm * D_MODEL * 2 + 2 * 2 * tm * D_MODEL * 4 + D_MODEL * D_MODEL * 2 + 2 * MM_SUB * D_MODEL * 4
              + 4 * ROW_CHUNK * D_MODEL * 4)
    return pl.pallas_call(
        _outproj_kernel,
        grid=(m_rows // tm,),
        in_specs=[
            pl.BlockSpec((tm, NA_W), row),
            pl.BlockSpec((tm, LRU_W), row),
            pl.BlockSpec((tm, FNET_W), row),
            pl.BlockSpec((tm, D_MODEL), row),
            pl.BlockSpec((None, None, 6, D_MODEL), lambda m: (layer, mod_row(m * tm), 0, 0)),
            pl.BlockSpec((None, D_MODEL, D_MODEL), lambda m: (layer, 0, 0), pipeline_mode=pl.Buffered(1)),
            pl.BlockSpec((None, 1, D_MODEL), lambda m: (layer, 0, 0)),
            pl.BlockSpec((None, 1, D_MODEL), lambda m: (layer, 0, 0)),
        ],
        out_specs=pl.BlockSpec((tm, D_MODEL), row),
        out_shape=jax.ShapeDtypeStruct((m_rows, D_MODEL), F32),
        compiler_params=_params(("arbitrary",), nbytes),
        name="outproj_residual",
    )(na, lru, f, res, mod4, w_out, ln_g, ln_b)


MLP_TF = 512
MLP_TILES = D_FF // MLP_TF


def _mlp_kernel(x_ref, mod_ref, w1_ref, b1_ref, w2_ref, b2_ref, g_ref, b_ref, o_ref, v_ref, h0_ref, h1_ref):
    j = pl.program_id(1)
    tm = x_ref.shape[0]

    def up(h_out, rows=slice(None)):
        h = _dot(v_ref[rows, :], w1_ref[...]) + b1_ref[...]
        h_out[rows, :] = jnp.square(jnp.maximum(h, 0.0)).astype(BF16)

    def down(h_in, first):
        for c in range(D_MODEL // COL_TILE):
            sl = slice(c * COL_TILE, (c + 1) * COL_TILE)
            part = _dot(h_in[...], w2_ref[:, sl])
            if first:
                o_ref[:, sl] = part
            else:
                o_ref[:, sl] += part

    @pl.when(j == 0)
    def _():
        shift = mod_ref[3:4, :]
        scale1 = 1.0 + mod_ref[4:5, :]
        for rows in _row_tiles(tm, MM_SUB):
            _ln_mod_rows(x_ref, shift, scale1, v_ref, rows)
            up(h0_ref, rows)

    @pl.when(j == 1)
    def _():
        down(h0_ref, True)
        up(h1_ref)

    @pl.when((j > 1) & (j < MLP_TILES) & (j % 2 == 0))
    def _():
        down(h1_ref, False)
        up(h0_ref)

    @pl.when((j > 1) & (j < MLP_TILES) & (j % 2 == 1))
    def _():
        down(h0_ref, False)
        up(h1_ref)

    @pl.when(j == MLP_TILES)
    def _():
        h_last = h1_ref if (MLP_TILES - 1) % 2 else h0_ref
        gate = mod_ref[5:6, :]
        for rows in _row_tiles(tm, MM_SUB):
            y = o_ref[rows, :] + _dot(h_last[rows, :], w2_ref[...]) + b2_ref[...]
            _residual_ln_store(x_ref, y, gate, g_ref[...], b_ref[...], o_ref, rows)


def _mlp_call(x1, mod4, mod_row, tm, w1, b1, w2, b2, layer, ln_g, ln_b):
    m_rows = x1.shape[0]
    row = lambda m, j: (m, 0)
    vec = lambda m, j: (layer, 0, 0)
    nbytes = (3 * tm * D_MODEL * 4 + tm * D_MODEL * 2 + 2 * 2 * D_MODEL * MLP_TF * 2 + 2 * tm * MLP_TF * 2
              + tm * MLP_TF * 4 + tm * COL_TILE * 4 + 2 * MM_SUB * D_MODEL * 4 + 4 * ROW_CHUNK * D_MODEL * 4)
    return pl.pallas_call(
        _mlp_kernel,
        grid=(m_rows // tm, MLP_TILES + 1),
        in_specs=[
            pl.BlockSpec((tm, D_MODEL), row, pipeline_mode=pl.Buffered(1)),
            pl.BlockSpec((None, None, 6, D_MODEL), lambda m, j: (layer, mod_row(m * tm), 0, 0)),
            pl.BlockSpec((None, D_MODEL, MLP_TF), lambda m, j: (layer, 0, jnp.minimum(j, MLP_TILES - 1))),
            pl.BlockSpec((None, 1, MLP_TF), lambda m, j: (layer, 0, jnp.minimum(j, MLP_TILES - 1))),
            pl.BlockSpec((None, MLP_TF, D_MODEL), lambda m, j: (layer, jnp.maximum(j - 1, 0), 0)),
            pl.BlockSpec((None, 1, D_MODEL), vec),
            pl.BlockSpec((None, 1, D_MODEL), vec),
            pl.BlockSpec((None, 1, D_MODEL), vec),
        ],
        out_specs=pl.BlockSpec((tm, D_MODEL), row),
        out_shape=jax.ShapeDtypeStruct((m_rows, D_MODEL), F32),
        scratch_shapes=[pltpu.VMEM((tm, D_MODEL), BF16), pltpu.VMEM((tm, MLP_TF), BF16),
                        pltpu.VMEM((tm, MLP_TF), BF16)],
        compiler_params=_params(("arbitrary", "arbitrary"), nbytes),
        name="mlp_residual",
    )(x1, mod4, w1, b1, w2, b2, ln_g, ln_b)


def kernel(x, c, ctx, c_ctx, w_mod, b_mod, w_in, rpb, conv_w, conv_b, lru_wa, lru_ba, lru_wx, lru_bx, lru_lambda,
           fno_w, fno_b, w_out, ln1_g, ln1_b, w_fc1, b_fc1, w_fc2, b_fc2, ln2_g, ln2_b):
    xl = x.reshape(BATCH * SEQ, D_MODEL)
    xc = ctx.reshape(BATCH * CTX_LEN, D_MODEL)
    s_in = jnp.concatenate([c, c_ctx[None], jnp.zeros((MOD_ROWS - BATCH - 1, D_MODEL), F32)], 0)
    mod4 = _mod_call(s_in, w_mod, b_mod).reshape(DEPTH, MOD_ROWS, 6, D_MODEL)
    cos_t, sin_t = _rope_tables()
    w_in, w_out, w_fc1, w_fc2 = (w.astype(BF16) for w in (w_in, w_out, w_fc1, w_fc2))
    vec3 = lambda a: a.reshape(DEPTH, 1, a.shape[-1])
    ln1_g3, ln1_b3, ln2_g3, ln2_b3 = vec3(ln1_g), vec3(ln1_b), vec3(ln2_g), vec3(ln2_b)
    b_fc1_3, b_fc2_3 = vec3(b_fc1), vec3(b_fc2)
    lat_row = lambda r0: r0 // SEQ
    ctx_row = lambda r0: CTX_MOD_ROW

    for layer in range(DEPTH):
        ctx_out = layer < DEPTH - 1
        q, qr, k, v, xo, go, f = _inproj_lat_call(xl, mod4, w_in, layer, cos_t, sin_t)
        if ctx_out:
            qc, kc, vc, xoc, goc, fc = _inproj_ctx_call(xc, mod4, w_in, layer, True)
        else:
            kc, vc, xoc = _inproj_ctx_call(xc, mod4, w_in, layer, False)
            goc = None

        na = _attn_call(q, qr, k, v, kc, vc, _attn_bias_table(rpb[layer]))

        w4 = jnp.concatenate([lru_wa[layer, 0], lru_wx[layer, 0], lru_wa[layer, 1], lru_wx[layer, 1]], -1)
        blk = lambda a: a.reshape(LRU_BLOCKS, 1, LRU_BW)
        b4 = jnp.concatenate([blk(lru_ba[layer, 0]), blk(lru_bx[layer, 0]),
                              blk(lru_ba[layer, 1]), blk(lru_bx[layer, 1])], -1)
        lru, lru_c = _lru_call(ctx_out, xo, go, xoc, goc, conv_w[layer], conv_b[layer][None], w4, b4,
                               lru_lambda[layer])

        fm = _fourier_call(f, SEQ, fno_w[layer], fno_b[layer][None])
        x1 = _outproj_call(na, lru, fm, xl, mod4, lat_row, w_out, layer, ln1_g3, ln1_b3)
        xl = _mlp_call(x1, mod4, lat_row, 1024, w_fc1, b_fc1_3, w_fc2, b_fc2_3, layer, ln2_g3, ln2_b3)

        if ctx_out:
            na_c = _ctx_attn_call(qc, kc, vc)
            fm_c = _fourier_call(fc, CTX_LEN, fno_w[layer], fno_b[layer][None])
            c1 = _outproj_call(na_c, lru_c, fm_c, xc, mod4, ctx_row, w_out, layer, ln1_g3, ln1_b3)
            xc = _mlp_call(c1, mod4, ctx_row, 512, w_fc1, b_fc1_3, w_fc2, b_fc2_3, layer, ln2_g3, ln2_b3)

    return xl.reshape(BATCH, SEQ, D_MODEL)
```

```python
import functools
import math

import jax
import jax.numpy as jnp
import numpy as np
from jax import lax
from jax.experimental import pallas as pl
from jax.experimental.pallas import tpu as pltpu

F32 = jnp.float32
BF16 = jnp.bfloat16

D_MODEL = 2048
BATCH = 2
SEQ = 4096
DEPTH = 2
GRID_W = 64
GRID_H = SEQ // GRID_W
CTX_LEN = 256
HEAD_DIM = 128
NA_W = D_MODEL // 2
NA_HEADS = NA_W // HEAD_DIM
WIN_H = 8
WIN_W = 16
LRU_W = D_MODEL // 4
LRU_BLOCKS = 4
LRU_BW = LRU_W // LRU_BLOCKS
CONV_W = 4
LRU_C = 8.0
FNET_W = D_MODEL // 4
FNET_GROUPS = 4
FNET_GW = FNET_W // FNET_GROUPS
IN_W = 3 * NA_W + 2 * LRU_W + FNET_W
D_FF = 4 * D_MODEL
ROPE_THETA = 10000.0
LN_EPS = 1e-5
NEG_INF = -1e30
ALPHA = (2.0 * DEPTH) ** 0.25
ATTN_SCALE = HEAD_DIM ** -0.5

V7X_LANES = 128
V7X_SUBLANES = 8
V7X_VMEM_BYTES = 64 * 1024 * 1024
VMEM_CEILING = V7X_VMEM_BYTES - 6 * 1024 * 1024

COL_TILE = 512
N_COL_TILES = IN_W // COL_TILE
ROW_CHUNK = 128
MOD_ROWS = 8
CTX_MOD_ROW = BATCH


def _vmem_limit(nbytes):
    return int(min(VMEM_CEILING, nbytes * 5 // 4 + (4 << 20)))


def _params(semantics, nbytes):
    return pltpu.CompilerParams(dimension_semantics=semantics, vmem_limit_bytes=_vmem_limit(nbytes))


def _ln(x):
    mu = jnp.mean(x, axis=-1, keepdims=True)
    xc = x - mu
    var = jnp.mean(xc * xc, axis=-1, keepdims=True)
    return xc * lax.rsqrt(var + LN_EPS)


def _sigmoid(x):
    return 1.0 / (1.0 + jnp.exp(-x))


def _gelu_tanh(x):
    return 0.5 * x * (1.0 + jnp.tanh(math.sqrt(2.0 / math.pi) * (x + 0.044715 * (x * x * x))))


def _dot(a, b):
    return jnp.dot(a, b, preferred_element_type=F32)


def _dot_nt(a, b):
    return lax.dot_general(a, b, (((1,), (1,)), ((), ())), preferred_element_type=F32)


MOD_TN = 1024


def _mod_kernel(s_ref, w_ref, b_ref, o_ref):
    s = s_ref[...]
    s = s * _sigmoid(s)
    o_ref[...] = _dot(s.astype(BF16), w_ref[...].astype(BF16)) + b_ref[...]


def _mod_call(s_in, w_mod, b_mod):
    n_out = w_mod.shape[-1]
    nbytes = 2 * (D_MODEL * MOD_TN * 4) + D_MODEL * MOD_TN * 2 + 4 * MOD_ROWS * n_out
    return pl.pallas_call(
        _mod_kernel,
        grid=(DEPTH, n_out // MOD_TN),
        in_specs=[
            pl.BlockSpec((MOD_ROWS, D_MODEL), lambda l, n: (0, 0)),
            pl.BlockSpec((None, D_MODEL, MOD_TN), lambda l, n: (l, 0, n)),
            pl.BlockSpec((None, 1, MOD_TN), lambda l, n: (l, 0, n)),
        ],
        out_specs=pl.BlockSpec((None, MOD_ROWS, MOD_TN), lambda l, n: (l, 0, n)),
        out_shape=jax.ShapeDtypeStruct((DEPTH, MOD_ROWS, n_out), F32),
        compiler_params=_params(("arbitrary", "arbitrary"), nbytes),
        name="modulation",
    )(s_in, w_mod, b_mod.reshape(DEPTH, 1, n_out))


MM_SUB = 256


def _row_tiles(n_rows, size):
    size = min(size, n_rows)
    return [slice(r, r + size) for r in range(0, n_rows, size)]


def _ln_mod_rows(x_ref, shift, scale1, dst_ref, rows):
    for piece in _row_tiles(rows.stop - rows.start, ROW_CHUNK):
        sl = slice(rows.start + piece.start, rows.start + piece.stop)
        dst_ref[sl, :] = (_ln(x_ref[sl, :]) * scale1 + shift).astype(dst_ref.dtype)


def _residual_ln_store(res, y, gate, gain, bias, o_ref, rows):
    for piece in _row_tiles(rows.stop - rows.start, ROW_CHUNK):
        sl = slice(rows.start + piece.start, rows.start + piece.stop)
        z = ALPHA * res[sl, :] + gate * y[piece, :]
        o_ref[sl, :] = _ln(z) * gain + bias


def _rope(a, cos, sin):
    lane = lax.broadcasted_iota(jnp.int32, a.shape, 1)
    first = (lane % (HEAD_DIM // 2)) < (HEAD_DIM // 4)
    partner = jnp.where(first, pltpu.roll(a, HEAD_DIM - HEAD_DIM // 4, 1), pltpu.roll(a, HEAD_DIM // 4, 1))
    return a * cos + partner * sin


def _inproj_lat_kernel(x_ref, mod_ref, w_ref, cos_ref, sin_ref,
                       q_ref, qr_ref, k_ref, v_ref, xo_ref, go_ref, f_ref, xn_ref):
    shift = mod_ref[0:1, :]
    scale1 = 1.0 + mod_ref[1:2, :]

    def emit_q(rows, cols, acc):
        q_ref[rows, cols] = (acc * ATTN_SCALE).astype(q_ref.dtype)
        for h in range(COL_TILE // HEAD_DIM):
            sl = slice(h * HEAD_DIM, (h + 1) * HEAD_DIM)
            dst = slice(cols.start + sl.start, cols.start + sl.stop)
            rot = _rope(acc[:, sl], cos_ref[rows, :], sin_ref[rows, :])
            qr_ref[rows, dst] = (rot * ATTN_SCALE).astype(qr_ref.dtype)

    def emit_k(rows, cols, acc):
        for h in range(COL_TILE // HEAD_DIM):
            sl = slice(h * HEAD_DIM, (h + 1) * HEAD_DIM)
            dst = slice(cols.start + sl.start, cols.start + sl.stop)
            k_ref[rows, dst] = _rope(acc[:, sl], cos_ref[rows, :], sin_ref[rows, :]).astype(k_ref.dtype)

    def emit_to(ref):
        def emit(rows, cols, acc):
            ref[rows, cols] = acc.astype(ref.dtype)
        return emit

    half = [slice(0, COL_TILE), slice(COL_TILE, 2 * COL_TILE)]
    plan = ([(emit_q, c) for c in half] + [(emit_k, c) for c in half] + [(emit_to(v_ref), c) for c in half]
            + [(emit_to(xo_ref), half[0]), (emit_to(go_ref), half[0]), (emit_to(f_ref), half[0])])
    for rows in _row_tiles(x_ref.shape[0], MM_SUB):
        _ln_mod_rows(x_ref, shift, scale1, xn_ref, rows)
        for n, (emit, cols) in enumerate(plan):
            emit(rows, cols, _dot(xn_ref[rows, :], w_ref[:, n * COL_TILE:(n + 1) * COL_TILE]))


def _inproj_lat_call(x2d, mod4, w_in, layer, cos_t, sin_t):
    m_rows = x2d.shape[0]
    tm = 512
    tiles_per_seq = SEQ // tm
    row = lambda m: (m, 0)
    nbytes = (2 * tm * D_MODEL * 4 + tm * D_MODEL * 2 + D_MODEL * IN_W * 2 + 4 * tm * HEAD_DIM * 4
              + 2 * tm * (4 * NA_W * 2 + 2 * LRU_W * 4 + FNET_W * 2) + 6 * MM_SUB * COL_TILE * 4
              + 4 * ROW_CHUNK * D_MODEL * 4)
    bf = lambda w: jax.ShapeDtypeStruct((m_rows, w), BF16)
    ff = lambda w: jax.ShapeDtypeStruct((m_rows, w), F32)
    widths = [NA_W, NA_W, NA_W, NA_W, LRU_W, LRU_W, FNET_W]
    return pl.pallas_call(
        _inproj_lat_kernel,
        grid=(m_rows // tm,),
        in_specs=[
            pl.BlockSpec((tm, D_MODEL), row),
            pl.BlockSpec((None, None, 6, D_MODEL), lambda m: (layer, m // tiles_per_seq, 0, 0)),
            pl.BlockSpec((None, D_MODEL, IN_W), lambda m: (layer, 0, 0), pipeline_mode=pl.Buffered(1)),
            pl.BlockSpec((tm, HEAD_DIM), lambda m: (m % tiles_per_seq, 0)),
            pl.BlockSpec((tm, HEAD_DIM), lambda m: (m % tiles_per_seq, 0)),
        ],
        out_specs=[pl.BlockSpec((tm, w), row) for w in widths],
        out_shape=[bf(NA_W), bf(NA_W), bf(NA_W), bf(NA_W), ff(LRU_W), ff(LRU_W), bf(FNET_W)],
        scratch_shapes=[pltpu.VMEM((tm, D_MODEL), BF16)],
        compiler_params=_params(("arbitrary",), nbytes),
        name="inproj_latent",
    )(x2d, mod4, w_in, cos_t, sin_t)


def _inproj_ctx_kernel(tile_lo, with_q, with_gf, x_ref, mod_ref, w_ref, *refs):
    refs = list(refs)
    xn_ref = refs.pop()
    q_ref = refs.pop(0) if with_q else None
    k_ref, v_ref, xo_ref = refs[0], refs[1], refs[2]
    go_ref, f_ref = (refs[3], refs[4]) if with_gf else (None, None)
    n = pl.program_id(1) + tile_lo

    @pl.when(pl.program_id(1) == 0)
    def _():
        for rows in _row_tiles(x_ref.shape[0], MM_SUB):
            _ln_mod_rows(x_ref, mod_ref[0:1, :], 1.0 + mod_ref[1:2, :], xn_ref, rows)

    acc = _dot(xn_ref[...], w_ref[...])

    if with_q:
        @pl.when(n < 2)
        def _():
            q_ref[...] = (acc * ATTN_SCALE).astype(q_ref.dtype)

    @pl.when((n >= 2) & (n < 4))
    def _():
        k_ref[...] = acc.astype(k_ref.dtype)

    @pl.when((n >= 4) & (n < 6))
    def _():
        v_ref[...] = acc.astype(v_ref.dtype)

    @pl.when(n == 6)
    def _():
        xo_ref[...] = acc

    if with_gf:
        @pl.when(n == 7)
        def _():
            go_ref[...] = acc

        @pl.when(n == 8)
        def _():
            f_ref[...] = acc.astype(f_ref.dtype)


def _inproj_ctx_call(c2d, mod4, w_in, layer, full):
    m_rows = c2d.shape[0]
    tm = m_rows
    tile_lo, tile_hi = (0, N_COL_TILES) if full else (2, 7)

    def col(lo):
        return lambda m, n: (m, jnp.clip(n + tile_lo - lo, 0, 1))

    one = lambda m, n: (m, 0)
    bf = lambda w: jax.ShapeDtypeStruct((m_rows, w), BF16)
    ff = lambda w: jax.ShapeDtypeStruct((m_rows, w), F32)
    out_specs, out_shape = [], []
    if full:
        out_specs.append(pl.BlockSpec((tm, COL_TILE), col(0)))
        out_shape.append(bf(NA_W))
    out_specs += [pl.BlockSpec((tm, COL_TILE), col(2)), pl.BlockSpec((tm, COL_TILE), col(4)),
                  pl.BlockSpec((tm, COL_TILE), one)]
    out_shape += [bf(NA_W), bf(NA_W), ff(LRU_W)]
    if full:
        out_specs += [pl.BlockSpec((tm, COL_TILE), one), pl.BlockSpec((tm, COL_TILE), one)]
        out_shape += [ff(LRU_W), bf(FNET_W)]
    nbytes = (2 * tm * D_MODEL * 4 + tm * D_MODEL * 2 + 2 * D_MODEL * COL_TILE * 2
              + 2 * 6 * tm * COL_TILE * 4 + 3 * tm * COL_TILE * 4 + 4 * ROW_CHUNK * D_MODEL * 4)
    return pl.pallas_call(
        functools.partial(_inproj_ctx_kernel, tile_lo, full, full),
        grid=(1, tile_hi - tile_lo),
        in_specs=[
            pl.BlockSpec((tm, D_MODEL), lambda m, n: (m, 0)),
            pl.BlockSpec((None, None, 6, D_MODEL), lambda m, n: (layer, CTX_MOD_ROW, 0, 0)),
            pl.BlockSpec((None, D_MODEL, COL_TILE), lambda m, n: (layer, 0, n + tile_lo)),
        ],
        out_specs=out_specs,
        out_shape=out_shape,
        scratch_shapes=[pltpu.VMEM((tm, D_MODEL), BF16)],
        compiler_params=_params(("arbitrary", "arbitrary"), nbytes),
        name="inproj_context",
    )(c2d, mod4, w_in)


QBLK_ROWS = 4
KBLK_ROWS = 12
N_QBLK = GRID_H // QBLK_ROWS
QBLK = QBLK_ROWS * GRID_W
KBLK = KBLK_ROWS * GRID_W
KEY_TILE = V7X_LANES
KEY_TILES = KBLK // KEY_TILE
N_DR = 2 * WIN_H - 1
BIAS_BOTH = 0
BIAS_SECOND = BIAS_BOTH + N_DR - 1
BIAS_FIRST = BIAS_SECOND + N_DR
N_BIAS = BIAS_FIRST + N_DR


def _kblk_start(first_query_row):
    return int(np.clip(first_query_row - WIN_H // 2, 0, GRID_H - KBLK_ROWS))


def _bias_plan(first_query_row):
    plan = []
    for u in range(QBLK_ROWS):
        q_row = first_query_row + u
        row_start = int(np.clip(q_row - WIN_H // 2, 0, GRID_H - WIN_H))
        row = []
        for c in range(KEY_TILES):
            k_rows = [_kblk_start(first_query_row) + 2 * c + i for i in range(2)]
            inside = [row_start <= kr < row_start + WIN_H for kr in k_rows]
            dr = [kr - q_row + (WIN_H - 1) for kr in k_rows]
            if inside[0] and inside[1]:
                row.append(BIAS_BOTH + dr[0])
            elif inside[1]:
                row.append(BIAS_SECOND + dr[1])
            elif inside[0]:
                row.append(BIAS_FIRST + dr[0])
            else:
                row.append(None)
        plan.append(row)
    return plan


def _attn_kernel(q_ref, qr_ref, k_ref, v_ref, kc_ref, vc_ref, bias_ref, o_ref):
    kc = kc_ref[...]
    vc = vc_ref[...]
    zero_tile = jnp.zeros((GRID_W, KEY_TILE), BF16)

    def one_block(q0, k0, plan):
        s = _dot_nt(qr_ref[pl.ds(q0, QBLK), :], k_ref[pl.ds(k0, KBLK), :])
        sc = _dot_nt(q_ref[pl.ds(q0, QBLK), :], kc)
        p_rows, pc_rows, denoms = [], [], []
        for u in range(QBLK_ROWS):
            rows = slice(u * GRID_W, (u + 1) * GRID_W)
            band = {c: s[rows, c * KEY_TILE:(c + 1) * KEY_TILE] + bias_ref[idx]
                    for c, idx in enumerate(plan[u]) if idx is not None}
            ctx_tiles = [sc[rows, c * KEY_TILE:(c + 1) * KEY_TILE] for c in range(CTX_LEN // KEY_TILE)]
            tiles = list(band.values()) + ctx_tiles
            m = jnp.max(functools.reduce(jnp.maximum, tiles), axis=-1, keepdims=True)
            p_band = {c: jnp.exp(t - m) for c, t in band.items()}
            p_ctx = [jnp.exp(t - m) for t in ctx_tiles]
            total = functools.reduce(jnp.add, list(p_band.values()) + p_ctx)
            denoms.append(jnp.sum(total, axis=-1, keepdims=True))
            p_rows.append(jnp.concatenate(
                [p_band[c].astype(BF16) if c in p_band else zero_tile for c in range(KEY_TILES)], axis=1))
            pc_rows.append(jnp.concatenate([t.astype(BF16) for t in p_ctx], axis=1))
        p = jnp.concatenate(p_rows, axis=0)
        pc = jnp.concatenate(pc_rows, axis=0)
        o = _dot(p, v_ref[pl.ds(k0, KBLK), :]) + _dot(pc, vc)
        o_ref[pl.ds(q0, QBLK), :] = (o / jnp.concatenate(denoms, axis=0)).astype(o_ref.dtype)

    def static_block(b):
        first_row = b * QBLK_ROWS
        one_block(b * QBLK, _kblk_start(first_row) * GRID_W, _bias_plan(first_row))

    interior_plan = _bias_plan(QBLK_ROWS)

    def interior_block(b):
        q0 = pl.multiple_of(b * QBLK, QBLK)
        k0 = pl.multiple_of((b * QBLK_ROWS - WIN_H // 2) * GRID_W, GRID_W)
        one_block(q0, k0, interior_plan)

    def body(i, carry):
        interior_block(2 * i + 1)
        interior_block(2 * i + 2)
        return carry

    for b in range(N_QBLK):
        static_block(b)


def _attn_call(q, qr, k, v, kc, vc, bias, layer):
    seq_blk = lambda b, h: (b, h)
    nbytes = (2 * (5 * SEQ * HEAD_DIM * 2 + 2 * CTX_LEN * HEAD_DIM * 2 + N_BIAS * GRID_W * KEY_TILE * 4)
              + 8 * QBLK * (KBLK + CTX_LEN) * 4)
    return pl.pallas_call(
        _attn_kernel,
        grid=(BATCH, NA_HEADS),
        in_specs=[
            pl.BlockSpec((SEQ, HEAD_DIM), seq_blk),
            pl.BlockSpec((SEQ, HEAD_DIM), seq_blk),
            pl.BlockSpec((SEQ, HEAD_DIM), seq_blk),
            pl.BlockSpec((SEQ, HEAD_DIM), seq_blk),
            pl.BlockSpec((CTX_LEN, HEAD_DIM), seq_blk),
            pl.BlockSpec((CTX_LEN, HEAD_DIM), seq_blk),
            pl.BlockSpec((None, None, N_BIAS, GRID_W, KEY_TILE), lambda b, h: (layer, h, 0, 0, 0)),
        ],
        out_specs=pl.BlockSpec((SEQ, HEAD_DIM), seq_blk),
        out_shape=jax.ShapeDtypeStruct((BATCH * SEQ, NA_W), BF16),
        compiler_params=_params(("arbitrary", "arbitrary"), nbytes),
        name="neighbourhood_attention",
    )(q, qr, k, v, kc, vc, bias)


def _ctx_attn_kernel(q_ref, k_ref, v_ref, o_ref):
    s = _dot_nt(q_ref[...], k_ref[...])
    m = jnp.max(s, axis=-1, keepdims=True)
    p = jnp.exp(s - m)
    denom = jnp.sum(p, axis=-1, keepdims=True)
    o_ref[...] = (_dot(p.astype(BF16), v_ref[...]) / denom).astype(o_ref.dtype)


def _ctx_attn_call(q, k, v):
    blk = pl.BlockSpec((CTX_LEN, HEAD_DIM), lambda b, h: (b, h))
    return pl.pallas_call(
        _ctx_attn_kernel,
        grid=(BATCH, NA_HEADS),
        in_specs=[blk, blk, blk],
        out_specs=blk,
        out_shape=jax.ShapeDtypeStruct((BATCH * CTX_LEN, NA_W), BF16),
        compiler_params=_params(("arbitrary", "arbitrary"), 16 << 20),
        name="context_attention",
    )(q, k, v)


def _attn_bias_table(rpb):
    col = np.arange(GRID_W)
    col_start = np.clip(col - WIN_W // 2, 0, GRID_W - WIN_W)
    in_win = (col[None, :] >= col_start[:, None]) & (col[None, :] < col_start[:, None] + WIN_W)
    dc = np.clip(col[None, :] - col[:, None] + (WIN_W - 1), 0, 2 * WIN_W - 2)
    t = jnp.full((DEPTH, NA_HEADS, N_DR, GRID_W, GRID_W), NEG_INF, F32)
    for j in range(2 * WIN_W - 1):
        t = jnp.where((in_win & (dc == j))[None, None, None], rpb[:, :, :, j][..., None, None], t)
    masked = jnp.full_like(t, NEG_INF)
    both = jnp.concatenate([t[:, :, :N_DR - 1], t[:, :, 1:]], axis=-1)
    second = jnp.concatenate([masked, t], axis=-1)
    first = jnp.concatenate([t, masked], axis=-1)
    return jnp.concatenate([both, second, first], axis=2)


def _rope_tables():
    quarter = HEAD_DIM // 4
    inv = ROPE_THETA ** (-jnp.arange(quarter, dtype=F32) / quarter)
    t = jnp.arange(SEQ)
    ang_r = (t // GRID_W).astype(F32)[:, None] * inv
    ang_c = (t % GRID_W).astype(F32)[:, None] * inv
    cos = jnp.concatenate([jnp.cos(ang_r), jnp.cos(ang_r), jnp.cos(ang_c), jnp.cos(ang_c)], -1)
    sin = jnp.concatenate([-jnp.sin(ang_r), jnp.sin(ang_r), -jnp.sin(ang_c), jnp.sin(ang_c)], -1)
    return cos, sin


HALO = V7X_SUBLANES
N_SEG = V7X_SUBLANES
SEG_PAD = V7X_SUBLANES
LRU_SCAN_UNROLL = 8


def _sigmoid_tanh(x):
    return 0.5 * jnp.tanh(0.5 * x) + 0.5


def _lru_coeffs(xp_ref, n_rows, cw_ref, cb_ref, w4, b4_ref, sp, a_refs, u_refs):
    seg = n_rows // N_SEG
    pitch = seg + SEG_PAD
    for s in range(N_SEG):
        base = HALO + s * seg
        xc = cb_ref[...] + xp_ref[base - CONV_W // 2:base - CONV_W // 2 + seg, :] * cw_ref[0:1, :]
        for j in range(1, CONV_W):
            off = base - CONV_W // 2 + j
            xc = xc + xp_ref[off:off + seg, :] * cw_ref[j:j + 1, :]
        z = _dot(xc.astype(BF16), w4) + b4_ref[...]
        for d in range(2):
            r = _sigmoid_tanh(z[:, (2 * d) * LRU_BW:(2 * d + 1) * LRU_BW])
            i = _sigmoid_tanh(z[:, (2 * d + 1) * LRU_BW:(2 * d + 2) * LRU_BW])
            log_a = (-LRU_C) * r * sp[d:d + 1, :]
            a = jnp.exp(log_a)
            a_refs[d][s * pitch:s * pitch + seg, :] = a
            one_minus_a2 = -jnp.tanh(log_a) * (a * a + 1.0)
            u_refs[d][s * pitch:s * pitch + seg, :] = jnp.sqrt(one_minus_a2) * (i * xc)


def _lru_local_scan(n_rows, coef_f, coef_b, state_f, state_b):
    seg = n_rows // N_SEG
    pitch = seg + SEG_PAD
    zero = jnp.zeros((N_SEG, LRU_BW), F32)
    one = jnp.ones((N_SEG, LRU_BW), F32)

    def step(coef, state, row, h, p):
        rows = pl.ds(row, N_SEG, stride=pitch)
        a = coef[0][rows, :]
        h = a * h + coef[1][rows, :]
        p = p * a
        state[0][rows, :] = p
        state[1][rows, :] = h
        return h, p

    def body(i, carry):
        hf, pf, hb, pb = carry
        for j in range(LRU_SCAN_UNROLL):
            t = i * LRU_SCAN_UNROLL + j
            hf, pf = step(coef_f, state_f, t, hf, pf)
            hb, pb = step(coef_b, state_b, seg - 1 - t, hb, pb)
        return hf, pf, hb, pb

    lax.fori_loop(0, seg // LRU_SCAN_UNROLL, body, (zero, one, zero, one))


def _lru_carries(n_rows, h_in_f, h_in_b, af, uf, ab, ub):
    seg = n_rows // N_SEG
    pitch = seg + SEG_PAD
    cf, cb = [h_in_f], [h_in_b]
    for s in range(N_SEG):
        last = s * pitch + seg - 1
        cf.append(uf[last:last + 1, :] + af[last:last + 1, :] * cf[-1])
        first = (N_SEG - 1 - s) * pitch
        cb.append(ub[first:first + 1, :] + ab[first:first + 1, :] * cb[-1])
    return cf[:N_SEG], cb[:N_SEG][::-1], cf[N_SEG], cb[N_SEG]


def _lru_emit(n_rows, cf, cb, af, uf, ab, ub, g_ref, o_ref):
    seg = n_rows // N_SEG
    pitch = seg + SEG_PAD
    for s in range(N_SEG):
        src = slice(s * pitch, s * pitch + seg)
        dst = slice(s * seg, (s + 1) * seg)
        y = (uf[src, :] + af[src, :] * cf[s]) + (ub[src, :] + ab[src, :] * cb[s])
        o_ref[dst, :] = (y * _gelu_tanh(g_ref[dst, :])).astype(o_ref.dtype)


def _lru_kernel(ctx_out, x_ref, g_ref, xc_ref, *refs):
    refs = list(refs)
    gc_ref = refs.pop(0) if ctx_out else None
    cw_ref, cb_ref, w4_ref, b4_ref, lam_ref, o_ref = refs[:6]
    refs = refs[6:]
    oc_ref = refs.pop(0) if ctx_out else None
    xp_ref, af, uf, ab, ub, pf, sf, pb, sb = refs

    lam = lam_ref[...]
    z = -lam
    sp = jnp.maximum(z, 0.0) + jnp.log1p(jnp.exp(-jnp.abs(z)))
    w4 = w4_ref[...].astype(BF16)
    zeros_halo = jnp.zeros((HALO, LRU_BW), F32)
    h0 = jnp.zeros((1, LRU_BW), F32)

    xp_ref[0:HALO, :] = zeros_halo
    xp_ref[HALO:HALO + CTX_LEN, :] = xc_ref[...]
    xp_ref[HALO + CTX_LEN:2 * HALO + CTX_LEN, :] = zeros_halo
    _lru_coeffs(xp_ref, CTX_LEN, cw_ref, cb_ref, w4, b4_ref, sp, (af, ab), (uf, ub))
    _lru_local_scan(CTX_LEN, (af, uf), (ab, ub), (pf, sf), (pb, sb))
    cf, cb, hf, hb = _lru_carries(CTX_LEN, h0, h0, pf, sf, pb, sb)
    if ctx_out:
        _lru_emit(CTX_LEN, cf, cb, pf, sf, pb, sb, gc_ref, oc_ref)

    xp_ref[HALO:HALO + SEQ, :] = x_ref[...]
    xp_ref[HALO + SEQ:2 * HALO + SEQ, :] = zeros_halo
    _lru_coeffs(xp_ref, SEQ, cw_ref, cb_ref, w4, b4_ref, sp, (af, ab), (uf, ub))
    _lru_local_scan(SEQ, (af, uf), (ab, ub), (pf, sf), (pb, sb))
    cf, cb, _, _ = _lru_carries(SEQ, hf, hb, pf, sf, pb, sb)
    _lru_emit(SEQ, cf, cb, pf, sf, pb, sb, g_ref, o_ref)


def _lru_call(ctx_out, xl, gl, xc, gc, conv_w_l, conv_b_l, w4, b4, lam_l):
    lat = pl.BlockSpec((SEQ, LRU_BW), lambda b, j: (b, j))
    cx = pl.BlockSpec((CTX_LEN, LRU_BW), lambda b, j: (b, j))
    in_specs = [lat, lat, cx] + ([cx] if ctx_out else []) + [
        pl.BlockSpec((CONV_W, LRU_BW), lambda b, j: (0, j)),
        pl.BlockSpec((1, LRU_BW), lambda b, j: (0, j)),
        pl.BlockSpec((None, LRU_BW, 4 * LRU_BW), lambda b, j: (j, 0, 0)),
        pl.BlockSpec((None, 1, 4 * LRU_BW), lambda b, j: (j, 0, 0)),
        pl.BlockSpec((2, LRU_BW), lambda b, j: (0, j)),
    ]
    out_specs = [lat] + ([cx] if ctx_out else [])
    out_shape = [jax.ShapeDtypeStruct((BATCH * SEQ, LRU_W), BF16)]
    if ctx_out:
        out_shape.append(jax.ShapeDtypeStruct((BATCH * CTX_LEN, LRU_W), BF16))
    seq_bytes = SEQ * LRU_BW * 4
    args = [xl, gl, xc] + ([gc] if ctx_out else []) + [conv_w_l, conv_b_l, w4, b4, lam_l]
    res = pl.pallas_call(
        functools.partial(_lru_kernel, ctx_out),
        grid=(BATCH, LRU_BLOCKS),
        in_specs=in_specs,
        out_specs=out_specs,
        out_shape=out_shape,
        scratch_shapes=([pltpu.VMEM((SEQ + 2 * HALO, LRU_BW), F32)]
                        + [pltpu.VMEM((SEQ + N_SEG * SEG_PAD, LRU_BW), F32)] * 8),
        compiler_params=_params(("arbitrary", "arbitrary"), 15 * seq_bytes + (8 << 20)),
        name="rglru",
    )(*args)
    return (res[0], res[1]) if ctx_out else (res[0], None)


def _fourier_kernel(norm, f_ref, cl_ref, sl_ref, cc_ref, sc_ref, w_ref, b_ref, o_ref, xc_ref, xs_ref):
    @pl.when(pl.program_id(1) == 0)
    def _():
        for g in range(FNET_GROUPS):
            sl = slice(g * FNET_GW, (g + 1) * FNET_GW)
            xg = f_ref[:, sl]
            xc_ref[:, sl] = _dot(xg, cc_ref[...]).astype(BF16)
            xs_ref[:, sl] = _dot(xg, sc_ref[...]).astype(BF16)

    y = (_dot(cl_ref[...], xc_ref[...]) - _dot(sl_ref[...], xs_ref[...])) * norm
    o_ref[...] = (_dot(y.astype(BF16), w_ref[...].astype(BF16)) + b_ref[...]).astype(o_ref.dtype)


DFT_SPLIT = 64


def _dft_matrices(n):
    t = np.arange(n, dtype=np.int64)

    def table(k):
        ang = (2.0 * np.pi / n) * ((k[:, None] * t[None, :]) % n).astype(np.float64)
        return jnp.asarray(np.cos(ang), F32), jnp.asarray(np.sin(ang), F32)

    if n <= DFT_SPLIT:
        c, s = table(t)
        return c.astype(BF16), s.astype(BF16)
    c1, s1 = table(DFT_SPLIT * np.arange(n // DFT_SPLIT, dtype=np.int64))
    c2, s2 = table(np.arange(DFT_SPLIT, dtype=np.int64))
    c = c1[:, None, :] * c2[None, :, :] - s1[:, None, :] * s2[None, :, :]
    s = s1[:, None, :] * c2[None, :, :] + c1[:, None, :] * s2[None, :, :]
    return c.reshape(n, n).astype(BF16), s.reshape(n, n).astype(BF16)


def _fourier_call(f2d, n_pos, fno_w_l, fno_b_l):
    tk = min(512, n_pos)
    steps = n_pos // tk
    cl, sl = _dft_matrices(n_pos)
    cc, sc = _dft_matrices(FNET_GW)
    norm = 1.0 / math.sqrt(n_pos * FNET_GW)
    nbytes = (2 * n_pos * FNET_W * 2 + 2 * 2 * tk * n_pos * 2 + 2 * n_pos * FNET_W * 2 + 2 * FNET_W * FNET_W * 4
              + 4 * tk * FNET_W * 4)
    return pl.pallas_call(
        functools.partial(_fourier_kernel, norm),
        grid=(BATCH, steps),
        in_specs=[
            pl.BlockSpec((n_pos, FNET_W), lambda b, k: (b, 0)),
            pl.BlockSpec((tk, n_pos), lambda b, k: (k, 0)),
            pl.BlockSpec((tk, n_pos), lambda b, k: (k, 0)),
            pl.BlockSpec((FNET_GW, FNET_GW), lambda b, k: (0, 0)),
            pl.BlockSpec((FNET_GW, FNET_GW), lambda b, k: (0, 0)),
            pl.BlockSpec((FNET_W, FNET_W), lambda b, k: (0, 0)),
            pl.BlockSpec((1, FNET_W), lambda b, k: (0, 0)),
        ],
        out_specs=pl.BlockSpec((tk, FNET_W), lambda b, k: (b * steps + k, 0)),
        out_shape=jax.ShapeDtypeStruct((BATCH * n_pos, FNET_W), BF16),
        scratch_shapes=[pltpu.VMEM((n_pos, FNET_W), BF16), pltpu.VMEM((n_pos, FNET_W), BF16)],
        compiler_params=_params(("arbitrary", "arbitrary"), nbytes),
        name="fourier_mix",
    )(f2d, cl, sl, cc, sc, fno_w_l, fno_b_l)


def _outproj_kernel(na_ref, lru_ref, f_ref, res_ref, mod_ref, w_ref, g_ref, b_ref, o_ref):
    gate = mod_ref[2:3, :]
    for rows in _row_tiles(res_ref.shape[0], MM_SUB):
        y = (_dot(na_ref[rows, :], w_ref[0:NA_W, :])
             + _dot(lru_ref[rows, :], w_ref[NA_W:NA_W + LRU_W, :])
             + _dot(f_ref[rows, :], w_ref[NA_W + LRU_W:D_MODEL, :]))
        _residual_ln_store(res_ref, y, gate, g_ref[...], b_ref[...], o_ref, rows)


def _outproj_call(na, lru, f, res, mod4, mod_row, w_out, layer, ln_g, ln_b):
    m_rows = res.shape[0]
    tm = 512
    row = lambda m: (m, 0)
    nbytes = (2 * tm * D_MODEL * 2 + 2 * 2 * tm * D_MODEL * 4 + D_MODEL * D_MODEL * 2 + 2 * MM_SUB * D_MODEL * 4
              + 4 * ROW_CHUNK * D_MODEL * 4)
    return pl.pallas_call(
        _outproj_kernel,
        grid=(m_rows // tm,),
        in_specs=[
            pl.BlockSpec((tm, NA_W), row),
            pl.BlockSpec((tm, LRU_W), row),
            pl.BlockSpec((tm, FNET_W), row),
            pl.BlockSpec((tm, D_MODEL), row),
            pl.BlockSpec((None, None, 6, D_MODEL), lambda m: (layer, mod_row(m * tm), 0, 0)),
            pl.BlockSpec((None, D_MODEL, D_MODEL), lambda m: (layer, 0, 0), pipeline_mode=pl.Buffered(1)),
            pl.BlockSpec((None, 1, D_MODEL), lambda m: (layer, 0, 0)),
            pl.BlockSpec((None, 1, D_MODEL), lambda m: (layer, 0, 0)),
        ],
        out_specs=pl.BlockSpec((tm, D_MODEL), row),
        out_shape=jax.ShapeDtypeStruct((m_rows, D_MODEL), F32),
        compiler_params=_params(("arbitrary",), nbytes),
        name="outproj_residual",
    )(na, lru, f, res, mod4, w_out, ln_g, ln_b)


MLP_TF = 512
MLP_TILES = D_FF // MLP_TF


def _mlp_kernel(x_ref, mod_ref, w1_ref, b1_ref, w2_ref, b2_ref, g_ref, b_ref, o_ref, v_ref, h0_ref, h1_ref):
    j = pl.program_id(1)
    tm = x_ref.shape[0]

    def up(h_out, rows=slice(None)):
        h = _dot(v_ref[rows, :], w1_ref[...]) + b1_ref[...]
        h_out[rows, :] = jnp.square(jnp.maximum(h, 0.0)).astype(BF16)

    def down(h_in, first):
        for c in range(D_MODEL // COL_TILE):
            sl = slice(c * COL_TILE, (c + 1) * COL_TILE)
            part = _dot(h_in[...], w2_ref[:, sl])
            if first:
                o_ref[:, sl] = part
            else:
                o_ref[:, sl] += part

    @pl.when(j == 0)
    def _():
        shift = mod_ref[3:4, :]
        scale1 = 1.0 + mod_ref[4:5, :]
        for rows in _row_tiles(tm, MM_SUB):
            _ln_mod_rows(x_ref, shift, scale1, v_ref, rows)
            up(h0_ref, rows)

    @pl.when(j == 1)
    def _():
        down(h0_ref, True)
        up(h1_ref)

    @pl.when((j > 1) & (j < MLP_TILES) & (j % 2 == 0))
    def _():
        down(h1_ref, False)
        up(h0_ref)

    @pl.when((j > 1) & (j < MLP_TILES) & (j % 2 == 1))
    def _():
        down(h0_ref, False)
        up(h1_ref)

    @pl.when(j == MLP_TILES)
    def _():
        h_last = h1_ref if (MLP_TILES - 1) % 2 else h0_ref
        gate = mod_ref[5:6, :]
        for rows in _row_tiles(tm, MM_SUB):
            y = o_ref[rows, :] + _dot(h_last[rows, :], w2_ref[...]) + b2_ref[...]
            _residual_ln_store(x_ref, y, gate, g_ref[...], b_ref[...], o_ref, rows)


def _mlp_call(x1, mod4, mod_row, tm, w1, b1, w2, b2, layer, ln_g, ln_b):
    m_rows = x1.shape[0]
    row = lambda m, j: (m, 0)
    vec = lambda m, j: (layer, 0, 0)
    nbytes = (4 * tm * D_MODEL * 4 + tm * D_MODEL * 2 + 2 * 2 * D_MODEL * MLP_TF * 2 + 2 * tm * MLP_TF * 2
              + tm * MLP_TF * 4 + tm * COL_TILE * 4 + 2 * MM_SUB * D_MODEL * 4 + 4 * ROW_CHUNK * D_MODEL * 4)
    return pl.pallas_call(
        _mlp_kernel,
        grid=(m_rows // tm, MLP_TILES + 1),
        in_specs=[
            pl.BlockSpec((tm, D_MODEL), row),
            pl.BlockSpec((None, None, 6, D_MODEL), lambda m, j: (layer, mod_row(m * tm), 0, 0)),
            pl.BlockSpec((None, D_MODEL, MLP_TF), lambda m, j: (layer, 0, jnp.minimum(j, MLP_TILES - 1))),
            pl.BlockSpec((None, 1, MLP_TF), lambda m, j: (layer, 0, jnp.minimum(j, MLP_TILES - 1))),
            pl.BlockSpec((None, MLP_TF, D_MODEL), lambda m, j: (layer, jnp.maximum(j - 1, 0), 0)),
            pl.BlockSpec((None, 1, D_MODEL), vec),
            pl.BlockSpec((None, 1, D_MODEL), vec),
            pl.BlockSpec((None, 1, D_MODEL), vec),
        ],
        out_specs=pl.BlockSpec((tm, D_MODEL), row),
        out_shape=jax.ShapeDtypeStruct((m_rows, D_MODEL), F32),
        scratch_shapes=[pltpu.VMEM((tm, D_MODEL), BF16), pltpu.VMEM((tm, MLP_TF), BF16),
                        pltpu.VMEM((tm, MLP_TF), BF16)],
        compiler_params=_params(("arbitrary", "arbitrary"), nbytes),
        name="mlp_residual",
    )(x1, mod4, w1, b1, w2, b2, ln_g, ln_b)


def kernel(x, c, ctx, c_ctx, w_mod, b_mod, w_in, rpb, conv_w, conv_b, lru_wa, lru_ba, lru_wx, lru_bx, lru_lambda,
           fno_w, fno_b, w_out, ln1_g, ln1_b, w_fc1, b_fc1, w_fc2, b_fc2, ln2_g, ln2_b):
    xl = x.reshape(BATCH * SEQ, D_MODEL)
    xc = ctx.reshape(BATCH * CTX_LEN, D_MODEL)
    s_in = jnp.concatenate([c, c_ctx[None], jnp.zeros((MOD_ROWS - BATCH - 1, D_MODEL), F32)], 0)
    mod4 = _mod_call(s_in, w_mod, b_mod).reshape(DEPTH, MOD_ROWS, 6, D_MODEL)
    cos_t, sin_t = _rope_tables()
    bias_tab = _attn_bias_table(rpb)
    w_in, w_out, w_fc1, w_fc2 = (w.astype(BF16) for w in (w_in, w_out, w_fc1, w_fc2))
    vec3 = lambda a: a.reshape(DEPTH, 1, a.shape[-1])
    ln1_g3, ln1_b3, ln2_g3, ln2_b3 = vec3(ln1_g), vec3(ln1_b), vec3(ln2_g), vec3(ln2_b)
    b_fc1_3, b_fc2_3 = vec3(b_fc1), vec3(b_fc2)
    lat_row = lambda r0: r0 // SEQ
    ctx_row = lambda r0: CTX_MOD_ROW

    for layer in range(DEPTH):
        ctx_out = layer < DEPTH - 1
        q, qr, k, v, xo, go, f = _inproj_lat_call(xl, mod4, w_in, layer, cos_t, sin_t)
        if ctx_out:
            qc, kc, vc, xoc, goc, fc = _inproj_ctx_call(xc, mod4, w_in, layer, True)
        else:
            kc, vc, xoc = _inproj_ctx_call(xc, mod4, w_in, layer, False)
            goc = None

        na = _attn_call(q, qr, k, v, kc, vc, bias_tab, layer)

        w4 = jnp.concatenate([lru_wa[layer, 0], lru_wx[layer, 0], lru_wa[layer, 1], lru_wx[layer, 1]], -1)
        blk = lambda a: a.reshape(LRU_BLOCKS, 1, LRU_BW)
        b4 = jnp.concatenate([blk(lru_ba[layer, 0]), blk(lru_bx[layer, 0]),
                              blk(lru_ba[layer, 1]), blk(lru_bx[layer, 1])], -1)
        lru, lru_c = _lru_call(ctx_out, xo, go, xoc, goc, conv_w[layer], conv_b[layer][None], w4, b4,
                               lru_lambda[layer])

        fm = _fourier_call(f, SEQ, fno_w[layer], fno_b[layer][None])
        x1 = _outproj_call(na, lru, fm, xl, mod4, lat_row, w_out, layer, ln1_g3, ln1_b3)
        xl = _mlp_call(x1, mod4, lat_row, 1024, w_fc1, b_fc1_3, w_fc2, b_fc2_3, layer, ln2_g3, ln2_b3)

        if ctx_out:
            na_c = _ctx_attn_call(qc, kc, vc)
            fm_c = _fourier_call(fc, CTX_LEN, fno_w[layer], fno_b[layer][None])
            c1 = _outproj_call(na_c, lru_c, fm_c, xc, mod4, ctx_row, w_out, layer, ln1_g3, ln1_b3)
            xc = _mlp_call(c1, mod4, ctx_row, 512, w_fc1, b_fc1_3, w_fc2, b_fc2_3, layer, ln2_g3, ln2_b3)

    return xl.reshape(BATCH, SEQ, D_MODEL)
```

```python
import functools
import math

import jax
import jax.numpy as jnp
import numpy as np
from jax import lax
from jax.experimental import pallas as pl
from jax.experimental.pallas import tpu as pltpu

F32 = jnp.float32
BF16 = jnp.bfloat16

D_MODEL = 2048
BATCH = 2
SEQ = 4096
DEPTH = 2
GRID_W = 64
GRID_H = SEQ // GRID_W
CTX_LEN = 256
HEAD_DIM = 128
NA_W = D_MODEL // 2
NA_HEADS = NA_W // HEAD_DIM
WIN_H = 8
WIN_W = 16
LRU_W = D_MODEL // 4
LRU_BLOCKS = 4
LRU_BW = LRU_W // LRU_BLOCKS
CONV_W = 4
LRU_C = 8.0
FNET_W = D_MODEL // 4
FNET_GROUPS = 4
FNET_GW = FNET_W // FNET_GROUPS
IN_W = 3 * NA_W + 2 * LRU_W + FNET_W
D_FF = 4 * D_MODEL
ROPE_THETA = 10000.0
LN_EPS = 1e-5
NEG_INF = -1e30
ALPHA = (2.0 * DEPTH) ** 0.25
ATTN_SCALE = HEAD_DIM ** -0.5

V7X_LANES = 128
V7X_SUBLANES = 8
BF16_ROWS = 2 * V7X_SUBLANES
V7X_VMEM_BYTES = 64 * 1024 * 1024
VMEM_CEILING = V7X_VMEM_BYTES - 6 * 1024 * 1024

COL_TILE = 512
N_COL_TILES = IN_W // COL_TILE
ROW_CHUNK = 128
MOD_ROWS = 8
CTX_MOD_ROW = BATCH


def _vmem_limit(nbytes):
    return int(min(VMEM_CEILING, nbytes * 5 // 4 + (4 << 20)))


def _params(semantics, nbytes):
    return pltpu.CompilerParams(dimension_semantics=semantics, vmem_limit_bytes=_vmem_limit(nbytes))


def _ln(x):
    mu = jnp.mean(x, axis=-1, keepdims=True)
    xc = x - mu
    var = jnp.mean(xc * xc, axis=-1, keepdims=True)
    return xc * lax.rsqrt(var + LN_EPS)


def _sigmoid(x):
    return 1.0 / (1.0 + jnp.exp(-x))


def _gelu_tanh(x):
    return 0.5 * x * (1.0 + jnp.tanh(math.sqrt(2.0 / math.pi) * (x + 0.044715 * (x * x * x))))


def _dot(a, b):
    return jnp.dot(a, b, preferred_element_type=F32)


def _dot_nt(a, b):
    return lax.dot_general(a, b, (((1,), (1,)), ((), ())), preferred_element_type=F32)


MOD_TN = 1024


def _mod_kernel(s_ref, w_ref, b_ref, o_ref):
    s = s_ref[...]
    s = s * _sigmoid(s)
    o_ref[...] = _dot(s.astype(BF16), w_ref[...].astype(BF16)) + b_ref[...]


def _mod_call(s_in, w_mod, b_mod):
    n_out = w_mod.shape[-1]
    nbytes = 2 * (D_MODEL * MOD_TN * 4) + D_MODEL * MOD_TN * 2 + 4 * MOD_ROWS * n_out
    return pl.pallas_call(
        _mod_kernel,
        grid=(DEPTH, n_out // MOD_TN),
        in_specs=[
            pl.BlockSpec((MOD_ROWS, D_MODEL), lambda l, n: (0, 0)),
            pl.BlockSpec((None, D_MODEL, MOD_TN), lambda l, n: (l, 0, n)),
            pl.BlockSpec((None, 1, MOD_TN), lambda l, n: (l, 0, n)),
        ],
        out_specs=pl.BlockSpec((None, MOD_ROWS, MOD_TN), lambda l, n: (l, 0, n)),
        out_shape=jax.ShapeDtypeStruct((DEPTH, MOD_ROWS, n_out), F32),
        compiler_params=_params(("arbitrary", "arbitrary"), nbytes),
        name="modulation",
    )(s_in, w_mod, b_mod.reshape(DEPTH, 1, n_out))


MM_SUB = 256


def _row_tiles(n_rows, size):
    size = min(size, n_rows)
    return [slice(r, r + size) for r in range(0, n_rows, size)]


def _ln_mod_rows(x_ref, shift, scale1, dst_ref, rows):
    for piece in _row_tiles(rows.stop - rows.start, ROW_CHUNK):
        sl = slice(rows.start + piece.start, rows.start + piece.stop)
        dst_ref[sl, :] = (_ln(x_ref[sl, :]) * scale1 + shift).astype(dst_ref.dtype)


def _residual_ln_store(res, y, gate, gain, bias, o_ref, rows):
    for piece in _row_tiles(rows.stop - rows.start, ROW_CHUNK):
        sl = slice(rows.start + piece.start, rows.start + piece.stop)
        z = ALPHA * res[sl, :] + gate * y[piece, :]
        o_ref[sl, :] = _ln(z) * gain + bias


def _rope(a, cos, sin):
    lane = lax.broadcasted_iota(jnp.int32, a.shape, 1)
    first = (lane % (HEAD_DIM // 2)) < (HEAD_DIM // 4)
    partner = jnp.where(first, pltpu.roll(a, HEAD_DIM - HEAD_DIM // 4, 1), pltpu.roll(a, HEAD_DIM // 4, 1))
    return a * cos + partner * sin


def _inproj_lat_kernel(x_ref, mod_ref, w_ref, cos_ref, sin_ref,
                       q_ref, qr_ref, k_ref, v_ref, xo_ref, go_ref, f_ref, xn_ref):
    shift = mod_ref[0:1, :]
    scale1 = 1.0 + mod_ref[1:2, :]

    def emit_q(rows, cols, acc):
        q_ref[rows, cols] = (acc * ATTN_SCALE).astype(q_ref.dtype)
        for h in range(COL_TILE // HEAD_DIM):
            sl = slice(h * HEAD_DIM, (h + 1) * HEAD_DIM)
            dst = slice(cols.start + sl.start, cols.start + sl.stop)
            rot = _rope(acc[:, sl], cos_ref[rows, :], sin_ref[rows, :])
            qr_ref[rows, dst] = (rot * ATTN_SCALE).astype(qr_ref.dtype)

    def emit_k(rows, cols, acc):
        for h in range(COL_TILE // HEAD_DIM):
            sl = slice(h * HEAD_DIM, (h + 1) * HEAD_DIM)
            dst = slice(cols.start + sl.start, cols.start + sl.stop)
            k_ref[rows, dst] = _rope(acc[:, sl], cos_ref[rows, :], sin_ref[rows, :]).astype(k_ref.dtype)

    def emit_to(ref):
        def emit(rows, cols, acc):
            ref[rows, cols] = acc.astype(ref.dtype)
        return emit

    half = [slice(0, COL_TILE), slice(COL_TILE, 2 * COL_TILE)]
    plan = ([(emit_q, c) for c in half] + [(emit_k, c) for c in half] + [(emit_to(v_ref), c) for c in half]
            + [(emit_to(xo_ref), half[0]), (emit_to(go_ref), half[0]), (emit_to(f_ref), half[0])])
    for rows in _row_tiles(x_ref.shape[0], MM_SUB):
        _ln_mod_rows(x_ref, shift, scale1, xn_ref, rows)
        for n, (emit, cols) in enumerate(plan):
            emit(rows, cols, _dot(xn_ref[rows, :], w_ref[:, n * COL_TILE:(n + 1) * COL_TILE]))


def _inproj_lat_call(x2d, mod4, w_in, layer, cos_t, sin_t):
    m_rows = x2d.shape[0]
    tm = 512
    tiles_per_seq = SEQ // tm
    row = lambda m: (m, 0)
    nbytes = (2 * tm * D_MODEL * 4 + tm * D_MODEL * 2 + D_MODEL * IN_W * 2 + 4 * tm * HEAD_DIM * 4
              + 2 * tm * (4 * NA_W * 2 + 2 * LRU_W * 4 + FNET_W * 2) + 6 * MM_SUB * COL_TILE * 4
              + 4 * ROW_CHUNK * D_MODEL * 4)
    bf = lambda w: jax.ShapeDtypeStruct((m_rows, w), BF16)
    ff = lambda w: jax.ShapeDtypeStruct((m_rows, w), F32)
    widths = [NA_W, NA_W, NA_W, NA_W, LRU_W, LRU_W, FNET_W]
    return pl.pallas_call(
        _inproj_lat_kernel,
        grid=(m_rows // tm,),
        in_specs=[
            pl.BlockSpec((tm, D_MODEL), row),
            pl.BlockSpec((None, None, 6, D_MODEL), lambda m: (layer, m // tiles_per_seq, 0, 0)),
            pl.BlockSpec((None, D_MODEL, IN_W), lambda m: (layer, 0, 0), pipeline_mode=pl.Buffered(1)),
            pl.BlockSpec((tm, HEAD_DIM), lambda m: (m % tiles_per_seq, 0)),
            pl.BlockSpec((tm, HEAD_DIM), lambda m: (m % tiles_per_seq, 0)),
        ],
        out_specs=[pl.BlockSpec((tm, w), row) for w in widths],
        out_shape=[bf(NA_W), bf(NA_W), bf(NA_W), bf(NA_W), ff(LRU_W), ff(LRU_W), bf(FNET_W)],
        scratch_shapes=[pltpu.VMEM((tm, D_MODEL), BF16)],
        compiler_params=_params(("arbitrary",), nbytes),
        name="inproj_latent",
    )(x2d, mod4, w_in, cos_t, sin_t)


def _inproj_ctx_kernel(tile_lo, with_q, with_gf, x_ref, mod_ref, w_ref, *refs):
    refs = list(refs)
    xn_ref = refs.pop()
    q_ref = refs.pop(0) if with_q else None
    k_ref, v_ref, xo_ref = refs[0], refs[1], refs[2]
    go_ref, f_ref = (refs[3], refs[4]) if with_gf else (None, None)
    n = pl.program_id(1) + tile_lo

    @pl.when(pl.program_id(1) == 0)
    def _():
        for rows in _row_tiles(x_ref.shape[0], MM_SUB):
            _ln_mod_rows(x_ref, mod_ref[0:1, :], 1.0 + mod_ref[1:2, :], xn_ref, rows)

    acc = _dot(xn_ref[...], w_ref[...])

    if with_q:
        @pl.when(n < 2)
        def _():
            q_ref[...] = (acc * ATTN_SCALE).astype(q_ref.dtype)

    @pl.when((n >= 2) & (n < 4))
    def _():
        k_ref[...] = acc.astype(k_ref.dtype)

    @pl.when((n >= 4) & (n < 6))
    def _():
        v_ref[...] = acc.astype(v_ref.dtype)

    @pl.when(n == 6)
    def _():
        xo_ref[...] = acc

    if with_gf:
        @pl.when(n == 7)
        def _():
            go_ref[...] = acc

        @pl.when(n == 8)
        def _():
            f_ref[...] = acc.astype(f_ref.dtype)


def _inproj_ctx_call(c2d, mod4, w_in, layer, full):
    m_rows = c2d.shape[0]
    tm = m_rows
    tile_lo, tile_hi = (0, N_COL_TILES) if full else (2, 7)

    def col(lo):
        return lambda m, n: (m, jnp.clip(n + tile_lo - lo, 0, 1))

    one = lambda m, n: (m, 0)
    bf = lambda w: jax.ShapeDtypeStruct((m_rows, w), BF16)
    ff = lambda w: jax.ShapeDtypeStruct((m_rows, w), F32)
    out_specs, out_shape = [], []
    if full:
        out_specs.append(pl.BlockSpec((tm, COL_TILE), col(0)))
        out_shape.append(bf(NA_W))
    out_specs += [pl.BlockSpec((tm, COL_TILE), col(2)), pl.BlockSpec((tm, COL_TILE), col(4)),
                  pl.BlockSpec((tm, COL_TILE), one)]
    out_shape += [bf(NA_W), bf(NA_W), ff(LRU_W)]
    if full:
        out_specs += [pl.BlockSpec((tm, COL_TILE), one), pl.BlockSpec((tm, COL_TILE), one)]
        out_shape += [ff(LRU_W), bf(FNET_W)]
    nbytes = (2 * tm * D_MODEL * 4 + tm * D_MODEL * 2 + 2 * D_MODEL * COL_TILE * 2
              + 2 * 6 * tm * COL_TILE * 4 + 3 * tm * COL_TILE * 4 + 4 * ROW_CHUNK * D_MODEL * 4)
    return pl.pallas_call(
        functools.partial(_inproj_ctx_kernel, tile_lo, full, full),
        grid=(1, tile_hi - tile_lo),
        in_specs=[
            pl.BlockSpec((tm, D_MODEL), lambda m, n: (m, 0)),
            pl.BlockSpec((None, None, 6, D_MODEL), lambda m, n: (layer, CTX_MOD_ROW, 0, 0)),
            pl.BlockSpec((None, D_MODEL, COL_TILE), lambda m, n: (layer, 0, n + tile_lo)),
        ],
        out_specs=out_specs,
        out_shape=out_shape,
        scratch_shapes=[pltpu.VMEM((tm, D_MODEL), BF16)],
        compiler_params=_params(("arbitrary", "arbitrary"), nbytes),
        name="inproj_context",
    )(c2d, mod4, w_in)


QBLK_ROWS = 4
KBLK_ROWS = 12
N_QBLK = GRID_H // QBLK_ROWS
QBLK = QBLK_ROWS * GRID_W
KBLK = KBLK_ROWS * GRID_W
KEY_TILE = V7X_LANES
KEY_TILES = KBLK // KEY_TILE
N_DR = 2 * WIN_H - 1
BIAS_BOTH = 0
BIAS_SECOND = BIAS_BOTH + N_DR - 1
BIAS_FIRST = BIAS_SECOND + N_DR
N_BIAS = BIAS_FIRST + N_DR


def _kblk_start(first_query_row):
    return int(np.clip(first_query_row - WIN_H // 2, 0, GRID_H - KBLK_ROWS))


def _bias_plan(first_query_row):
    plan = []
    for u in range(QBLK_ROWS):
        q_row = first_query_row + u
        row_start = int(np.clip(q_row - WIN_H // 2, 0, GRID_H - WIN_H))
        row = []
        for c in range(KEY_TILES):
            k_rows = [_kblk_start(first_query_row) + 2 * c + i for i in range(2)]
            inside = [row_start <= kr < row_start + WIN_H for kr in k_rows]
            dr = [kr - q_row + (WIN_H - 1) for kr in k_rows]
            if inside[0] and inside[1]:
                row.append(BIAS_BOTH + dr[0])
            elif inside[1]:
                row.append(BIAS_SECOND + dr[1])
            elif inside[0]:
                row.append(BIAS_FIRST + dr[0])
            else:
                row.append(None)
        plan.append(row)
    return plan


def _attn_kernel(q_ref, qr_ref, k_ref, v_ref, kc_ref, vc_ref, bias_ref, o_ref):
    kc = kc_ref[...]
    vc = vc_ref[...]
    zero_tile = jnp.zeros((GRID_W, KEY_TILE), BF16)

    def one_block(q0, k0, plan):
        s = _dot_nt(qr_ref[pl.ds(q0, QBLK), :], k_ref[pl.ds(k0, KBLK), :])
        sc = _dot_nt(q_ref[pl.ds(q0, QBLK), :], kc)
        p_rows, pc_rows, denoms = [], [], []
        for u in range(QBLK_ROWS):
            rows = slice(u * GRID_W, (u + 1) * GRID_W)
            band = {c: s[rows, c * KEY_TILE:(c + 1) * KEY_TILE] + bias_ref[idx]
                    for c, idx in enumerate(plan[u]) if idx is not None}
            ctx_tiles = [sc[rows, c * KEY_TILE:(c + 1) * KEY_TILE] for c in range(CTX_LEN // KEY_TILE)]
            tiles = list(band.values()) + ctx_tiles
            m = jnp.max(functools.reduce(jnp.maximum, tiles), axis=-1, keepdims=True)
            p_band = {c: jnp.exp(t - m) for c, t in band.items()}
            p_ctx = [jnp.exp(t - m) for t in ctx_tiles]
            total = functools.reduce(jnp.add, list(p_band.values()) + p_ctx)
            denoms.append(jnp.sum(total, axis=-1, keepdims=True))
            p_rows.append(jnp.concatenate(
                [p_band[c].astype(BF16) if c in p_band else zero_tile for c in range(KEY_TILES)], axis=1))
            pc_rows.append(jnp.concatenate([t.astype(BF16) for t in p_ctx], axis=1))
        p = jnp.concatenate(p_rows, axis=0)
        pc = jnp.concatenate(pc_rows, axis=0)
        o = _dot(p, v_ref[pl.ds(k0, KBLK), :]) + _dot(pc, vc)
        o_ref[pl.ds(q0, QBLK), :] = (o / jnp.concatenate(denoms, axis=0)).astype(o_ref.dtype)

    def static_block(b):
        first_row = b * QBLK_ROWS
        one_block(b * QBLK, _kblk_start(first_row) * GRID_W, _bias_plan(first_row))

    interior_plan = _bias_plan(QBLK_ROWS)

    def interior_block(b):
        q0 = pl.multiple_of(b * QBLK, QBLK)
        k0 = pl.multiple_of((b * QBLK_ROWS - WIN_H // 2) * GRID_W, GRID_W)
        one_block(q0, k0, interior_plan)

    def body(i, carry):
        interior_block(2 * i + 1)
        interior_block(2 * i + 2)
        return carry

    for b in range(N_QBLK):
        static_block(b)


def _attn_call(q, qr, k, v, kc, vc, bias, layer):
    seq_blk = lambda b, h: (b, h)
    nbytes = (2 * (5 * SEQ * HEAD_DIM * 2 + 2 * CTX_LEN * HEAD_DIM * 2 + N_BIAS * GRID_W * KEY_TILE * 4)
              + 8 * QBLK * (KBLK + CTX_LEN) * 4)
    return pl.pallas_call(
        _attn_kernel,
        grid=(BATCH, NA_HEADS),
        in_specs=[
            pl.BlockSpec((SEQ, HEAD_DIM), seq_blk),
            pl.BlockSpec((SEQ, HEAD_DIM), seq_blk),
            pl.BlockSpec((SEQ, HEAD_DIM), seq_blk),
            pl.BlockSpec((SEQ, HEAD_DIM), seq_blk),
            pl.BlockSpec((CTX_LEN, HEAD_DIM), seq_blk),
            pl.BlockSpec((CTX_LEN, HEAD_DIM), seq_blk),
            pl.BlockSpec((None, None, N_BIAS, GRID_W, KEY_TILE), lambda b, h: (layer, h, 0, 0, 0)),
        ],
        out_specs=pl.BlockSpec((SEQ, HEAD_DIM), seq_blk),
        out_shape=jax.ShapeDtypeStruct((BATCH * SEQ, NA_W), BF16),
        compiler_params=_params(("arbitrary", "arbitrary"), nbytes),
        name="neighbourhood_attention",
    )(q, qr, k, v, kc, vc, bias)


def _ctx_attn_kernel(q_ref, k_ref, v_ref, o_ref):
    s = _dot_nt(q_ref[...], k_ref[...])
    m = jnp.max(s, axis=-1, keepdims=True)
    p = jnp.exp(s - m)
    denom = jnp.sum(p, axis=-1, keepdims=True)
    o_ref[...] = (_dot(p.astype(BF16), v_ref[...]) / denom).astype(o_ref.dtype)


def _ctx_attn_call(q, k, v):
    blk = pl.BlockSpec((CTX_LEN, HEAD_DIM), lambda b, h: (b, h))
    return pl.pallas_call(
        _ctx_attn_kernel,
        grid=(BATCH, NA_HEADS),
        in_specs=[blk, blk, blk],
        out_specs=blk,
        out_shape=jax.ShapeDtypeStruct((BATCH * CTX_LEN, NA_W), BF16),
        compiler_params=_params(("arbitrary", "arbitrary"), 16 << 20),
        name="context_attention",
    )(q, k, v)


def _attn_bias_table(rpb):
    col = np.arange(GRID_W)
    col_start = np.clip(col - WIN_W // 2, 0, GRID_W - WIN_W)
    in_win = (col[None, :] >= col_start[:, None]) & (col[None, :] < col_start[:, None] + WIN_W)
    dc = np.clip(col[None, :] - col[:, None] + (WIN_W - 1), 0, 2 * WIN_W - 2)
    onehot = (in_win[None] & (dc[None] == np.arange(2 * WIN_W - 1)[:, None, None])).astype(np.float32)
    t = jnp.einsum('lhdj,jqk->lhdqk', rpb, jnp.asarray(onehot), precision=lax.Precision.HIGHEST)
    t = jnp.where(in_win[None, None, None], t, NEG_INF)
    masked = jnp.full_like(t, NEG_INF)
    both = jnp.concatenate([t[:, :, :N_DR - 1], t[:, :, 1:]], axis=-1)
    second = jnp.concatenate([masked, t], axis=-1)
    first = jnp.concatenate([t, masked], axis=-1)
    return jnp.concatenate([both, second, first], axis=2)


def _rope_tables():
    quarter = HEAD_DIM // 4
    inv = ROPE_THETA ** (-jnp.arange(quarter, dtype=F32) / quarter)
    t = jnp.arange(SEQ)
    ang_r = (t // GRID_W).astype(F32)[:, None] * inv
    ang_c = (t % GRID_W).astype(F32)[:, None] * inv
    cos = jnp.concatenate([jnp.cos(ang_r), jnp.cos(ang_r), jnp.cos(ang_c), jnp.cos(ang_c)], -1)
    sin = jnp.concatenate([-jnp.sin(ang_r), jnp.sin(ang_r), -jnp.sin(ang_c), jnp.sin(ang_c)], -1)
    return cos, sin


HALO = V7X_SUBLANES
N_SEG = V7X_SUBLANES
SEG_PAD = V7X_SUBLANES
LRU_SCAN_UNROLL = 8


def _sigmoid_tanh(x):
    return 0.5 * jnp.tanh(0.5 * x) + 0.5


def _lru_coeffs(xp_ref, n_rows, cw_ref, cb_ref, w4, b4_ref, sp, a_refs, u_refs):
    seg = n_rows // N_SEG
    pitch = seg + SEG_PAD
    for s in range(N_SEG):
        base = HALO + s * seg
        xc = cb_ref[...] + xp_ref[base - CONV_W // 2:base - CONV_W // 2 + seg, :] * cw_ref[0:1, :]
        for j in range(1, CONV_W):
            off = base - CONV_W // 2 + j
            xc = xc + xp_ref[off:off + seg, :] * cw_ref[j:j + 1, :]
        z = _dot(xc.astype(BF16), w4) + b4_ref[...]
        for d in range(2):
            r = _sigmoid_tanh(z[:, (2 * d) * LRU_BW:(2 * d + 1) * LRU_BW])
            i = _sigmoid_tanh(z[:, (2 * d + 1) * LRU_BW:(2 * d + 2) * LRU_BW])
            log_a = (-LRU_C) * r * sp[d:d + 1, :]
            a = jnp.exp(log_a)
            a_refs[d][s * pitch:s * pitch + seg, :] = a
            one_minus_a2 = -jnp.tanh(log_a) * (a * a + 1.0)
            u_refs[d][s * pitch:s * pitch + seg, :] = jnp.sqrt(one_minus_a2) * (i * xc)


def _lru_local_scan(n_rows, coef_f, coef_b, state_f, state_b):
    seg = n_rows // N_SEG
    pitch = seg + SEG_PAD
    zero = jnp.zeros((N_SEG, LRU_BW), F32)
    one = jnp.ones((N_SEG, LRU_BW), F32)

    def step(coef, state, row, h, p):
        rows = pl.ds(row, N_SEG, stride=pitch)
        a = coef[0][rows, :]
        h = a * h + coef[1][rows, :]
        p = p * a
        state[0][rows, :] = p
        state[1][rows, :] = h
        return h, p

    def body(i, carry):
        hf, pf, hb, pb = carry
        for j in range(LRU_SCAN_UNROLL):
            t = i * LRU_SCAN_UNROLL + j
            hf, pf = step(coef_f, state_f, t, hf, pf)
            hb, pb = step(coef_b, state_b, seg - 1 - t, hb, pb)
        return hf, pf, hb, pb

    lax.fori_loop(0, seg // LRU_SCAN_UNROLL, body, (zero, one, zero, one))


def _lru_carries(n_rows, h_in_f, h_in_b, af, uf, ab, ub):
    seg = n_rows // N_SEG
    pitch = seg + SEG_PAD
    cf, cb = [h_in_f], [h_in_b]
    for s in range(N_SEG):
        last = s * pitch + seg - 1
        cf.append(uf[last:last + 1, :] + af[last:last + 1, :] * cf[-1])
        first = (N_SEG - 1 - s) * pitch
        cb.append(ub[first:first + 1, :] + ab[first:first + 1, :] * cb[-1])
    return cf[:N_SEG], cb[:N_SEG][::-1], cf[N_SEG], cb[N_SEG]


def _lru_emit(n_rows, cf, cb, af, uf, ab, ub, g_ref, o_ref):
    seg = n_rows // N_SEG
    pitch = seg + SEG_PAD
    for s in range(N_SEG):
        src = slice(s * pitch, s * pitch + seg)
        dst = slice(s * seg, (s + 1) * seg)
        y = (uf[src, :] + af[src, :] * cf[s]) + (ub[src, :] + ab[src, :] * cb[s])
        o_ref[dst, :] = (y * _gelu_tanh(g_ref[dst, :])).astype(o_ref.dtype)


def _lru_kernel(ctx_out, x_ref, g_ref, xc_ref, *refs):
    refs = list(refs)
    gc_ref = refs.pop(0) if ctx_out else None
    cw_ref, cb_ref, w4_ref, b4_ref, lam_ref, o_ref = refs[:6]
    refs = refs[6:]
    oc_ref = refs.pop(0) if ctx_out else None
    xp_ref, af, uf, ab, ub, pf, sf, pb, sb = refs

    lam = lam_ref[...]
    z = -lam
    sp = jnp.maximum(z, 0.0) + jnp.log1p(jnp.exp(-jnp.abs(z)))
    w4 = w4_ref[...].astype(BF16)
    zeros_halo = jnp.zeros((HALO, LRU_BW), F32)
    h0 = jnp.zeros((1, LRU_BW), F32)

    xp_ref[0:HALO, :] = zeros_halo
    xp_ref[HALO:HALO + CTX_LEN, :] = xc_ref[...]
    xp_ref[HALO + CTX_LEN:2 * HALO + CTX_LEN, :] = zeros_halo
    _lru_coeffs(xp_ref, CTX_LEN, cw_ref, cb_ref, w4, b4_ref, sp, (af, ab), (uf, ub))
    _lru_local_scan(CTX_LEN, (af, uf), (ab, ub), (pf, sf), (pb, sb))
    cf, cb, hf, hb = _lru_carries(CTX_LEN, h0, h0, pf, sf, pb, sb)
    if ctx_out:
        _lru_emit(CTX_LEN, cf, cb, pf, sf, pb, sb, gc_ref, oc_ref)

    xp_ref[HALO:HALO + SEQ, :] = x_ref[...]
    xp_ref[HALO + SEQ:2 * HALO + SEQ, :] = zeros_halo
    _lru_coeffs(xp_ref, SEQ, cw_ref, cb_ref, w4, b4_ref, sp, (af, ab), (uf, ub))
    _lru_local_scan(SEQ, (af, uf), (ab, ub), (pf, sf), (pb, sb))
    cf, cb, _, _ = _lru_carries(SEQ, hf, hb, pf, sf, pb, sb)
    _lru_emit(SEQ, cf, cb, pf, sf, pb, sb, g_ref, o_ref)


def _lru_call(ctx_out, xl, gl, xc, gc, conv_w_l, conv_b_l, w4, b4, lam_l):
    lat = pl.BlockSpec((SEQ, LRU_BW), lambda b, j: (b, j))
    cx = pl.BlockSpec((CTX_LEN, LRU_BW), lambda b, j: (b, j))
    in_specs = [lat, lat, cx] + ([cx] if ctx_out else []) + [
        pl.BlockSpec((CONV_W, LRU_BW), lambda b, j: (0, j)),
        pl.BlockSpec((1, LRU_BW), lambda b, j: (0, j)),
        pl.BlockSpec((None, LRU_BW, 4 * LRU_BW), lambda b, j: (j, 0, 0)),
        pl.BlockSpec((None, 1, 4 * LRU_BW), lambda b, j: (j, 0, 0)),
        pl.BlockSpec((2, LRU_BW), lambda b, j: (0, j)),
    ]
    out_specs = [lat] + ([cx] if ctx_out else [])
    out_shape = [jax.ShapeDtypeStruct((BATCH * SEQ, LRU_W), BF16)]
    if ctx_out:
        out_shape.append(jax.ShapeDtypeStruct((BATCH * CTX_LEN, LRU_W), BF16))
    seq_bytes = SEQ * LRU_BW * 4
    args = [xl, gl, xc] + ([gc] if ctx_out else []) + [conv_w_l, conv_b_l, w4, b4, lam_l]
    res = pl.pallas_call(
        functools.partial(_lru_kernel, ctx_out),
        grid=(BATCH, LRU_BLOCKS),
        in_specs=in_specs,
        out_specs=out_specs,
        out_shape=out_shape,
        scratch_shapes=([pltpu.VMEM((SEQ + 2 * HALO, LRU_BW), F32)]
                        + [pltpu.VMEM((SEQ + N_SEG * SEG_PAD, LRU_BW), F32)] * 8),
        compiler_params=_params(("arbitrary", "arbitrary"), 15 * seq_bytes + (8 << 20)),
        name="rglru",
    )(*args)
    return (res[0], res[1]) if ctx_out else (res[0], None)


def _fourier_kernel(n_pos, blk, f_ref, ch_ref, sh_ref, cc_ref, sc_ref, rev_ref, alt_ref, w_ref, b_ref, o_ref,
                    ec_ref, es_ref, mir_ref, mid_ref):
    half = n_pos // 2
    n_lo = half // blk
    step = pl.program_id(1)
    norm = 1.0 / math.sqrt(n_pos * FNET_GW)
    w = w_ref[...].astype(BF16)

    def linear(y):
        return (_dot(y.astype(BF16), w) + b_ref[...]).astype(o_ref.dtype)

    @pl.when(step == 0)
    def _():
        for i in range(n_lo):
            lo = blk * (2 * n_lo - 1 - i)
            if i == 0:
                mirrored = _dot(rev_ref[:, 0:blk], f_ref[lo:lo + blk, :])
            else:
                mirrored = _dot(rev_ref[...], f_ref[lo:lo + 2 * blk, :])
            rows = slice(i * blk, (i + 1) * blk)
            x = f_ref[rows, :].astype(F32)
            even = (x + mirrored).astype(BF16)
            odd = (x - mirrored).astype(BF16)
            for g in range(FNET_GROUPS):
                sl = slice(g * FNET_GW, (g + 1) * FNET_GW)
                ec_ref[rows, sl] = _dot(even[:, sl], cc_ref[...]).astype(BF16)
                es_ref[rows, sl] = _dot(odd[:, sl], sc_ref[...]).astype(BF16)
        for g in range(FNET_GROUPS):
            sl = slice(g * FNET_GW, (g + 1) * FNET_GW)
            mid_ref[0:BF16_ROWS, sl] = _dot(f_ref[half:half + BF16_ROWS, sl], cc_ref[...])
        mid_ref[BF16_ROWS:2 * BF16_ROWS, :] = _dot(alt_ref[...], ec_ref[...])

    mid = mid_ref[0:1, :]

    @pl.when(step < n_lo)
    def _():
        a = _dot(ch_ref[...], ec_ref[...])
        b = _dot(sh_ref[...], es_ref[...])
        row = lax.broadcasted_iota(jnp.int32, (blk, 1), 0)
        base = jnp.where(row % 2 == 0, 1.0, -1.0) * mid
        o_ref[...] = linear((a - b + base) * norm)
        r0 = pl.multiple_of(step * blk, blk)
        mir_ref[pl.ds(r0, blk), :] = ((a + b + base) * norm).astype(BF16)

    @pl.when(step == n_lo)
    def _():
        y = _dot(rev_ref[:, 0:blk], mir_ref[(n_lo - 1) * blk:n_lo * blk, :])
        nyquist = (mid_ref[BF16_ROWS:BF16_ROWS + 1, :] + mid) * norm
        row = lax.broadcasted_iota(jnp.int32, (blk, 1), 0)
        o_ref[...] = linear(jnp.where(row == 0, nyquist, y))

    if n_lo > 1:
        @pl.when(step > n_lo)
        def _():
            r0 = pl.multiple_of((2 * n_lo - 1 - step) * blk, blk)
            o_ref[...] = linear(_dot(rev_ref[...], mir_ref[pl.ds(r0, 2 * blk), :]))


DFT_SPLIT = 64


def _dft_matrices(n, size):
    t = np.arange(size, dtype=np.int64)

    def table(k):
        ang = (2.0 * np.pi / n) * ((k[:, None] * t[None, :]) % n).astype(np.float64)
        return jnp.asarray(np.cos(ang), F32), jnp.asarray(np.sin(ang), F32)

    if size <= DFT_SPLIT:
        c, s = table(t)
        return c.astype(BF16), s.astype(BF16)
    c1, s1 = table(DFT_SPLIT * np.arange(size // DFT_SPLIT, dtype=np.int64))
    c2, s2 = table(np.arange(DFT_SPLIT, dtype=np.int64))
    c = c1[:, None, :] * c2[None, :, :] - s1[:, None, :] * s2[None, :, :]
    s = s1[:, None, :] * c2[None, :, :] + c1[:, None, :] * s2[None, :, :]
    return c.reshape(size, size).astype(BF16), s.reshape(size, size).astype(BF16)


FOURIER_BLK = 256


def _fourier_call(f2d, n_pos, fno_w_l, fno_b_l):
    half = n_pos // 2
    blk = min(FOURIER_BLK, half)
    steps = n_pos // blk
    n_lo = half // blk
    ch, sh = _dft_matrices(n_pos, half)
    cc, sc = _dft_matrices(FNET_GW, FNET_GW)
    rev = np.zeros((blk, 2 * blk), np.float32)
    rev[np.arange(1, blk), blk - np.arange(1, blk)] = 1.0
    rev[0, blk] = 1.0
    alt = np.zeros((BF16_ROWS, half), np.float32)
    alt[0] = 1.0 - 2.0 * (np.arange(half) % 2)
    const = lambda b, k: (0, 0)
    dft_tile = lambda b, k: (jnp.minimum(k, n_lo - 1), 0)
    nbytes = (2 * n_pos * FNET_W * 2 + 2 * 2 * blk * half * 2 + 3 * half * FNET_W * 2 + 2 * FNET_W * FNET_W * 4
              + 8 * blk * FNET_W * 4 + 2 * blk * 2 * blk * 2)
    return pl.pallas_call(
        functools.partial(_fourier_kernel, n_pos, blk),
        grid=(BATCH, steps),
        in_specs=[
            pl.BlockSpec((n_pos, FNET_W), lambda b, k: (b, 0)),
            pl.BlockSpec((blk, half), dft_tile),
            pl.BlockSpec((blk, half), dft_tile),
            pl.BlockSpec((FNET_GW, FNET_GW), const),
            pl.BlockSpec((FNET_GW, FNET_GW), const),
            pl.BlockSpec((blk, 2 * blk), const),
            pl.BlockSpec((BF16_ROWS, half), const),
            pl.BlockSpec((FNET_W, FNET_W), const),
            pl.BlockSpec((1, FNET_W), const),
        ],
        out_specs=pl.BlockSpec((blk, FNET_W), lambda b, k: (b * steps + k, 0)),
        out_shape=jax.ShapeDtypeStruct((BATCH * n_pos, FNET_W), BF16),
        scratch_shapes=[pltpu.VMEM((half, FNET_W), BF16), pltpu.VMEM((half, FNET_W), BF16),
                        pltpu.VMEM((half, FNET_W), BF16), pltpu.VMEM((2 * BF16_ROWS, FNET_W), F32)],
        compiler_params=_params(("arbitrary", "arbitrary"), nbytes),
        name="fourier_mix",
    )(f2d, ch, sh, cc, sc, jnp.asarray(rev, BF16), jnp.asarray(alt, BF16), fno_w_l, fno_b_l)


def _outproj_kernel(na_ref, lru_ref, f_ref, res_ref, mod_ref, w_ref, g_ref, b_ref, o_ref):
    gate = mod_ref[2:3, :]
    for rows in _row_tiles(res_ref.shape[0], MM_SUB):
        y = (_dot(na_ref[rows, :], w_ref[0:NA_W, :])
             + _dot(lru_ref[rows, :], w_ref[NA_W:NA_W + LRU_W, :])
             + _dot(f_ref[rows, :], w_ref[NA_W + LRU_W:D_MODEL, :]))
        _residual_ln_store(res_ref, y, gate, g_ref[...], b_ref[...], o_ref, rows)


def _outproj_call(na, lru, f, res, mod4, mod_row, w_out, layer, ln_g, ln_b):
    m_rows = res.shape[0]
    tm = 512
    row = lambda m: (m, 0)
    nbytes = (2 * tm * D_MODEL * 2 + 2 * 2 * tm * D_MODEL * 4 + D_MODEL * D_MODEL * 2 + 2 * MM_SUB * D_MODEL * 4
              + 4 * ROW_CHUNK * D_MODEL * 4)
    return pl.pallas_call(
        _outproj_kernel,
        grid=(m_rows // tm,),
        in_specs=[
            pl.BlockSpec((tm, NA_W), row),
            pl.BlockSpec((tm, LRU_W), row),
            pl.BlockSpec((tm, FNET_W), row),
            pl.BlockSpec((tm, D_MODEL), row),
            pl.BlockSpec((None, None, 6, D_MODEL), lambda m: (layer, mod_row(m * tm), 0, 0)),
            pl.BlockSpec((None, D_MODEL, D_MODEL), lambda m: (layer, 0, 0), pipeline_mode=pl.Buffered(1)),
            pl.BlockSpec((None, 1, D_MODEL), lambda m: (layer, 0, 0)),
            pl.BlockSpec((None, 1, D_MODEL), lambda m: (layer, 0, 0)),
        ],
        out_specs=pl.BlockSpec((tm, D_MODEL), row),
        out_shape=jax.ShapeDtypeStruct((m_rows, D_MODEL), F32),
        compiler_params=_params(("arbitrary",), nbytes),
        name="outproj_residual",
    )(na, lru, f, res, mod4, w_out, ln_g, ln_b)


MLP_TF = 512
MLP_TILES = D_FF // MLP_TF


def _mlp_kernel(x_ref, mod_ref, w1_ref, b1_ref, w2_ref, b2_ref, g_ref, b_ref, o_ref, v_ref, h0_ref, h1_ref):
    j = pl.program_id(1)
    tm = x_ref.shape[0]

    def up(h_out, rows=slice(None)):
        h = _dot(v_ref[rows, :], w1_ref[...]) + b1_ref[...]
        h_out[rows, :] = jnp.square(jnp.maximum(h, 0.0)).astype(BF16)

    def down(h_in, first):
        for c in range(D_MODEL // COL_TILE):
            sl = slice(c * COL_TILE, (c + 1) * COL_TILE)
            part = _dot(h_in[...], w2_ref[:, sl])
            if first:
                o_ref[:, sl] = part
            else:
                o_ref[:, sl] += part

    @pl.when(j == 0)
    def _():
        shift = mod_ref[3:4, :]
        scale1 = 1.0 + mod_ref[4:5, :]
        for rows in _row_tiles(tm, MM_SUB):
            _ln_mod_rows(x_ref, shift, scale1, v_ref, rows)
            up(h0_ref, rows)

    @pl.when(j == 1)
    def _():
        down(h0_ref, True)
        up(h1_ref)

    @pl.when((j > 1) & (j < MLP_TILES) & (j % 2 == 0))
    def _():
        down(h1_ref, False)
        up(h0_ref)

    @pl.when((j > 1) & (j < MLP_TILES) & (j % 2 == 1))
    def _():
        down(h0_ref, False)
        up(h1_ref)

    @pl.when(j == MLP_TILES)
    def _():
        h_last = h1_ref if (MLP_TILES - 1) % 2 else h0_ref
        gate = mod_ref[5:6, :]
        for rows in _row_tiles(tm, MM_SUB):
            y = o_ref[rows, :] + _dot(h_last[rows, :], w2_ref[...]) + b2_ref[...]
            _residual_ln_store(x_ref, y, gate, g_ref[...], b_ref[...], o_ref, rows)


def _mlp_call(x1, mod4, mod_row, tm, w1, b1, w2, b2, layer, ln_g, ln_b):
    m_rows = x1.shape[0]
    row = lambda m, j: (m, 0)
    vec = lambda m, j: (layer, 0, 0)
    nbytes = (4 * tm * D_MODEL * 4 + tm * D_MODEL * 2 + 2 * 2 * D_MODEL * MLP_TF * 2 + 2 * tm * MLP_TF * 2
              + tm * MLP_TF * 4 + tm * COL_TILE * 4 + 2 * MM_SUB * D_MODEL * 4 + 4 * ROW_CHUNK * D_MODEL * 4)
    return pl.pallas_call(
        _mlp_kernel,
        grid=(m_rows // tm, MLP_TILES + 1),
        in_specs=[
            pl.BlockSpec((tm, D_MODEL), row),
            pl.BlockSpec((None, None, 6, D_MODEL), lambda m, j: (layer, mod_row(m * tm), 0, 0)),
            pl.BlockSpec((None, D_MODEL, MLP_TF), lambda m, j: (layer, 0, jnp.minimum(j, MLP_TILES - 1))),
            pl.BlockSpec((None, 1, MLP_TF), lambda m, j: (layer, 0, jnp.minimum(j, MLP_TILES - 1))),
            pl.BlockSpec((None, MLP_TF, D_MODEL), lambda m, j: (layer, jnp.maximum(j - 1, 0), 0)),
            pl.BlockSpec((None, 1, D_MODEL), vec),
            pl.BlockSpec((None, 1, D_MODEL), vec),
            pl.BlockSpec((None, 1, D_MODEL), vec),
        ],
        out_specs=pl.BlockSpec((tm, D_MODEL), row),
        out_shape=jax.ShapeDtypeStruct((m_rows, D_MODEL), F32),
        scratch_shapes=[pltpu.VMEM((tm, D_MODEL), BF16), pltpu.VMEM((tm, MLP_TF), BF16),
                        pltpu.VMEM((tm, MLP_TF), BF16)],
        compiler_params=_params(("arbitrary", "arbitrary"), nbytes),
        name="mlp_residual",
    )(x1, mod4, w1, b1, w2, b2, ln_g, ln_b)


def kernel(x, c, ctx, c_ctx, w_mod, b_mod, w_in, rpb, conv_w, conv_b, lru_wa, lru_ba, lru_wx, lru_bx, lru_lambda,
           fno_w, fno_b, w_out, ln1_g, ln1_b, w_fc1, b_fc1, w_fc2, b_fc2, ln2_g, ln2_b):
    xl = x.reshape(BATCH * SEQ, D_MODEL)
    xc = ctx.reshape(BATCH * CTX_LEN, D_MODEL)
    s_in = jnp.concatenate([c, c_ctx[None], jnp.zeros((MOD_ROWS - BATCH - 1, D_MODEL), F32)], 0)
    mod4 = _mod_call(s_in, w_mod, b_mod).reshape(DEPTH, MOD_ROWS, 6, D_MODEL)
    cos_t, sin_t = _rope_tables()
    bias_tab = _attn_bias_table(rpb)
    w_in, w_out, w_fc1, w_fc2 = (w.astype(BF16) for w in (w_in, w_out, w_fc1, w_fc2))
    vec3 = lambda a: a.reshape(DEPTH, 1, a.shape[-1])
    ln1_g3, ln1_b3, ln2_g3, ln2_b3 = vec3(ln1_g), vec3(ln1_b), vec3(ln2_g), vec3(ln2_b)
    b_fc1_3, b_fc2_3 = vec3(b_fc1), vec3(b_fc2)
    lat_row = lambda r0: r0 // SEQ
    ctx_row = lambda r0: CTX_MOD_ROW

    for layer in range(DEPTH):
        ctx_out = layer < DEPTH - 1
        q, qr, k, v, xo, go, f = _inproj_lat_call(xl, mod4, w_in, layer, cos_t, sin_t)
        if ctx_out:
            qc, kc, vc, xoc, goc, fc = _inproj_ctx_call(xc, mod4, w_in, layer, True)
        else:
            kc, vc, xoc = _inproj_ctx_call(xc, mod4, w_in, layer, False)
            goc = None

        na = _attn_call(q, qr, k, v, kc, vc, bias_tab, layer)

        w4 = jnp.concatenate([lru_wa[layer, 0], lru_wx[layer, 0], lru_wa[layer, 1], lru_wx[layer, 1]], -1)
        blk = lambda a: a.reshape(LRU_BLOCKS, 1, LRU_BW)
        b4 = jnp.concatenate([blk(lru_ba[layer, 0]), blk(lru_bx[layer, 0]),
                              blk(lru_ba[layer, 1]), blk(lru_bx[layer, 1])], -1)
        lru, lru_c = _lru_call(ctx_out, xo, go, xoc, goc, conv_w[layer], conv_b[layer][None], w4, b4,
                               lru_lambda[layer])

        fm = _fourier_call(f, SEQ, fno_w[layer], fno_b[layer][None])
        x1 = _outproj_call(na, lru, fm, xl, mod4, lat_row, w_out, layer, ln1_g3, ln1_b3)
        xl = _mlp_call(x1, mod4, lat_row, 1024, w_fc1, b_fc1_3, w_fc2, b_fc2_3, layer, ln2_g3, ln2_b3)

        if ctx_out:
            na_c = _ctx_attn_call(qc, kc, vc)
            fm_c = _fourier_call(fc, CTX_LEN, fno_w[layer], fno_b[layer][None])
            c1 = _outproj_call(na_c, lru_c, fm_c, xc, mod4, ctx_row, w_out, layer, ln1_g3, ln1_b3)
            xc = _mlp_call(c1, mod4, ctx_row, 512, w_fc1, b_fc1_3, w_fc2, b_fc2_3, layer, ln2_g3, ln2_b3)

    return xl.reshape(BATCH, SEQ, D_MODEL)
```

```python
import functools
import math

import jax
import jax.numpy as jnp
import numpy as np
from jax import lax
from jax.experimental import pallas as pl
from jax.experimental.pallas import tpu as pltpu

F32 = jnp.float32
BF16 = jnp.bfloat16

D_MODEL = 2048
BATCH = 2
SEQ = 4096
DEPTH = 2
GRID_W = 64
GRID_H = SEQ // GRID_W
CTX_LEN = 256
HEAD_DIM = 128
NA_W = D_MODEL // 2
NA_HEADS = NA_W // HEAD_DIM
WIN_H = 8
WIN_W = 16
LRU_W = D_MODEL // 4
LRU_BLOCKS = 4
LRU_BW = LRU_W // LRU_BLOCKS
CONV_W = 4
LRU_C = 8.0
FNET_W = D_MODEL // 4
FNET_GROUPS = 4
FNET_GW = FNET_W // FNET_GROUPS
IN_W = 3 * NA_W + 2 * LRU_W + FNET_W
D_FF = 4 * D_MODEL
ROPE_THETA = 10000.0
LN_EPS = 1e-5
NEG_INF = -1e30
ALPHA = (2.0 * DEPTH) ** 0.25
ATTN_SCALE = HEAD_DIM ** -0.5

V7X_LANES = 128
V7X_SUBLANES = 8
BF16_ROWS = 2 * V7X_SUBLANES
V7X_VMEM_BYTES = 64 * 1024 * 1024
VMEM_CEILING = V7X_VMEM_BYTES - 6 * 1024 * 1024

COL_TILE = 512
N_COL_TILES = IN_W // COL_TILE
ROW_CHUNK = 128
MOD_ROWS = 8
CTX_MOD_ROW = BATCH


def _vmem_limit(nbytes):
    return int(min(VMEM_CEILING, nbytes * 5 // 4 + (4 << 20)))


def _params(semantics, nbytes):
    return pltpu.CompilerParams(dimension_semantics=semantics, vmem_limit_bytes=_vmem_limit(nbytes))


def _ln(x):
    mu = jnp.mean(x, axis=-1, keepdims=True)
    xc = x - mu
    var = jnp.mean(xc * xc, axis=-1, keepdims=True)
    return xc * lax.rsqrt(var + LN_EPS)


def _sigmoid(x):
    return 1.0 / (1.0 + jnp.exp(-x))


def _gelu_tanh(x):
    return 0.5 * x * (1.0 + jnp.tanh(math.sqrt(2.0 / math.pi) * (x + 0.044715 * (x * x * x))))


def _dot(a, b):
    return jnp.dot(a, b, preferred_element_type=F32)


def _dot_nt(a, b):
    return lax.dot_general(a, b, (((1,), (1,)), ((), ())), preferred_element_type=F32)


def _rider(w, layer, axis):
    return (w, layer, axis)


def _rider_specs(riders, n_steps, step_of):
    in_specs, out_specs, out_shapes, nbytes = [], [], [], 0
    for w, layer, axis in riders:
        rows, cols = w.shape[1:]
        if axis == 0:
            blk = (rows // n_steps, cols)
            in_idx = lambda *g, layer=layer: (layer, step_of(*g), 0)
            out_idx = lambda *g: (step_of(*g), 0)
        else:
            blk = (rows, cols // n_steps)
            in_idx = lambda *g, layer=layer: (layer, 0, step_of(*g))
            out_idx = lambda *g: (0, step_of(*g))
        in_specs.append(pl.BlockSpec((None,) + blk, in_idx))
        out_specs.append(pl.BlockSpec(blk, out_idx))
        out_shapes.append(jax.ShapeDtypeStruct((rows, cols), BF16))
        nbytes += 2 * blk[0] * blk[1] * (4 + 2)
    return in_specs, out_specs, out_shapes, nbytes


def _run_riders(in_refs, out_refs):
    for src, dst in zip(in_refs, out_refs):
        dst[...] = src[...].astype(dst.dtype)


MOD_TN = 1024


def _mod_kernel(s_ref, w_ref, b_ref, o_ref):
    s = s_ref[...]
    s = s * _sigmoid(s)
    o_ref[...] = _dot(s.astype(BF16), w_ref[...].astype(BF16)) + b_ref[...]


def _mod_call(s_in, w_mod, b_mod):
    n_out = w_mod.shape[-1]
    nbytes = 2 * (D_MODEL * MOD_TN * 4) + D_MODEL * MOD_TN * 2 + 4 * MOD_ROWS * n_out
    return pl.pallas_call(
        _mod_kernel,
        grid=(DEPTH, n_out // MOD_TN),
        in_specs=[
            pl.BlockSpec((MOD_ROWS, D_MODEL), lambda l, n: (0, 0)),
            pl.BlockSpec((None, D_MODEL, MOD_TN), lambda l, n: (l, 0, n)),
            pl.BlockSpec((None, 1, MOD_TN), lambda l, n: (l, 0, n)),
        ],
        out_specs=pl.BlockSpec((None, MOD_ROWS, MOD_TN), lambda l, n: (l, 0, n)),
        out_shape=jax.ShapeDtypeStruct((DEPTH, MOD_ROWS, n_out), F32),
        compiler_params=_params(("arbitrary", "arbitrary"), nbytes),
        name="modulation",
    )(s_in, w_mod, b_mod.reshape(DEPTH, 1, n_out))


MM_SUB = 256


def _row_tiles(n_rows, size):
    size = min(size, n_rows)
    return [slice(r, r + size) for r in range(0, n_rows, size)]


def _ln_mod_rows(x_ref, shift, scale1, dst_ref, rows):
    for piece in _row_tiles(rows.stop - rows.start, ROW_CHUNK):
        sl = slice(rows.start + piece.start, rows.start + piece.stop)
        dst_ref[sl, :] = (_ln(x_ref[sl, :]) * scale1 + shift).astype(dst_ref.dtype)


def _residual_ln_store(res, y, gate, gain, bias, o_ref, rows):
    for piece in _row_tiles(rows.stop - rows.start, ROW_CHUNK):
        sl = slice(rows.start + piece.start, rows.start + piece.stop)
        z = ALPHA * res[sl, :] + gate * y[piece, :]
        o_ref[sl, :] = _ln(z) * gain + bias


def _rope(a, cos, sin):
    lane = lax.broadcasted_iota(jnp.int32, a.shape, 1)
    first = (lane % (HEAD_DIM // 2)) < (HEAD_DIM // 4)
    partner = jnp.where(first, pltpu.roll(a, HEAD_DIM - HEAD_DIM // 4, 1), pltpu.roll(a, HEAD_DIM // 4, 1))
    return a * cos + partner * sin


def _inproj_lat_kernel(x_ref, mod_ref, w_ref, cos_ref, sin_ref,
                       q_ref, qr_ref, k_ref, v_ref, xo_ref, go_ref, f_ref, xn_ref):
    shift = mod_ref[0:1, :]
    scale1 = 1.0 + mod_ref[1:2, :]

    def emit_q(rows, cols, acc):
        q_ref[rows, cols] = (acc * ATTN_SCALE).astype(q_ref.dtype)
        for h in range(COL_TILE // HEAD_DIM):
            sl = slice(h * HEAD_DIM, (h + 1) * HEAD_DIM)
            dst = slice(cols.start + sl.start, cols.start + sl.stop)
            rot = _rope(acc[:, sl], cos_ref[rows, :], sin_ref[rows, :])
            qr_ref[rows, dst] = (rot * ATTN_SCALE).astype(qr_ref.dtype)

    def emit_k(rows, cols, acc):
        for h in range(COL_TILE // HEAD_DIM):
            sl = slice(h * HEAD_DIM, (h + 1) * HEAD_DIM)
            dst = slice(cols.start + sl.start, cols.start + sl.stop)
            k_ref[rows, dst] = _rope(acc[:, sl], cos_ref[rows, :], sin_ref[rows, :]).astype(k_ref.dtype)

    def emit_to(ref):
        def emit(rows, cols, acc):
            ref[rows, cols] = acc.astype(ref.dtype)
        return emit

    half = [slice(0, COL_TILE), slice(COL_TILE, 2 * COL_TILE)]
    plan = ([(emit_q, c) for c in half] + [(emit_k, c) for c in half] + [(emit_to(v_ref), c) for c in half]
            + [(emit_to(xo_ref), half[0]), (emit_to(go_ref), half[0]), (emit_to(f_ref), half[0])])
    for rows in _row_tiles(x_ref.shape[0], MM_SUB):
        _ln_mod_rows(x_ref, shift, scale1, xn_ref, rows)
        for n, (emit, cols) in enumerate(plan):
            emit(rows, cols, _dot(xn_ref[rows, :], w_ref[:, n * COL_TILE:(n + 1) * COL_TILE]))


def _inproj_lat_call(x2d, mod4, w_in, layer, cos_t, sin_t):
    m_rows = x2d.shape[0]
    tm = 512
    tiles_per_seq = SEQ // tm
    row = lambda m: (m, 0)
    nbytes = (2 * tm * D_MODEL * 4 + tm * D_MODEL * 2 + D_MODEL * IN_W * 2 + 4 * tm * HEAD_DIM * 4
              + 2 * tm * (4 * NA_W * 2 + 2 * LRU_W * 4 + FNET_W * 2) + 6 * MM_SUB * COL_TILE * 4
              + 4 * ROW_CHUNK * D_MODEL * 4)
    bf = lambda w: jax.ShapeDtypeStruct((m_rows, w), BF16)
    ff = lambda w: jax.ShapeDtypeStruct((m_rows, w), F32)
    widths = [NA_W, NA_W, NA_W, NA_W, LRU_W, LRU_W, FNET_W]
    return pl.pallas_call(
        _inproj_lat_kernel,
        grid=(m_rows // tm,),
        in_specs=[
            pl.BlockSpec((tm, D_MODEL), row),
            pl.BlockSpec((None, None, 6, D_MODEL), lambda m: (layer, m // tiles_per_seq, 0, 0)),
            pl.BlockSpec((D_MODEL, IN_W), lambda m: (0, 0), pipeline_mode=pl.Buffered(1)),
            pl.BlockSpec((tm, HEAD_DIM), lambda m: (m % tiles_per_seq, 0)),
            pl.BlockSpec((tm, HEAD_DIM), lambda m: (m % tiles_per_seq, 0)),
        ],
        out_specs=[pl.BlockSpec((tm, w), row) for w in widths],
        out_shape=[bf(NA_W), bf(NA_W), bf(NA_W), bf(NA_W), ff(LRU_W), ff(LRU_W), bf(FNET_W)],
        scratch_shapes=[pltpu.VMEM((tm, D_MODEL), BF16)],
        compiler_params=_params(("arbitrary",), nbytes),
        name="inproj_latent",
    )(x2d, mod4, w_in, cos_t, sin_t)


def _inproj_ctx_kernel(tile_lo, with_q, with_gf, x_ref, mod_ref, w_ref, *refs):
    refs = list(refs)
    xn_ref = refs.pop()
    q_ref = refs.pop(0) if with_q else None
    k_ref, v_ref, xo_ref = refs[0], refs[1], refs[2]
    go_ref, f_ref = (refs[3], refs[4]) if with_gf else (None, None)
    n = pl.program_id(1) + tile_lo

    @pl.when(pl.program_id(1) == 0)
    def _():
        for rows in _row_tiles(x_ref.shape[0], MM_SUB):
            _ln_mod_rows(x_ref, mod_ref[0:1, :], 1.0 + mod_ref[1:2, :], xn_ref, rows)

    acc = _dot(xn_ref[...], w_ref[...])

    if with_q:
        @pl.when(n < 2)
        def _():
            q_ref[...] = (acc * ATTN_SCALE).astype(q_ref.dtype)

    @pl.when((n >= 2) & (n < 4))
    def _():
        k_ref[...] = acc.astype(k_ref.dtype)

    @pl.when((n >= 4) & (n < 6))
    def _():
        v_ref[...] = acc.astype(v_ref.dtype)

    @pl.when(n == 6)
    def _():
        xo_ref[...] = acc

    if with_gf:
        @pl.when(n == 7)
        def _():
            go_ref[...] = acc

        @pl.when(n == 8)
        def _():
            f_ref[...] = acc.astype(f_ref.dtype)


def _inproj_ctx_call(c2d, mod4, w_in, layer, full):
    m_rows = c2d.shape[0]
    tm = m_rows
    tile_lo, tile_hi = (0, N_COL_TILES) if full else (2, 7)

    def col(lo):
        return lambda m, n: (m, jnp.clip(n + tile_lo - lo, 0, 1))

    one = lambda m, n: (m, 0)
    bf = lambda w: jax.ShapeDtypeStruct((m_rows, w), BF16)
    ff = lambda w: jax.ShapeDtypeStruct((m_rows, w), F32)
    out_specs, out_shape = [], []
    if full:
        out_specs.append(pl.BlockSpec((tm, COL_TILE), col(0)))
        out_shape.append(bf(NA_W))
    out_specs += [pl.BlockSpec((tm, COL_TILE), col(2)), pl.BlockSpec((tm, COL_TILE), col(4)),
                  pl.BlockSpec((tm, COL_TILE), one)]
    out_shape += [bf(NA_W), bf(NA_W), ff(LRU_W)]
    if full:
        out_specs += [pl.BlockSpec((tm, COL_TILE), one), pl.BlockSpec((tm, COL_TILE), one)]
        out_shape += [ff(LRU_W), bf(FNET_W)]
    nbytes = (2 * tm * D_MODEL * 4 + tm * D_MODEL * 2 + 2 * D_MODEL * COL_TILE * 2
              + 2 * 6 * tm * COL_TILE * 4 + 3 * tm * COL_TILE * 4 + 4 * ROW_CHUNK * D_MODEL * 4)
    return pl.pallas_call(
        functools.partial(_inproj_ctx_kernel, tile_lo, full, full),
        grid=(1, tile_hi - tile_lo),
        in_specs=[
            pl.BlockSpec((tm, D_MODEL), lambda m, n: (m, 0)),
            pl.BlockSpec((None, None, 6, D_MODEL), lambda m, n: (layer, CTX_MOD_ROW, 0, 0)),
            pl.BlockSpec((D_MODEL, COL_TILE), lambda m, n: (0, n + tile_lo)),
        ],
        out_specs=out_specs,
        out_shape=out_shape,
        scratch_shapes=[pltpu.VMEM((tm, D_MODEL), BF16)],
        compiler_params=_params(("arbitrary", "arbitrary"), nbytes),
        name="inproj_context",
    )(c2d, mod4, w_in)


QBLK_ROWS = 4
KBLK_ROWS = 12
N_QBLK = GRID_H // QBLK_ROWS
QBLK = QBLK_ROWS * GRID_W
KBLK = KBLK_ROWS * GRID_W
KEY_TILE = V7X_LANES
KEY_TILES = KBLK // KEY_TILE
N_DR = 2 * WIN_H - 1
BIAS_BOTH = 0
BIAS_SECOND = BIAS_BOTH + N_DR - 1
BIAS_FIRST = BIAS_SECOND + N_DR
N_BIAS = BIAS_FIRST + N_DR


def _kblk_start(first_query_row):
    return int(np.clip(first_query_row - WIN_H // 2, 0, GRID_H - KBLK_ROWS))


def _bias_plan(first_query_row):
    plan = []
    for u in range(QBLK_ROWS):
        q_row = first_query_row + u
        row_start = int(np.clip(q_row - WIN_H // 2, 0, GRID_H - WIN_H))
        row = []
        for c in range(KEY_TILES):
            k_rows = [_kblk_start(first_query_row) + 2 * c + i for i in range(2)]
            inside = [row_start <= kr < row_start + WIN_H for kr in k_rows]
            dr = [kr - q_row + (WIN_H - 1) for kr in k_rows]
            if inside[0] and inside[1]:
                row.append(BIAS_BOTH + dr[0])
            elif inside[1]:
                row.append(BIAS_SECOND + dr[1])
            elif inside[0]:
                row.append(BIAS_FIRST + dr[0])
            else:
                row.append(None)
        plan.append(row)
    return plan


def _attn_kernel(n_riders, q_ref, qr_ref, k_ref, v_ref, kc_ref, vc_ref, bias_ref, *rest):
    o_ref = rest[n_riders]
    _run_riders(rest[:n_riders], rest[n_riders + 1:])
    kc = kc_ref[...]
    vc = vc_ref[...]
    zero_tile = jnp.zeros((GRID_W, KEY_TILE), BF16)

    def one_block(q0, k0, plan):
        s = _dot_nt(qr_ref[pl.ds(q0, QBLK), :], k_ref[pl.ds(k0, KBLK), :])
        sc = _dot_nt(q_ref[pl.ds(q0, QBLK), :], kc)
        p_rows, pc_rows, denoms = [], [], []
        for u in range(QBLK_ROWS):
            rows = slice(u * GRID_W, (u + 1) * GRID_W)
            band = {c: s[rows, c * KEY_TILE:(c + 1) * KEY_TILE] + bias_ref[idx]
                    for c, idx in enumerate(plan[u]) if idx is not None}
            ctx_tiles = [sc[rows, c * KEY_TILE:(c + 1) * KEY_TILE] for c in range(CTX_LEN // KEY_TILE)]
            tiles = list(band.values()) + ctx_tiles
            m = jnp.max(functools.reduce(jnp.maximum, tiles), axis=-1, keepdims=True)
            p_band = {c: jnp.exp(t - m) for c, t in band.items()}
            p_ctx = [jnp.exp(t - m) for t in ctx_tiles]
            total = functools.reduce(jnp.add, list(p_band.values()) + p_ctx)
            denoms.append(jnp.sum(total, axis=-1, keepdims=True))
            p_rows.append(jnp.concatenate(
                [p_band[c].astype(BF16) if c in p_band else zero_tile for c in range(KEY_TILES)], axis=1))
            pc_rows.append(jnp.concatenate([t.astype(BF16) for t in p_ctx], axis=1))
        p = jnp.concatenate(p_rows, axis=0)
        pc = jnp.concatenate(pc_rows, axis=0)
        o = _dot(p, v_ref[pl.ds(k0, KBLK), :]) + _dot(pc, vc)
        o_ref[pl.ds(q0, QBLK), :] = (o / jnp.concatenate(denoms, axis=0)).astype(o_ref.dtype)

    def static_block(b):
        first_row = b * QBLK_ROWS
        one_block(b * QBLK, _kblk_start(first_row) * GRID_W, _bias_plan(first_row))

    interior_plan = _bias_plan(QBLK_ROWS)

    def interior_block(b):
        q0 = pl.multiple_of(b * QBLK, QBLK)
        k0 = pl.multiple_of((b * QBLK_ROWS - WIN_H // 2) * GRID_W, GRID_W)
        one_block(q0, k0, interior_plan)

    def body(i, carry):
        interior_block(2 * i + 1)
        interior_block(2 * i + 2)
        return carry

    for b in range(N_QBLK):
        static_block(b)


def _attn_call(q, qr, k, v, kc, vc, bias, layer, riders):
    seq_blk = lambda b, h: (b, h)
    n_steps = BATCH * NA_HEADS
    step = lambda b, h: b * NA_HEADS + h
    r_in, r_out, r_shape, r_bytes = _rider_specs(riders, n_steps, step)
    nbytes = (2 * (5 * SEQ * HEAD_DIM * 2 + 2 * CTX_LEN * HEAD_DIM * 2 + N_BIAS * GRID_W * KEY_TILE * 4)
              + 8 * QBLK * (KBLK + CTX_LEN) * 4 + r_bytes)
    res = pl.pallas_call(
        functools.partial(_attn_kernel, len(riders)),
        grid=(BATCH, NA_HEADS),
        in_specs=[
            pl.BlockSpec((SEQ, HEAD_DIM), seq_blk),
            pl.BlockSpec((SEQ, HEAD_DIM), seq_blk),
            pl.BlockSpec((SEQ, HEAD_DIM), seq_blk),
            pl.BlockSpec((SEQ, HEAD_DIM), seq_blk),
            pl.BlockSpec((CTX_LEN, HEAD_DIM), seq_blk),
            pl.BlockSpec((CTX_LEN, HEAD_DIM), seq_blk),
            pl.BlockSpec((None, None, N_BIAS, GRID_W, KEY_TILE), lambda b, h: (layer, h, 0, 0, 0)),
        ] + r_in,
        out_specs=[pl.BlockSpec((SEQ, HEAD_DIM), seq_blk)] + r_out,
        out_shape=[jax.ShapeDtypeStruct((BATCH * SEQ, NA_W), BF16)] + r_shape,
        compiler_params=_params(("arbitrary", "arbitrary"), nbytes),
        name="neighbourhood_attention",
    )(q, qr, k, v, kc, vc, bias, *[r[0] for r in riders])
    return res[0], res[1:]


def _ctx_attn_kernel(q_ref, k_ref, v_ref, o_ref):
    s = _dot_nt(q_ref[...], k_ref[...])
    m = jnp.max(s, axis=-1, keepdims=True)
    p = jnp.exp(s - m)
    denom = jnp.sum(p, axis=-1, keepdims=True)
    o_ref[...] = (_dot(p.astype(BF16), v_ref[...]) / denom).astype(o_ref.dtype)


def _ctx_attn_call(q, k, v):
    blk = pl.BlockSpec((CTX_LEN, HEAD_DIM), lambda b, h: (b, h))
    return pl.pallas_call(
        _ctx_attn_kernel,
        grid=(BATCH, NA_HEADS),
        in_specs=[blk, blk, blk],
        out_specs=blk,
        out_shape=jax.ShapeDtypeStruct((BATCH * CTX_LEN, NA_W), BF16),
        compiler_params=_params(("arbitrary", "arbitrary"), 16 << 20),
        name="context_attention",
    )(q, k, v)


def _attn_bias_table(rpb):
    col = np.arange(GRID_W)
    col_start = np.clip(col - WIN_W // 2, 0, GRID_W - WIN_W)
    in_win = (col[None, :] >= col_start[:, None]) & (col[None, :] < col_start[:, None] + WIN_W)
    dc = np.clip(col[None, :] - col[:, None] + (WIN_W - 1), 0, 2 * WIN_W - 2)
    onehot = (in_win[None] & (dc[None] == np.arange(2 * WIN_W - 1)[:, None, None])).astype(np.float32)
    t = jnp.einsum('lhdj,jqk->lhdqk', rpb, jnp.asarray(onehot), precision=lax.Precision.HIGHEST)
    t = jnp.where(in_win[None, None, None], t, NEG_INF)
    masked = jnp.full_like(t, NEG_INF)
    both = jnp.concatenate([t[:, :, :N_DR - 1], t[:, :, 1:]], axis=-1)
    second = jnp.concatenate([masked, t], axis=-1)
    first = jnp.concatenate([t, masked], axis=-1)
    return jnp.concatenate([both, second, first], axis=2)


def _rope_tables():
    quarter = HEAD_DIM // 4
    inv = ROPE_THETA ** (-jnp.arange(quarter, dtype=F32) / quarter)
    t = jnp.arange(SEQ)
    ang_r = (t // GRID_W).astype(F32)[:, None] * inv
    ang_c = (t % GRID_W).astype(F32)[:, None] * inv
    cos = jnp.concatenate([jnp.cos(ang_r), jnp.cos(ang_r), jnp.cos(ang_c), jnp.cos(ang_c)], -1)
    sin = jnp.concatenate([-jnp.sin(ang_r), jnp.sin(ang_r), -jnp.sin(ang_c), jnp.sin(ang_c)], -1)
    return cos, sin


HALO = V7X_SUBLANES
N_SEG = V7X_SUBLANES
SEG_PAD = V7X_SUBLANES
LRU_SCAN_UNROLL = 8


def _sigmoid_tanh(x):
    return 0.5 * jnp.tanh(0.5 * x) + 0.5


def _lru_coeffs(xp_ref, n_rows, cw_ref, cb_ref, w4, b4_ref, sp, a_refs, u_refs):
    seg = n_rows // N_SEG
    pitch = seg + SEG_PAD
    for s in range(N_SEG):
        base = HALO + s * seg
        xc = cb_ref[...] + xp_ref[base - CONV_W // 2:base - CONV_W // 2 + seg, :] * cw_ref[0:1, :]
        for j in range(1, CONV_W):
            off = base - CONV_W // 2 + j
            xc = xc + xp_ref[off:off + seg, :] * cw_ref[j:j + 1, :]
        z = _dot(xc.astype(BF16), w4) + b4_ref[...]
        for d in range(2):
            r = _sigmoid_tanh(z[:, (2 * d) * LRU_BW:(2 * d + 1) * LRU_BW])
            i = _sigmoid_tanh(z[:, (2 * d + 1) * LRU_BW:(2 * d + 2) * LRU_BW])
            log_a = (-LRU_C) * r * sp[d:d + 1, :]
            a = jnp.exp(log_a)
            a_refs[d][s * pitch:s * pitch + seg, :] = a
            one_minus_a2 = -jnp.tanh(log_a) * (a * a + 1.0)
            u_refs[d][s * pitch:s * pitch + seg, :] = jnp.sqrt(one_minus_a2) * (i * xc)


def _lru_local_scan(n_rows, coef_f, coef_b, state_f, state_b):
    seg = n_rows // N_SEG
    pitch = seg + SEG_PAD
    zero = jnp.zeros((N_SEG, LRU_BW), F32)
    one = jnp.ones((N_SEG, LRU_BW), F32)

    def step(coef, state, row, h, p):
        rows = pl.ds(row, N_SEG, stride=pitch)
        a = coef[0][rows, :]
        h = a * h + coef[1][rows, :]
        p = p * a
        state[0][rows, :] = p
        state[1][rows, :] = h
        return h, p

    def body(i, carry):
        hf, pf, hb, pb = carry
        for j in range(LRU_SCAN_UNROLL):
            t = i * LRU_SCAN_UNROLL + j
            hf, pf = step(coef_f, state_f, t, hf, pf)
            hb, pb = step(coef_b, state_b, seg - 1 - t, hb, pb)
        return hf, pf, hb, pb

    lax.fori_loop(0, seg // LRU_SCAN_UNROLL, body, (zero, one, zero, one))


def _lru_carries(n_rows, h_in_f, h_in_b, af, uf, ab, ub):
    seg = n_rows // N_SEG
    pitch = seg + SEG_PAD
    cf, cb = [h_in_f], [h_in_b]
    for s in range(N_SEG):
        last = s * pitch + seg - 1
        cf.append(uf[last:last + 1, :] + af[last:last + 1, :] * cf[-1])
        first = (N_SEG - 1 - s) * pitch
        cb.append(ub[first:first + 1, :] + ab[first:first + 1, :] * cb[-1])
    return cf[:N_SEG], cb[:N_SEG][::-1], cf[N_SEG], cb[N_SEG]


def _lru_emit(n_rows, cf, cb, af, uf, ab, ub, g_ref, o_ref):
    seg = n_rows // N_SEG
    pitch = seg + SEG_PAD
    for s in range(N_SEG):
        src = slice(s * pitch, s * pitch + seg)
        dst = slice(s * seg, (s + 1) * seg)
        y = (uf[src, :] + af[src, :] * cf[s]) + (ub[src, :] + ab[src, :] * cb[s])
        o_ref[dst, :] = (y * _gelu_tanh(g_ref[dst, :])).astype(o_ref.dtype)


def _lru_kernel(ctx_out, n_riders, x_ref, g_ref, xc_ref, *refs):
    refs = list(refs)
    gc_ref = refs.pop(0) if ctx_out else None
    cw_ref, cb_ref, w4_ref, b4_ref, lam_ref = refs[:5]
    rider_in, refs = refs[5:5 + n_riders], refs[5 + n_riders:]
    o_ref = refs.pop(0)
    oc_ref = refs.pop(0) if ctx_out else None
    rider_out, refs = refs[:n_riders], refs[n_riders:]
    xp_ref, af, uf, ab, ub, pf, sf, pb, sb = refs
    _run_riders(rider_in, rider_out)

    lam = lam_ref[...]
    z = -lam
    sp = jnp.maximum(z, 0.0) + jnp.log1p(jnp.exp(-jnp.abs(z)))
    w4 = w4_ref[...].astype(BF16)
    zeros_halo = jnp.zeros((HALO, LRU_BW), F32)
    h0 = jnp.zeros((1, LRU_BW), F32)

    xp_ref[0:HALO, :] = zeros_halo
    xp_ref[HALO:HALO + CTX_LEN, :] = xc_ref[...]
    xp_ref[HALO + CTX_LEN:2 * HALO + CTX_LEN, :] = zeros_halo
    _lru_coeffs(xp_ref, CTX_LEN, cw_ref, cb_ref, w4, b4_ref, sp, (af, ab), (uf, ub))
    _lru_local_scan(CTX_LEN, (af, uf), (ab, ub), (pf, sf), (pb, sb))
    cf, cb, hf, hb = _lru_carries(CTX_LEN, h0, h0, pf, sf, pb, sb)
    if ctx_out:
        _lru_emit(CTX_LEN, cf, cb, pf, sf, pb, sb, gc_ref, oc_ref)

    xp_ref[HALO:HALO + SEQ, :] = x_ref[...]
    xp_ref[HALO + SEQ:2 * HALO + SEQ, :] = zeros_halo
    _lru_coeffs(xp_ref, SEQ, cw_ref, cb_ref, w4, b4_ref, sp, (af, ab), (uf, ub))
    _lru_local_scan(SEQ, (af, uf), (ab, ub), (pf, sf), (pb, sb))
    cf, cb, _, _ = _lru_carries(SEQ, hf, hb, pf, sf, pb, sb)
    _lru_emit(SEQ, cf, cb, pf, sf, pb, sb, g_ref, o_ref)


def _lru_call(ctx_out, xl, gl, xc, gc, conv_w_l, conv_b_l, w4, b4, lam_l, riders):
    lat = pl.BlockSpec((SEQ, LRU_BW), lambda b, j: (b, j))
    cx = pl.BlockSpec((CTX_LEN, LRU_BW), lambda b, j: (b, j))
    r_in, r_out, r_shape, r_bytes = _rider_specs(riders, BATCH * LRU_BLOCKS, lambda b, j: b * LRU_BLOCKS + j)
    in_specs = [lat, lat, cx] + ([cx] if ctx_out else []) + [
        pl.BlockSpec((CONV_W, LRU_BW), lambda b, j: (0, j)),
        pl.BlockSpec((1, LRU_BW), lambda b, j: (0, j)),
        pl.BlockSpec((None, LRU_BW, 4 * LRU_BW), lambda b, j: (j, 0, 0)),
        pl.BlockSpec((None, 1, 4 * LRU_BW), lambda b, j: (j, 0, 0)),
        pl.BlockSpec((2, LRU_BW), lambda b, j: (0, j)),
    ] + r_in
    out_specs = [lat] + ([cx] if ctx_out else []) + r_out
    out_shape = [jax.ShapeDtypeStruct((BATCH * SEQ, LRU_W), BF16)]
    if ctx_out:
        out_shape.append(jax.ShapeDtypeStruct((BATCH * CTX_LEN, LRU_W), BF16))
    out_shape += r_shape
    seq_bytes = SEQ * LRU_BW * 4
    args = ([xl, gl, xc] + ([gc] if ctx_out else []) + [conv_w_l, conv_b_l, w4, b4, lam_l]
            + [r[0] for r in riders])
    res = pl.pallas_call(
        functools.partial(_lru_kernel, ctx_out, len(riders)),
        grid=(BATCH, LRU_BLOCKS),
        in_specs=in_specs,
        out_specs=out_specs,
        out_shape=out_shape,
        scratch_shapes=([pltpu.VMEM((SEQ + 2 * HALO, LRU_BW), F32)]
                        + [pltpu.VMEM((SEQ + N_SEG * SEG_PAD, LRU_BW), F32)] * 8),
        compiler_params=_params(("arbitrary", "arbitrary"), 15 * seq_bytes + (8 << 20) + r_bytes),
        name="rglru",
    )(*args)
    n_main = 2 if ctx_out else 1
    return res[0], (res[1] if ctx_out else None), res[n_main:]


def _fourier_kernel(n_pos, blk, f_ref, ch_ref, sh_ref, cc_ref, sc_ref, rev_ref, alt_ref, w_ref, b_ref, o_ref,
                    ec_ref, es_ref, mir_ref, mid_ref):
    half = n_pos // 2
    n_lo = half // blk
    step = pl.program_id(1)
    norm = 1.0 / math.sqrt(n_pos * FNET_GW)
    w = w_ref[...].astype(BF16)

    def linear(y):
        return (_dot(y.astype(BF16), w) + b_ref[...]).astype(o_ref.dtype)

    @pl.when(step == 0)
    def _():
        for i in range(n_lo):
            lo = blk * (2 * n_lo - 1 - i)
            if i == 0:
                mirrored = _dot(rev_ref[:, 0:blk], f_ref[lo:lo + blk, :])
            else:
                mirrored = _dot(rev_ref[...], f_ref[lo:lo + 2 * blk, :])
            rows = slice(i * blk, (i + 1) * blk)
            x = f_ref[rows, :].astype(F32)
            even = (x + mirrored).astype(BF16)
            odd = (x - mirrored).astype(BF16)
            for g in range(FNET_GROUPS):
                sl = slice(g * FNET_GW, (g + 1) * FNET_GW)
                ec_ref[rows, sl] = _dot(even[:, sl], cc_ref[...]).astype(BF16)
                es_ref[rows, sl] = _dot(odd[:, sl], sc_ref[...]).astype(BF16)
        for g in range(FNET_GROUPS):
            sl = slice(g * FNET_GW, (g + 1) * FNET_GW)
            mid_ref[0:BF16_ROWS, sl] = _dot(f_ref[half:half + BF16_ROWS, sl], cc_ref[...])
        mid_ref[BF16_ROWS:2 * BF16_ROWS, :] = _dot(alt_ref[...], ec_ref[...])

    mid = mid_ref[0:1, :]

    @pl.when(step < n_lo)
    def _():
        a = _dot(ch_ref[...], ec_ref[...])
        b = _dot(sh_ref[...], es_ref[...])
        row = lax.broadcasted_iota(jnp.int32, (blk, 1), 0)
        base = jnp.where(row % 2 == 0, 1.0, -1.0) * mid
        o_ref[...] = linear((a - b + base) * norm)
        r0 = pl.multiple_of(step * blk, blk)
        mir_ref[pl.ds(r0, blk), :] = ((a + b + base) * norm).astype(BF16)

    @pl.when(step == n_lo)
    def _():
        y = _dot(rev_ref[:, 0:blk], mir_ref[(n_lo - 1) * blk:n_lo * blk, :])
        nyquist = (mid_ref[BF16_ROWS:BF16_ROWS + 1, :] + mid) * norm
        row = lax.broadcasted_iota(jnp.int32, (blk, 1), 0)
        o_ref[...] = linear(jnp.where(row == 0, nyquist, y))

    if n_lo > 1:
        @pl.when(step > n_lo)
        def _():
            r0 = pl.multiple_of((2 * n_lo - 1 - step) * blk, blk)
            o_ref[...] = linear(_dot(rev_ref[...], mir_ref[pl.ds(r0, 2 * blk), :]))


DFT_SPLIT = 64


def _dft_matrices(n, size):
    t = np.arange(size, dtype=np.int64)

    def table(k):
        ang = (2.0 * np.pi / n) * ((k[:, None] * t[None, :]) % n).astype(np.float64)
        return jnp.asarray(np.cos(ang), F32), jnp.asarray(np.sin(ang), F32)

    if size <= DFT_SPLIT:
        c, s = table(t)
        return c.astype(BF16), s.astype(BF16)
    c1, s1 = table(DFT_SPLIT * np.arange(size // DFT_SPLIT, dtype=np.int64))
    c2, s2 = table(np.arange(DFT_SPLIT, dtype=np.int64))
    c = c1[:, None, :] * c2[None, :, :] - s1[:, None, :] * s2[None, :, :]
    s = s1[:, None, :] * c2[None, :, :] + c1[:, None, :] * s2[None, :, :]
    return c.reshape(size, size).astype(BF16), s.reshape(size, size).astype(BF16)


FOURIER_BLK = 256


def _fourier_call(f2d, n_pos, fno_w_l, fno_b_l):
    half = n_pos // 2
    blk = min(FOURIER_BLK, half)
    steps = n_pos // blk
    n_lo = half // blk
    ch, sh = _dft_matrices(n_pos, half)
    cc, sc = _dft_matrices(FNET_GW, FNET_GW)
    rev = np.zeros((blk, 2 * blk), np.float32)
    rev[np.arange(1, blk), blk - np.arange(1, blk)] = 1.0
    rev[0, blk] = 1.0
    alt = np.zeros((BF16_ROWS, half), np.float32)
    alt[0] = 1.0 - 2.0 * (np.arange(half) % 2)
    const = lambda b, k: (0, 0)
    dft_tile = lambda b, k: (jnp.minimum(k, n_lo - 1), 0)
    nbytes = (2 * n_pos * FNET_W * 2 + 2 * 2 * blk * half * 2 + 3 * half * FNET_W * 2 + 2 * FNET_W * FNET_W * 4
              + 8 * blk * FNET_W * 4 + 2 * blk * 2 * blk * 2)
    return pl.pallas_call(
        functools.partial(_fourier_kernel, n_pos, blk),
        grid=(BATCH, steps),
        in_specs=[
            pl.BlockSpec((n_pos, FNET_W), lambda b, k: (b, 0)),
            pl.BlockSpec((blk, half), dft_tile),
            pl.BlockSpec((blk, half), dft_tile),
            pl.BlockSpec((FNET_GW, FNET_GW), const),
            pl.BlockSpec((FNET_GW, FNET_GW), const),
            pl.BlockSpec((blk, 2 * blk), const),
            pl.BlockSpec((BF16_ROWS, half), const),
            pl.BlockSpec((FNET_W, FNET_W), const),
            pl.BlockSpec((1, FNET_W), const),
        ],
        out_specs=pl.BlockSpec((blk, FNET_W), lambda b, k: (b * steps + k, 0)),
        out_shape=jax.ShapeDtypeStruct((BATCH * n_pos, FNET_W), BF16),
        scratch_shapes=[pltpu.VMEM((half, FNET_W), BF16), pltpu.VMEM((half, FNET_W), BF16),
                        pltpu.VMEM((half, FNET_W), BF16), pltpu.VMEM((2 * BF16_ROWS, FNET_W), F32)],
        compiler_params=_params(("arbitrary", "arbitrary"), nbytes),
        name="fourier_mix",
    )(f2d, ch, sh, cc, sc, jnp.asarray(rev, BF16), jnp.asarray(alt, BF16), fno_w_l, fno_b_l)


def _outproj_kernel(na_ref, lru_ref, f_ref, res_ref, mod_ref, w_ref, g_ref, b_ref, o_ref):
    gate = mod_ref[2:3, :]
    for rows in _row_tiles(res_ref.shape[0], MM_SUB):
        y = (_dot(na_ref[rows, :], w_ref[0:NA_W, :])
             + _dot(lru_ref[rows, :], w_ref[NA_W:NA_W + LRU_W, :])
             + _dot(f_ref[rows, :], w_ref[NA_W + LRU_W:D_MODEL, :]))
        _residual_ln_store(res_ref, y, gate, g_ref[...], b_ref[...], o_ref, rows)


def _outproj_call(na, lru, f, res, mod4, mod_row, w_out, layer, ln_g, ln_b):
    m_rows = res.shape[0]
    tm = 512
    row = lambda m: (m, 0)
    nbytes = (2 * tm * D_MODEL * 2 + 2 * 2 * tm * D_MODEL * 4 + D_MODEL * D_MODEL * 2 + 2 * MM_SUB * D_MODEL * 4
              + 4 * ROW_CHUNK * D_MODEL * 4)
    return pl.pallas_call(
        _outproj_kernel,
        grid=(m_rows // tm,),
        in_specs=[
            pl.BlockSpec((tm, NA_W), row),
            pl.BlockSpec((tm, LRU_W), row),
            pl.BlockSpec((tm, FNET_W), row),
            pl.BlockSpec((tm, D_MODEL), row),
            pl.BlockSpec((None, None, 6, D_MODEL), lambda m: (layer, mod_row(m * tm), 0, 0)),
            pl.BlockSpec((D_MODEL, D_MODEL), lambda m: (0, 0), pipeline_mode=pl.Buffered(1)),
            pl.BlockSpec((None, 1, D_MODEL), lambda m: (layer, 0, 0)),
            pl.BlockSpec((None, 1, D_MODEL), lambda m: (layer, 0, 0)),
        ],
        out_specs=pl.BlockSpec((tm, D_MODEL), row),
        out_shape=jax.ShapeDtypeStruct((m_rows, D_MODEL), F32),
        compiler_params=_params(("arbitrary",), nbytes),
        name="outproj_residual",
    )(na, lru, f, res, mod4, w_out, ln_g, ln_b)


MLP_TF = 512
MLP_TILES = D_FF // MLP_TF


def _mlp_kernel(x_ref, mod_ref, w1_ref, b1_ref, w2_ref, b2_ref, g_ref, b_ref, o_ref, v_ref, h0_ref, h1_ref):
    j = pl.program_id(1)
    tm = x_ref.shape[0]

    def up(h_out, rows=slice(None)):
        h = _dot(v_ref[rows, :], w1_ref[...]) + b1_ref[...]
        h_out[rows, :] = jnp.square(jnp.maximum(h, 0.0)).astype(BF16)

    def down(h_in, first):
        for c in range(D_MODEL // COL_TILE):
            sl = slice(c * COL_TILE, (c + 1) * COL_TILE)
            part = _dot(h_in[...], w2_ref[:, sl])
            if first:
                o_ref[:, sl] = part
            else:
                o_ref[:, sl] += part

    @pl.when(j == 0)
    def _():
        shift = mod_ref[3:4, :]
        scale1 = 1.0 + mod_ref[4:5, :]
        for rows in _row_tiles(tm, MM_SUB):
            _ln_mod_rows(x_ref, shift, scale1, v_ref, rows)
            up(h0_ref, rows)

    @pl.when(j == 1)
    def _():
        down(h0_ref, True)
        up(h1_ref)

    @pl.when((j > 1) & (j < MLP_TILES) & (j % 2 == 0))
    def _():
        down(h1_ref, False)
        up(h0_ref)

    @pl.when((j > 1) & (j < MLP_TILES) & (j % 2 == 1))
    def _():
        down(h0_ref, False)
        up(h1_ref)

    @pl.when(j == MLP_TILES)
    def _():
        h_last = h1_ref if (MLP_TILES - 1) % 2 else h0_ref
        gate = mod_ref[5:6, :]
        for rows in _row_tiles(tm, MM_SUB):
            y = o_ref[rows, :] + _dot(h_last[rows, :], w2_ref[...]) + b2_ref[...]
            _residual_ln_store(x_ref, y, gate, g_ref[...], b_ref[...], o_ref, rows)


def _mlp_call(x1, mod4, mod_row, tm, w1, b1, w2, b2, layer, ln_g, ln_b):
    m_rows = x1.shape[0]
    row = lambda m, j: (m, 0)
    vec = lambda m, j: (layer, 0, 0)
    nbytes = (4 * tm * D_MODEL * 4 + tm * D_MODEL * 2 + 2 * 2 * D_MODEL * MLP_TF * 2 + 2 * tm * MLP_TF * 2
              + tm * MLP_TF * 4 + tm * COL_TILE * 4 + 2 * MM_SUB * D_MODEL * 4 + 4 * ROW_CHUNK * D_MODEL * 4)
    return pl.pallas_call(
        _mlp_kernel,
        grid=(m_rows // tm, MLP_TILES + 1),
        in_specs=[
            pl.BlockSpec((tm, D_MODEL), row),
            pl.BlockSpec((None, None, 6, D_MODEL), lambda m, j: (layer, mod_row(m * tm), 0, 0)),
            pl.BlockSpec((D_MODEL, MLP_TF), lambda m, j: (0, jnp.minimum(j, MLP_TILES - 1))),
            pl.BlockSpec((None, 1, MLP_TF), lambda m, j: (layer, 0, jnp.minimum(j, MLP_TILES - 1))),
            pl.BlockSpec((MLP_TF, D_MODEL), lambda m, j: (jnp.maximum(j - 1, 0), 0)),
            pl.BlockSpec((None, 1, D_MODEL), vec),
            pl.BlockSpec((None, 1, D_MODEL), vec),
            pl.BlockSpec((None, 1, D_MODEL), vec),
        ],
        out_specs=pl.BlockSpec((tm, D_MODEL), row),
        out_shape=jax.ShapeDtypeStruct((m_rows, D_MODEL), F32),
        scratch_shapes=[pltpu.VMEM((tm, D_MODEL), BF16), pltpu.VMEM((tm, MLP_TF), BF16),
                        pltpu.VMEM((tm, MLP_TF), BF16)],
        compiler_params=_params(("arbitrary", "arbitrary"), nbytes),
        name="mlp_residual",
    )(x1, mod4, w1, b1, w2, b2, ln_g, ln_b)


def kernel(x, c, ctx, c_ctx, w_mod, b_mod, w_in, rpb, conv_w, conv_b, lru_wa, lru_ba, lru_wx, lru_bx, lru_lambda,
           fno_w, fno_b, w_out, ln1_g, ln1_b, w_fc1, b_fc1, w_fc2, b_fc2, ln2_g, ln2_b):
    xl = x.reshape(BATCH * SEQ, D_MODEL)
    xc = ctx.reshape(BATCH * CTX_LEN, D_MODEL)
    s_in = jnp.concatenate([c, c_ctx[None], jnp.zeros((MOD_ROWS - BATCH - 1, D_MODEL), F32)], 0)
    mod4 = _mod_call(s_in, w_mod, b_mod).reshape(DEPTH, MOD_ROWS, 6, D_MODEL)
    cos_t, sin_t = _rope_tables()
    bias_tab = _attn_bias_table(rpb)
    w_in_l = w_in[0].astype(BF16)
    vec3 = lambda a: a.reshape(DEPTH, 1, a.shape[-1])
    ln1_g3, ln1_b3, ln2_g3, ln2_b3 = vec3(ln1_g), vec3(ln1_b), vec3(ln2_g), vec3(ln2_b)
    b_fc1_3, b_fc2_3 = vec3(b_fc1), vec3(b_fc2)
    lat_row = lambda r0: r0 // SEQ
    ctx_row = lambda r0: CTX_MOD_ROW

    for layer in range(DEPTH):
        ctx_out = layer < DEPTH - 1
        q, qr, k, v, xo, go, f = _inproj_lat_call(xl, mod4, w_in_l, layer, cos_t, sin_t)
        if ctx_out:
            qc, kc, vc, xoc, goc, fc = _inproj_ctx_call(xc, mod4, w_in_l, layer, True)
        else:
            kc, vc, xoc = _inproj_ctx_call(xc, mod4, w_in_l, layer, False)
            goc = None

        na, (w_out_l, w_fc1_l, w_fc2_l) = _attn_call(
            q, qr, k, v, kc, vc, bias_tab, layer,
            [_rider(w_out, layer, 0), _rider(w_fc1, layer, 1), _rider(w_fc2, layer, 0)])

        w4 = jnp.concatenate([lru_wa[layer, 0], lru_wx[layer, 0], lru_wa[layer, 1], lru_wx[layer, 1]], -1)
        blk = lambda a: a.reshape(LRU_BLOCKS, 1, LRU_BW)
        b4 = jnp.concatenate([blk(lru_ba[layer, 0]), blk(lru_bx[layer, 0]),
                              blk(lru_ba[layer, 1]), blk(lru_bx[layer, 1])], -1)
        next_w_in = [_rider(w_in, layer + 1, 0)] if layer + 1 < DEPTH else []
        lru, lru_c, cast = _lru_call(ctx_out, xo, go, xoc, goc, conv_w[layer], conv_b[layer][None], w4, b4,
                                     lru_lambda[layer], next_w_in)
        if next_w_in:
            w_in_l = cast[0]

        fm = _fourier_call(f, SEQ, fno_w[layer], fno_b[layer][None])
        x1 = _outproj_call(na, lru, fm, xl, mod4, lat_row, w_out_l, layer, ln1_g3, ln1_b3)
        xl = _mlp_call(x1, mod4, lat_row, 1024, w_fc1_l, b_fc1_3, w_fc2_l, b_fc2_3, layer, ln2_g3, ln2_b3)

        if ctx_out:
            na_c = _ctx_attn_call(qc, kc, vc)
            fm_c = _fourier_call(fc, CTX_LEN, fno_w[layer], fno_b[layer][None])
            c1 = _outproj_call(na_c, lru_c, fm_c, xc, mod4, ctx_row, w_out_l, layer, ln1_g3, ln1_b3)
            xc = _mlp_call(c1, mod4, ctx_row, 512, w_fc1_l, b_fc1_3, w_fc2_l, b_fc2_3, layer, ln2_g3, ln2_b3)

    return xl.reshape(BATCH, SEQ, D_MODEL)
```

```python
import functools
import math

import jax
import jax.numpy as jnp
import numpy as np
from jax import lax
from jax.experimental import pallas as pl
from jax.experimental.pallas import tpu as pltpu

F32 = jnp.float32
BF16 = jnp.bfloat16

D_MODEL = 2048
BATCH = 2
SEQ = 4096
DEPTH = 2
GRID_W = 64
GRID_H = SEQ // GRID_W
CTX_LEN = 256
HEAD_DIM = 128
NA_W = D_MODEL // 2
NA_HEADS = NA_W // HEAD_DIM
WIN_H = 8
WIN_W = 16
LRU_W = D_MODEL // 4
LRU_BLOCKS = 4
LRU_BW = LRU_W // LRU_BLOCKS
CONV_W = 4
LRU_C = 8.0
FNET_W = D_MODEL // 4
FNET_GROUPS = 4
FNET_GW = FNET_W // FNET_GROUPS
IN_W = 3 * NA_W + 2 * LRU_W + FNET_W
D_FF = 4 * D_MODEL
ROPE_THETA = 10000.0
LN_EPS = 1e-5
NEG_INF = -1e30
ALPHA = (2.0 * DEPTH) ** 0.25
ATTN_SCALE = HEAD_DIM ** -0.5

V7X_LANES = 128
V7X_SUBLANES = 8
BF16_ROWS = 2 * V7X_SUBLANES
V7X_VMEM_BYTES = 64 * 1024 * 1024
VMEM_CEILING = V7X_VMEM_BYTES - 6 * 1024 * 1024

COL_TILE = 512
N_COL_TILES = IN_W // COL_TILE
ROW_CHUNK = 128
MOD_ROWS = 8
CTX_MOD_ROW = BATCH


def _vmem_limit(nbytes):
    return int(min(VMEM_CEILING, nbytes * 5 // 4 + (4 << 20)))


def _params(semantics, nbytes):
    return pltpu.CompilerParams(dimension_semantics=semantics, vmem_limit_bytes=_vmem_limit(nbytes))


def _ln(x):
    mu = jnp.mean(x, axis=-1, keepdims=True)
    xc = x - mu
    var = jnp.mean(xc * xc, axis=-1, keepdims=True)
    return xc * lax.rsqrt(var + LN_EPS)


def _sigmoid(x):
    return 1.0 / (1.0 + jnp.exp(-x))


def _gelu_tanh(x):
    return 0.5 * x * (1.0 + jnp.tanh(math.sqrt(2.0 / math.pi) * (x + 0.044715 * (x * x * x))))


def _dot(a, b):
    return jnp.dot(a, b, preferred_element_type=F32)


def _dot_nt(a, b):
    return lax.dot_general(a, b, (((1,), (1,)), ((), ())), preferred_element_type=F32)


def _rider(w, layer, axis):
    return (w, layer, axis)


def _rider_specs(riders, n_steps, step_of):
    in_specs, out_specs, out_shapes, nbytes = [], [], [], 0
    for w, layer, axis in riders:
        rows, cols = w.shape[1:]
        if axis == 0:
            blk = (rows // n_steps, cols)
            in_idx = lambda *g, layer=layer: (layer, step_of(*g), 0)
            out_idx = lambda *g: (step_of(*g), 0)
        else:
            blk = (rows, cols // n_steps)
            in_idx = lambda *g, layer=layer: (layer, 0, step_of(*g))
            out_idx = lambda *g: (0, step_of(*g))
        in_specs.append(pl.BlockSpec((None,) + blk, in_idx))
        out_specs.append(pl.BlockSpec(blk, out_idx))
        out_shapes.append(jax.ShapeDtypeStruct((rows, cols), BF16))
        nbytes += 2 * blk[0] * blk[1] * (4 + 2)
    return in_specs, out_specs, out_shapes, nbytes


def _run_riders(in_refs, out_refs):
    for src, dst in zip(in_refs, out_refs):
        dst[...] = src[...].astype(dst.dtype)


MOD_TN = 1024


def _mod_kernel(s_ref, w_ref, b_ref, o_ref):
    s = s_ref[...]
    s = s * _sigmoid(s)
    o_ref[...] = _dot(s.astype(BF16), w_ref[...].astype(BF16)) + b_ref[...]


def _mod_call(s_in, w_mod, b_mod):
    n_out = w_mod.shape[-1]
    nbytes = 2 * (D_MODEL * MOD_TN * 4) + D_MODEL * MOD_TN * 2 + 4 * MOD_ROWS * n_out
    return pl.pallas_call(
        _mod_kernel,
        grid=(DEPTH, n_out // MOD_TN),
        in_specs=[
            pl.BlockSpec((MOD_ROWS, D_MODEL), lambda l, n: (0, 0)),
            pl.BlockSpec((None, D_MODEL, MOD_TN), lambda l, n: (l, 0, n)),
            pl.BlockSpec((None, 1, MOD_TN), lambda l, n: (l, 0, n)),
        ],
        out_specs=pl.BlockSpec((None, MOD_ROWS, MOD_TN), lambda l, n: (l, 0, n)),
        out_shape=jax.ShapeDtypeStruct((DEPTH, MOD_ROWS, n_out), F32),
        compiler_params=_params(("arbitrary", "arbitrary"), nbytes),
        name="modulation",
    )(s_in, w_mod, b_mod.reshape(DEPTH, 1, n_out))


MM_SUB = 256


def _row_tiles(n_rows, size):
    size = min(size, n_rows)
    return [slice(r, r + size) for r in range(0, n_rows, size)]


def _ln_mod_rows(x_ref, shift, scale1, dst_ref, rows):
    for piece in _row_tiles(rows.stop - rows.start, ROW_CHUNK):
        sl = slice(rows.start + piece.start, rows.start + piece.stop)
        dst_ref[sl, :] = (_ln(x_ref[sl, :]) * scale1 + shift).astype(dst_ref.dtype)


def _residual_ln_store(res, y, gate, gain, bias, o_ref, rows):
    for piece in _row_tiles(rows.stop - rows.start, ROW_CHUNK):
        sl = slice(rows.start + piece.start, rows.start + piece.stop)
        z = ALPHA * res[sl, :] + gate * y[piece, :]
        o_ref[sl, :] = _ln(z) * gain + bias


def _rope(a, cos, sin):
    lane = lax.broadcasted_iota(jnp.int32, a.shape, 1)
    first = (lane % (HEAD_DIM // 2)) < (HEAD_DIM // 4)
    partner = jnp.where(first, pltpu.roll(a, HEAD_DIM - HEAD_DIM // 4, 1), pltpu.roll(a, HEAD_DIM // 4, 1))
    return a * cos + partner * sin


def _inproj_lat_kernel(x_ref, mod_ref, w_ref, cos_ref, sin_ref,
                       q_ref, qr_ref, k_ref, v_ref, xo_ref, go_ref, f_ref, xn_ref):
    shift = mod_ref[0:1, :]
    scale1 = 1.0 + mod_ref[1:2, :]

    def emit_q(rows, cols, acc):
        q_ref[rows, cols] = (acc * ATTN_SCALE).astype(q_ref.dtype)
        for h in range(COL_TILE // HEAD_DIM):
            sl = slice(h * HEAD_DIM, (h + 1) * HEAD_DIM)
            dst = slice(cols.start + sl.start, cols.start + sl.stop)
            rot = _rope(acc[:, sl], cos_ref[rows, :], sin_ref[rows, :])
            qr_ref[rows, dst] = (rot * ATTN_SCALE).astype(qr_ref.dtype)

    def emit_k(rows, cols, acc):
        for h in range(COL_TILE // HEAD_DIM):
            sl = slice(h * HEAD_DIM, (h + 1) * HEAD_DIM)
            dst = slice(cols.start + sl.start, cols.start + sl.stop)
            k_ref[rows, dst] = _rope(acc[:, sl], cos_ref[rows, :], sin_ref[rows, :]).astype(k_ref.dtype)

    def emit_to(ref):
        def emit(rows, cols, acc):
            ref[rows, cols] = acc.astype(ref.dtype)
        return emit

    half = [slice(0, COL_TILE), slice(COL_TILE, 2 * COL_TILE)]
    plan = ([(emit_q, c) for c in half] + [(emit_k, c) for c in half] + [(emit_to(v_ref), c) for c in half]
            + [(emit_to(xo_ref), half[0]), (emit_to(go_ref), half[0]), (emit_to(f_ref), half[0])])
    for rows in _row_tiles(x_ref.shape[0], MM_SUB):
        _ln_mod_rows(x_ref, shift, scale1, xn_ref, rows)
        for n, (emit, cols) in enumerate(plan):
            emit(rows, cols, _dot(xn_ref[rows, :], w_ref[:, n * COL_TILE:(n + 1) * COL_TILE]))


def _inproj_lat_call(x2d, mod4, w_in, layer, cos_t, sin_t):
    m_rows = x2d.shape[0]
    tm = 512
    tiles_per_seq = SEQ // tm
    row = lambda m: (m, 0)
    nbytes = (2 * tm * D_MODEL * 4 + tm * D_MODEL * 2 + D_MODEL * IN_W * 2 + 4 * tm * HEAD_DIM * 4
              + 2 * tm * (4 * NA_W * 2 + 2 * LRU_W * 4 + FNET_W * 2) + 6 * MM_SUB * COL_TILE * 4
              + 4 * ROW_CHUNK * D_MODEL * 4)
    bf = lambda w: jax.ShapeDtypeStruct((m_rows, w), BF16)
    ff = lambda w: jax.ShapeDtypeStruct((m_rows, w), F32)
    widths = [NA_W, NA_W, NA_W, NA_W, LRU_W, LRU_W, FNET_W]
    return pl.pallas_call(
        _inproj_lat_kernel,
        grid=(m_rows // tm,),
        in_specs=[
            pl.BlockSpec((tm, D_MODEL), row),
            pl.BlockSpec((None, None, 6, D_MODEL), lambda m: (layer, m // tiles_per_seq, 0, 0)),
            pl.BlockSpec((D_MODEL, IN_W), lambda m: (0, 0), pipeline_mode=pl.Buffered(1)),
            pl.BlockSpec((tm, HEAD_DIM), lambda m: (m % tiles_per_seq, 0)),
            pl.BlockSpec((tm, HEAD_DIM), lambda m: (m % tiles_per_seq, 0)),
        ],
        out_specs=[pl.BlockSpec((tm, w), row) for w in widths],
        out_shape=[bf(NA_W), bf(NA_W), bf(NA_W), bf(NA_W), ff(LRU_W), ff(LRU_W), bf(FNET_W)],
        scratch_shapes=[pltpu.VMEM((tm, D_MODEL), BF16)],
        compiler_params=_params(("arbitrary",), nbytes),
        name="inproj_latent",
    )(x2d, mod4, w_in, cos_t, sin_t)


def _inproj_ctx_kernel(tile_lo, with_q, with_gf, x_ref, mod_ref, w_ref, *refs):
    refs = list(refs)
    xn_ref = refs.pop()
    q_ref = refs.pop(0) if with_q else None
    k_ref, v_ref, xo_ref = refs[0], refs[1], refs[2]
    go_ref, f_ref = (refs[3], refs[4]) if with_gf else (None, None)
    n = pl.program_id(1) + tile_lo

    @pl.when(pl.program_id(1) == 0)
    def _():
        for rows in _row_tiles(x_ref.shape[0], MM_SUB):
            _ln_mod_rows(x_ref, mod_ref[0:1, :], 1.0 + mod_ref[1:2, :], xn_ref, rows)

    acc = _dot(xn_ref[...], w_ref[...])

    if with_q:
        @pl.when(n < 2)
        def _():
            q_ref[...] = (acc * ATTN_SCALE).astype(q_ref.dtype)

    @pl.when((n >= 2) & (n < 4))
    def _():
        k_ref[...] = acc.astype(k_ref.dtype)

    @pl.when((n >= 4) & (n < 6))
    def _():
        v_ref[...] = acc.astype(v_ref.dtype)

    @pl.when(n == 6)
    def _():
        xo_ref[...] = acc

    if with_gf:
        @pl.when(n == 7)
        def _():
            go_ref[...] = acc

        @pl.when(n == 8)
        def _():
            f_ref[...] = acc.astype(f_ref.dtype)


def _inproj_ctx_call(c2d, mod4, w_in, layer, full):
    m_rows = c2d.shape[0]
    tm = m_rows
    tile_lo, tile_hi = (0, N_COL_TILES) if full else (2, 7)

    def col(lo):
        return lambda m, n: (m, jnp.clip(n + tile_lo - lo, 0, 1))

    one = lambda m, n: (m, 0)
    bf = lambda w: jax.ShapeDtypeStruct((m_rows, w), BF16)
    ff = lambda w: jax.ShapeDtypeStruct((m_rows, w), F32)
    out_specs, out_shape = [], []
    if full:
        out_specs.append(pl.BlockSpec((tm, COL_TILE), col(0)))
        out_shape.append(bf(NA_W))
    out_specs += [pl.BlockSpec((tm, COL_TILE), col(2)), pl.BlockSpec((tm, COL_TILE), col(4)),
                  pl.BlockSpec((tm, COL_TILE), one)]
    out_shape += [bf(NA_W), bf(NA_W), ff(LRU_W)]
    if full:
        out_specs += [pl.BlockSpec((tm, COL_TILE), one), pl.BlockSpec((tm, COL_TILE), one)]
        out_shape += [ff(LRU_W), bf(FNET_W)]
    nbytes = (2 * tm * D_MODEL * 4 + tm * D_MODEL * 2 + 2 * D_MODEL * COL_TILE * 2
              + 2 * 6 * tm * COL_TILE * 4 + 3 * tm * COL_TILE * 4 + 4 * ROW_CHUNK * D_MODEL * 4)
    return pl.pallas_call(
        functools.partial(_inproj_ctx_kernel, tile_lo, full, full),
        grid=(1, tile_hi - tile_lo),
        in_specs=[
            pl.BlockSpec((tm, D_MODEL), lambda m, n: (m, 0)),
            pl.BlockSpec((None, None, 6, D_MODEL), lambda m, n: (layer, CTX_MOD_ROW, 0, 0)),
            pl.BlockSpec((D_MODEL, COL_TILE), lambda m, n: (0, n + tile_lo)),
        ],
        out_specs=out_specs,
        out_shape=out_shape,
        scratch_shapes=[pltpu.VMEM((tm, D_MODEL), BF16)],
        compiler_params=_params(("arbitrary", "arbitrary"), nbytes),
        name="inproj_context",
    )(c2d, mod4, w_in)


QBLK_ROWS = 4
KBLK_ROWS = 12
N_QBLK = GRID_H // QBLK_ROWS
QBLK = QBLK_ROWS * GRID_W
KBLK = KBLK_ROWS * GRID_W
KEY_TILE = V7X_LANES
KEY_TILES = KBLK // KEY_TILE
N_DR = 2 * WIN_H - 1
BIAS_BOTH, BIAS_SECOND, BIAS_FIRST = "both", "second", "first"


def _kblk_start(first_query_row):
    return int(np.clip(first_query_row - WIN_H // 2, 0, GRID_H - KBLK_ROWS))


def _bias_entries(first_query_row):
    entries = []
    for u in range(QBLK_ROWS):
        q_row = first_query_row + u
        row_start = int(np.clip(q_row - WIN_H // 2, 0, GRID_H - WIN_H))
        row = []
        for c in range(KEY_TILES):
            k_rows = [_kblk_start(first_query_row) + 2 * c + i for i in range(2)]
            inside = [row_start <= kr < row_start + WIN_H for kr in k_rows]
            dr = [kr - q_row + (WIN_H - 1) for kr in k_rows]
            if inside[0] and inside[1]:
                row.append((BIAS_BOTH, dr[0]))
            elif inside[1]:
                row.append((BIAS_SECOND, dr[1]))
            elif inside[0]:
                row.append((BIAS_FIRST, dr[0]))
            else:
                row.append(None)
        entries.append(row)
    return entries


BIAS_TABLE = sorted({e for b in range(N_QBLK) for row in _bias_entries(b * QBLK_ROWS) for e in row if e})
BIAS_SLOT = {e: i for i, e in enumerate(BIAS_TABLE)}
N_BIAS = len(BIAS_TABLE)


def _bias_plan(first_query_row):
    return [[BIAS_SLOT[e] if e else None for e in row] for row in _bias_entries(first_query_row)]


def _attn_kernel(n_riders, q_ref, qr_ref, k_ref, v_ref, kc_ref, vc_ref, bias_ref, *rest):
    o_ref = rest[n_riders]
    _run_riders(rest[:n_riders], rest[n_riders + 1:])
    kc = kc_ref[...]
    vc = vc_ref[...]
    zero_tile = jnp.zeros((GRID_W, KEY_TILE), BF16)

    def one_block(q0, k0, plan):
        s = _dot_nt(qr_ref[pl.ds(q0, QBLK), :], k_ref[pl.ds(k0, KBLK), :])
        sc = _dot_nt(q_ref[pl.ds(q0, QBLK), :], kc)
        p_rows, pc_rows, denoms = [], [], []
        for u in range(QBLK_ROWS):
            rows = slice(u * GRID_W, (u + 1) * GRID_W)
            band = {c: s[rows, c * KEY_TILE:(c + 1) * KEY_TILE] + bias_ref[idx]
                    for c, idx in enumerate(plan[u]) if idx is not None}
            ctx_tiles = [sc[rows, c * KEY_TILE:(c + 1) * KEY_TILE] for c in range(CTX_LEN // KEY_TILE)]
            tiles = list(band.values()) + ctx_tiles
            m = jnp.max(functools.reduce(jnp.maximum, tiles), axis=-1, keepdims=True)
            p_band = {c: jnp.exp(t - m) for c, t in band.items()}
            p_ctx = [jnp.exp(t - m) for t in ctx_tiles]
            total = functools.reduce(jnp.add, list(p_band.values()) + p_ctx)
            denoms.append(jnp.sum(total, axis=-1, keepdims=True))
            p_rows.append(jnp.concatenate(
                [p_band[c].astype(BF16) if c in p_band else zero_tile for c in range(KEY_TILES)], axis=1))
            pc_rows.append(jnp.concatenate([t.astype(BF16) for t in p_ctx], axis=1))
        p = jnp.concatenate(p_rows, axis=0)
        pc = jnp.concatenate(pc_rows, axis=0)
        o = _dot(p, v_ref[pl.ds(k0, KBLK), :]) + _dot(pc, vc)
        o_ref[pl.ds(q0, QBLK), :] = (o / jnp.concatenate(denoms, axis=0)).astype(o_ref.dtype)

    def static_block(b):
        first_row = b * QBLK_ROWS
        one_block(b * QBLK, _kblk_start(first_row) * GRID_W, _bias_plan(first_row))

    interior_plan = _bias_plan(QBLK_ROWS)

    def interior_block(b):
        q0 = pl.multiple_of(b * QBLK, QBLK)
        k0 = pl.multiple_of((b * QBLK_ROWS - WIN_H // 2) * GRID_W, GRID_W)
        one_block(q0, k0, interior_plan)

    def body(i, carry):
        interior_block(2 * i + 1)
        interior_block(2 * i + 2)
        return carry

    for b in range(N_QBLK):
        static_block(b)


def _attn_call(q, qr, k, v, kc, vc, bias, layer, riders):
    seq_blk = lambda b, h: (b, h)
    n_steps = BATCH * NA_HEADS
    step = lambda b, h: b * NA_HEADS + h
    r_in, r_out, r_shape, r_bytes = _rider_specs(riders, n_steps, step)
    nbytes = (2 * (5 * SEQ * HEAD_DIM * 2 + 2 * CTX_LEN * HEAD_DIM * 2 + N_BIAS * GRID_W * KEY_TILE * 4)
              + 8 * QBLK * (KBLK + CTX_LEN) * 4 + r_bytes)
    res = pl.pallas_call(
        functools.partial(_attn_kernel, len(riders)),
        grid=(BATCH, NA_HEADS),
        in_specs=[
            pl.BlockSpec((SEQ, HEAD_DIM), seq_blk),
            pl.BlockSpec((SEQ, HEAD_DIM), seq_blk),
            pl.BlockSpec((SEQ, HEAD_DIM), seq_blk),
            pl.BlockSpec((SEQ, HEAD_DIM), seq_blk),
            pl.BlockSpec((CTX_LEN, HEAD_DIM), seq_blk),
            pl.BlockSpec((CTX_LEN, HEAD_DIM), seq_blk),
            pl.BlockSpec((None, None, N_BIAS, GRID_W, KEY_TILE), lambda b, h: (layer, h, 0, 0, 0)),
        ] + r_in,
        out_specs=[pl.BlockSpec((SEQ, HEAD_DIM), seq_blk)] + r_out,
        out_shape=[jax.ShapeDtypeStruct((BATCH * SEQ, NA_W), BF16)] + r_shape,
        compiler_params=_params(("arbitrary", "arbitrary"), nbytes),
        name="neighbourhood_attention",
    )(q, qr, k, v, kc, vc, bias, *[r[0] for r in riders])
    return res[0], res[1:]


def _ctx_attn_kernel(q_ref, k_ref, v_ref, o_ref):
    s = _dot_nt(q_ref[...], k_ref[...])
    m = jnp.max(s, axis=-1, keepdims=True)
    p = jnp.exp(s - m)
    denom = jnp.sum(p, axis=-1, keepdims=True)
    o_ref[...] = (_dot(p.astype(BF16), v_ref[...]) / denom).astype(o_ref.dtype)


def _ctx_attn_call(q, k, v):
    blk = pl.BlockSpec((CTX_LEN, HEAD_DIM), lambda b, h: (b, h))
    return pl.pallas_call(
        _ctx_attn_kernel,
        grid=(BATCH, NA_HEADS),
        in_specs=[blk, blk, blk],
        out_specs=blk,
        out_shape=jax.ShapeDtypeStruct((BATCH * CTX_LEN, NA_W), BF16),
        compiler_params=_params(("arbitrary", "arbitrary"), 16 << 20),
        name="context_attention",
    )(q, k, v)


def _attn_bias_table(rpb):
    col = np.arange(GRID_W)
    col_start = np.clip(col - WIN_W // 2, 0, GRID_W - WIN_W)
    in_win = (col[None, :] >= col_start[:, None]) & (col[None, :] < col_start[:, None] + WIN_W)
    dc = np.clip(col[None, :] - col[:, None] + (WIN_W - 1), 0, 2 * WIN_W - 2)
    onehot = (in_win[None] & (dc[None] == np.arange(2 * WIN_W - 1)[:, None, None])).astype(np.float32)
    t = jnp.einsum('lhdj,jqk->lhdqk', rpb, jnp.asarray(onehot), precision=lax.Precision.HIGHEST)
    t = jnp.where(in_win[None, None, None], t, NEG_INF)
    masked = jnp.full((DEPTH, NA_HEADS, GRID_W, GRID_W), NEG_INF, F32)
    tiles = []
    for kind, dr in BIAS_TABLE:
        left = masked if kind == BIAS_SECOND else t[:, :, dr]
        right = masked if kind == BIAS_FIRST else t[:, :, dr + 1 if kind == BIAS_BOTH else dr]
        tiles.append(jnp.concatenate([left, right], axis=-1))
    return jnp.stack(tiles, axis=2)


def _rope_tables():
    quarter = HEAD_DIM // 4
    inv = ROPE_THETA ** (-jnp.arange(quarter, dtype=F32) / quarter)
    t = jnp.arange(SEQ)
    ang_r = (t // GRID_W).astype(F32)[:, None] * inv
    ang_c = (t % GRID_W).astype(F32)[:, None] * inv
    cos = jnp.concatenate([jnp.cos(ang_r), jnp.cos(ang_r), jnp.cos(ang_c), jnp.cos(ang_c)], -1)
    sin = jnp.concatenate([-jnp.sin(ang_r), jnp.sin(ang_r), -jnp.sin(ang_c), jnp.sin(ang_c)], -1)
    return cos, sin


HALO = V7X_SUBLANES
N_SEG = V7X_SUBLANES
SEG_PAD = V7X_SUBLANES
LRU_SCAN_UNROLL = 8


def _lru_coeffs(xp_ref, n_rows, cw_ref, cb_ref, w4, b4_ref, sp, a_refs, u_refs):
    seg = n_rows // N_SEG
    pitch = seg + SEG_PAD
    for s in range(N_SEG):
        base = HALO + s * seg
        xc = cb_ref[...] + xp_ref[base - CONV_W // 2:base - CONV_W // 2 + seg, :] * cw_ref[0:1, :]
        for j in range(1, CONV_W):
            off = base - CONV_W // 2 + j
            xc = xc + xp_ref[off:off + seg, :] * cw_ref[j:j + 1, :]
        th = jnp.tanh(_dot(xc.astype(BF16), w4) + b4_ref[...])
        half_xc = 0.5 * xc
        for d in range(2):
            r2 = th[:, (2 * d) * LRU_BW:(2 * d + 1) * LRU_BW] + 1.0
            i2 = th[:, (2 * d + 1) * LRU_BW:(2 * d + 2) * LRU_BW] + 1.0
            log_a = r2 * sp[d:d + 1, :]
            a = jnp.exp(log_a)
            a_refs[d][s * pitch:s * pitch + seg, :] = a
            one_minus_a2 = -jnp.tanh(log_a) * (a * a + 1.0)
            u_refs[d][s * pitch:s * pitch + seg, :] = jnp.sqrt(one_minus_a2) * (i2 * half_xc)


def _lru_local_scan(n_rows, coef_f, coef_b, state_f, state_b):
    seg = n_rows // N_SEG
    pitch = seg + SEG_PAD
    zero = jnp.zeros((N_SEG, LRU_BW), F32)
    one = jnp.ones((N_SEG, LRU_BW), F32)

    def step(coef, state, row, h, p):
        rows = pl.ds(row, N_SEG, stride=pitch)
        a = coef[0][rows, :]
        h = a * h + coef[1][rows, :]
        p = p * a
        state[0][rows, :] = p
        state[1][rows, :] = h
        return h, p

    def body(i, carry):
        hf, pf, hb, pb = carry
        for j in range(LRU_SCAN_UNROLL):
            t = i * LRU_SCAN_UNROLL + j
            hf, pf = step(coef_f, state_f, t, hf, pf)
            hb, pb = step(coef_b, state_b, seg - 1 - t, hb, pb)
        return hf, pf, hb, pb

    lax.fori_loop(0, seg // LRU_SCAN_UNROLL, body, (zero, one, zero, one))


def _lru_carries(n_rows, h_in_f, h_in_b, af, uf, ab, ub):
    seg = n_rows // N_SEG
    pitch = seg + SEG_PAD
    cf, cb = [h_in_f], [h_in_b]
    for s in range(N_SEG):
        last = s * pitch + seg - 1
        cf.append(uf[last:last + 1, :] + af[last:last + 1, :] * cf[-1])
        first = (N_SEG - 1 - s) * pitch
        cb.append(ub[first:first + 1, :] + ab[first:first + 1, :] * cb[-1])
    return cf[:N_SEG], cb[:N_SEG][::-1], cf[N_SEG], cb[N_SEG]


def _lru_emit(n_rows, cf, cb, af, uf, ab, ub, g_ref, o_ref):
    seg = n_rows // N_SEG
    pitch = seg + SEG_PAD
    for s in range(N_SEG):
        src = slice(s * pitch, s * pitch + seg)
        dst = slice(s * seg, (s + 1) * seg)
        y = (uf[src, :] + af[src, :] * cf[s]) + (ub[src, :] + ab[src, :] * cb[s])
        o_ref[dst, :] = (y * _gelu_tanh(g_ref[dst, :])).astype(o_ref.dtype)


def _lru_kernel(ctx_out, n_riders, x_ref, g_ref, xc_ref, *refs):
    refs = list(refs)
    gc_ref = refs.pop(0) if ctx_out else None
    cw_ref, cb_ref, w4_ref, b4_ref, lam_ref = refs[:5]
    rider_in, refs = refs[5:5 + n_riders], refs[5 + n_riders:]
    o_ref = refs.pop(0)
    oc_ref = refs.pop(0) if ctx_out else None
    rider_out, refs = refs[:n_riders], refs[n_riders:]
    xp_ref, af, uf, ab, ub, pf, sf, pb, sb = refs
    _run_riders(rider_in, rider_out)

    lam = lam_ref[...]
    z = -lam
    sp = (-0.5 * LRU_C) * (jnp.maximum(z, 0.0) + jnp.log1p(jnp.exp(-jnp.abs(z))))
    w4 = w4_ref[...].astype(BF16)
    zeros_halo = jnp.zeros((HALO, LRU_BW), F32)
    h0 = jnp.zeros((1, LRU_BW), F32)

    xp_ref[0:HALO, :] = zeros_halo
    xp_ref[HALO:HALO + CTX_LEN, :] = xc_ref[...]
    xp_ref[HALO + CTX_LEN:2 * HALO + CTX_LEN, :] = zeros_halo
    _lru_coeffs(xp_ref, CTX_LEN, cw_ref, cb_ref, w4, b4_ref, sp, (af, ab), (uf, ub))
    _lru_local_scan(CTX_LEN, (af, uf), (ab, ub), (pf, sf), (pb, sb))
    cf, cb, hf, hb = _lru_carries(CTX_LEN, h0, h0, pf, sf, pb, sb)
    if ctx_out:
        _lru_emit(CTX_LEN, cf, cb, pf, sf, pb, sb, gc_ref, oc_ref)

    xp_ref[HALO:HALO + SEQ, :] = x_ref[...]
    xp_ref[HALO + SEQ:2 * HALO + SEQ, :] = zeros_halo
    _lru_coeffs(xp_ref, SEQ, cw_ref, cb_ref, w4, b4_ref, sp, (af, ab), (uf, ub))
    _lru_local_scan(SEQ, (af, uf), (ab, ub), (pf, sf), (pb, sb))
    cf, cb, _, _ = _lru_carries(SEQ, hf, hb, pf, sf, pb, sb)
    _lru_emit(SEQ, cf, cb, pf, sf, pb, sb, g_ref, o_ref)


def _lru_call(ctx_out, xl, gl, xc, gc, conv_w_l, conv_b_l, w4, b4, lam_l, riders):
    lat = pl.BlockSpec((SEQ, LRU_BW), lambda b, j: (b, j))
    cx = pl.BlockSpec((CTX_LEN, LRU_BW), lambda b, j: (b, j))
    r_in, r_out, r_shape, r_bytes = _rider_specs(riders, BATCH * LRU_BLOCKS, lambda b, j: b * LRU_BLOCKS + j)
    in_specs = [lat, lat, cx] + ([cx] if ctx_out else []) + [
        pl.BlockSpec((CONV_W, LRU_BW), lambda b, j: (0, j)),
        pl.BlockSpec((1, LRU_BW), lambda b, j: (0, j)),
        pl.BlockSpec((None, LRU_BW, 4 * LRU_BW), lambda b, j: (j, 0, 0)),
        pl.BlockSpec((None, 1, 4 * LRU_BW), lambda b, j: (j, 0, 0)),
        pl.BlockSpec((2, LRU_BW), lambda b, j: (0, j)),
    ] + r_in
    out_specs = [lat] + ([cx] if ctx_out else []) + r_out
    out_shape = [jax.ShapeDtypeStruct((BATCH * SEQ, LRU_W), BF16)]
    if ctx_out:
        out_shape.append(jax.ShapeDtypeStruct((BATCH * CTX_LEN, LRU_W), BF16))
    out_shape += r_shape
    seq_bytes = SEQ * LRU_BW * 4
    args = ([xl, gl, xc] + ([gc] if ctx_out else []) + [conv_w_l, conv_b_l, w4, b4, lam_l]
            + [r[0] for r in riders])
    res = pl.pallas_call(
        functools.partial(_lru_kernel, ctx_out, len(riders)),
        grid=(BATCH, LRU_BLOCKS),
        in_specs=in_specs,
        out_specs=out_specs,
        out_shape=out_shape,
        scratch_shapes=([pltpu.VMEM((SEQ + 2 * HALO, LRU_BW), F32)]
                        + [pltpu.VMEM((SEQ + N_SEG * SEG_PAD, LRU_BW), F32)] * 8),
        compiler_params=_params(("arbitrary", "arbitrary"), 15 * seq_bytes + (8 << 20) + r_bytes),
        name="rglru",
    )(*args)
    n_main = 2 if ctx_out else 1
    return res[0], (res[1] if ctx_out else None), res[n_main:]


def _fourier_kernel(n_pos, blk, f_ref, ch_ref, sh_ref, cc_ref, sc_ref, rev_ref, alt_ref, w_ref, b_ref, o_ref,
                    ec_ref, es_ref, mir_ref, mid_ref):
    half = n_pos // 2
    n_lo = half // blk
    step = pl.program_id(1)
    norm = 1.0 / math.sqrt(n_pos * FNET_GW)
    w = w_ref[...].astype(BF16)

    def linear(y):
        return (_dot(y.astype(BF16), w) + b_ref[...]).astype(o_ref.dtype)

    @pl.when(step == 0)
    def _():
        for i in range(n_lo):
            lo = blk * (2 * n_lo - 1 - i)
            if i == 0:
                mirrored = _dot(rev_ref[:, 0:blk], f_ref[lo:lo + blk, :])
            else:
                mirrored = _dot(rev_ref[...], f_ref[lo:lo + 2 * blk, :])
            rows = slice(i * blk, (i + 1) * blk)
            x = f_ref[rows, :].astype(F32)
            even = (x + mirrored).astype(BF16)
            odd = (x - mirrored).astype(BF16)
            for g in range(FNET_GROUPS):
                sl = slice(g * FNET_GW, (g + 1) * FNET_GW)
                ec_ref[rows, sl] = _dot(even[:, sl], cc_ref[...]).astype(BF16)
                es_ref[rows, sl] = _dot(odd[:, sl], sc_ref[...]).astype(BF16)
        for g in range(FNET_GROUPS):
            sl = slice(g * FNET_GW, (g + 1) * FNET_GW)
            mid_ref[0:BF16_ROWS, sl] = _dot(f_ref[half:half + BF16_ROWS, sl], cc_ref[...])
        mid_ref[BF16_ROWS:2 * BF16_ROWS, :] = _dot(alt_ref[...], ec_ref[...])

    mid = mid_ref[0:1, :]

    @pl.when(step < n_lo)
    def _():
        a = _dot(ch_ref[...], ec_ref[...])
        b = _dot(sh_ref[...], es_ref[...])
        row = lax.broadcasted_iota(jnp.int32, (blk, 1), 0)
        base = jnp.where(row % 2 == 0, 1.0, -1.0) * mid
        o_ref[...] = linear((a - b + base) * norm)
        r0 = pl.multiple_of(step * blk, blk)
        mir_ref[pl.ds(r0, blk), :] = ((a + b + base) * norm).astype(BF16)

    @pl.when(step == n_lo)
    def _():
        y = _dot(rev_ref[:, 0:blk], mir_ref[(n_lo - 1) * blk:n_lo * blk, :])
        nyquist = (mid_ref[BF16_ROWS:BF16_ROWS + 1, :] + mid) * norm
        row = lax.broadcasted_iota(jnp.int32, (blk, 1), 0)
        o_ref[...] = linear(jnp.where(row == 0, nyquist, y))

    if n_lo > 1:
        @pl.when(step > n_lo)
        def _():
            r0 = pl.multiple_of((2 * n_lo - 1 - step) * blk, blk)
            o_ref[...] = linear(_dot(rev_ref[...], mir_ref[pl.ds(r0, 2 * blk), :]))


DFT_SPLIT = 64


def _dft_matrices(n, size):
    t = np.arange(size, dtype=np.int64)

    def table(k):
        ang = (2.0 * np.pi / n) * ((k[:, None] * t[None, :]) % n).astype(np.float64)
        return jnp.asarray(np.cos(ang), F32), jnp.asarray(np.sin(ang), F32)

    if size <= DFT_SPLIT:
        c, s = table(t)
        return c.astype(BF16), s.astype(BF16)
    c1, s1 = table(DFT_SPLIT * np.arange(size // DFT_SPLIT, dtype=np.int64))
    c2, s2 = table(np.arange(DFT_SPLIT, dtype=np.int64))
    c = c1[:, None, :] * c2[None, :, :] - s1[:, None, :] * s2[None, :, :]
    s = s1[:, None, :] * c2[None, :, :] + c1[:, None, :] * s2[None, :, :]
    return c.reshape(size, size).astype(BF16), s.reshape(size, size).astype(BF16)


FOURIER_BLK = 256


def _fourier_call(f2d, n_pos, fno_w_l, fno_b_l):
    half = n_pos // 2
    blk = min(FOURIER_BLK, half)
    steps = n_pos // blk
    n_lo = half // blk
    ch, sh = _dft_matrices(n_pos, half)
    cc, sc = _dft_matrices(FNET_GW, FNET_GW)
    rev = np.zeros((blk, 2 * blk), np.float32)
    rev[np.arange(1, blk), blk - np.arange(1, blk)] = 1.0
    rev[0, blk] = 1.0
    alt = np.zeros((BF16_ROWS, half), np.float32)
    alt[0] = 1.0 - 2.0 * (np.arange(half) % 2)
    const = lambda b, k: (0, 0)
    dft_tile = lambda b, k: (jnp.minimum(k, n_lo - 1), 0)
    nbytes = (2 * n_pos * FNET_W * 2 + 2 * 2 * blk * half * 2 + 3 * half * FNET_W * 2 + 2 * FNET_W * FNET_W * 4
              + 8 * blk * FNET_W * 4 + 2 * blk * 2 * blk * 2)
    return pl.pallas_call(
        functools.partial(_fourier_kernel, n_pos, blk),
        grid=(BATCH, steps),
        in_specs=[
            pl.BlockSpec((n_pos, FNET_W), lambda b, k: (b, 0)),
            pl.BlockSpec((blk, half), dft_tile),
            pl.BlockSpec((blk, half), dft_tile),
            pl.BlockSpec((FNET_GW, FNET_GW), const),
            pl.BlockSpec((FNET_GW, FNET_GW), const),
            pl.BlockSpec((blk, 2 * blk), const),
            pl.BlockSpec((BF16_ROWS, half), const),
            pl.BlockSpec((FNET_W, FNET_W), const),
            pl.BlockSpec((1, FNET_W), const),
        ],
        out_specs=pl.BlockSpec((blk, FNET_W), lambda b, k: (b * steps + k, 0)),
        out_shape=jax.ShapeDtypeStruct((BATCH * n_pos, FNET_W), BF16),
        scratch_shapes=[pltpu.VMEM((half, FNET_W), BF16), pltpu.VMEM((half, FNET_W), BF16),
                        pltpu.VMEM((half, FNET_W), BF16), pltpu.VMEM((2 * BF16_ROWS, FNET_W), F32)],
        compiler_params=_params(("arbitrary", "arbitrary"), nbytes),
        name="fourier_mix",
    )(f2d, ch, sh, cc, sc, jnp.asarray(rev, BF16), jnp.asarray(alt, BF16), fno_w_l, fno_b_l)


def _outproj_kernel(na_ref, lru_ref, f_ref, res_ref, mod_ref, w_ref, g_ref, b_ref, o_ref):
    gate = mod_ref[2:3, :]
    for rows in _row_tiles(res_ref.shape[0], MM_SUB):
        y = (_dot(na_ref[rows, :], w_ref[0:NA_W, :])
             + _dot(lru_ref[rows, :], w_ref[NA_W:NA_W + LRU_W, :])
             + _dot(f_ref[rows, :], w_ref[NA_W + LRU_W:D_MODEL, :]))
        _residual_ln_store(res_ref, y, gate, g_ref[...], b_ref[...], o_ref, rows)


def _outproj_call(na, lru, f, res, mod4, mod_row, w_out, layer, ln_g, ln_b):
    m_rows = res.shape[0]
    tm = min(1024, m_rows)
    row = lambda m: (m, 0)
    nbytes = (2 * tm * D_MODEL * 2 + 2 * 2 * tm * D_MODEL * 4 + D_MODEL * D_MODEL * 2 + 2 * MM_SUB * D_MODEL * 4
              + 4 * ROW_CHUNK * D_MODEL * 4)
    return pl.pallas_call(
        _outproj_kernel,
        grid=(m_rows // tm,),
        in_specs=[
            pl.BlockSpec((tm, NA_W), row),
            pl.BlockSpec((tm, LRU_W), row),
            pl.BlockSpec((tm, FNET_W), row),
            pl.BlockSpec((tm, D_MODEL), row),
            pl.BlockSpec((None, None, 6, D_MODEL), lambda m: (layer, mod_row(m * tm), 0, 0)),
            pl.BlockSpec((D_MODEL, D_MODEL), lambda m: (0, 0), pipeline_mode=pl.Buffered(1)),
            pl.BlockSpec((None, 1, D_MODEL), lambda m: (layer, 0, 0)),
            pl.BlockSpec((None, 1, D_MODEL), lambda m: (layer, 0, 0)),
        ],
        out_specs=pl.BlockSpec((tm, D_MODEL), row),
        out_shape=jax.ShapeDtypeStruct((m_rows, D_MODEL), F32),
        compiler_params=_params(("arbitrary",), nbytes),
        name="outproj_residual",
    )(na, lru, f, res, mod4, w_out, ln_g, ln_b)


MLP_TF = 512
MLP_TILES = D_FF // MLP_TF


def _mlp_kernel(x_ref, mod_ref, w1_ref, b1_ref, w2_ref, b2_ref, g_ref, b_ref, o_ref, v_ref, h0_ref, h1_ref):
    j = pl.program_id(1)
    tm = x_ref.shape[0]

    def up(h_out, rows=slice(None)):
        h = _dot(v_ref[rows, :], w1_ref[...]) + b1_ref[...]
        h_out[rows, :] = jnp.square(jnp.maximum(h, 0.0)).astype(BF16)

    def down(h_in, first):
        for c in range(D_MODEL // COL_TILE):
            sl = slice(c * COL_TILE, (c + 1) * COL_TILE)
            part = _dot(h_in[...], w2_ref[:, sl])
            if first:
                o_ref[:, sl] = part
            else:
                o_ref[:, sl] += part

    @pl.when(j == 0)
    def _():
        shift = mod_ref[3:4, :]
        scale1 = 1.0 + mod_ref[4:5, :]
        for rows in _row_tiles(tm, MM_SUB):
            _ln_mod_rows(x_ref, shift, scale1, v_ref, rows)
            up(h0_ref, rows)

    @pl.when(j == 1)
    def _():
        down(h0_ref, True)
        up(h1_ref)

    @pl.when((j > 1) & (j < MLP_TILES) & (j % 2 == 0))
    def _():
        down(h1_ref, False)
        up(h0_ref)

    @pl.when((j > 1) & (j < MLP_TILES) & (j % 2 == 1))
    def _():
        down(h0_ref, False)
        up(h1_ref)

    @pl.when(j == MLP_TILES)
    def _():
        h_last = h1_ref if (MLP_TILES - 1) % 2 else h0_ref
        gate = mod_ref[5:6, :]
        for rows in _row_tiles(tm, MM_SUB):
            y = o_ref[rows, :] + _dot(h_last[rows, :], w2_ref[...]) + b2_ref[...]
            _residual_ln_store(x_ref, y, gate, g_ref[...], b_ref[...], o_ref, rows)


def _mlp_call(x1, mod4, mod_row, tm, w1, b1, w2, b2, layer, ln_g, ln_b):
    m_rows = x1.shape[0]
    row = lambda m, j: (m, 0)
    vec = lambda m, j: (layer, 0, 0)
    nbytes = (4 * tm * D_MODEL * 4 + tm * D_MODEL * 2 + 2 * 2 * D_MODEL * MLP_TF * 2 + 2 * tm * MLP_TF * 2
              + tm * MLP_TF * 4 + tm * COL_TILE * 4 + 2 * MM_SUB * D_MODEL * 4 + 4 * ROW_CHUNK * D_MODEL * 4)
    return pl.pallas_call(
        _mlp_kernel,
        grid=(m_rows // tm, MLP_TILES + 1),
        in_specs=[
            pl.BlockSpec((tm, D_MODEL), row),
            pl.BlockSpec((None, None, 6, D_MODEL), lambda m, j: (layer, mod_row(m * tm), 0, 0)),
            pl.BlockSpec((D_MODEL, MLP_TF), lambda m, j: (0, jnp.minimum(j, MLP_TILES - 1))),
            pl.BlockSpec((None, 1, MLP_TF), lambda m, j: (layer, 0, jnp.minimum(j, MLP_TILES - 1))),
            pl.BlockSpec((MLP_TF, D_MODEL), lambda m, j: (jnp.maximum(j - 1, 0), 0)),
            pl.BlockSpec((None, 1, D_MODEL), vec),
            pl.BlockSpec((None, 1, D_MODEL), vec),
            pl.BlockSpec((None, 1, D_MODEL), vec),
        ],
        out_specs=pl.BlockSpec((tm, D_MODEL), row),
        out_shape=jax.ShapeDtypeStruct((m_rows, D_MODEL), F32),
        scratch_shapes=[pltpu.VMEM((tm, D_MODEL), BF16), pltpu.VMEM((tm, MLP_TF), BF16),
                        pltpu.VMEM((tm, MLP_TF), BF16)],
        compiler_params=_params(("arbitrary", "arbitrary"), nbytes),
        name="mlp_residual",
    )(x1, mod4, w1, b1, w2, b2, ln_g, ln_b)


def kernel(x, c, ctx, c_ctx, w_mod, b_mod, w_in, rpb, conv_w, conv_b, lru_wa, lru_ba, lru_wx, lru_bx, lru_lambda,
           fno_w, fno_b, w_out, ln1_g, ln1_b, w_fc1, b_fc1, w_fc2, b_fc2, ln2_g, ln2_b):
    xl = x.reshape(BATCH * SEQ, D_MODEL)
    xc = ctx.reshape(BATCH * CTX_LEN, D_MODEL)
    s_in = jnp.concatenate([c, c_ctx[None], jnp.zeros((MOD_ROWS - BATCH - 1, D_MODEL), F32)], 0)
    mod4 = _mod_call(s_in, w_mod, b_mod).reshape(DEPTH, MOD_ROWS, 6, D_MODEL)
    cos_t, sin_t = _rope_tables()
    bias_tab = _attn_bias_table(rpb)
    w_in_l = w_in[0].astype(BF16)
    vec3 = lambda a: a.reshape(DEPTH, 1, a.shape[-1])
    ln1_g3, ln1_b3, ln2_g3, ln2_b3 = vec3(ln1_g), vec3(ln1_b), vec3(ln2_g), vec3(ln2_b)
    b_fc1_3, b_fc2_3 = vec3(b_fc1), vec3(b_fc2)
    lat_row = lambda r0: r0 // SEQ
    ctx_row = lambda r0: CTX_MOD_ROW

    for layer in range(DEPTH):
        ctx_out = layer < DEPTH - 1
        q, qr, k, v, xo, go, f = _inproj_lat_call(xl, mod4, w_in_l, layer, cos_t, sin_t)
        if ctx_out:
            qc, kc, vc, xoc, goc, fc = _inproj_ctx_call(xc, mod4, w_in_l, layer, True)
        else:
            kc, vc, xoc = _inproj_ctx_call(xc, mod4, w_in_l, layer, False)
            goc = None

        na, (w_out_l, w_fc1_l, w_fc2_l) = _attn_call(
            q, qr, k, v, kc, vc, bias_tab, layer,
            [_rider(w_out, layer, 0), _rider(w_fc1, layer, 1), _rider(w_fc2, layer, 0)])

        w4 = 0.5 * jnp.concatenate([lru_wa[layer, 0], lru_wx[layer, 0], lru_wa[layer, 1], lru_wx[layer, 1]], -1)
        blk = lambda a: a.reshape(LRU_BLOCKS, 1, LRU_BW)
        b4 = 0.5 * jnp.concatenate([blk(lru_ba[layer, 0]), blk(lru_bx[layer, 0]),
                                    blk(lru_ba[layer, 1]), blk(lru_bx[layer, 1])], -1)
        next_w_in = [_rider(w_in, layer + 1, 0)] if layer + 1 < DEPTH else []
        lru, lru_c, cast = _lru_call(ctx_out, xo, go, xoc, goc, conv_w[layer], conv_b[layer][None], w4, b4,
                                     lru_lambda[layer], next_w_in)
        if next_w_in:
            w_in_l = cast[0]

        fm = _fourier_call(f, SEQ, fno_w[layer], fno_b[layer][None])
        x1 = _outproj_call(na, lru, fm, xl, mod4, lat_row, w_out_l, layer, ln1_g3, ln1_b3)
        xl = _mlp_call(x1, mod4, lat_row, 1024, w_fc1_l, b_fc1_3, w_fc2_l, b_fc2_3, layer, ln2_g3, ln2_b3)

        if ctx_out:
            na_c = _ctx_attn_call(qc, kc, vc)
            fm_c = _fourier_call(fc, CTX_LEN, fno_w[layer], fno_b[layer][None])
            c1 = _outproj_call(na_c, lru_c, fm_c, xc, mod4, ctx_row, w_out_l, layer, ln1_g3, ln1_b3)
            xc = _mlp_call(c1, mod4, ctx_row, 512, w_fc1_l, b_fc1_3, w_fc2_l, b_fc2_3, layer, ln2_g3, ln2_b3)

    return xl.reshape(BATCH, SEQ, D_MODEL)
```

```python
import functools
import math

import jax
import jax.numpy as jnp
import numpy as np
from jax import lax
from jax.experimental import pallas as pl
from jax.experimental.pallas import tpu as pltpu

F32 = jnp.float32
BF16 = jnp.bfloat16

D_MODEL = 2048
BATCH = 2
SEQ = 4096
DEPTH = 2
GRID_W = 64
GRID_H = SEQ // GRID_W
CTX_LEN = 256
HEAD_DIM = 128
NA_W = D_MODEL // 2
NA_HEADS = NA_W // HEAD_DIM
WIN_H = 8
WIN_W = 16
LRU_W = D_MODEL // 4
LRU_BLOCKS = 4
LRU_BW = LRU_W // LRU_BLOCKS
CONV_W = 4
LRU_C = 8.0
FNET_W = D_MODEL // 4
FNET_GROUPS = 4
FNET_GW = FNET_W // FNET_GROUPS
IN_W = 3 * NA_W + 2 * LRU_W + FNET_W
D_FF = 4 * D_MODEL
ROPE_THETA = 10000.0
LN_EPS = 1e-5
NEG_INF = -1e30
ALPHA = (2.0 * DEPTH) ** 0.25
ATTN_SCALE = HEAD_DIM ** -0.5

V7X_LANES = 128
V7X_SUBLANES = 8
BF16_ROWS = 2 * V7X_SUBLANES
V7X_VMEM_BYTES = 64 * 1024 * 1024
VMEM_CEILING = V7X_VMEM_BYTES - 6 * 1024 * 1024

COL_TILE = 512
N_COL_TILES = IN_W // COL_TILE
ROW_CHUNK = 128
MOD_ROWS = 8
CTX_MOD_ROW = BATCH


def _vmem_limit(nbytes):
    return int(min(VMEM_CEILING, nbytes * 5 // 4 + (4 << 20)))


def _params(semantics, nbytes):
    return pltpu.CompilerParams(dimension_semantics=semantics, vmem_limit_bytes=_vmem_limit(nbytes))


def _ln(x):
    mu = jnp.mean(x, axis=-1, keepdims=True)
    xc = x - mu
    var = jnp.mean(xc * xc, axis=-1, keepdims=True)
    return xc * lax.rsqrt(var + LN_EPS)


def _sigmoid(x):
    return 1.0 / (1.0 + jnp.exp(-x))


def _gelu_tanh(x):
    return 0.5 * x * (1.0 + jnp.tanh(math.sqrt(2.0 / math.pi) * (x + 0.044715 * (x * x * x))))


def _dot(a, b):
    return jnp.dot(a, b, preferred_element_type=F32)


def _dot_nt(a, b):
    return lax.dot_general(a, b, (((1,), (1,)), ((), ())), preferred_element_type=F32)


def _rider(w, layer, axis):
    return (w, layer, axis)


def _rider_specs(riders, n_steps, step_of):
    in_specs, out_specs, out_shapes, nbytes = [], [], [], 0
    for w, layer, axis in riders:
        rows, cols = w.shape[1:]
        if axis == 0:
            blk = (rows // n_steps, cols)
            in_idx = lambda *g, layer=layer: (layer, step_of(*g), 0)
            out_idx = lambda *g: (step_of(*g), 0)
        else:
            blk = (rows, cols // n_steps)
            in_idx = lambda *g, layer=layer: (layer, 0, step_of(*g))
            out_idx = lambda *g: (0, step_of(*g))
        in_specs.append(pl.BlockSpec((None,) + blk, in_idx))
        out_specs.append(pl.BlockSpec(blk, out_idx))
        out_shapes.append(jax.ShapeDtypeStruct((rows, cols), BF16))
        nbytes += 2 * blk[0] * blk[1] * (4 + 2)
    return in_specs, out_specs, out_shapes, nbytes


def _run_riders(in_refs, out_refs):
    for src, dst in zip(in_refs, out_refs):
        dst[...] = src[...].astype(dst.dtype)


MOD_TN = 1024


def _mod_kernel(s_ref, w_ref, b_ref, o_ref):
    s = s_ref[...]
    s = s * _sigmoid(s)
    o_ref[...] = _dot(s.astype(BF16), w_ref[...].astype(BF16)) + b_ref[...]


def _mod_call(s_in, w_mod, b_mod):
    n_out = w_mod.shape[-1]
    nbytes = 2 * (D_MODEL * MOD_TN * 4) + D_MODEL * MOD_TN * 2 + 4 * MOD_ROWS * n_out
    return pl.pallas_call(
        _mod_kernel,
        grid=(DEPTH, n_out // MOD_TN),
        in_specs=[
            pl.BlockSpec((MOD_ROWS, D_MODEL), lambda l, n: (0, 0)),
            pl.BlockSpec((None, D_MODEL, MOD_TN), lambda l, n: (l, 0, n)),
            pl.BlockSpec((None, 1, MOD_TN), lambda l, n: (l, 0, n)),
        ],
        out_specs=pl.BlockSpec((None, MOD_ROWS, MOD_TN), lambda l, n: (l, 0, n)),
        out_shape=jax.ShapeDtypeStruct((DEPTH, MOD_ROWS, n_out), F32),
        compiler_params=_params(("arbitrary", "arbitrary"), nbytes),
        name="modulation",
    )(s_in, w_mod, b_mod.reshape(DEPTH, 1, n_out))


MM_SUB = 256


def _row_tiles(n_rows, size):
    size = min(size, n_rows)
    return [slice(r, r + size) for r in range(0, n_rows, size)]


def _ln_mod_rows(x_ref, shift, scale1, dst_ref, rows):
    for piece in _row_tiles(rows.stop - rows.start, ROW_CHUNK):
        sl = slice(rows.start + piece.start, rows.start + piece.stop)
        dst_ref[sl, :] = (_ln(x_ref[sl, :]) * scale1 + shift).astype(dst_ref.dtype)


def _residual_ln_store(res, y, gate, gain, bias, o_ref, rows):
    for piece in _row_tiles(rows.stop - rows.start, ROW_CHUNK):
        sl = slice(rows.start + piece.start, rows.start + piece.stop)
        z = ALPHA * res[sl, :] + gate * y[piece, :]
        o_ref[sl, :] = _ln(z) * gain + bias


def _rope(a, cos, sin):
    lane = lax.broadcasted_iota(jnp.int32, a.shape, 1)
    first = (lane % (HEAD_DIM // 2)) < (HEAD_DIM // 4)
    partner = jnp.where(first, pltpu.roll(a, HEAD_DIM - HEAD_DIM // 4, 1), pltpu.roll(a, HEAD_DIM // 4, 1))
    return a * cos + partner * sin


def _inproj_lat_kernel(x_ref, mod_ref, w_ref, cos_ref, sin_ref,
                       q_ref, qr_ref, k_ref, v_ref, xo_ref, go_ref, f_ref, xn_ref):
    shift = mod_ref[0:1, :]
    scale1 = 1.0 + mod_ref[1:2, :]

    def emit_q(rows, cols, acc):
        q_ref[rows, cols] = (acc * ATTN_SCALE).astype(q_ref.dtype)
        for h in range(COL_TILE // HEAD_DIM):
            sl = slice(h * HEAD_DIM, (h + 1) * HEAD_DIM)
            dst = slice(cols.start + sl.start, cols.start + sl.stop)
            rot = _rope(acc[:, sl], cos_ref[rows, :], sin_ref[rows, :])
            qr_ref[rows, dst] = (rot * ATTN_SCALE).astype(qr_ref.dtype)

    def emit_k(rows, cols, acc):
        for h in range(COL_TILE // HEAD_DIM):
            sl = slice(h * HEAD_DIM, (h + 1) * HEAD_DIM)
            dst = slice(cols.start + sl.start, cols.start + sl.stop)
            k_ref[rows, dst] = _rope(acc[:, sl], cos_ref[rows, :], sin_ref[rows, :]).astype(k_ref.dtype)

    def emit_to(ref):
        def emit(rows, cols, acc):
            ref[rows, cols] = acc.astype(ref.dtype)
        return emit

    half = [slice(0, COL_TILE), slice(COL_TILE, 2 * COL_TILE)]
    plan = ([(emit_q, c) for c in half] + [(emit_k, c) for c in half] + [(emit_to(v_ref), c) for c in half]
            + [(emit_to(xo_ref), half[0]), (emit_to(go_ref), half[0]), (emit_to(f_ref), half[0])])
    for rows in _row_tiles(x_ref.shape[0], MM_SUB):
        _ln_mod_rows(x_ref, shift, scale1, xn_ref, rows)
        for n, (emit, cols) in enumerate(plan):
            emit(rows, cols, _dot(xn_ref[rows, :], w_ref[:, n * COL_TILE:(n + 1) * COL_TILE]))


def _inproj_lat_call(x2d, mod4, w_in, layer, cos_t, sin_t):
    m_rows = x2d.shape[0]
    tm = 512
    tiles_per_seq = SEQ // tm
    row = lambda m: (m, 0)
    nbytes = (2 * tm * D_MODEL * 4 + tm * D_MODEL * 2 + D_MODEL * IN_W * 2 + 4 * tm * HEAD_DIM * 4
              + 2 * tm * (4 * NA_W * 2 + 2 * LRU_W * 4 + FNET_W * 2) + 6 * MM_SUB * COL_TILE * 4
              + 4 * ROW_CHUNK * D_MODEL * 4)
    bf = lambda w: jax.ShapeDtypeStruct((m_rows, w), BF16)
    ff = lambda w: jax.ShapeDtypeStruct((m_rows, w), F32)
    widths = [NA_W, NA_W, NA_W, NA_W, LRU_W, LRU_W, FNET_W]
    return pl.pallas_call(
        _inproj_lat_kernel,
        grid=(m_rows // tm,),
        in_specs=[
            pl.BlockSpec((tm, D_MODEL), row),
            pl.BlockSpec((None, None, 6, D_MODEL), lambda m: (layer, m // tiles_per_seq, 0, 0)),
            pl.BlockSpec((D_MODEL, IN_W), lambda m: (0, 0), pipeline_mode=pl.Buffered(1)),
            pl.BlockSpec((tm, HEAD_DIM), lambda m: (m % tiles_per_seq, 0)),
            pl.BlockSpec((tm, HEAD_DIM), lambda m: (m % tiles_per_seq, 0)),
        ],
        out_specs=[pl.BlockSpec((tm, w), row) for w in widths],
        out_shape=[bf(NA_W), bf(NA_W), bf(NA_W), bf(NA_W), ff(LRU_W), ff(LRU_W), bf(FNET_W)],
        scratch_shapes=[pltpu.VMEM((tm, D_MODEL), BF16)],
        compiler_params=_params(("arbitrary",), nbytes),
        name="inproj_latent",
    )(x2d, mod4, w_in, cos_t, sin_t)


def _inproj_ctx_kernel(tile_lo, with_q, with_gf, x_ref, mod_ref, w_ref, *refs):
    refs = list(refs)
    xn_ref = refs.pop()
    q_ref = refs.pop(0) if with_q else None
    k_ref, v_ref, xo_ref = refs[0], refs[1], refs[2]
    go_ref, f_ref = (refs[3], refs[4]) if with_gf else (None, None)
    n = pl.program_id(1) + tile_lo

    @pl.when(pl.program_id(1) == 0)
    def _():
        for rows in _row_tiles(x_ref.shape[0], MM_SUB):
            _ln_mod_rows(x_ref, mod_ref[0:1, :], 1.0 + mod_ref[1:2, :], xn_ref, rows)

    acc = _dot(xn_ref[...], w_ref[...])

    if with_q:
        @pl.when(n < 2)
        def _():
            q_ref[...] = (acc * ATTN_SCALE).astype(q_ref.dtype)

    @pl.when((n >= 2) & (n < 4))
    def _():
        k_ref[...] = acc.astype(k_ref.dtype)

    @pl.when((n >= 4) & (n < 6))
    def _():
        v_ref[...] = acc.astype(v_ref.dtype)

    @pl.when(n == 6)
    def _():
        xo_ref[...] = acc

    if with_gf:
        @pl.when(n == 7)
        def _():
            go_ref[...] = acc

        @pl.when(n == 8)
        def _():
            f_ref[...] = acc.astype(f_ref.dtype)


def _inproj_ctx_call(c2d, mod4, w_in, layer, full):
    m_rows = c2d.shape[0]
    tm = m_rows
    tile_lo, tile_hi = (0, N_COL_TILES) if full else (2, 7)

    def col(lo):
        return lambda m, n: (m, jnp.clip(n + tile_lo - lo, 0, 1))

    one = lambda m, n: (m, 0)
    bf = lambda w: jax.ShapeDtypeStruct((m_rows, w), BF16)
    ff = lambda w: jax.ShapeDtypeStruct((m_rows, w), F32)
    out_specs, out_shape = [], []
    if full:
        out_specs.append(pl.BlockSpec((tm, COL_TILE), col(0)))
        out_shape.append(bf(NA_W))
    out_specs += [pl.BlockSpec((tm, COL_TILE), col(2)), pl.BlockSpec((tm, COL_TILE), col(4)),
                  pl.BlockSpec((tm, COL_TILE), one)]
    out_shape += [bf(NA_W), bf(NA_W), ff(LRU_W)]
    if full:
        out_specs += [pl.BlockSpec((tm, COL_TILE), one), pl.BlockSpec((tm, COL_TILE), one)]
        out_shape += [ff(LRU_W), bf(FNET_W)]
    nbytes = (2 * tm * D_MODEL * 4 + tm * D_MODEL * 2 + 2 * D_MODEL * COL_TILE * 2
              + 2 * 6 * tm * COL_TILE * 4 + 3 * tm * COL_TILE * 4 + 4 * ROW_CHUNK * D_MODEL * 4)
    return pl.pallas_call(
        functools.partial(_inproj_ctx_kernel, tile_lo, full, full),
        grid=(1, tile_hi - tile_lo),
        in_specs=[
            pl.BlockSpec((tm, D_MODEL), lambda m, n: (m, 0)),
            pl.BlockSpec((None, None, 6, D_MODEL), lambda m, n: (layer, CTX_MOD_ROW, 0, 0)),
            pl.BlockSpec((D_MODEL, COL_TILE), lambda m, n: (0, n + tile_lo)),
        ],
        out_specs=out_specs,
        out_shape=out_shape,
        scratch_shapes=[pltpu.VMEM((tm, D_MODEL), BF16)],
        compiler_params=_params(("arbitrary", "arbitrary"), nbytes),
        name="inproj_context",
    )(c2d, mod4, w_in)


QBLK_ROWS = 4
KBLK_ROWS = 12
N_QBLK = GRID_H // QBLK_ROWS
QBLK = QBLK_ROWS * GRID_W
KBLK = KBLK_ROWS * GRID_W
KEY_TILE = V7X_LANES
KEY_TILES = KBLK // KEY_TILE
N_DR = 2 * WIN_H - 1
BIAS_BOTH, BIAS_SECOND, BIAS_FIRST = "both", "second", "first"


def _kblk_start(first_query_row):
    return int(np.clip(first_query_row - WIN_H // 2, 0, GRID_H - KBLK_ROWS))


def _bias_entries(first_query_row):
    entries = []
    for u in range(QBLK_ROWS):
        q_row = first_query_row + u
        row_start = int(np.clip(q_row - WIN_H // 2, 0, GRID_H - WIN_H))
        row = []
        for c in range(KEY_TILES):
            k_rows = [_kblk_start(first_query_row) + 2 * c + i for i in range(2)]
            inside = [row_start <= kr < row_start + WIN_H for kr in k_rows]
            dr = [kr - q_row + (WIN_H - 1) for kr in k_rows]
            if inside[0] and inside[1]:
                row.append((BIAS_BOTH, dr[0]))
            elif inside[1]:
                row.append((BIAS_SECOND, dr[1]))
            elif inside[0]:
                row.append((BIAS_FIRST, dr[0]))
            else:
                row.append(None)
        entries.append(row)
    return entries


BIAS_TABLE = sorted({e for b in range(N_QBLK) for row in _bias_entries(b * QBLK_ROWS) for e in row if e})
BIAS_SLOT = {e: i for i, e in enumerate(BIAS_TABLE)}
N_BIAS = len(BIAS_TABLE)


def _bias_plan(first_query_row):
    return [[BIAS_SLOT[e] if e else None for e in row] for row in _bias_entries(first_query_row)]


def _attn_kernel(n_riders, q_ref, qr_ref, k_ref, v_ref, kc_ref, vc_ref, bias_ref, *rest):
    o_ref = rest[n_riders]
    _run_riders(rest[:n_riders], rest[n_riders + 1:])
    kc = kc_ref[...]
    vc = vc_ref[...]
    zero_tile = jnp.zeros((GRID_W, KEY_TILE), BF16)

    def one_block(q0, k0, plan):
        s = _dot_nt(qr_ref[pl.ds(q0, QBLK), :], k_ref[pl.ds(k0, KBLK), :])
        sc = _dot_nt(q_ref[pl.ds(q0, QBLK), :], kc)
        p_rows, pc_rows, denoms = [], [], []
        for u in range(QBLK_ROWS):
            rows = slice(u * GRID_W, (u + 1) * GRID_W)
            band = {c: s[rows, c * KEY_TILE:(c + 1) * KEY_TILE] + bias_ref[idx]
                    for c, idx in enumerate(plan[u]) if idx is not None}
            ctx_tiles = [sc[rows, c * KEY_TILE:(c + 1) * KEY_TILE] for c in range(CTX_LEN // KEY_TILE)]
            tiles = list(band.values()) + ctx_tiles
            m = jnp.max(functools.reduce(jnp.maximum, tiles), axis=-1, keepdims=True)
            p_band = {c: jnp.exp(t - m) for c, t in band.items()}
            p_ctx = [jnp.exp(t - m) for t in ctx_tiles]
            total = functools.reduce(jnp.add, list(p_band.values()) + p_ctx)
            denoms.append(jnp.sum(total, axis=-1, keepdims=True))
            p_rows.append(jnp.concatenate(
                [p_band[c].astype(BF16) if c in p_band else zero_tile for c in range(KEY_TILES)], axis=1))
            pc_rows.append(jnp.concatenate([t.astype(BF16) for t in p_ctx], axis=1))
        p = jnp.concatenate(p_rows, axis=0)
        pc = jnp.concatenate(pc_rows, axis=0)
        o = _dot(p, v_ref[pl.ds(k0, KBLK), :]) + _dot(pc, vc)
        o_ref[pl.ds(q0, QBLK), :] = (o / jnp.concatenate(denoms, axis=0)).astype(o_ref.dtype)

    def static_block(b):
        first_row = b * QBLK_ROWS
        one_block(b * QBLK, _kblk_start(first_row) * GRID_W, _bias_plan(first_row))

    interior_plan = _bias_plan(QBLK_ROWS)

    def interior_block(b):
        q0 = pl.multiple_of(b * QBLK, QBLK)
        k0 = pl.multiple_of((b * QBLK_ROWS - WIN_H // 2) * GRID_W, GRID_W)
        one_block(q0, k0, interior_plan)

    def body(i, carry):
        interior_block(2 * i + 1)
        interior_block(2 * i + 2)
        return carry

    for b in range(N_QBLK):
        static_block(b)


def _attn_call(q, qr, k, v, kc, vc, bias, layer, riders):
    seq_blk = lambda b, h: (b, h)
    n_steps = BATCH * NA_HEADS
    step = lambda b, h: b * NA_HEADS + h
    r_in, r_out, r_shape, r_bytes = _rider_specs(riders, n_steps, step)
    nbytes = (2 * (5 * SEQ * HEAD_DIM * 2 + 2 * CTX_LEN * HEAD_DIM * 2 + N_BIAS * GRID_W * KEY_TILE * 4)
              + 8 * QBLK * (KBLK + CTX_LEN) * 4 + r_bytes)
    res = pl.pallas_call(
        functools.partial(_attn_kernel, len(riders)),
        grid=(BATCH, NA_HEADS),
        in_specs=[
            pl.BlockSpec((SEQ, HEAD_DIM), seq_blk),
            pl.BlockSpec((SEQ, HEAD_DIM), seq_blk),
            pl.BlockSpec((SEQ, HEAD_DIM), seq_blk),
            pl.BlockSpec((SEQ, HEAD_DIM), seq_blk),
            pl.BlockSpec((CTX_LEN, HEAD_DIM), seq_blk),
            pl.BlockSpec((CTX_LEN, HEAD_DIM), seq_blk),
            pl.BlockSpec((None, None, N_BIAS, GRID_W, KEY_TILE), lambda b, h: (layer, h, 0, 0, 0)),
        ] + r_in,
        out_specs=[pl.BlockSpec((SEQ, HEAD_DIM), seq_blk)] + r_out,
        out_shape=[jax.ShapeDtypeStruct((BATCH * SEQ, NA_W), BF16)] + r_shape,
        compiler_params=_params(("arbitrary", "arbitrary"), nbytes),
        name="neighbourhood_attention",
    )(q, qr, k, v, kc, vc, bias, *[r[0] for r in riders])
    return res[0], res[1:]


def _ctx_attn_kernel(q_ref, k_ref, v_ref, o_ref):
    s = _dot_nt(q_ref[...], k_ref[...])
    m = jnp.max(s, axis=-1, keepdims=True)
    p = jnp.exp(s - m)
    denom = jnp.sum(p, axis=-1, keepdims=True)
    o_ref[...] = (_dot(p.astype(BF16), v_ref[...]) / denom).astype(o_ref.dtype)


def _ctx_attn_call(q, k, v):
    blk = pl.BlockSpec((CTX_LEN, HEAD_DIM), lambda b, h: (b, h))
    return pl.pallas_call(
        _ctx_attn_kernel,
        grid=(BATCH, NA_HEADS),
        in_specs=[blk, blk, blk],
        out_specs=blk,
        out_shape=jax.ShapeDtypeStruct((BATCH * CTX_LEN, NA_W), BF16),
        compiler_params=_params(("arbitrary", "arbitrary"), 16 << 20),
        name="context_attention",
    )(q, k, v)


def _attn_bias_table(rpb):
    col = np.arange(GRID_W)
    col_start = np.clip(col - WIN_W // 2, 0, GRID_W - WIN_W)
    in_win = (col[None, :] >= col_start[:, None]) & (col[None, :] < col_start[:, None] + WIN_W)
    dc = np.clip(col[None, :] - col[:, None] + (WIN_W - 1), 0, 2 * WIN_W - 2)
    onehot = (in_win[None] & (dc[None] == np.arange(2 * WIN_W - 1)[:, None, None])).astype(np.float32)
    t = jnp.einsum('lhdj,jqk->lhdqk', rpb, jnp.asarray(onehot), precision=lax.Precision.HIGHEST)
    t = jnp.where(in_win[None, None, None], t, NEG_INF)
    masked = jnp.full((DEPTH, NA_HEADS, GRID_W, GRID_W), NEG_INF, F32)
    tiles = []
    for kind, dr in BIAS_TABLE:
        left = masked if kind == BIAS_SECOND else t[:, :, dr]
        right = masked if kind == BIAS_FIRST else t[:, :, dr + 1 if kind == BIAS_BOTH else dr]
        tiles.append(jnp.concatenate([left, right], axis=-1))
    return jnp.stack(tiles, axis=2)


def _rope_tables():
    quarter = HEAD_DIM // 4
    inv = ROPE_THETA ** (-jnp.arange(quarter, dtype=F32) / quarter)
    t = jnp.arange(SEQ)
    ang_r = (t // GRID_W).astype(F32)[:, None] * inv
    ang_c = (t % GRID_W).astype(F32)[:, None] * inv
    cos = jnp.concatenate([jnp.cos(ang_r), jnp.cos(ang_r), jnp.cos(ang_c), jnp.cos(ang_c)], -1)
    sin = jnp.concatenate([-jnp.sin(ang_r), jnp.sin(ang_r), -jnp.sin(ang_c), jnp.sin(ang_c)], -1)
    return cos, sin


HALO = V7X_SUBLANES
N_SEG = V7X_SUBLANES
SEG_PAD = V7X_SUBLANES
LRU_SCAN_UNROLL = 8


def _lru_coeffs(xp_ref, n_rows, cw_ref, cb_ref, w4, b4_ref, sp, a_refs, u_refs):
    seg = n_rows // N_SEG
    pitch = seg + SEG_PAD
    for s in range(N_SEG):
        base = HALO + s * seg
        xc = cb_ref[...] + xp_ref[base - CONV_W // 2:base - CONV_W // 2 + seg, :] * cw_ref[0:1, :]
        for j in range(1, CONV_W):
            off = base - CONV_W // 2 + j
            xc = xc + xp_ref[off:off + seg, :] * cw_ref[j:j + 1, :]
        th = jnp.tanh(_dot(xc.astype(BF16), w4) + b4_ref[...])
        half_xc = 0.5 * xc
        for d in range(2):
            r2 = th[:, (2 * d) * LRU_BW:(2 * d + 1) * LRU_BW] + 1.0
            i2 = th[:, (2 * d + 1) * LRU_BW:(2 * d + 2) * LRU_BW] + 1.0
            log_a = r2 * sp[d:d + 1, :]
            a = jnp.exp(log_a)
            a_refs[d][s * pitch:s * pitch + seg, :] = a
            one_minus_a2 = -jnp.tanh(log_a) * (a * a + 1.0)
            u_refs[d][s * pitch:s * pitch + seg, :] = jnp.sqrt(one_minus_a2) * (i2 * half_xc)


def _lru_local_scan(n_rows, coef_f, coef_b, state_f, state_b):
    seg = n_rows // N_SEG
    pitch = seg + SEG_PAD
    zero = jnp.zeros((N_SEG, LRU_BW), F32)
    one = jnp.ones((N_SEG, LRU_BW), F32)

    def step(coef, state, row, h, p):
        rows = pl.ds(row, N_SEG, stride=pitch)
        a = coef[0][rows, :]
        h = a * h + coef[1][rows, :]
        p = p * a
        state[0][rows, :] = p
        state[1][rows, :] = h
        return h, p

    def body(i, carry):
        hf, pf, hb, pb = carry
        for j in range(LRU_SCAN_UNROLL):
            t = i * LRU_SCAN_UNROLL + j
            hf, pf = step(coef_f, state_f, t, hf, pf)
            hb, pb = step(coef_b, state_b, seg - 1 - t, hb, pb)
        return hf, pf, hb, pb

    lax.fori_loop(0, seg // LRU_SCAN_UNROLL, body, (zero, one, zero, one))


def _lru_carries(n_rows, h_in_f, h_in_b, af, uf, ab, ub):
    seg = n_rows // N_SEG
    pitch = seg + SEG_PAD
    cf, cb = [h_in_f], [h_in_b]
    for s in range(N_SEG):
        last = s * pitch + seg - 1
        cf.append(uf[last:last + 1, :] + af[last:last + 1, :] * cf[-1])
        first = (N_SEG - 1 - s) * pitch
        cb.append(ub[first:first + 1, :] + ab[first:first + 1, :] * cb[-1])
    return cf[:N_SEG], cb[:N_SEG][::-1], cf[N_SEG], cb[N_SEG]


def _lru_emit(n_rows, cf, cb, af, uf, ab, ub, g_ref, o_ref):
    seg = n_rows // N_SEG
    pitch = seg + SEG_PAD
    for s in range(N_SEG):
        src = slice(s * pitch, s * pitch + seg)
        dst = slice(s * seg, (s + 1) * seg)
        y = (uf[src, :] + af[src, :] * cf[s]) + (ub[src, :] + ab[src, :] * cb[s])
        o_ref[dst, :] = (y * _gelu_tanh(g_ref[dst, :])).astype(o_ref.dtype)


def _lru_kernel(ctx_out, n_riders, x_ref, g_ref, xc_ref, *refs):
    refs = list(refs)
    gc_ref = refs.pop(0) if ctx_out else None
    cw_ref, cb_ref, w4_ref, b4_ref, lam_ref = refs[:5]
    rider_in, refs = refs[5:5 + n_riders], refs[5 + n_riders:]
    o_ref = refs.pop(0)
    oc_ref = refs.pop(0) if ctx_out else None
    rider_out, refs = refs[:n_riders], refs[n_riders:]
    xp_ref, af, uf, ab, ub, pf, sf, pb, sb = refs
    _run_riders(rider_in, rider_out)

    lam = lam_ref[...]
    z = -lam
    sp = (-0.5 * LRU_C) * (jnp.maximum(z, 0.0) + jnp.log1p(jnp.exp(-jnp.abs(z))))
    w4 = w4_ref[...].astype(BF16)
    zeros_halo = jnp.zeros((HALO, LRU_BW), F32)
    h0 = jnp.zeros((1, LRU_BW), F32)

    xp_ref[0:HALO, :] = zeros_halo
    xp_ref[HALO:HALO + CTX_LEN, :] = xc_ref[...]
    xp_ref[HALO + CTX_LEN:2 * HALO + CTX_LEN, :] = zeros_halo
    _lru_coeffs(xp_ref, CTX_LEN, cw_ref, cb_ref, w4, b4_ref, sp, (af, ab), (uf, ub))
    _lru_local_scan(CTX_LEN, (af, uf), (ab, ub), (pf, sf), (pb, sb))
    cf, cb, hf, hb = _lru_carries(CTX_LEN, h0, h0, pf, sf, pb, sb)
    if ctx_out:
        _lru_emit(CTX_LEN, cf, cb, pf, sf, pb, sb, gc_ref, oc_ref)

    xp_ref[HALO:HALO + SEQ, :] = x_ref[...]
    xp_ref[HALO + SEQ:2 * HALO + SEQ, :] = zeros_halo
    _lru_coeffs(xp_ref, SEQ, cw_ref, cb_ref, w4, b4_ref, sp, (af, ab), (uf, ub))
    _lru_local_scan(SEQ, (af, uf), (ab, ub), (pf, sf), (pb, sb))
    cf, cb, _, _ = _lru_carries(SEQ, hf, hb, pf, sf, pb, sb)
    _lru_emit(SEQ, cf, cb, pf, sf, pb, sb, g_ref, o_ref)


def _lru_call(ctx_out, xl, gl, xc, gc, conv_w_l, conv_b_l, w4, b4, lam_l, riders):
    lat = pl.BlockSpec((SEQ, LRU_BW), lambda b, j: (b, j))
    cx = pl.BlockSpec((CTX_LEN, LRU_BW), lambda b, j: (b, j))
    r_in, r_out, r_shape, r_bytes = _rider_specs(riders, BATCH * LRU_BLOCKS, lambda b, j: b * LRU_BLOCKS + j)
    in_specs = [lat, lat, cx] + ([cx] if ctx_out else []) + [
        pl.BlockSpec((CONV_W, LRU_BW), lambda b, j: (0, j)),
        pl.BlockSpec((1, LRU_BW), lambda b, j: (0, j)),
        pl.BlockSpec((None, LRU_BW, 4 * LRU_BW), lambda b, j: (j, 0, 0)),
        pl.BlockSpec((None, 1, 4 * LRU_BW), lambda b, j: (j, 0, 0)),
        pl.BlockSpec((2, LRU_BW), lambda b, j: (0, j)),
    ] + r_in
    out_specs = [lat] + ([cx] if ctx_out else []) + r_out
    out_shape = [jax.ShapeDtypeStruct((BATCH * SEQ, LRU_W), BF16)]
    if ctx_out:
        out_shape.append(jax.ShapeDtypeStruct((BATCH * CTX_LEN, LRU_W), BF16))
    out_shape += r_shape
    seq_bytes = SEQ * LRU_BW * 4
    args = ([xl, gl, xc] + ([gc] if ctx_out else []) + [conv_w_l, conv_b_l, w4, b4, lam_l]
            + [r[0] for r in riders])
    res = pl.pallas_call(
        functools.partial(_lru_kernel, ctx_out, len(riders)),
        grid=(BATCH, LRU_BLOCKS),
        in_specs=in_specs,
        out_specs=out_specs,
        out_shape=out_shape,
        scratch_shapes=([pltpu.VMEM((SEQ + 2 * HALO, LRU_BW), F32)]
                        + [pltpu.VMEM((SEQ + N_SEG * SEG_PAD, LRU_BW), F32)] * 8),
        compiler_params=_params(("arbitrary", "arbitrary"), 15 * seq_bytes + (8 << 20) + r_bytes),
        name="rglru",
    )(*args)
    n_main = 2 if ctx_out else 1
    return res[0], (res[1] if ctx_out else None), res[n_main:]


def _fourier_kernel(n_pos, blk, f_ref, ch_ref, sh_ref, cc_ref, sc_ref, rev_ref, alt_ref, w_ref, b_ref, o_ref,
                    ec_ref, es_ref, mir_ref, mid_ref):
    half = n_pos // 2
    n_lo = half // blk
    step = pl.program_id(1)
    norm = 1.0 / math.sqrt(n_pos * FNET_GW)
    w = w_ref[...].astype(BF16)

    def linear(y):
        return (_dot(y.astype(BF16), w) + b_ref[...]).astype(o_ref.dtype)

    @pl.when(step == 0)
    def _():
        for i in range(n_lo):
            lo = blk * (2 * n_lo - 1 - i)
            if i == 0:
                mirrored = _dot(rev_ref[:, 0:blk], f_ref[lo:lo + blk, :])
            else:
                mirrored = _dot(rev_ref[...], f_ref[lo:lo + 2 * blk, :])
            rows = slice(i * blk, (i + 1) * blk)
            x = f_ref[rows, :].astype(F32)
            even = (x + mirrored).astype(BF16)
            odd = (x - mirrored).astype(BF16)
            for g in range(FNET_GROUPS):
                sl = slice(g * FNET_GW, (g + 1) * FNET_GW)
                ec_ref[rows, sl] = _dot(even[:, sl], cc_ref[...]).astype(BF16)
                es_ref[rows, sl] = _dot(odd[:, sl], sc_ref[...]).astype(BF16)
        for g in range(FNET_GROUPS):
            sl = slice(g * FNET_GW, (g + 1) * FNET_GW)
            mid_ref[0:BF16_ROWS, sl] = _dot(f_ref[half:half + BF16_ROWS, sl], cc_ref[...])
        mid_ref[BF16_ROWS:2 * BF16_ROWS, :] = _dot(alt_ref[...], ec_ref[...])

    mid = mid_ref[0:1, :]

    @pl.when(step < n_lo)
    def _():
        a = _dot(ch_ref[...], ec_ref[...])
        b = _dot(sh_ref[...], es_ref[...])
        row = lax.broadcasted_iota(jnp.int32, (blk, 1), 0)
        base = jnp.where(row % 2 == 0, 1.0, -1.0) * mid
        o_ref[...] = linear((a - b + base) * norm)
        r0 = pl.multiple_of(step * blk, blk)
        mir_ref[pl.ds(r0, blk), :] = ((a + b + base) * norm).astype(BF16)

    @pl.when(step == n_lo)
    def _():
        y = _dot(rev_ref[:, 0:blk], mir_ref[(n_lo - 1) * blk:n_lo * blk, :])
        nyquist = (mid_ref[BF16_ROWS:BF16_ROWS + 1, :] + mid) * norm
        row = lax.broadcasted_iota(jnp.int32, (blk, 1), 0)
        o_ref[...] = linear(jnp.where(row == 0, nyquist, y))

    if n_lo > 1:
        @pl.when(step > n_lo)
        def _():
            r0 = pl.multiple_of((2 * n_lo - 1 - step) * blk, blk)
            o_ref[...] = linear(_dot(rev_ref[...], mir_ref[pl.ds(r0, 2 * blk), :]))


DFT_SPLIT = 64


def _dft_matrices(n, size):
    t = np.arange(size, dtype=np.int64)

    def table(k):
        ang = (2.0 * np.pi / n) * ((k[:, None] * t[None, :]) % n).astype(np.float64)
        return jnp.asarray(np.cos(ang), F32), jnp.asarray(np.sin(ang), F32)

    if size <= DFT_SPLIT:
        c, s = table(t)
        return c.astype(BF16), s.astype(BF16)
    c1, s1 = table(DFT_SPLIT * np.arange(size // DFT_SPLIT, dtype=np.int64))
    c2, s2 = table(np.arange(DFT_SPLIT, dtype=np.int64))
    c = c1[:, None, :] * c2[None, :, :] - s1[:, None, :] * s2[None, :, :]
    s = s1[:, None, :] * c2[None, :, :] + c1[:, None, :] * s2[None, :, :]
    return c.reshape(size, size).astype(BF16), s.reshape(size, size).astype(BF16)


FOURIER_BLK = 256


def _fourier_call(f2d, n_pos, fno_w_l, fno_b_l):
    half = n_pos // 2
    blk = min(FOURIER_BLK, half)
    steps = n_pos // blk
    n_lo = half // blk
    ch, sh = _dft_matrices(n_pos, half)
    cc, sc = _dft_matrices(FNET_GW, FNET_GW)
    rev = np.zeros((blk, 2 * blk), np.float32)
    rev[np.arange(1, blk), blk - np.arange(1, blk)] = 1.0
    rev[0, blk] = 1.0
    alt = np.zeros((BF16_ROWS, half), np.float32)
    alt[0] = 1.0 - 2.0 * (np.arange(half) % 2)
    const = lambda b, k: (0, 0)
    dft_tile = lambda b, k: (jnp.minimum(k, n_lo - 1), 0)
    nbytes = (2 * n_pos * FNET_W * 2 + 2 * 2 * blk * half * 2 + 3 * half * FNET_W * 2 + 2 * FNET_W * FNET_W * 4
              + 8 * blk * FNET_W * 4 + 2 * blk * 2 * blk * 2)
    return pl.pallas_call(
        functools.partial(_fourier_kernel, n_pos, blk),
        grid=(BATCH, steps),
        in_specs=[
            pl.BlockSpec((n_pos, FNET_W), lambda b, k: (b, 0)),
            pl.BlockSpec((blk, half), dft_tile),
            pl.BlockSpec((blk, half), dft_tile),
            pl.BlockSpec((FNET_GW, FNET_GW), const),
            pl.BlockSpec((FNET_GW, FNET_GW), const),
            pl.BlockSpec((blk, 2 * blk), const),
            pl.BlockSpec((BF16_ROWS, half), const),
            pl.BlockSpec((FNET_W, FNET_W), const),
            pl.BlockSpec((1, FNET_W), const),
        ],
        out_specs=pl.BlockSpec((blk, FNET_W), lambda b, k: (b * steps + k, 0)),
        out_shape=jax.ShapeDtypeStruct((BATCH * n_pos, FNET_W), BF16),
        scratch_shapes=[pltpu.VMEM((half, FNET_W), BF16), pltpu.VMEM((half, FNET_W), BF16),
                        pltpu.VMEM((half, FNET_W), BF16), pltpu.VMEM((2 * BF16_ROWS, FNET_W), F32)],
        compiler_params=_params(("arbitrary", "arbitrary"), nbytes),
        name="fourier_mix",
    )(f2d, ch, sh, cc, sc, jnp.asarray(rev, BF16), jnp.asarray(alt, BF16), fno_w_l, fno_b_l)


def _outproj_kernel(na_ref, lru_ref, f_ref, res_ref, mod_ref, w_ref, g_ref, b_ref, o_ref):
    gate = mod_ref[2:3, :]
    for rows in _row_tiles(res_ref.shape[0], MM_SUB):
        y = (_dot(na_ref[rows, :], w_ref[0:NA_W, :])
             + _dot(lru_ref[rows, :], w_ref[NA_W:NA_W + LRU_W, :])
             + _dot(f_ref[rows, :], w_ref[NA_W + LRU_W:D_MODEL, :]))
        _residual_ln_store(res_ref, y, gate, g_ref[...], b_ref[...], o_ref, rows)


def _outproj_call(na, lru, f, res, mod4, mod_row, w_out, layer, ln_g, ln_b):
    m_rows = res.shape[0]
    tm = 512
    row = lambda m: (m, 0)
    nbytes = (2 * tm * D_MODEL * 2 + 2 * 2 * tm * D_MODEL * 4 + D_MODEL * D_MODEL * 2 + 2 * MM_SUB * D_MODEL * 4
              + 4 * ROW_CHUNK * D_MODEL * 4)
    return pl.pallas_call(
        _outproj_kernel,
        grid=(m_rows // tm,),
        in_specs=[
            pl.BlockSpec((tm, NA_W), row),
            pl.BlockSpec((tm, LRU_W), row),
            pl.BlockSpec((tm, FNET_W), row),
            pl.BlockSpec((tm, D_MODEL), row),
            pl.BlockSpec((None, None, 6, D_MODEL), lambda m: (layer, mod_row(m * tm), 0, 0)),
            pl.BlockSpec((D_MODEL, D_MODEL), lambda m: (0, 0), pipeline_mode=pl.Buffered(1)),
            pl.BlockSpec((None, 1, D_MODEL), lambda m: (layer, 0, 0)),
            pl.BlockSpec((None, 1, D_MODEL), lambda m: (layer, 0, 0)),
        ],
        out_specs=pl.BlockSpec((tm, D_MODEL), row),
        out_shape=jax.ShapeDtypeStruct((m_rows, D_MODEL), F32),
        compiler_params=_params(("arbitrary",), nbytes),
        name="outproj_residual",
    )(na, lru, f, res, mod4, w_out, ln_g, ln_b)


MLP_TF = 512
MLP_TILES = D_FF // MLP_TF


def _mlp_kernel(x_ref, mod_ref, w1_ref, b1_ref, w2_ref, b2_ref, g_ref, b_ref, o_ref, v_ref, h0_ref, h1_ref):
    j = pl.program_id(1)
    tm = x_ref.shape[0]

    def up(h_out, rows=slice(None)):
        h = _dot(v_ref[rows, :], w1_ref[...]) + b1_ref[...]
        h_out[rows, :] = jnp.square(jnp.maximum(h, 0.0)).astype(BF16)

    def down(h_in, first):
        for c in range(D_MODEL // COL_TILE):
            sl = slice(c * COL_TILE, (c + 1) * COL_TILE)
            part = _dot(h_in[...], w2_ref[:, sl])
            if first:
                o_ref[:, sl] = part
            else:
                o_ref[:, sl] += part

    @pl.when(j == 0)
    def _():
        shift = mod_ref[3:4, :]
        scale1 = 1.0 + mod_ref[4:5, :]
        for rows in _row_tiles(tm, MM_SUB):
            _ln_mod_rows(x_ref, shift, scale1, v_ref, rows)
            up(h0_ref, rows)

    @pl.when(j == 1)
    def _():
        down(h0_ref, True)
        up(h1_ref)

    @pl.when((j > 1) & (j < MLP_TILES) & (j % 2 == 0))
    def _():
        down(h1_ref, False)
        up(h0_ref)

    @pl.when((j > 1) & (j < MLP_TILES) & (j % 2 == 1))
    def _():
        down(h0_ref, False)
        up(h1_ref)

    @pl.when(j == MLP_TILES)
    def _():
        h_last = h1_ref if (MLP_TILES - 1) % 2 else h0_ref
        gate = mod_ref[5:6, :]
        for rows in _row_tiles(tm, MM_SUB):
            y = o_ref[rows, :] + _dot(h_last[rows, :], w2_ref[...]) + b2_ref[...]
            _residual_ln_store(x_ref, y, gate, g_ref[...], b_ref[...], o_ref, rows)


def _mlp_call(x1, mod4, mod_row, tm, w1, b1, w2, b2, layer, ln_g, ln_b):
    m_rows = x1.shape[0]
    row = lambda m, j: (m, 0)
    vec = lambda m, j: (layer, 0, 0)
    nbytes = (4 * tm * D_MODEL * 4 + tm * D_MODEL * 2 + 2 * 2 * D_MODEL * MLP_TF * 2 + 2 * tm * MLP_TF * 2
              + tm * MLP_TF * 4 + tm * COL_TILE * 4 + 2 * MM_SUB * D_MODEL * 4 + 4 * ROW_CHUNK * D_MODEL * 4)
    return pl.pallas_call(
        _mlp_kernel,
        grid=(m_rows // tm, MLP_TILES + 1),
        in_specs=[
            pl.BlockSpec((tm, D_MODEL), row),
            pl.BlockSpec((None, None, 6, D_MODEL), lambda m, j: (layer, mod_row(m * tm), 0, 0)),
            pl.BlockSpec((D_MODEL, MLP_TF), lambda m, j: (0, jnp.minimum(j, MLP_TILES - 1))),
            pl.BlockSpec((None, 1, MLP_TF), lambda m, j: (layer, 0, jnp.minimum(j, MLP_TILES - 1))),
            pl.BlockSpec((MLP_TF, D_MODEL), lambda m, j: (jnp.maximum(j - 1, 0), 0)),
            pl.BlockSpec((None, 1, D_MODEL), vec),
            pl.BlockSpec((None, 1, D_MODEL), vec),
            pl.BlockSpec((None, 1, D_MODEL), vec),
        ],
        out_specs=pl.BlockSpec((tm, D_MODEL), row),
        out_shape=jax.ShapeDtypeStruct((m_rows, D_MODEL), F32),
        scratch_shapes=[pltpu.VMEM((tm, D_MODEL), BF16), pltpu.VMEM((tm, MLP_TF), BF16),
                        pltpu.VMEM((tm, MLP_TF), BF16)],
        compiler_params=_params(("arbitrary", "arbitrary"), nbytes),
        name="mlp_residual",
    )(x1, mod4, w1, b1, w2, b2, ln_g, ln_b)


def kernel(x, c, ctx, c_ctx, w_mod, b_mod, w_in, rpb, conv_w, conv_b, lru_wa, lru_ba, lru_wx, lru_bx, lru_lambda,
           fno_w, fno_b, w_out, ln1_g, ln1_b, w_fc1, b_fc1, w_fc2, b_fc2, ln2_g, ln2_b):
    xl = x.reshape(BATCH * SEQ, D_MODEL)
    xc = ctx.reshape(BATCH * CTX_LEN, D_MODEL)
    s_in = jnp.concatenate([c, c_ctx[None], jnp.zeros((MOD_ROWS - BATCH - 1, D_MODEL), F32)], 0)
    mod4 = _mod_call(s_in, w_mod, b_mod).reshape(DEPTH, MOD_ROWS, 6, D_MODEL)
    cos_t, sin_t = _rope_tables()
    bias_tab = _attn_bias_table(rpb)
    w_in_l = w_in[0].astype(BF16)
    vec3 = lambda a: a.reshape(DEPTH, 1, a.shape[-1])
    ln1_g3, ln1_b3, ln2_g3, ln2_b3 = vec3(ln1_g), vec3(ln1_b), vec3(ln2_g), vec3(ln2_b)
    b_fc1_3, b_fc2_3 = vec3(b_fc1), vec3(b_fc2)
    lat_row = lambda r0: r0 // SEQ
    ctx_row = lambda r0: CTX_MOD_ROW

    for layer in range(DEPTH):
        ctx_out = layer < DEPTH - 1
        q, qr, k, v, xo, go, f = _inproj_lat_call(xl, mod4, w_in_l, layer, cos_t, sin_t)
        if ctx_out:
            qc, kc, vc, xoc, goc, fc = _inproj_ctx_call(xc, mod4, w_in_l, layer, True)
        else:
            kc, vc, xoc = _inproj_ctx_call(xc, mod4, w_in_l, layer, False)
            goc = None

        na, (w_out_l, w_fc1_l, w_fc2_l) = _attn_call(
            q, qr, k, v, kc, vc, bias_tab, layer,
            [_rider(w_out, layer, 0), _rider(w_fc1, layer, 1), _rider(w_fc2, layer, 0)])

        w4 = 0.5 * jnp.concatenate([lru_wa[layer, 0], lru_wx[layer, 0], lru_wa[layer, 1], lru_wx[layer, 1]], -1)
        blk = lambda a: a.reshape(LRU_BLOCKS, 1, LRU_BW)
        b4 = 0.5 * jnp.concatenate([blk(lru_ba[layer, 0]), blk(lru_bx[layer, 0]),
                                    blk(lru_ba[layer, 1]), blk(lru_bx[layer, 1])], -1)
        next_w_in = [_rider(w_in, layer + 1, 0)] if layer + 1 < DEPTH else []
        lru, lru_c, cast = _lru_call(ctx_out, xo, go, xoc, goc, conv_w[layer], conv_b[layer][None], w4, b4,
                                     lru_lambda[layer], next_w_in)
        if next_w_in:
            w_in_l = cast[0]

        fm = _fourier_call(f, SEQ, fno_w[layer], fno_b[layer][None])
        x1 = _outproj_call(na, lru, fm, xl, mod4, lat_row, w_out_l, layer, ln1_g3, ln1_b3)
        xl = _mlp_call(x1, mod4, lat_row, 1024, w_fc1_l, b_fc1_3, w_fc2_l, b_fc2_3, layer, ln2_g3, ln2_b3)

        if ctx_out:
            na_c = _ctx_attn_call(qc, kc, vc)
            fm_c = _fourier_call(fc, CTX_LEN, fno_w[layer], fno_b[layer][None])
            c1 = _outproj_call(na_c, lru_c, fm_c, xc, mod4, ctx_row, w_out_l, layer, ln1_g3, ln1_b3)
            xc = _mlp_call(c1, mod4, ctx_row, 512, w_fc1_l, b_fc1_3, w_fc2_l, b_fc2_3, layer, ln2_g3, ln2_b3)

    return xl.reshape(BATCH, SEQ, D_MODEL)
```

```python
import functools
import math

import jax
import jax.numpy as jnp
import numpy as np
from jax import lax
from jax.experimental import pallas as pl
from jax.experimental.pallas import tpu as pltpu

F32 = jnp.float32
BF16 = jnp.bfloat16

D_MODEL = 2048
BATCH = 2
SEQ = 4096
DEPTH = 2
GRID_W = 64
GRID_H = SEQ // GRID_W
CTX_LEN = 256
HEAD_DIM = 128
NA_W = D_MODEL // 2
NA_HEADS = NA_W // HEAD_DIM
WIN_H = 8
WIN_W = 16
LRU_W = D_MODEL // 4
LRU_BLOCKS = 4
LRU_BW = LRU_W // LRU_BLOCKS
CONV_W = 4
LRU_C = 8.0
FNET_W = D_MODEL // 4
FNET_GROUPS = 4
FNET_GW = FNET_W // FNET_GROUPS
IN_W = 3 * NA_W + 2 * LRU_W + FNET_W
D_FF = 4 * D_MODEL
ROPE_THETA = 10000.0
LN_EPS = 1e-5
NEG_INF = -1e30
ALPHA = (2.0 * DEPTH) ** 0.25
ATTN_SCALE = HEAD_DIM ** -0.5

V7X_LANES = 128
V7X_SUBLANES = 8
BF16_ROWS = 2 * V7X_SUBLANES
V7X_VMEM_BYTES = 64 * 1024 * 1024
VMEM_CEILING = V7X_VMEM_BYTES - 6 * 1024 * 1024

COL_TILE = 512
N_COL_TILES = IN_W // COL_TILE
ROW_CHUNK = 128
MOD_ROWS = 8
CTX_MOD_ROW = BATCH


def _vmem_limit(nbytes):
    return int(min(VMEM_CEILING, nbytes * 5 // 4 + (4 << 20)))


def _params(semantics, nbytes):
    return pltpu.CompilerParams(dimension_semantics=semantics, vmem_limit_bytes=_vmem_limit(nbytes))


def _ln(x):
    mu = jnp.mean(x, axis=-1, keepdims=True)
    xc = x - mu
    var = jnp.mean(xc * xc, axis=-1, keepdims=True)
    return xc * lax.rsqrt(var + LN_EPS)


def _sigmoid(x):
    return 1.0 / (1.0 + jnp.exp(-x))


def _gelu_tanh(x):
    return 0.5 * x * (1.0 + jnp.tanh(math.sqrt(2.0 / math.pi) * (x + 0.044715 * (x * x * x))))


def _dot(a, b):
    return jnp.dot(a, b, preferred_element_type=F32)


def _dot_nt(a, b):
    return lax.dot_general(a, b, (((1,), (1,)), ((), ())), preferred_element_type=F32)


def _rider(w, layer, axis):
    return (w, layer, axis)


def _rider_specs(riders, n_steps, step_of):
    in_specs, out_specs, out_shapes, nbytes = [], [], [], 0
    for w, layer, axis in riders:
        rows, cols = w.shape[1:]
        if axis == 0:
            blk = (rows // n_steps, cols)
            in_idx = lambda *g, layer=layer: (layer, step_of(*g), 0)
            out_idx = lambda *g: (step_of(*g), 0)
        else:
            blk = (rows, cols // n_steps)
            in_idx = lambda *g, layer=layer: (layer, 0, step_of(*g))
            out_idx = lambda *g: (0, step_of(*g))
        in_specs.append(pl.BlockSpec((None,) + blk, in_idx))
        out_specs.append(pl.BlockSpec(blk, out_idx))
        out_shapes.append(jax.ShapeDtypeStruct((rows, cols), BF16))
        nbytes += 2 * blk[0] * blk[1] * (4 + 2)
    return in_specs, out_specs, out_shapes, nbytes


def _run_riders(in_refs, out_refs):
    for src, dst in zip(in_refs, out_refs):
        dst[...] = src[...].astype(dst.dtype)


MOD_TN = 1024


def _mod_kernel(s_ref, w_ref, b_ref, o_ref):
    s = s_ref[...]
    s = s * _sigmoid(s)
    o_ref[...] = _dot(s.astype(BF16), w_ref[...].astype(BF16)) + b_ref[...]


def _mod_call(s_in, w_mod, b_mod3, layer):
    n_out = w_mod.shape[-1]
    nbytes = 2 * (D_MODEL * MOD_TN * 4) + D_MODEL * MOD_TN * 2 + 4 * MOD_ROWS * n_out
    return pl.pallas_call(
        _mod_kernel,
        grid=(n_out // MOD_TN,),
        in_specs=[
            pl.BlockSpec((MOD_ROWS, D_MODEL), lambda n: (0, 0)),
            pl.BlockSpec((None, D_MODEL, MOD_TN), lambda n: (layer, 0, n)),
            pl.BlockSpec((None, 1, MOD_TN), lambda n: (layer, 0, n)),
        ],
        out_specs=pl.BlockSpec((MOD_ROWS, MOD_TN), lambda n: (0, n)),
        out_shape=jax.ShapeDtypeStruct((MOD_ROWS, n_out), F32),
        compiler_params=_params(("arbitrary",), nbytes),
        name="modulation",
    )(s_in, w_mod, b_mod3)


MM_SUB = 256


def _row_tiles(n_rows, size):
    size = min(size, n_rows)
    return [slice(r, r + size) for r in range(0, n_rows, size)]


def _ln_mod_rows(x_ref, shift, scale1, dst_ref, rows):
    for piece in _row_tiles(rows.stop - rows.start, ROW_CHUNK):
        sl = slice(rows.start + piece.start, rows.start + piece.stop)
        dst_ref[sl, :] = (_ln(x_ref[sl, :]) * scale1 + shift).astype(dst_ref.dtype)


def _residual_ln_store(res, y, gate, gain, bias, o_ref, rows):
    for piece in _row_tiles(rows.stop - rows.start, ROW_CHUNK):
        sl = slice(rows.start + piece.start, rows.start + piece.stop)
        z = ALPHA * res[sl, :] + gate * y[piece, :]
        o_ref[sl, :] = _ln(z) * gain + bias


def _rope(a, cos, sin):
    lane = lax.broadcasted_iota(jnp.int32, a.shape, 1)
    first = (lane % (HEAD_DIM // 2)) < (HEAD_DIM // 4)
    partner = jnp.where(first, pltpu.roll(a, HEAD_DIM - HEAD_DIM // 4, 1), pltpu.roll(a, HEAD_DIM // 4, 1))
    return a * cos + partner * sin


def _inproj_lat_kernel(x_ref, mod_ref, w_ref, cos_ref, sin_ref,
                       q_ref, qr_ref, k_ref, v_ref, xo_ref, go_ref, f_ref, xn_ref):
    shift = mod_ref[0:1, :]
    scale1 = 1.0 + mod_ref[1:2, :]

    def emit_q(rows, cols, acc):
        q_ref[rows, cols] = (acc * ATTN_SCALE).astype(q_ref.dtype)
        for h in range(COL_TILE // HEAD_DIM):
            sl = slice(h * HEAD_DIM, (h + 1) * HEAD_DIM)
            dst = slice(cols.start + sl.start, cols.start + sl.stop)
            rot = _rope(acc[:, sl], cos_ref[rows, :], sin_ref[rows, :])
            qr_ref[rows, dst] = (rot * ATTN_SCALE).astype(qr_ref.dtype)

    def emit_k(rows, cols, acc):
        for h in range(COL_TILE // HEAD_DIM):
            sl = slice(h * HEAD_DIM, (h + 1) * HEAD_DIM)
            dst = slice(cols.start + sl.start, cols.start + sl.stop)
            k_ref[rows, dst] = _rope(acc[:, sl], cos_ref[rows, :], sin_ref[rows, :]).astype(k_ref.dtype)

    def emit_to(ref, fn=lambda a: a):
        def emit(rows, cols, acc):
            ref[rows, cols] = fn(acc).astype(ref.dtype)
        return emit

    half = [slice(0, COL_TILE), slice(COL_TILE, 2 * COL_TILE)]
    plan = ([(emit_q, c) for c in half] + [(emit_k, c) for c in half] + [(emit_to(v_ref), c) for c in half]
            + [(emit_to(xo_ref), half[0]), (emit_to(go_ref, _gelu_tanh), half[0]), (emit_to(f_ref), half[0])])
    for rows in _row_tiles(x_ref.shape[0], MM_SUB):
        _ln_mod_rows(x_ref, shift, scale1, xn_ref, rows)
        for n, (emit, cols) in enumerate(plan):
            emit(rows, cols, _dot(xn_ref[rows, :], w_ref[:, n * COL_TILE:(n + 1) * COL_TILE]))


def _inproj_lat_call(x2d, mod4, w_in, layer, cos_t, sin_t):
    m_rows = x2d.shape[0]
    tm = 512
    tiles_per_seq = SEQ // tm
    row = lambda m: (m, 0)
    nbytes = (2 * tm * D_MODEL * 4 + tm * D_MODEL * 2 + D_MODEL * IN_W * 2 + 4 * tm * HEAD_DIM * 4
              + 2 * tm * (4 * NA_W * 2 + 2 * LRU_W * 4 + FNET_W * 2) + 6 * MM_SUB * COL_TILE * 4
              + 4 * ROW_CHUNK * D_MODEL * 4)
    bf = lambda w: jax.ShapeDtypeStruct((m_rows, w), BF16)
    ff = lambda w: jax.ShapeDtypeStruct((m_rows, w), F32)
    widths = [NA_W, NA_W, NA_W, NA_W, LRU_W, LRU_W, FNET_W]
    return pl.pallas_call(
        _inproj_lat_kernel,
        grid=(m_rows // tm,),
        in_specs=[
            pl.BlockSpec((tm, D_MODEL), row),
            pl.BlockSpec((None, None, 6, D_MODEL), lambda m: (0, m // tiles_per_seq, 0, 0)),
            pl.BlockSpec((D_MODEL, IN_W), lambda m: (0, 0), pipeline_mode=pl.Buffered(1)),
            pl.BlockSpec((tm, HEAD_DIM), lambda m: (m % tiles_per_seq, 0)),
            pl.BlockSpec((tm, HEAD_DIM), lambda m: (m % tiles_per_seq, 0)),
        ],
        out_specs=[pl.BlockSpec((tm, w), row) for w in widths],
        out_shape=[bf(NA_W), bf(NA_W), bf(NA_W), bf(NA_W), ff(LRU_W), ff(LRU_W), bf(FNET_W)],
        scratch_shapes=[pltpu.VMEM((tm, D_MODEL), BF16)],
        compiler_params=_params(("arbitrary",), nbytes),
        name="inproj_latent",
    )(x2d, mod4, w_in, cos_t, sin_t)


def _inproj_ctx_kernel(tile_lo, with_q, with_gf, x_ref, mod_ref, w_ref, *refs):
    refs = list(refs)
    xn_ref = refs.pop()
    q_ref = refs.pop(0) if with_q else None
    k_ref, v_ref, xo_ref = refs[0], refs[1], refs[2]
    go_ref, f_ref = (refs[3], refs[4]) if with_gf else (None, None)
    n = pl.program_id(1) + tile_lo

    @pl.when(pl.program_id(1) == 0)
    def _():
        for rows in _row_tiles(x_ref.shape[0], MM_SUB):
            _ln_mod_rows(x_ref, mod_ref[0:1, :], 1.0 + mod_ref[1:2, :], xn_ref, rows)

    acc = _dot(xn_ref[...], w_ref[...])

    if with_q:
        @pl.when(n < 2)
        def _():
            q_ref[...] = (acc * ATTN_SCALE).astype(q_ref.dtype)

    @pl.when((n >= 2) & (n < 4))
    def _():
        k_ref[...] = acc.astype(k_ref.dtype)

    @pl.when((n >= 4) & (n < 6))
    def _():
        v_ref[...] = acc.astype(v_ref.dtype)

    @pl.when(n == 6)
    def _():
        xo_ref[...] = acc

    if with_gf:
        @pl.when(n == 7)
        def _():
            go_ref[...] = _gelu_tanh(acc)

        @pl.when(n == 8)
        def _():
            f_ref[...] = acc.astype(f_ref.dtype)


def _inproj_ctx_call(c2d, mod4, w_in, layer, full):
    m_rows = c2d.shape[0]
    tm = m_rows
    tile_lo, tile_hi = (0, N_COL_TILES) if full else (2, 7)

    def col(lo):
        return lambda m, n: (m, jnp.clip(n + tile_lo - lo, 0, 1))

    one = lambda m, n: (m, 0)
    bf = lambda w: jax.ShapeDtypeStruct((m_rows, w), BF16)
    ff = lambda w: jax.ShapeDtypeStruct((m_rows, w), F32)
    out_specs, out_shape = [], []
    if full:
        out_specs.append(pl.BlockSpec((tm, COL_TILE), col(0)))
        out_shape.append(bf(NA_W))
    out_specs += [pl.BlockSpec((tm, COL_TILE), col(2)), pl.BlockSpec((tm, COL_TILE), col(4)),
                  pl.BlockSpec((tm, COL_TILE), one)]
    out_shape += [bf(NA_W), bf(NA_W), ff(LRU_W)]
    if full:
        out_specs += [pl.BlockSpec((tm, COL_TILE), one), pl.BlockSpec((tm, COL_TILE), one)]
        out_shape += [ff(LRU_W), bf(FNET_W)]
    nbytes = (2 * tm * D_MODEL * 4 + tm * D_MODEL * 2 + 2 * D_MODEL * COL_TILE * 2
              + 2 * 6 * tm * COL_TILE * 4 + 3 * tm * COL_TILE * 4 + 4 * ROW_CHUNK * D_MODEL * 4)
    return pl.pallas_call(
        functools.partial(_inproj_ctx_kernel, tile_lo, full, full),
        grid=(1, tile_hi - tile_lo),
        in_specs=[
            pl.BlockSpec((tm, D_MODEL), lambda m, n: (m, 0)),
            pl.BlockSpec((None, None, 6, D_MODEL), lambda m, n: (0, CTX_MOD_ROW, 0, 0)),
            pl.BlockSpec((D_MODEL, COL_TILE), lambda m, n: (0, n + tile_lo)),
        ],
        out_specs=out_specs,
        out_shape=out_shape,
        scratch_shapes=[pltpu.VMEM((tm, D_MODEL), BF16)],
        compiler_params=_params(("arbitrary", "arbitrary"), nbytes),
        name="inproj_context",
    )(c2d, mod4, w_in)


QBLK_ROWS = 4
KBLK_ROWS = 12
N_QBLK = GRID_H // QBLK_ROWS
QBLK = QBLK_ROWS * GRID_W
KBLK = KBLK_ROWS * GRID_W
KEY_TILE = V7X_LANES
KEY_TILES = KBLK // KEY_TILE
N_DR = 2 * WIN_H - 1
BIAS_BOTH, BIAS_SECOND, BIAS_FIRST = "both", "second", "first"


def _kblk_start(first_query_row):
    return int(np.clip(first_query_row - WIN_H // 2, 0, GRID_H - KBLK_ROWS))


def _bias_entries(first_query_row):
    entries = []
    for u in range(QBLK_ROWS):
        q_row = first_query_row + u
        row_start = int(np.clip(q_row - WIN_H // 2, 0, GRID_H - WIN_H))
        row = []
        for c in range(KEY_TILES):
            k_rows = [_kblk_start(first_query_row) + 2 * c + i for i in range(2)]
            inside = [row_start <= kr < row_start + WIN_H for kr in k_rows]
            dr = [kr - q_row + (WIN_H - 1) for kr in k_rows]
            if inside[0] and inside[1]:
                row.append((BIAS_BOTH, dr[0]))
            elif inside[1]:
                row.append((BIAS_SECOND, dr[1]))
            elif inside[0]:
                row.append((BIAS_FIRST, dr[0]))
            else:
                row.append(None)
        entries.append(row)
    return entries


BIAS_TABLE = sorted({e for b in range(N_QBLK) for row in _bias_entries(b * QBLK_ROWS) for e in row if e})
BIAS_SLOT = {e: i for i, e in enumerate(BIAS_TABLE)}
N_BIAS = len(BIAS_TABLE)


def _bias_plan(first_query_row):
    return [[BIAS_SLOT[e] if e else None for e in row] for row in _bias_entries(first_query_row)]


def _attn_kernel(n_riders, q_ref, qr_ref, k_ref, v_ref, kc_ref, vc_ref, bias_ref, *rest):
    o_ref = rest[n_riders]
    _run_riders(rest[:n_riders], rest[n_riders + 1:])
    kc = kc_ref[...]
    vc = vc_ref[...]
    zero_tile = jnp.zeros((GRID_W, KEY_TILE), BF16)

    def one_block(q0, k0, plan):
        s = _dot_nt(qr_ref[pl.ds(q0, QBLK), :], k_ref[pl.ds(k0, KBLK), :])
        sc = _dot_nt(q_ref[pl.ds(q0, QBLK), :], kc)
        p_rows, pc_rows, denoms = [], [], []
        for u in range(QBLK_ROWS):
            rows = slice(u * GRID_W, (u + 1) * GRID_W)
            band = {c: s[rows, c * KEY_TILE:(c + 1) * KEY_TILE] + bias_ref[idx]
                    for c, idx in enumerate(plan[u]) if idx is not None}
            ctx_tiles = [sc[rows, c * KEY_TILE:(c + 1) * KEY_TILE] for c in range(CTX_LEN // KEY_TILE)]
            tiles = list(band.values()) + ctx_tiles
            m = jnp.max(functools.reduce(jnp.maximum, tiles), axis=-1, keepdims=True)
            p_band = {c: jnp.exp(t - m) for c, t in band.items()}
            p_ctx = [jnp.exp(t - m) for t in ctx_tiles]
            total = functools.reduce(jnp.add, list(p_band.values()) + p_ctx)
            denoms.append(jnp.sum(total, axis=-1, keepdims=True))
            p_rows.append(jnp.concatenate(
                [p_band[c].astype(BF16) if c in p_band else zero_tile for c in range(KEY_TILES)], axis=1))
            pc_rows.append(jnp.concatenate([t.astype(BF16) for t in p_ctx], axis=1))
        p = jnp.concatenate(p_rows, axis=0)
        pc = jnp.concatenate(pc_rows, axis=0)
        o = _dot(p, v_ref[pl.ds(k0, KBLK), :]) + _dot(pc, vc)
        o_ref[pl.ds(q0, QBLK), :] = (o / jnp.concatenate(denoms, axis=0)).astype(o_ref.dtype)

    def static_block(b):
        first_row = b * QBLK_ROWS
        one_block(b * QBLK, _kblk_start(first_row) * GRID_W, _bias_plan(first_row))

    interior_plan = _bias_plan(QBLK_ROWS)

    def interior_block(b):
        q0 = pl.multiple_of(b * QBLK, QBLK)
        k0 = pl.multiple_of((b * QBLK_ROWS - WIN_H // 2) * GRID_W, GRID_W)
        one_block(q0, k0, interior_plan)

    def body(i, carry):
        interior_block(2 * i + 1)
        interior_block(2 * i + 2)
        return carry

    for b in range(N_QBLK):
        static_block(b)


def _attn_call(q, qr, k, v, kc, vc, bias, layer, riders):
    seq_blk = lambda b, h: (b, h)
    n_steps = BATCH * NA_HEADS
    step = lambda b, h: b * NA_HEADS + h
    r_in, r_out, r_shape, r_bytes = _rider_specs(riders, n_steps, step)
    nbytes = (2 * (5 * SEQ * HEAD_DIM * 2 + 2 * CTX_LEN * HEAD_DIM * 2 + N_BIAS * GRID_W * KEY_TILE * 4)
              + 8 * QBLK * (KBLK + CTX_LEN) * 4 + r_bytes)
    res = pl.pallas_call(
        functools.partial(_attn_kernel, len(riders)),
        grid=(BATCH, NA_HEADS),
        in_specs=[
            pl.BlockSpec((SEQ, HEAD_DIM), seq_blk),
            pl.BlockSpec((SEQ, HEAD_DIM), seq_blk),
            pl.BlockSpec((SEQ, HEAD_DIM), seq_blk),
            pl.BlockSpec((SEQ, HEAD_DIM), seq_blk),
            pl.BlockSpec((CTX_LEN, HEAD_DIM), seq_blk),
            pl.BlockSpec((CTX_LEN, HEAD_DIM), seq_blk),
            pl.BlockSpec((None, None, N_BIAS, GRID_W, KEY_TILE), lambda b, h: (layer, h, 0, 0, 0)),
        ] + r_in,
        out_specs=[pl.BlockSpec((SEQ, HEAD_DIM), seq_blk)] + r_out,
        out_shape=[jax.ShapeDtypeStruct((BATCH * SEQ, NA_W), BF16)] + r_shape,
        compiler_params=_params(("arbitrary", "arbitrary"), nbytes),
        name="neighbourhood_attention",
    )(q, qr, k, v, kc, vc, bias, *[r[0] for r in riders])
    return res[0], res[1:]


def _ctx_attn_kernel(q_ref, k_ref, v_ref, o_ref):
    s = _dot_nt(q_ref[...], k_ref[...])
    m = jnp.max(s, axis=-1, keepdims=True)
    p = jnp.exp(s - m)
    denom = jnp.sum(p, axis=-1, keepdims=True)
    o_ref[...] = (_dot(p.astype(BF16), v_ref[...]) / denom).astype(o_ref.dtype)


def _ctx_attn_call(q, k, v):
    blk = pl.BlockSpec((CTX_LEN, HEAD_DIM), lambda b, h: (b, h))
    return pl.pallas_call(
        _ctx_attn_kernel,
        grid=(BATCH, NA_HEADS),
        in_specs=[blk, blk, blk],
        out_specs=blk,
        out_shape=jax.ShapeDtypeStruct((BATCH * CTX_LEN, NA_W), BF16),
        compiler_params=_params(("arbitrary", "arbitrary"), 16 << 20),
        name="context_attention",
    )(q, k, v)


def _attn_bias_table(rpb):
    col = np.arange(GRID_W)
    col_start = np.clip(col - WIN_W // 2, 0, GRID_W - WIN_W)
    in_win = (col[None, :] >= col_start[:, None]) & (col[None, :] < col_start[:, None] + WIN_W)
    dc = np.clip(col[None, :] - col[:, None] + (WIN_W - 1), 0, 2 * WIN_W - 2)
    onehot = (in_win[None] & (dc[None] == np.arange(2 * WIN_W - 1)[:, None, None])).astype(np.float32)
    t = jnp.einsum('lhdj,jqk->lhdqk', rpb, jnp.asarray(onehot), precision=lax.Precision.HIGHEST)
    t = jnp.where(in_win[None, None, None], t, NEG_INF)
    masked = jnp.full((DEPTH, NA_HEADS, GRID_W, GRID_W), NEG_INF, F32)
    tiles = []
    for kind, dr in BIAS_TABLE:
        left = masked if kind == BIAS_SECOND else t[:, :, dr]
        right = masked if kind == BIAS_FIRST else t[:, :, dr + 1 if kind == BIAS_BOTH else dr]
        tiles.append(jnp.concatenate([left, right], axis=-1))
    return jnp.stack(tiles, axis=2)


def _rope_tables():
    quarter = HEAD_DIM // 4
    inv = ROPE_THETA ** (-jnp.arange(quarter, dtype=F32) / quarter)
    t = jnp.arange(SEQ)
    ang_r = (t // GRID_W).astype(F32)[:, None] * inv
    ang_c = (t % GRID_W).astype(F32)[:, None] * inv
    cos = jnp.concatenate([jnp.cos(ang_r), jnp.cos(ang_r), jnp.cos(ang_c), jnp.cos(ang_c)], -1)
    sin = jnp.concatenate([-jnp.sin(ang_r), jnp.sin(ang_r), -jnp.sin(ang_c), jnp.sin(ang_c)], -1)
    return cos, sin


HALO = V7X_SUBLANES
N_SEG = V7X_SUBLANES
SEG_PAD = V7X_SUBLANES
LRU_SCAN_UNROLL = 8


def _lru_coeffs(xp_ref, n_rows, cw_ref, cb_ref, w4, b4_ref, sp, a_refs, u_refs):
    seg = n_rows // N_SEG
    pitch = seg + SEG_PAD
    for s in range(N_SEG):
        base = HALO + s * seg
        xc = cb_ref[...] + xp_ref[base - CONV_W // 2:base - CONV_W // 2 + seg, :] * cw_ref[0:1, :]
        for j in range(1, CONV_W):
            off = base - CONV_W // 2 + j
            xc = xc + xp_ref[off:off + seg, :] * cw_ref[j:j + 1, :]
        th = jnp.tanh(_dot(xc.astype(BF16), w4) + b4_ref[...])
        half_xc = 0.5 * xc
        for d in range(2):
            r2 = th[:, (2 * d) * LRU_BW:(2 * d + 1) * LRU_BW] + 1.0
            i2 = th[:, (2 * d + 1) * LRU_BW:(2 * d + 2) * LRU_BW] + 1.0
            log_a = r2 * sp[d:d + 1, :]
            a = jnp.exp(log_a)
            a_refs[d][s * pitch:s * pitch + seg, :] = a
            one_minus_a2 = -jnp.tanh(log_a) * (a * a + 1.0)
            u_refs[d][s * pitch:s * pitch + seg, :] = jnp.sqrt(one_minus_a2) * (i2 * half_xc)


def _lru_local_scan(n_rows, coef_f, coef_b, state_f, state_b):
    seg = n_rows // N_SEG
    pitch = seg + SEG_PAD
    zero = jnp.zeros((N_SEG, LRU_BW), F32)
    one = jnp.ones((N_SEG, LRU_BW), F32)

    def step(coef, state, row, h, p):
        rows = pl.ds(row, N_SEG, stride=pitch)
        a = coef[0][rows, :]
        h = a * h + coef[1][rows, :]
        p = p * a
        state[0][rows, :] = p
        state[1][rows, :] = h
        return h, p

    def body(i, carry):
        hf, pf, hb, pb = carry
        for j in range(LRU_SCAN_UNROLL):
            t = i * LRU_SCAN_UNROLL + j
            hf, pf = step(coef_f, state_f, t, hf, pf)
            hb, pb = step(coef_b, state_b, seg - 1 - t, hb, pb)
        return hf, pf, hb, pb

    lax.fori_loop(0, seg // LRU_SCAN_UNROLL, body, (zero, one, zero, one))


def _lru_carries(n_rows, h_in_f, h_in_b, af, uf, ab, ub):
    seg = n_rows // N_SEG
    pitch = seg + SEG_PAD
    cf, cb = [h_in_f], [h_in_b]
    for s in range(N_SEG):
        last = s * pitch + seg - 1
        cf.append(uf[last:last + 1, :] + af[last:last + 1, :] * cf[-1])
        first = (N_SEG - 1 - s) * pitch
        cb.append(ub[first:first + 1, :] + ab[first:first + 1, :] * cb[-1])
    return cf[:N_SEG], cb[:N_SEG][::-1], cf[N_SEG], cb[N_SEG]


def _lru_emit(n_rows, cf, cb, af, uf, ab, ub, g_ref, o_ref):
    seg = n_rows // N_SEG
    pitch = seg + SEG_PAD
    for s in range(N_SEG):
        src = slice(s * pitch, s * pitch + seg)
        dst = slice(s * seg, (s + 1) * seg)
        y = (uf[src, :] + af[src, :] * cf[s]) + (ub[src, :] + ab[src, :] * cb[s])
        o_ref[dst, :] = (y * g_ref[dst, :]).astype(o_ref.dtype)


def _lru_kernel(ctx_out, n_riders, x_ref, g_ref, xc_ref, *refs):
    refs = list(refs)
    gc_ref = refs.pop(0) if ctx_out else None
    cw_ref, cb_ref, w4_ref, b4_ref, lam_ref = refs[:5]
    rider_in, refs = refs[5:5 + n_riders], refs[5 + n_riders:]
    o_ref = refs.pop(0)
    oc_ref = refs.pop(0) if ctx_out else None
    rider_out, refs = refs[:n_riders], refs[n_riders:]
    xp_ref, af, uf, ab, ub, pf, sf, pb, sb = refs
    _run_riders(rider_in, rider_out)

    lam = lam_ref[...]
    z = -lam
    sp = (-0.5 * LRU_C) * (jnp.maximum(z, 0.0) + jnp.log1p(jnp.exp(-jnp.abs(z))))
    w4 = w4_ref[...].astype(BF16)
    zeros_halo = jnp.zeros((HALO, LRU_BW), F32)
    h0 = jnp.zeros((1, LRU_BW), F32)

    xp_ref[0:HALO, :] = zeros_halo
    xp_ref[HALO:HALO + CTX_LEN, :] = xc_ref[...]
    xp_ref[HALO + CTX_LEN:2 * HALO + CTX_LEN, :] = zeros_halo
    _lru_coeffs(xp_ref, CTX_LEN, cw_ref, cb_ref, w4, b4_ref, sp, (af, ab), (uf, ub))
    _lru_local_scan(CTX_LEN, (af, uf), (ab, ub), (pf, sf), (pb, sb))
    cf, cb, hf, hb = _lru_carries(CTX_LEN, h0, h0, pf, sf, pb, sb)
    if ctx_out:
        _lru_emit(CTX_LEN, cf, cb, pf, sf, pb, sb, gc_ref, oc_ref)

    xp_ref[HALO:HALO + SEQ, :] = x_ref[...]
    xp_ref[HALO + SEQ:2 * HALO + SEQ, :] = zeros_halo
    _lru_coeffs(xp_ref, SEQ, cw_ref, cb_ref, w4, b4_ref, sp, (af, ab), (uf, ub))
    _lru_local_scan(SEQ, (af, uf), (ab, ub), (pf, sf), (pb, sb))
    cf, cb, _, _ = _lru_carries(SEQ, hf, hb, pf, sf, pb, sb)
    _lru_emit(SEQ, cf, cb, pf, sf, pb, sb, g_ref, o_ref)


def _lru_call(ctx_out, xl, gl, xc, gc, conv_w_l, conv_b_l, w4, b4, lam_l, riders):
    lat = pl.BlockSpec((SEQ, LRU_BW), lambda b, j: (b, j))
    cx = pl.BlockSpec((CTX_LEN, LRU_BW), lambda b, j: (b, j))
    r_in, r_out, r_shape, r_bytes = _rider_specs(riders, BATCH * LRU_BLOCKS, lambda b, j: b * LRU_BLOCKS + j)
    in_specs = [lat, lat, cx] + ([cx] if ctx_out else []) + [
        pl.BlockSpec((CONV_W, LRU_BW), lambda b, j: (0, j)),
        pl.BlockSpec((1, LRU_BW), lambda b, j: (0, j)),
        pl.BlockSpec((None, LRU_BW, 4 * LRU_BW), lambda b, j: (j, 0, 0)),
        pl.BlockSpec((None, 1, 4 * LRU_BW), lambda b, j: (j, 0, 0)),
        pl.BlockSpec((2, LRU_BW), lambda b, j: (0, j)),
    ] + r_in
    out_specs = [lat] + ([cx] if ctx_out else []) + r_out
    out_shape = [jax.ShapeDtypeStruct((BATCH * SEQ, LRU_W), BF16)]
    if ctx_out:
        out_shape.append(jax.ShapeDtypeStruct((BATCH * CTX_LEN, LRU_W), BF16))
    out_shape += r_shape
    seq_bytes = SEQ * LRU_BW * 4
    args = ([xl, gl, xc] + ([gc] if ctx_out else []) + [conv_w_l, conv_b_l, w4, b4, lam_l]
            + [r[0] for r in riders])
    res = pl.pallas_call(
        functools.partial(_lru_kernel, ctx_out, len(riders)),
        grid=(BATCH, LRU_BLOCKS),
        in_specs=in_specs,
        out_specs=out_specs,
        out_shape=out_shape,
        scratch_shapes=([pltpu.VMEM((SEQ + 2 * HALO, LRU_BW), F32)]
                        + [pltpu.VMEM((SEQ + N_SEG * SEG_PAD, LRU_BW), F32)] * 8),
        compiler_params=_params(("arbitrary", "arbitrary"), 15 * seq_bytes + (8 << 20) + r_bytes),
        name="rglru",
    )(*args)
    n_main = 2 if ctx_out else 1
    return res[0], (res[1] if ctx_out else None), res[n_main:]


def _fourier_kernel(n_pos, blk, f_ref, ch_ref, sh_ref, cc_ref, sc_ref, rev_ref, alt_ref, w_ref, b_ref, o_ref,
                    ec_ref, es_ref, mir_ref, mid_ref):
    half = n_pos // 2
    n_lo = half // blk
    step = pl.program_id(1)
    norm = 1.0 / math.sqrt(n_pos * FNET_GW)
    w = w_ref[...].astype(BF16)

    def linear(y):
        return (_dot(y.astype(BF16), w) + b_ref[...]).astype(o_ref.dtype)

    @pl.when(step == 0)
    def _():
        for i in range(n_lo):
            lo = blk * (2 * n_lo - 1 - i)
            if i == 0:
                mirrored = _dot(rev_ref[:, 0:blk], f_ref[lo:lo + blk, :])
            else:
                mirrored = _dot(rev_ref[...], f_ref[lo:lo + 2 * blk, :])
            rows = slice(i * blk, (i + 1) * blk)
            x = f_ref[rows, :].astype(F32)
            even = (x + mirrored).astype(BF16)
            odd = (x - mirrored).astype(BF16)
            for g in range(FNET_GROUPS):
                sl = slice(g * FNET_GW, (g + 1) * FNET_GW)
                ec_ref[rows, sl] = _dot(even[:, sl], cc_ref[...]).astype(BF16)
                es_ref[rows, sl] = _dot(odd[:, sl], sc_ref[...]).astype(BF16)
        for g in range(FNET_GROUPS):
            sl = slice(g * FNET_GW, (g + 1) * FNET_GW)
            mid_ref[0:BF16_ROWS, sl] = _dot(f_ref[half:half + BF16_ROWS, sl], cc_ref[...])
        mid_ref[BF16_ROWS:2 * BF16_ROWS, :] = _dot(alt_ref[...], ec_ref[...])

    mid = mid_ref[0:1, :]

    @pl.when(step < n_lo)
    def _():
        a = _dot(ch_ref[...], ec_ref[...])
        b = _dot(sh_ref[...], es_ref[...])
        row = lax.broadcasted_iota(jnp.int32, (blk, 1), 0)
        base = jnp.where(row % 2 == 0, 1.0, -1.0) * mid
        o_ref[...] = linear((a - b + base) * norm)
        r0 = pl.multiple_of(step * blk, blk)
        mir_ref[pl.ds(r0, blk), :] = ((a + b + base) * norm).astype(BF16)

    @pl.when(step == n_lo)
    def _():
        y = _dot(rev_ref[:, 0:blk], mir_ref[(n_lo - 1) * blk:n_lo * blk, :])
        nyquist = (mid_ref[BF16_ROWS:BF16_ROWS + 1, :] + mid) * norm
        row = lax.broadcasted_iota(jnp.int32, (blk, 1), 0)
        o_ref[...] = linear(jnp.where(row == 0, nyquist, y))

    if n_lo > 1:
        @pl.when(step > n_lo)
        def _():
            r0 = pl.multiple_of((2 * n_lo - 1 - step) * blk, blk)
            o_ref[...] = linear(_dot(rev_ref[...], mir_ref[pl.ds(r0, 2 * blk), :]))


DFT_SPLIT = 64


def _dft_matrices(n, size):
    t = np.arange(size, dtype=np.int64)

    def table(k):
        ang = (2.0 * np.pi / n) * ((k[:, None] * t[None, :]) % n).astype(np.float64)
        return jnp.asarray(np.cos(ang), F32), jnp.asarray(np.sin(ang), F32)

    if size <= DFT_SPLIT:
        c, s = table(t)
        return c.astype(BF16), s.astype(BF16)
    c1, s1 = table(DFT_SPLIT * np.arange(size // DFT_SPLIT, dtype=np.int64))
    c2, s2 = table(np.arange(DFT_SPLIT, dtype=np.int64))
    c = c1[:, None, :] * c2[None, :, :] - s1[:, None, :] * s2[None, :, :]
    s = s1[:, None, :] * c2[None, :, :] + c1[:, None, :] * s2[None, :, :]
    return c.reshape(size, size).astype(BF16), s.reshape(size, size).astype(BF16)


FOURIER_BLK = 256


def _fourier_call(f2d, n_pos, fno_w_l, fno_b_l):
    half = n_pos // 2
    blk = min(FOURIER_BLK, half)
    steps = n_pos // blk
    n_lo = half // blk
    ch, sh = _dft_matrices(n_pos, half)
    cc, sc = _dft_matrices(FNET_GW, FNET_GW)
    rev = np.zeros((blk, 2 * blk), np.float32)
    rev[np.arange(1, blk), blk - np.arange(1, blk)] = 1.0
    rev[0, blk] = 1.0
    alt = np.zeros((BF16_ROWS, half), np.float32)
    alt[0] = 1.0 - 2.0 * (np.arange(half) % 2)
    const = lambda b, k: (0, 0)
    dft_tile = lambda b, k: (jnp.minimum(k, n_lo - 1), 0)
    nbytes = (2 * n_pos * FNET_W * 2 + 2 * 2 * blk * half * 2 + 3 * half * FNET_W * 2 + 2 * FNET_W * FNET_W * 4
              + 8 * blk * FNET_W * 4 + 2 * blk * 2 * blk * 2)
    return pl.pallas_call(
        functools.partial(_fourier_kernel, n_pos, blk),
        grid=(BATCH, steps),
        in_specs=[
            pl.BlockSpec((n_pos, FNET_W), lambda b, k: (b, 0)),
            pl.BlockSpec((blk, half), dft_tile),
            pl.BlockSpec((blk, half), dft_tile),
            pl.BlockSpec((FNET_GW, FNET_GW), const),
            pl.BlockSpec((FNET_GW, FNET_GW), const),
            pl.BlockSpec((blk, 2 * blk), const),
            pl.BlockSpec((BF16_ROWS, half), const),
            pl.BlockSpec((FNET_W, FNET_W), const),
            pl.BlockSpec((1, FNET_W), const),
        ],
        out_specs=pl.BlockSpec((blk, FNET_W), lambda b, k: (b * steps + k, 0)),
        out_shape=jax.ShapeDtypeStruct((BATCH * n_pos, FNET_W), BF16),
        scratch_shapes=[pltpu.VMEM((half, FNET_W), BF16), pltpu.VMEM((half, FNET_W), BF16),
                        pltpu.VMEM((half, FNET_W), BF16), pltpu.VMEM((2 * BF16_ROWS, FNET_W), F32)],
        compiler_params=_params(("arbitrary", "arbitrary"), nbytes),
        name="fourier_mix",
    )(f2d, ch, sh, cc, sc, jnp.asarray(rev, BF16), jnp.asarray(alt, BF16), fno_w_l, fno_b_l)


def _outproj_kernel(carry_mod, na_ref, lru_ref, f_ref, res_ref, mod_ref, w_ref, g_ref, b_ref, *rest):
    if carry_mod:
        s_ref, wm_ref, bm_ref, o_ref, mo_ref = rest
        _mod_kernel(s_ref, wm_ref, bm_ref, mo_ref)
    else:
        (o_ref,) = rest
    gate = mod_ref[2:3, :]
    for rows in _row_tiles(res_ref.shape[0], MM_SUB):
        y = (_dot(na_ref[rows, :], w_ref[0:NA_W, :])
             + _dot(lru_ref[rows, :], w_ref[NA_W:NA_W + LRU_W, :])
             + _dot(f_ref[rows, :], w_ref[NA_W + LRU_W:D_MODEL, :]))
        _residual_ln_store(res_ref, y, gate, g_ref[...], b_ref[...], o_ref, rows)


def _outproj_call(na, lru, f, res, mod4, mod_row, w_out, layer, ln_g, ln_b, next_mod=None):
    m_rows = res.shape[0]
    tm = 512
    steps = m_rows // tm
    row = lambda m: (m, 0)
    nbytes = (2 * tm * D_MODEL * 2 + 2 * 2 * tm * D_MODEL * 4 + D_MODEL * D_MODEL * 2 + 2 * MM_SUB * D_MODEL * 4
              + 4 * ROW_CHUNK * D_MODEL * 4)
    in_specs = [
        pl.BlockSpec((tm, NA_W), row),
        pl.BlockSpec((tm, LRU_W), row),
        pl.BlockSpec((tm, FNET_W), row),
        pl.BlockSpec((tm, D_MODEL), row),
        pl.BlockSpec((None, None, 6, D_MODEL), lambda m: (0, mod_row(m * tm), 0, 0)),
        pl.BlockSpec((D_MODEL, D_MODEL), lambda m: (0, 0), pipeline_mode=pl.Buffered(1)),
        pl.BlockSpec((None, 1, D_MODEL), lambda m: (layer, 0, 0)),
        pl.BlockSpec((None, 1, D_MODEL), lambda m: (layer, 0, 0)),
    ]
    out_specs = [pl.BlockSpec((tm, D_MODEL), row)]
    out_shape = [jax.ShapeDtypeStruct((m_rows, D_MODEL), F32)]
    args = [na, lru, f, res, mod4, w_out, ln_g, ln_b]
    if next_mod is not None:
        s_in, w_mod, b_mod3, mod_layer = next_mod
        n_out = w_mod.shape[-1]
        slab = n_out // steps
        in_specs += [
            pl.BlockSpec((MOD_ROWS, D_MODEL), lambda m: (0, 0)),
            pl.BlockSpec((None, D_MODEL, slab), lambda m: (mod_layer, 0, m)),
            pl.BlockSpec((None, 1, slab), lambda m: (mod_layer, 0, m)),
        ]
        out_specs.append(pl.BlockSpec((MOD_ROWS, slab), lambda m: (0, m)))
        out_shape.append(jax.ShapeDtypeStruct((MOD_ROWS, n_out), F32))
        args += [s_in, w_mod, b_mod3]
        nbytes += 2 * D_MODEL * slab * 4 + D_MODEL * slab * 2
    res = pl.pallas_call(
        functools.partial(_outproj_kernel, next_mod is not None),
        grid=(steps,),
        in_specs=in_specs,
        out_specs=out_specs,
        out_shape=out_shape,
        compiler_params=_params(("arbitrary",), nbytes),
        name="outproj_residual",
    )(*args)
    return (res[0], res[1]) if next_mod is not None else (res[0], None)


MLP_TF = 512
MLP_TILES = D_FF // MLP_TF


def _mlp_kernel(x_ref, mod_ref, w1_ref, b1_ref, w2_ref, b2_ref, g_ref, b_ref, o_ref, v_ref, h0_ref, h1_ref):
    j = pl.program_id(1)
    tm = x_ref.shape[0]

    def up(h_out, rows=slice(None)):
        h = _dot(v_ref[rows, :], w1_ref[...]) + b1_ref[...]
        h_out[rows, :] = jnp.square(jnp.maximum(h, 0.0)).astype(BF16)

    def down(h_in, first):
        for c in range(D_MODEL // COL_TILE):
            sl = slice(c * COL_TILE, (c + 1) * COL_TILE)
            part = _dot(h_in[...], w2_ref[:, sl])
            if first:
                o_ref[:, sl] = part
            else:
                o_ref[:, sl] += part

    @pl.when(j == 0)
    def _():
        shift = mod_ref[3:4, :]
        scale1 = 1.0 + mod_ref[4:5, :]
        for rows in _row_tiles(tm, MM_SUB):
            _ln_mod_rows(x_ref, shift, scale1, v_ref, rows)
            up(h0_ref, rows)

    @pl.when(j == 1)
    def _():
        down(h0_ref, True)
        up(h1_ref)

    @pl.when((j > 1) & (j < MLP_TILES) & (j % 2 == 0))
    def _():
        down(h1_ref, False)
        up(h0_ref)

    @pl.when((j > 1) & (j < MLP_TILES) & (j % 2 == 1))
    def _():
        down(h0_ref, False)
        up(h1_ref)

    @pl.when(j == MLP_TILES)
    def _():
        h_last = h1_ref if (MLP_TILES - 1) % 2 else h0_ref
        gate = mod_ref[5:6, :]
        for rows in _row_tiles(tm, MM_SUB):
            y = o_ref[rows, :] + _dot(h_last[rows, :], w2_ref[...]) + b2_ref[...]
            _residual_ln_store(x_ref, y, gate, g_ref[...], b_ref[...], o_ref, rows)


def _mlp_call(x1, mod4, mod_row, tm, w1, b1, w2, b2, layer, ln_g, ln_b):
    m_rows = x1.shape[0]
    row = lambda m, j: (m, 0)
    vec = lambda m, j: (layer, 0, 0)
    nbytes = (4 * tm * D_MODEL * 4 + tm * D_MODEL * 2 + 2 * 2 * D_MODEL * MLP_TF * 2 + 2 * tm * MLP_TF * 2
              + tm * MLP_TF * 4 + tm * COL_TILE * 4 + 2 * MM_SUB * D_MODEL * 4 + 4 * ROW_CHUNK * D_MODEL * 4)
    return pl.pallas_call(
        _mlp_kernel,
        grid=(m_rows // tm, MLP_TILES + 1),
        in_specs=[
            pl.BlockSpec((tm, D_MODEL), row),
            pl.BlockSpec((None, None, 6, D_MODEL), lambda m, j: (0, mod_row(m * tm), 0, 0)),
            pl.BlockSpec((D_MODEL, MLP_TF), lambda m, j: (0, jnp.minimum(j, MLP_TILES - 1))),
            pl.BlockSpec((None, 1, MLP_TF), lambda m, j: (layer, 0, jnp.minimum(j, MLP_TILES - 1))),
            pl.BlockSpec((MLP_TF, D_MODEL), lambda m, j: (jnp.maximum(j - 1, 0), 0)),
            pl.BlockSpec((None, 1, D_MODEL), vec),
            pl.BlockSpec((None, 1, D_MODEL), vec),
            pl.BlockSpec((None, 1, D_MODEL), vec),
        ],
        out_specs=pl.BlockSpec((tm, D_MODEL), row),
        out_shape=jax.ShapeDtypeStruct((m_rows, D_MODEL), F32),
        scratch_shapes=[pltpu.VMEM((tm, D_MODEL), BF16), pltpu.VMEM((tm, MLP_TF), BF16),
                        pltpu.VMEM((tm, MLP_TF), BF16)],
        compiler_params=_params(("arbitrary", "arbitrary"), nbytes),
        name="mlp_residual",
    )(x1, mod4, w1, b1, w2, b2, ln_g, ln_b)


def kernel(x, c, ctx, c_ctx, w_mod, b_mod, w_in, rpb, conv_w, conv_b, lru_wa, lru_ba, lru_wx, lru_bx, lru_lambda,
           fno_w, fno_b, w_out, ln1_g, ln1_b, w_fc1, b_fc1, w_fc2, b_fc2, ln2_g, ln2_b):
    xl = x.reshape(BATCH * SEQ, D_MODEL)
    xc = ctx.reshape(BATCH * CTX_LEN, D_MODEL)
    s_in = jnp.concatenate([c, c_ctx[None], jnp.zeros((MOD_ROWS - BATCH - 1, D_MODEL), F32)], 0)
    b_mod3 = b_mod.reshape(DEPTH, 1, 6 * D_MODEL)
    as_mod4 = lambda m: m.reshape(1, MOD_ROWS, 6, D_MODEL)
    mod4 = as_mod4(_mod_call(s_in, w_mod, b_mod3, 0))
    cos_t, sin_t = _rope_tables()
    bias_tab = _attn_bias_table(rpb)
    w_in_l = w_in[0].astype(BF16)
    vec3 = lambda a: a.reshape(DEPTH, 1, a.shape[-1])
    ln1_g3, ln1_b3, ln2_g3, ln2_b3 = vec3(ln1_g), vec3(ln1_b), vec3(ln2_g), vec3(ln2_b)
    b_fc1_3, b_fc2_3 = vec3(b_fc1), vec3(b_fc2)
    lat_row = lambda r0: r0 // SEQ
    ctx_row = lambda r0: CTX_MOD_ROW

    for layer in range(DEPTH):
        ctx_out = layer < DEPTH - 1
        q, qr, k, v, xo, go, f = _inproj_lat_call(xl, mod4, w_in_l, layer, cos_t, sin_t)
        if ctx_out:
            qc, kc, vc, xoc, goc, fc = _inproj_ctx_call(xc, mod4, w_in_l, layer, True)
        else:
            kc, vc, xoc = _inproj_ctx_call(xc, mod4, w_in_l, layer, False)
            goc = None

        na, (w_out_l, w_fc1_l, w_fc2_l) = _attn_call(
            q, qr, k, v, kc, vc, bias_tab, layer,
            [_rider(w_out, layer, 0), _rider(w_fc1, layer, 1), _rider(w_fc2, layer, 0)])

        w4 = 0.5 * jnp.concatenate([lru_wa[layer, 0], lru_wx[layer, 0], lru_wa[layer, 1], lru_wx[layer, 1]], -1)
        blk = lambda a: a.reshape(LRU_BLOCKS, 1, LRU_BW)
        b4 = 0.5 * jnp.concatenate([blk(lru_ba[layer, 0]), blk(lru_bx[layer, 0]),
                                    blk(lru_ba[layer, 1]), blk(lru_bx[layer, 1])], -1)
        next_w_in = [_rider(w_in, layer + 1, 0)] if layer + 1 < DEPTH else []
        lru, lru_c, cast = _lru_call(ctx_out, xo, go, xoc, goc, conv_w[layer], conv_b[layer][None], w4, b4,
                                     lru_lambda[layer], next_w_in)
        if next_w_in:
            w_in_l = cast[0]

        fm = _fourier_call(f, SEQ, fno_w[layer], fno_b[layer][None])
        next_mod = (s_in, w_mod, b_mod3, layer + 1) if layer + 1 < DEPTH else None
        x1, mod_next = _outproj_call(na, lru, fm, xl, mod4, lat_row, w_out_l, layer, ln1_g3, ln1_b3, next_mod)
        xl = _mlp_call(x1, mod4, lat_row, 1024, w_fc1_l, b_fc1_3, w_fc2_l, b_fc2_3, layer, ln2_g3, ln2_b3)

        if ctx_out:
            na_c = _ctx_attn_call(qc, kc, vc)
            fm_c = _fourier_call(fc, CTX_LEN, fno_w[layer], fno_b[layer][None])
            c1, _ = _outproj_call(na_c, lru_c, fm_c, xc, mod4, ctx_row, w_out_l, layer, ln1_g3, ln1_b3)
            xc = _mlp_call(c1, mod4, ctx_row, 512, w_fc1_l, b_fc1_3, w_fc2_l, b_fc2_3, layer, ln2_g3, ln2_b3)
        if mod_next is not None:
            mod4 = as_mod4(mod_next)

    return xl.reshape(BATCH, SEQ, D_MODEL)
```

```python
import functools
import math

import jax
import jax.numpy as jnp
import numpy as np
from jax import lax
from jax.experimental import pallas as pl
from jax.experimental.pallas import tpu as pltpu

F32 = jnp.float32
BF16 = jnp.bfloat16

D_MODEL = 2048
BATCH = 2
SEQ = 4096
DEPTH = 2
GRID_W = 64
GRID_H = SEQ // GRID_W
CTX_LEN = 256
HEAD_DIM = 128
NA_W = D_MODEL // 2
NA_HEADS = NA_W // HEAD_DIM
WIN_H = 8
WIN_W = 16
LRU_W = D_MODEL // 4
LRU_BLOCKS = 4
LRU_BW = LRU_W // LRU_BLOCKS
CONV_W = 4
LRU_C = 8.0
FNET_W = D_MODEL // 4
FNET_GROUPS = 4
FNET_GW = FNET_W // FNET_GROUPS
IN_W = 3 * NA_W + 2 * LRU_W + FNET_W
D_FF = 4 * D_MODEL
ROPE_THETA = 10000.0
LN_EPS = 1e-5
NEG_INF = -1e30
ALPHA = (2.0 * DEPTH) ** 0.25
ATTN_SCALE = HEAD_DIM ** -0.5

V7X_LANES = 128
V7X_SUBLANES = 8
BF16_ROWS = 2 * V7X_SUBLANES
V7X_VMEM_BYTES = 64 * 1024 * 1024
VMEM_CEILING = V7X_VMEM_BYTES - 6 * 1024 * 1024

COL_TILE = 512
N_COL_TILES = IN_W // COL_TILE
ROW_CHUNK = 128
MOD_ROWS = 8
CTX_MOD_ROW = BATCH


def _vmem_limit(nbytes):
    return int(min(VMEM_CEILING, nbytes * 5 // 4 + (4 << 20)))


def _params(semantics, nbytes):
    return pltpu.CompilerParams(dimension_semantics=semantics, vmem_limit_bytes=_vmem_limit(nbytes))


def _ln(x):
    mu = jnp.mean(x, axis=-1, keepdims=True)
    xc = x - mu
    var = jnp.mean(xc * xc, axis=-1, keepdims=True)
    return xc * lax.rsqrt(var + LN_EPS)


def _sigmoid(x):
    return 1.0 / (1.0 + jnp.exp(-x))


def _gelu_tanh(x):
    return 0.5 * x * (1.0 + jnp.tanh(math.sqrt(2.0 / math.pi) * (x + 0.044715 * (x * x * x))))


def _dot(a, b):
    return jnp.dot(a, b, preferred_element_type=F32)


def _dot_nt(a, b):
    return lax.dot_general(a, b, (((1,), (1,)), ((), ())), preferred_element_type=F32)


def _rider(w, layer, axis):
    return (w, layer, axis)


def _rider_specs(riders, n_steps, step_of):
    in_specs, out_specs, out_shapes, nbytes = [], [], [], 0
    for w, layer, axis in riders:
        rows, cols = w.shape[1:]
        if axis == 0:
            blk = (rows // n_steps, cols)
            in_idx = lambda *g, layer=layer: (layer, step_of(*g), 0)
            out_idx = lambda *g: (step_of(*g), 0)
        else:
            blk = (rows, cols // n_steps)
            in_idx = lambda *g, layer=layer: (layer, 0, step_of(*g))
            out_idx = lambda *g: (0, step_of(*g))
        in_specs.append(pl.BlockSpec((None,) + blk, in_idx))
        out_specs.append(pl.BlockSpec(blk, out_idx))
        out_shapes.append(jax.ShapeDtypeStruct((rows, cols), BF16))
        nbytes += 2 * blk[0] * blk[1] * (4 + 2)
    return in_specs, out_specs, out_shapes, nbytes


def _run_riders(in_refs, out_refs):
    for src, dst in zip(in_refs, out_refs):
        dst[...] = src[...].astype(dst.dtype)


MOD_TN = 1024


def _mod_kernel(s_ref, w_ref, b_ref, o_ref):
    s = s_ref[...]
    s = s * _sigmoid(s)
    o_ref[...] = _dot(s.astype(BF16), w_ref[...].astype(BF16)) + b_ref[...]


def _mod_call(s_in, w_mod, b_mod3, layer):
    n_out = w_mod.shape[-1]
    nbytes = 2 * (D_MODEL * MOD_TN * 4) + D_MODEL * MOD_TN * 2 + 4 * MOD_ROWS * n_out
    return pl.pallas_call(
        _mod_kernel,
        grid=(n_out // MOD_TN,),
        in_specs=[
            pl.BlockSpec((MOD_ROWS, D_MODEL), lambda n: (0, 0)),
            pl.BlockSpec((None, D_MODEL, MOD_TN), lambda n: (layer, 0, n)),
            pl.BlockSpec((None, 1, MOD_TN), lambda n: (layer, 0, n)),
        ],
        out_specs=pl.BlockSpec((MOD_ROWS, MOD_TN), lambda n: (0, n)),
        out_shape=jax.ShapeDtypeStruct((MOD_ROWS, n_out), F32),
        compiler_params=_params(("arbitrary",), nbytes),
        name="modulation",
    )(s_in, w_mod, b_mod3)


MM_SUB = 256


def _row_tiles(n_rows, size):
    size = min(size, n_rows)
    return [slice(r, r + size) for r in range(0, n_rows, size)]


def _ln_mod_rows(x_ref, shift, scale1, dst_ref, rows):
    for piece in _row_tiles(rows.stop - rows.start, ROW_CHUNK):
        sl = slice(rows.start + piece.start, rows.start + piece.stop)
        dst_ref[sl, :] = (_ln(x_ref[sl, :]) * scale1 + shift).astype(dst_ref.dtype)


def _residual_ln_store(res, y, gate, gain, bias, o_ref, rows):
    for piece in _row_tiles(rows.stop - rows.start, ROW_CHUNK):
        sl = slice(rows.start + piece.start, rows.start + piece.stop)
        z = ALPHA * res[sl, :] + gate * y[piece, :]
        o_ref[sl, :] = _ln(z) * gain + bias


def _rope(a, cos, sin):
    lane = lax.broadcasted_iota(jnp.int32, a.shape, 1)
    first = (lane % (HEAD_DIM // 2)) < (HEAD_DIM // 4)
    partner = jnp.where(first, pltpu.roll(a, HEAD_DIM - HEAD_DIM // 4, 1), pltpu.roll(a, HEAD_DIM // 4, 1))
    return a * cos + partner * sin


def _inproj_lat_kernel(x_ref, mod_ref, w_ref, cos_ref, sin_ref,
                       q_ref, qr_ref, k_ref, v_ref, xo_ref, go_ref, f_ref, xn_ref):
    shift = mod_ref[0:1, :]
    scale1 = 1.0 + mod_ref[1:2, :]

    def emit_q(rows, cols, acc):
        q_ref[rows, cols] = (acc * ATTN_SCALE).astype(q_ref.dtype)
        for h in range(COL_TILE // HEAD_DIM):
            sl = slice(h * HEAD_DIM, (h + 1) * HEAD_DIM)
            dst = slice(cols.start + sl.start, cols.start + sl.stop)
            rot = _rope(acc[:, sl], cos_ref[rows, :], sin_ref[rows, :])
            qr_ref[rows, dst] = (rot * ATTN_SCALE).astype(qr_ref.dtype)

    def emit_k(rows, cols, acc):
        for h in range(COL_TILE // HEAD_DIM):
            sl = slice(h * HEAD_DIM, (h + 1) * HEAD_DIM)
            dst = slice(cols.start + sl.start, cols.start + sl.stop)
            k_ref[rows, dst] = _rope(acc[:, sl], cos_ref[rows, :], sin_ref[rows, :]).astype(k_ref.dtype)

    def emit_to(ref, fn=lambda a: a):
        def emit(rows, cols, acc):
            ref[rows, cols] = fn(acc).astype(ref.dtype)
        return emit

    half = [slice(0, COL_TILE), slice(COL_TILE, 2 * COL_TILE)]
    plan = ([(emit_q, c) for c in half] + [(emit_k, c) for c in half] + [(emit_to(v_ref), c) for c in half]
            + [(emit_to(xo_ref), half[0]), (emit_to(go_ref, _gelu_tanh), half[0]), (emit_to(f_ref), half[0])])
    for rows in _row_tiles(x_ref.shape[0], MM_SUB):
        _ln_mod_rows(x_ref, shift, scale1, xn_ref, rows)
        for n, (emit, cols) in enumerate(plan):
            emit(rows, cols, _dot(xn_ref[rows, :], w_ref[:, n * COL_TILE:(n + 1) * COL_TILE]))


def _inproj_lat_call(x2d, mod4, w_in, layer, cos_t, sin_t):
    m_rows = x2d.shape[0]
    tm = 512
    tiles_per_seq = SEQ // tm
    row = lambda m: (m, 0)
    nbytes = (2 * tm * D_MODEL * 4 + tm * D_MODEL * 2 + D_MODEL * IN_W * 2 + 4 * tm * HEAD_DIM * 4
              + 2 * tm * (4 * NA_W * 2 + 2 * LRU_W * 4 + FNET_W * 2) + 6 * MM_SUB * COL_TILE * 4
              + 4 * ROW_CHUNK * D_MODEL * 4)
    bf = lambda w: jax.ShapeDtypeStruct((m_rows, w), BF16)
    ff = lambda w: jax.ShapeDtypeStruct((m_rows, w), F32)
    widths = [NA_W, NA_W, NA_W, NA_W, LRU_W, LRU_W, FNET_W]
    return pl.pallas_call(
        _inproj_lat_kernel,
        grid=(m_rows // tm,),
        in_specs=[
            pl.BlockSpec((tm, D_MODEL), row),
            pl.BlockSpec((None, None, 6, D_MODEL), lambda m: (0, m // tiles_per_seq, 0, 0)),
            pl.BlockSpec((D_MODEL, IN_W), lambda m: (0, 0), pipeline_mode=pl.Buffered(1)),
            pl.BlockSpec((tm, HEAD_DIM), lambda m: (m % tiles_per_seq, 0)),
            pl.BlockSpec((tm, HEAD_DIM), lambda m: (m % tiles_per_seq, 0)),
        ],
        out_specs=[pl.BlockSpec((tm, w), row) for w in widths],
        out_shape=[bf(NA_W), bf(NA_W), bf(NA_W), bf(NA_W), ff(LRU_W), ff(LRU_W), bf(FNET_W)],
        scratch_shapes=[pltpu.VMEM((tm, D_MODEL), BF16)],
        compiler_params=_params(("arbitrary",), nbytes),
        name="inproj_latent",
    )(x2d, mod4, w_in, cos_t, sin_t)


def _inproj_ctx_kernel(tile_lo, with_q, with_gf, x_ref, mod_ref, w_ref, *refs):
    refs = list(refs)
    xn_ref = refs.pop()
    q_ref = refs.pop(0) if with_q else None
    k_ref, v_ref, xo_ref = refs[0], refs[1], refs[2]
    go_ref, f_ref = (refs[3], refs[4]) if with_gf else (None, None)
    n = pl.program_id(1) + tile_lo

    @pl.when(pl.program_id(1) == 0)
    def _():
        for rows in _row_tiles(x_ref.shape[0], MM_SUB):
            _ln_mod_rows(x_ref, mod_ref[0:1, :], 1.0 + mod_ref[1:2, :], xn_ref, rows)

    acc = _dot(xn_ref[...], w_ref[...])

    if with_q:
        @pl.when(n < 2)
        def _():
            q_ref[...] = (acc * ATTN_SCALE).astype(q_ref.dtype)

    @pl.when((n >= 2) & (n < 4))
    def _():
        k_ref[...] = acc.astype(k_ref.dtype)

    @pl.when((n >= 4) & (n < 6))
    def _():
        v_ref[...] = acc.astype(v_ref.dtype)

    @pl.when(n == 6)
    def _():
        xo_ref[...] = acc

    if with_gf:
        @pl.when(n == 7)
        def _():
            go_ref[...] = _gelu_tanh(acc)

        @pl.when(n == 8)
        def _():
            f_ref[...] = acc.astype(f_ref.dtype)


def _inproj_ctx_call(c2d, mod4, w_in, layer, full):
    m_rows = c2d.shape[0]
    tm = m_rows
    tile_lo, tile_hi = (0, N_COL_TILES) if full else (2, 7)

    def col(lo):
        return lambda m, n: (m, jnp.clip(n + tile_lo - lo, 0, 1))

    one = lambda m, n: (m, 0)
    bf = lambda w: jax.ShapeDtypeStruct((m_rows, w), BF16)
    ff = lambda w: jax.ShapeDtypeStruct((m_rows, w), F32)
    out_specs, out_shape = [], []
    if full:
        out_specs.append(pl.BlockSpec((tm, COL_TILE), col(0)))
        out_shape.append(bf(NA_W))
    out_specs += [pl.BlockSpec((tm, COL_TILE), col(2)), pl.BlockSpec((tm, COL_TILE), col(4)),
                  pl.BlockSpec((tm, COL_TILE), one)]
    out_shape += [bf(NA_W), bf(NA_W), ff(LRU_W)]
    if full:
        out_specs += [pl.BlockSpec((tm, COL_TILE), one), pl.BlockSpec((tm, COL_TILE), one)]
        out_shape += [ff(LRU_W), bf(FNET_W)]
    nbytes = (2 * tm * D_MODEL * 4 + tm * D_MODEL * 2 + 2 * D_MODEL * COL_TILE * 2
              + 2 * 6 * tm * COL_TILE * 4 + 3 * tm * COL_TILE * 4 + 4 * ROW_CHUNK * D_MODEL * 4)
    return pl.pallas_call(
        functools.partial(_inproj_ctx_kernel, tile_lo, full, full),
        grid=(1, tile_hi - tile_lo),
        in_specs=[
            pl.BlockSpec((tm, D_MODEL), lambda m, n: (m, 0)),
            pl.BlockSpec((None, None, 6, D_MODEL), lambda m, n: (0, CTX_MOD_ROW, 0, 0)),
            pl.BlockSpec((D_MODEL, COL_TILE), lambda m, n: (0, n + tile_lo)),
        ],
        out_specs=out_specs,
        out_shape=out_shape,
        scratch_shapes=[pltpu.VMEM((tm, D_MODEL), BF16)],
        compiler_params=_params(("arbitrary", "arbitrary"), nbytes),
        name="inproj_context",
    )(c2d, mod4, w_in)


QBLK_ROWS = 4
KBLK_ROWS = 12
N_QBLK = GRID_H // QBLK_ROWS
QBLK = QBLK_ROWS * GRID_W
KBLK = KBLK_ROWS * GRID_W
KEY_TILE = V7X_LANES
KEY_TILES = KBLK // KEY_TILE
N_DR = 2 * WIN_H - 1
BIAS_BOTH, BIAS_SECOND, BIAS_FIRST = "both", "second", "first"


def _kblk_start(first_query_row):
    return int(np.clip(first_query_row - WIN_H // 2, 0, GRID_H - KBLK_ROWS))


def _bias_entries(first_query_row):
    entries = []
    for u in range(QBLK_ROWS):
        q_row = first_query_row + u
        row_start = int(np.clip(q_row - WIN_H // 2, 0, GRID_H - WIN_H))
        row = []
        for c in range(KEY_TILES):
            k_rows = [_kblk_start(first_query_row) + 2 * c + i for i in range(2)]
            inside = [row_start <= kr < row_start + WIN_H for kr in k_rows]
            dr = [kr - q_row + (WIN_H - 1) for kr in k_rows]
            if inside[0] and inside[1]:
                row.append((BIAS_BOTH, dr[0]))
            elif inside[1]:
                row.append((BIAS_SECOND, dr[1]))
            elif inside[0]:
                row.append((BIAS_FIRST, dr[0]))
            else:
                row.append(None)
        entries.append(row)
    return entries


BIAS_TABLE = sorted({e for b in range(N_QBLK) for row in _bias_entries(b * QBLK_ROWS) for e in row if e})
BIAS_SLOT = {e: i for i, e in enumerate(BIAS_TABLE)}
N_BIAS = len(BIAS_TABLE)


def _bias_plan(first_query_row):
    return [[BIAS_SLOT[e] if e else None for e in row] for row in _bias_entries(first_query_row)]


def _attn_kernel(n_riders, q_ref, qr_ref, k_ref, v_ref, kc_ref, vc_ref, bias_ref, *rest):
    o_ref = rest[n_riders]
    vt_ref = rest[2 * n_riders + 1]
    _run_riders(rest[:n_riders], rest[n_riders + 1:2 * n_riders + 1])
    kc = kc_ref[...]
    zero_tile = jnp.zeros((GRID_W, KEY_TILE), BF16)
    for c in range(SEQ // QBLK):
        vt_ref[:, c * QBLK:(c + 1) * QBLK] = v_ref[c * QBLK:(c + 1) * QBLK, :].T
    vct = vc_ref[...].T

    def one_block(q0, k0, plan):
        s = _dot_nt(qr_ref[pl.ds(q0, QBLK), :], k_ref[pl.ds(k0, KBLK), :])
        sc = _dot_nt(q_ref[pl.ds(q0, QBLK), :], kc)
        p_rows, pc_rows, denoms = [], [], []
        for u in range(QBLK_ROWS):
            rows = slice(u * GRID_W, (u + 1) * GRID_W)
            band = {c: s[rows, c * KEY_TILE:(c + 1) * KEY_TILE] + bias_ref[idx]
                    for c, idx in enumerate(plan[u]) if idx is not None}
            ctx_tiles = [sc[rows, c * KEY_TILE:(c + 1) * KEY_TILE] for c in range(CTX_LEN // KEY_TILE)]
            tiles = list(band.values()) + ctx_tiles
            m = jnp.max(functools.reduce(jnp.maximum, tiles), axis=-1, keepdims=True)
            p_band = {c: jnp.exp(t - m) for c, t in band.items()}
            p_ctx = [jnp.exp(t - m) for t in ctx_tiles]
            total = functools.reduce(jnp.add, list(p_band.values()) + p_ctx)
            denoms.append(jnp.sum(total, axis=-1, keepdims=True))
            p_rows.append(jnp.concatenate(
                [p_band[c].astype(BF16) if c in p_band else zero_tile for c in range(KEY_TILES)], axis=1))
            pc_rows.append(jnp.concatenate([t.astype(BF16) for t in p_ctx], axis=1))
        p = jnp.concatenate(p_rows, axis=0)
        pc = jnp.concatenate(pc_rows, axis=0)
        o_t = _dot_nt(vt_ref[:, k0:k0 + KBLK], p) + _dot_nt(vct, pc)
        o_ref[pl.ds(q0, QBLK), :] = (o_t.T / jnp.concatenate(denoms, axis=0)).astype(o_ref.dtype)

    for b in range(N_QBLK):
        first_row = b * QBLK_ROWS
        one_block(b * QBLK, _kblk_start(first_row) * GRID_W, _bias_plan(first_row))


def _attn_call(q, qr, k, v, kc, vc, bias, layer, riders):
    seq_blk = lambda b, h: (b, h)
    n_steps = BATCH * NA_HEADS
    step = lambda b, h: b * NA_HEADS + h
    r_in, r_out, r_shape, r_bytes = _rider_specs(riders, n_steps, step)
    nbytes = (2 * (5 * SEQ * HEAD_DIM * 2 + 2 * CTX_LEN * HEAD_DIM * 2 + N_BIAS * GRID_W * KEY_TILE * 4)
              + 8 * QBLK * (KBLK + CTX_LEN) * 4 + r_bytes)
    res = pl.pallas_call(
        functools.partial(_attn_kernel, len(riders)),
        grid=(BATCH, NA_HEADS),
        in_specs=[
            pl.BlockSpec((SEQ, HEAD_DIM), seq_blk),
            pl.BlockSpec((SEQ, HEAD_DIM), seq_blk),
            pl.BlockSpec((SEQ, HEAD_DIM), seq_blk),
            pl.BlockSpec((SEQ, HEAD_DIM), seq_blk),
            pl.BlockSpec((CTX_LEN, HEAD_DIM), seq_blk),
            pl.BlockSpec((CTX_LEN, HEAD_DIM), seq_blk),
            pl.BlockSpec((None, None, N_BIAS, GRID_W, KEY_TILE), lambda b, h: (layer, h, 0, 0, 0)),
        ] + r_in,
        out_specs=[pl.BlockSpec((SEQ, HEAD_DIM), seq_blk)] + r_out,
        out_shape=[jax.ShapeDtypeStruct((BATCH * SEQ, NA_W), BF16)] + r_shape,
        scratch_shapes=[pltpu.VMEM((HEAD_DIM, SEQ), BF16)],
        compiler_params=_params(("arbitrary", "arbitrary"), nbytes),
        name="neighbourhood_attention",
    )(q, qr, k, v, kc, vc, bias, *[r[0] for r in riders])
    return res[0], res[1:]


def _ctx_attn_kernel(q_ref, k_ref, v_ref, o_ref):
    s = _dot_nt(q_ref[...], k_ref[...])
    m = jnp.max(s, axis=-1, keepdims=True)
    p = jnp.exp(s - m)
    denom = jnp.sum(p, axis=-1, keepdims=True)
    o_ref[...] = (_dot(p.astype(BF16), v_ref[...]) / denom).astype(o_ref.dtype)


def _ctx_attn_call(q, k, v):
    blk = pl.BlockSpec((CTX_LEN, HEAD_DIM), lambda b, h: (b, h))
    return pl.pallas_call(
        _ctx_attn_kernel,
        grid=(BATCH, NA_HEADS),
        in_specs=[blk, blk, blk],
        out_specs=blk,
        out_shape=jax.ShapeDtypeStruct((BATCH * CTX_LEN, NA_W), BF16),
        compiler_params=_params(("arbitrary", "arbitrary"), 16 << 20),
        name="context_attention",
    )(q, k, v)


def _attn_bias_table(rpb):
    col = np.arange(GRID_W)
    col_start = np.clip(col - WIN_W // 2, 0, GRID_W - WIN_W)
    in_win = (col[None, :] >= col_start[:, None]) & (col[None, :] < col_start[:, None] + WIN_W)
    dc = np.clip(col[None, :] - col[:, None] + (WIN_W - 1), 0, 2 * WIN_W - 2)
    onehot = (in_win[None] & (dc[None] == np.arange(2 * WIN_W - 1)[:, None, None])).astype(np.float32)
    t = jnp.einsum('lhdj,jqk->lhdqk', rpb, jnp.asarray(onehot), precision=lax.Precision.HIGHEST)
    t = jnp.where(in_win[None, None, None], t, NEG_INF)
    masked = jnp.full((DEPTH, NA_HEADS, GRID_W, GRID_W), NEG_INF, F32)
    tiles = []
    for kind, dr in BIAS_TABLE:
        left = masked if kind == BIAS_SECOND else t[:, :, dr]
        right = masked if kind == BIAS_FIRST else t[:, :, dr + 1 if kind == BIAS_BOTH else dr]
        tiles.append(jnp.concatenate([left, right], axis=-1))
    return jnp.stack(tiles, axis=2)


def _rope_tables():
    quarter = HEAD_DIM // 4
    inv = ROPE_THETA ** (-jnp.arange(quarter, dtype=F32) / quarter)
    t = jnp.arange(SEQ)
    ang_r = (t // GRID_W).astype(F32)[:, None] * inv
    ang_c = (t % GRID_W).astype(F32)[:, None] * inv
    cos = jnp.concatenate([jnp.cos(ang_r), jnp.cos(ang_r), jnp.cos(ang_c), jnp.cos(ang_c)], -1)
    sin = jnp.concatenate([-jnp.sin(ang_r), jnp.sin(ang_r), -jnp.sin(ang_c), jnp.sin(ang_c)], -1)
    return cos, sin


HALO = V7X_SUBLANES
N_SEG = V7X_SUBLANES
SEG_PAD = V7X_SUBLANES
LRU_SCAN_UNROLL = 8


def _lru_coeffs(xp_ref, n_rows, cw_ref, cb_ref, w4, b4_ref, sp, a_refs, u_refs):
    seg = n_rows // N_SEG
    pitch = seg + SEG_PAD
    for s in range(N_SEG):
        base = HALO + s * seg
        xc = cb_ref[...] + xp_ref[base - CONV_W // 2:base - CONV_W // 2 + seg, :] * cw_ref[0:1, :]
        for j in range(1, CONV_W):
            off = base - CONV_W // 2 + j
            xc = xc + xp_ref[off:off + seg, :] * cw_ref[j:j + 1, :]
        th = jnp.tanh(_dot(xc.astype(BF16), w4) + b4_ref[...])
        half_xc = 0.5 * xc
        for d in range(2):
            r2 = th[:, (2 * d) * LRU_BW:(2 * d + 1) * LRU_BW] + 1.0
            i2 = th[:, (2 * d + 1) * LRU_BW:(2 * d + 2) * LRU_BW] + 1.0
            log_a = r2 * sp[d:d + 1, :]
            a = jnp.exp(log_a)
            a_refs[d][s * pitch:s * pitch + seg, :] = a
            one_minus_a2 = -jnp.tanh(log_a) * (a * a + 1.0)
            u_refs[d][s * pitch:s * pitch + seg, :] = jnp.sqrt(one_minus_a2) * (i2 * half_xc)


def _lru_local_scan(n_rows, coef_f, coef_b, state_f, state_b):
    seg = n_rows // N_SEG
    pitch = seg + SEG_PAD
    zero = jnp.zeros((N_SEG, LRU_BW), F32)
    one = jnp.ones((N_SEG, LRU_BW), F32)

    def step(coef, state, row, h, p):
        rows = pl.ds(row, N_SEG, stride=pitch)
        a = coef[0][rows, :]
        h = a * h + coef[1][rows, :]
        p = p * a
        state[0][rows, :] = p
        state[1][rows, :] = h
        return h, p

    def body(i, carry):
        hf, pf, hb, pb = carry
        for j in range(LRU_SCAN_UNROLL):
            t = i * LRU_SCAN_UNROLL + j
            hf, pf = step(coef_f, state_f, t, hf, pf)
            hb, pb = step(coef_b, state_b, seg - 1 - t, hb, pb)
        return hf, pf, hb, pb

    lax.fori_loop(0, seg // LRU_SCAN_UNROLL, body, (zero, one, zero, one))


def _lru_carries(n_rows, h_in_f, h_in_b, af, uf, ab, ub):
    seg = n_rows // N_SEG
    pitch = seg + SEG_PAD
    cf, cb = [h_in_f], [h_in_b]
    for s in range(N_SEG):
        last = s * pitch + seg - 1
        cf.append(uf[last:last + 1, :] + af[last:last + 1, :] * cf[-1])
        first = (N_SEG - 1 - s) * pitch
        cb.append(ub[first:first + 1, :] + ab[first:first + 1, :] * cb[-1])
    return cf[:N_SEG], cb[:N_SEG][::-1], cf[N_SEG], cb[N_SEG]


def _lru_emit(n_rows, cf, cb, af, uf, ab, ub, g_ref, o_ref):
    seg = n_rows // N_SEG
    pitch = seg + SEG_PAD
    for s in range(N_SEG):
        src = slice(s * pitch, s * pitch + seg)
        dst = slice(s * seg, (s + 1) * seg)
        y = (uf[src, :] + af[src, :] * cf[s]) + (ub[src, :] + ab[src, :] * cb[s])
        o_ref[dst, :] = (y * g_ref[dst, :]).astype(o_ref.dtype)


def _lru_kernel(ctx_out, n_riders, x_ref, g_ref, xc_ref, *refs):
    refs = list(refs)
    gc_ref = refs.pop(0) if ctx_out else None
    cw_ref, cb_ref, w4_ref, b4_ref, lam_ref = refs[:5]
    rider_in, refs = refs[5:5 + n_riders], refs[5 + n_riders:]
    o_ref = refs.pop(0)
    oc_ref = refs.pop(0) if ctx_out else None
    rider_out, refs = refs[:n_riders], refs[n_riders:]
    xp_ref, af, uf, ab, ub, pf, sf, pb, sb = refs
    _run_riders(rider_in, rider_out)

    lam = lam_ref[...]
    z = -lam
    sp = (-0.5 * LRU_C) * (jnp.maximum(z, 0.0) + jnp.log1p(jnp.exp(-jnp.abs(z))))
    w4 = w4_ref[...].astype(BF16)
    zeros_halo = jnp.zeros((HALO, LRU_BW), F32)
    h0 = jnp.zeros((1, LRU_BW), F32)

    xp_ref[0:HALO, :] = zeros_halo
    xp_ref[HALO:HALO + CTX_LEN, :] = xc_ref[...]
    xp_ref[HALO + CTX_LEN:2 * HALO + CTX_LEN, :] = zeros_halo
    _lru_coeffs(xp_ref, CTX_LEN, cw_ref, cb_ref, w4, b4_ref, sp, (af, ab), (uf, ub))
    _lru_local_scan(CTX_LEN, (af, uf), (ab, ub), (pf, sf), (pb, sb))
    cf, cb, hf, hb = _lru_carries(CTX_LEN, h0, h0, pf, sf, pb, sb)
    if ctx_out:
        _lru_emit(CTX_LEN, cf, cb, pf, sf, pb, sb, gc_ref, oc_ref)

    xp_ref[HALO:HALO + SEQ, :] = x_ref[...]
    xp_ref[HALO + SEQ:2 * HALO + SEQ, :] = zeros_halo
    _lru_coeffs(xp_ref, SEQ, cw_ref, cb_ref, w4, b4_ref, sp, (af, ab), (uf, ub))
    _lru_local_scan(SEQ, (af, uf), (ab, ub), (pf, sf), (pb, sb))
    cf, cb, _, _ = _lru_carries(SEQ, hf, hb, pf, sf, pb, sb)
    _lru_emit(SEQ, cf, cb, pf, sf, pb, sb, g_ref, o_ref)


def _lru_call(ctx_out, xl, gl, xc, gc, conv_w_l, conv_b_l, w4, b4, lam_l, riders):
    lat = pl.BlockSpec((SEQ, LRU_BW), lambda b, j: (b, j))
    cx = pl.BlockSpec((CTX_LEN, LRU_BW), lambda b, j: (b, j))
    r_in, r_out, r_shape, r_bytes = _rider_specs(riders, BATCH * LRU_BLOCKS, lambda b, j: b * LRU_BLOCKS + j)
    in_specs = [lat, lat, cx] + ([cx] if ctx_out else []) + [
        pl.BlockSpec((CONV_W, LRU_BW), lambda b, j: (0, j)),
        pl.BlockSpec((1, LRU_BW), lambda b, j: (0, j)),
        pl.BlockSpec((None, LRU_BW, 4 * LRU_BW), lambda b, j: (j, 0, 0)),
        pl.BlockSpec((None, 1, 4 * LRU_BW), lambda b, j: (j, 0, 0)),
        pl.BlockSpec((2, LRU_BW), lambda b, j: (0, j)),
    ] + r_in
    out_specs = [lat] + ([cx] if ctx_out else []) + r_out
    out_shape = [jax.ShapeDtypeStruct((BATCH * SEQ, LRU_W), BF16)]
    if ctx_out:
        out_shape.append(jax.ShapeDtypeStruct((BATCH * CTX_LEN, LRU_W), BF16))
    out_shape += r_shape
    seq_bytes = SEQ * LRU_BW * 4
    args = ([xl, gl, xc] + ([gc] if ctx_out else []) + [conv_w_l, conv_b_l, w4, b4, lam_l]
            + [r[0] for r in riders])
    res = pl.pallas_call(
        functools.partial(_lru_kernel, ctx_out, len(riders)),
        grid=(BATCH, LRU_BLOCKS),
        in_specs=in_specs,
        out_specs=out_specs,
        out_shape=out_shape,
        scratch_shapes=([pltpu.VMEM((SEQ + 2 * HALO, LRU_BW), F32)]
                        + [pltpu.VMEM((SEQ + N_SEG * SEG_PAD, LRU_BW), F32)] * 8),
        compiler_params=_params(("arbitrary", "arbitrary"), 15 * seq_bytes + (8 << 20) + r_bytes),
        name="rglru",
    )(*args)
    n_main = 2 if ctx_out else 1
    return res[0], (res[1] if ctx_out else None), res[n_main:]


def _fourier_kernel(n_pos, blk, f_ref, ch_ref, sh_ref, cc_ref, sc_ref, rev_ref, alt_ref, w_ref, b_ref, o_ref,
                    ec_ref, es_ref, mir_ref, mid_ref):
    half = n_pos // 2
    n_lo = half // blk
    step = pl.program_id(1)
    norm = 1.0 / math.sqrt(n_pos * FNET_GW)
    w = w_ref[...].astype(BF16)

    def linear(y):
        return (_dot(y.astype(BF16), w) + b_ref[...]).astype(o_ref.dtype)

    @pl.when(step == 0)
    def _():
        for i in range(n_lo):
            lo = blk * (2 * n_lo - 1 - i)
            if i == 0:
                mirrored = _dot(rev_ref[:, 0:blk], f_ref[lo:lo + blk, :])
            else:
                mirrored = _dot(rev_ref[...], f_ref[lo:lo + 2 * blk, :])
            rows = slice(i * blk, (i + 1) * blk)
            x = f_ref[rows, :].astype(F32)
            even = (x + mirrored).astype(BF16)
            odd = (x - mirrored).astype(BF16)
            for g in range(FNET_GROUPS):
                sl = slice(g * FNET_GW, (g + 1) * FNET_GW)
                ec_ref[rows, sl] = _dot(even[:, sl], cc_ref[...]).astype(BF16)
                es_ref[rows, sl] = _dot(odd[:, sl], sc_ref[...]).astype(BF16)
        for g in range(FNET_GROUPS):
            sl = slice(g * FNET_GW, (g + 1) * FNET_GW)
            mid_ref[0:BF16_ROWS, sl] = _dot(f_ref[half:half + BF16_ROWS, sl], cc_ref[...])
        mid_ref[BF16_ROWS:2 * BF16_ROWS, :] = _dot(alt_ref[...], ec_ref[...])

    mid = mid_ref[0:1, :]

    @pl.when(step < n_lo)
    def _():
        a = _dot(ch_ref[...], ec_ref[...])
        b = _dot(sh_ref[...], es_ref[...])
        row = lax.broadcasted_iota(jnp.int32, (blk, 1), 0)
        base = jnp.where(row % 2 == 0, 1.0, -1.0) * mid
        o_ref[...] = linear((a - b + base) * norm)
        r0 = pl.multiple_of(step * blk, blk)
        mir_ref[pl.ds(r0, blk), :] = ((a + b + base) * norm).astype(BF16)

    @pl.when(step == n_lo)
    def _():
        y = _dot(rev_ref[:, 0:blk], mir_ref[(n_lo - 1) * blk:n_lo * blk, :])
        nyquist = (mid_ref[BF16_ROWS:BF16_ROWS + 1, :] + mid) * norm
        row = lax.broadcasted_iota(jnp.int32, (blk, 1), 0)
        o_ref[...] = linear(jnp.where(row == 0, nyquist, y))

    if n_lo > 1:
        @pl.when(step > n_lo)
        def _():
            r0 = pl.multiple_of((2 * n_lo - 1 - step) * blk, blk)
            o_ref[...] = linear(_dot(rev_ref[...], mir_ref[pl.ds(r0, 2 * blk), :]))


DFT_SPLIT = 64


def _dft_matrices(n, size):
    t = np.arange(size, dtype=np.int64)

    def table(k):
        ang = (2.0 * np.pi / n) * ((k[:, None] * t[None, :]) % n).astype(np.float64)
        return jnp.asarray(np.cos(ang), F32), jnp.asarray(np.sin(ang), F32)

    if size <= DFT_SPLIT:
        c, s = table(t)
        return c.astype(BF16), s.astype(BF16)
    c1, s1 = table(DFT_SPLIT * np.arange(size // DFT_SPLIT, dtype=np.int64))
    c2, s2 = table(np.arange(DFT_SPLIT, dtype=np.int64))
    c = c1[:, None, :] * c2[None, :, :] - s1[:, None, :] * s2[None, :, :]
    s = s1[:, None, :] * c2[None, :, :] + c1[:, None, :] * s2[None, :, :]
    return c.reshape(size, size).astype(BF16), s.reshape(size, size).astype(BF16)


FOURIER_BLK = 256


def _fourier_call(f2d, n_pos, fno_w_l, fno_b_l):
    half = n_pos // 2
    blk = min(FOURIER_BLK, half)
    steps = n_pos // blk
    n_lo = half // blk
    ch, sh = _dft_matrices(n_pos, half)
    cc, sc = _dft_matrices(FNET_GW, FNET_GW)
    rev = np.zeros((blk, 2 * blk), np.float32)
    rev[np.arange(1, blk), blk - np.arange(1, blk)] = 1.0
    rev[0, blk] = 1.0
    alt = np.zeros((BF16_ROWS, half), np.float32)
    alt[0] = 1.0 - 2.0 * (np.arange(half) % 2)
    const = lambda b, k: (0, 0)
    dft_tile = lambda b, k: (jnp.minimum(k, n_lo - 1), 0)
    nbytes = (2 * n_pos * FNET_W * 2 + 2 * 2 * blk * half * 2 + 3 * half * FNET_W * 2 + 2 * FNET_W * FNET_W * 4
              + 8 * blk * FNET_W * 4 + 2 * blk * 2 * blk * 2)
    return pl.pallas_call(
        functools.partial(_fourier_kernel, n_pos, blk),
        grid=(BATCH, steps),
        in_specs=[
            pl.BlockSpec((n_pos, FNET_W), lambda b, k: (b, 0)),
            pl.BlockSpec((blk, half), dft_tile),
            pl.BlockSpec((blk, half), dft_tile),
            pl.BlockSpec((FNET_GW, FNET_GW), const),
            pl.BlockSpec((FNET_GW, FNET_GW), const),
            pl.BlockSpec((blk, 2 * blk), const),
            pl.BlockSpec((BF16_ROWS, half), const),
            pl.BlockSpec((FNET_W, FNET_W), const),
            pl.BlockSpec((1, FNET_W), const),
        ],
        out_specs=pl.BlockSpec((blk, FNET_W), lambda b, k: (b * steps + k, 0)),
        out_shape=jax.ShapeDtypeStruct((BATCH * n_pos, FNET_W), BF16),
        scratch_shapes=[pltpu.VMEM((half, FNET_W), BF16), pltpu.VMEM((half, FNET_W), BF16),
                        pltpu.VMEM((half, FNET_W), BF16), pltpu.VMEM((2 * BF16_ROWS, FNET_W), F32)],
        compiler_params=_params(("arbitrary", "arbitrary"), nbytes),
        name="fourier_mix",
    )(f2d, ch, sh, cc, sc, jnp.asarray(rev, BF16), jnp.asarray(alt, BF16), fno_w_l, fno_b_l)


def _outproj_kernel(carry_mod, na_ref, lru_ref, f_ref, res_ref, mod_ref, w_ref, g_ref, b_ref, *rest):
    if carry_mod:
        s_ref, wm_ref, bm_ref, o_ref, mo_ref = rest
        _mod_kernel(s_ref, wm_ref, bm_ref, mo_ref)
    else:
        (o_ref,) = rest
    gate = mod_ref[2:3, :]
    for rows in _row_tiles(res_ref.shape[0], MM_SUB):
        y = (_dot(na_ref[rows, :], w_ref[0:NA_W, :])
             + _dot(lru_ref[rows, :], w_ref[NA_W:NA_W + LRU_W, :])
             + _dot(f_ref[rows, :], w_ref[NA_W + LRU_W:D_MODEL, :]))
        _residual_ln_store(res_ref, y, gate, g_ref[...], b_ref[...], o_ref, rows)


def _outproj_call(na, lru, f, res, mod4, mod_row, w_out, layer, ln_g, ln_b, next_mod=None):
    m_rows = res.shape[0]
    tm = 512
    steps = m_rows // tm
    row = lambda m: (m, 0)
    nbytes = (2 * tm * D_MODEL * 2 + 2 * 2 * tm * D_MODEL * 4 + D_MODEL * D_MODEL * 2 + 2 * MM_SUB * D_MODEL * 4
              + 4 * ROW_CHUNK * D_MODEL * 4)
    in_specs = [
        pl.BlockSpec((tm, NA_W), row),
        pl.BlockSpec((tm, LRU_W), row),
        pl.BlockSpec((tm, FNET_W), row),
        pl.BlockSpec((tm, D_MODEL), row),
        pl.BlockSpec((None, None, 6, D_MODEL), lambda m: (0, mod_row(m * tm), 0, 0)),
        pl.BlockSpec((D_MODEL, D_MODEL), lambda m: (0, 0), pipeline_mode=pl.Buffered(1)),
        pl.BlockSpec((None, 1, D_MODEL), lambda m: (layer, 0, 0)),
        pl.BlockSpec((None, 1, D_MODEL), lambda m: (layer, 0, 0)),
    ]
    out_specs = [pl.BlockSpec((tm, D_MODEL), row)]
    out_shape = [jax.ShapeDtypeStruct((m_rows, D_MODEL), F32)]
    args = [na, lru, f, res, mod4, w_out, ln_g, ln_b]
    if next_mod is not None:
        s_in, w_mod, b_mod3, mod_layer = next_mod
        n_out = w_mod.shape[-1]
        slab = n_out // steps
        in_specs += [
            pl.BlockSpec((MOD_ROWS, D_MODEL), lambda m: (0, 0)),
            pl.BlockSpec((None, D_MODEL, slab), lambda m: (mod_layer, 0, m)),
            pl.BlockSpec((None, 1, slab), lambda m: (mod_layer, 0, m)),
        ]
        out_specs.append(pl.BlockSpec((MOD_ROWS, slab), lambda m: (0, m)))
        out_shape.append(jax.ShapeDtypeStruct((MOD_ROWS, n_out), F32))
        args += [s_in, w_mod, b_mod3]
        nbytes += 2 * D_MODEL * slab * 4 + D_MODEL * slab * 2
    res = pl.pallas_call(
        functools.partial(_outproj_kernel, next_mod is not None),
        grid=(steps,),
        in_specs=in_specs,
        out_specs=out_specs,
        out_shape=out_shape,
        compiler_params=_params(("arbitrary",), nbytes),
        name="outproj_residual",
    )(*args)
    return (res[0], res[1]) if next_mod is not None else (res[0], None)


MLP_TF = 512
MLP_TILES = D_FF // MLP_TF


def _mlp_kernel(x_ref, mod_ref, w1_ref, b1_ref, w2_ref, b2_ref, g_ref, b_ref, o_ref, v_ref, h0_ref, h1_ref):
    j = pl.program_id(1)
    tm = x_ref.shape[0]

    def up(h_out, rows=slice(None)):
        h = _dot(v_ref[rows, :], w1_ref[...]) + b1_ref[...]
        h_out[rows, :] = jnp.square(jnp.maximum(h, 0.0)).astype(BF16)

    def down(h_in, first):
        for c in range(D_MODEL // COL_TILE):
            sl = slice(c * COL_TILE, (c + 1) * COL_TILE)
            part = _dot(h_in[...], w2_ref[:, sl])
            if first:
                o_ref[:, sl] = part
            else:
                o_ref[:, sl] += part

    @pl.when(j == 0)
    def _():
        shift = mod_ref[3:4, :]
        scale1 = 1.0 + mod_ref[4:5, :]
        for rows in _row_tiles(tm, MM_SUB):
            _ln_mod_rows(x_ref, shift, scale1, v_ref, rows)
            up(h0_ref, rows)

    @pl.when(j == 1)
    def _():
        down(h0_ref, True)
        up(h1_ref)

    @pl.when((j > 1) & (j < MLP_TILES) & (j % 2 == 0))
    def _():
        down(h1_ref, False)
        up(h0_ref)

    @pl.when((j > 1) & (j < MLP_TILES) & (j % 2 == 1))
    def _():
        down(h0_ref, False)
        up(h1_ref)

    @pl.when(j == MLP_TILES)
    def _():
        h_last = h1_ref if (MLP_TILES - 1) % 2 else h0_ref
        gate = mod_ref[5:6, :]
        for rows in _row_tiles(tm, MM_SUB):
            y = o_ref[rows, :] + _dot(h_last[rows, :], w2_ref[...]) + b2_ref[...]
            _residual_ln_store(x_ref, y, gate, g_ref[...], b_ref[...], o_ref, rows)


def _mlp_call(x1, mod4, mod_row, tm, w1, b1, w2, b2, layer, ln_g, ln_b):
    m_rows = x1.shape[0]
    row = lambda m, j: (m, 0)
    vec = lambda m, j: (layer, 0, 0)
    nbytes = (4 * tm * D_MODEL * 4 + tm * D_MODEL * 2 + 2 * 2 * D_MODEL * MLP_TF * 2 + 2 * tm * MLP_TF * 2
              + tm * MLP_TF * 4 + tm * COL_TILE * 4 + 2 * MM_SUB * D_MODEL * 4 + 4 * ROW_CHUNK * D_MODEL * 4)
    return pl.pallas_call(
        _mlp_kernel,
        grid=(m_rows // tm, MLP_TILES + 1),
        in_specs=[
            pl.BlockSpec((tm, D_MODEL), row),
            pl.BlockSpec((None, None, 6, D_MODEL), lambda m, j: (0, mod_row(m * tm), 0, 0)),
            pl.BlockSpec((D_MODEL, MLP_TF), lambda m, j: (0, jnp.minimum(j, MLP_TILES - 1))),
            pl.BlockSpec((None, 1, MLP_TF), lambda m, j: (layer, 0, jnp.minimum(j, MLP_TILES - 1))),
            pl.BlockSpec((MLP_TF, D_MODEL), lambda m, j: (jnp.maximum(j - 1, 0), 0)),
            pl.BlockSpec((None, 1, D_MODEL), vec),
            pl.BlockSpec((None, 1, D_MODEL), vec),
            pl.BlockSpec((None, 1, D_MODEL), vec),
        ],
        out_specs=pl.BlockSpec((tm, D_MODEL), row),
        out_shape=jax.ShapeDtypeStruct((m_rows, D_MODEL), F32),
        scratch_shapes=[pltpu.VMEM((tm, D_MODEL), BF16), pltpu.VMEM((tm, MLP_TF), BF16),
                        pltpu.VMEM((tm, MLP_TF), BF16)],
        compiler_params=_params(("arbitrary", "arbitrary"), nbytes),
        name="mlp_residual",
    )(x1, mod4, w1, b1, w2, b2, ln_g, ln_b)


def kernel(x, c, ctx, c_ctx, w_mod, b_mod, w_in, rpb, conv_w, conv_b, lru_wa, lru_ba, lru_wx, lru_bx, lru_lambda,
           fno_w, fno_b, w_out, ln1_g, ln1_b, w_fc1, b_fc1, w_fc2, b_fc2, ln2_g, ln2_b):
    xl = x.reshape(BATCH * SEQ, D_MODEL)
    xc = ctx.reshape(BATCH * CTX_LEN, D_MODEL)
    s_in = jnp.concatenate([c, c_ctx[None], jnp.zeros((MOD_ROWS - BATCH - 1, D_MODEL), F32)], 0)
    b_mod3 = b_mod.reshape(DEPTH, 1, 6 * D_MODEL)
    as_mod4 = lambda m: m.reshape(1, MOD_ROWS, 6, D_MODEL)
    mod4 = as_mod4(_mod_call(s_in, w_mod, b_mod3, 0))
    cos_t, sin_t = _rope_tables()
    bias_tab = _attn_bias_table(rpb)
    w_in_l = w_in[0].astype(BF16)
    vec3 = lambda a: a.reshape(DEPTH, 1, a.shape[-1])
    ln1_g3, ln1_b3, ln2_g3, ln2_b3 = vec3(ln1_g), vec3(ln1_b), vec3(ln2_g), vec3(ln2_b)
    b_fc1_3, b_fc2_3 = vec3(b_fc1), vec3(b_fc2)
    lat_row = lambda r0: r0 // SEQ
    ctx_row = lambda r0: CTX_MOD_ROW

    for layer in range(DEPTH):
        ctx_out = layer < DEPTH - 1
        q, qr, k, v, xo, go, f = _inproj_lat_call(xl, mod4, w_in_l, layer, cos_t, sin_t)
        if ctx_out:
            qc, kc, vc, xoc, goc, fc = _inproj_ctx_call(xc, mod4, w_in_l, layer, True)
        else:
            kc, vc, xoc = _inproj_ctx_call(xc, mod4, w_in_l, layer, False)
            goc = None

        na, (w_out_l, w_fc1_l, w_fc2_l) = _attn_call(
            q, qr, k, v, kc, vc, bias_tab, layer,
            [_rider(w_out, layer, 0), _rider(w_fc1, layer, 1), _rider(w_fc2, layer, 0)])

        w4 = 0.5 * jnp.concatenate([lru_wa[layer, 0], lru_wx[layer, 0], lru_wa[layer, 1], lru_wx[layer, 1]], -1)
        blk = lambda a: a.reshape(LRU_BLOCKS, 1, LRU_BW)
        b4 = 0.5 * jnp.concatenate([blk(lru_ba[layer, 0]), blk(lru_bx[layer, 0]),
                                    blk(lru_ba[layer, 1]), blk(lru_bx[layer, 1])], -1)
        next_w_in = [_rider(w_in, layer + 1, 0)] if layer + 1 < DEPTH else []
        lru, lru_c, cast = _lru_call(ctx_out, xo, go, xoc, goc, conv_w[layer], conv_b[layer][None], w4, b4,
                                     lru_lambda[layer], next_w_in)
        if next_w_in:
            w_in_l = cast[0]

        fm = _fourier_call(f, SEQ, fno_w[layer], fno_b[layer][None])
        next_mod = (s_in, w_mod, b_mod3, layer + 1) if layer + 1 < DEPTH else None
        x1, mod_next = _outproj_call(na, lru, fm, xl, mod4, lat_row, w_out_l, layer, ln1_g3, ln1_b3, next_mod)
        xl = _mlp_call(x1, mod4, lat_row, 1024, w_fc1_l, b_fc1_3, w_fc2_l, b_fc2_3, layer, ln2_g3, ln2_b3)

        if ctx_out:
            na_c = _ctx_attn_call(qc, kc, vc)
            fm_c = _fourier_call(fc, CTX_LEN, fno_w[layer], fno_b[layer][None])
            c1, _ = _outproj_call(na_c, lru_c, fm_c, xc, mod4, ctx_row, w_out_l, layer, ln1_g3, ln1_b3)
            xc = _mlp_call(c1, mod4, ctx_row, 512, w_fc1_l, b_fc1_3, w_fc2_l, b_fc2_3, layer, ln2_g3, ln2_b3)
        if mod_next is not None:
            mod4 = as_mod4(mod_next)

    return xl.reshape(BATCH, SEQ, D_MODEL)
```

```python
import functools
import math

import jax
import jax.numpy as jnp
import numpy as np
from jax import lax
from jax.experimental import pallas as pl
from jax.experimental.pallas import tpu as pltpu

F32 = jnp.float32
BF16 = jnp.bfloat16

D_MODEL = 2048
BATCH = 2
SEQ = 4096
DEPTH = 2
GRID_W = 64
GRID_H = SEQ // GRID_W
CTX_LEN = 256
HEAD_DIM = 128
NA_W = D_MODEL // 2
NA_HEADS = NA_W // HEAD_DIM
WIN_H = 8
WIN_W = 16
LRU_W = D_MODEL // 4
LRU_BLOCKS = 4
LRU_BW = LRU_W // LRU_BLOCKS
CONV_W = 4
LRU_C = 8.0
FNET_W = D_MODEL // 4
FNET_GROUPS = 4
FNET_GW = FNET_W // FNET_GROUPS
IN_W = 3 * NA_W + 2 * LRU_W + FNET_W
D_FF = 4 * D_MODEL
ROPE_THETA = 10000.0
LN_EPS = 1e-5
NEG_INF = -1e30
ALPHA = (2.0 * DEPTH) ** 0.25
ATTN_SCALE = HEAD_DIM ** -0.5

V7X_LANES = 128
V7X_SUBLANES = 8
BF16_ROWS = 2 * V7X_SUBLANES
V7X_VMEM_BYTES = 64 * 1024 * 1024
VMEM_CEILING = V7X_VMEM_BYTES - 6 * 1024 * 1024

COL_TILE = 512
N_COL_TILES = IN_W // COL_TILE
ROW_CHUNK = 128
MOD_ROWS = 8
CTX_MOD_ROW = BATCH


def _vmem_limit(nbytes):
    return int(min(VMEM_CEILING, nbytes * 5 // 4 + (4 << 20)))


def _params(semantics, nbytes):
    return pltpu.CompilerParams(dimension_semantics=semantics, vmem_limit_bytes=_vmem_limit(nbytes))


def _ln(x):
    mu = jnp.mean(x, axis=-1, keepdims=True)
    xc = x - mu
    var = jnp.mean(xc * xc, axis=-1, keepdims=True)
    return xc * lax.rsqrt(var + LN_EPS)


def _sigmoid(x):
    return 1.0 / (1.0 + jnp.exp(-x))


def _gelu_tanh(x):
    return 0.5 * x * (1.0 + jnp.tanh(math.sqrt(2.0 / math.pi) * (x + 0.044715 * (x * x * x))))


def _dot(a, b):
    return jnp.dot(a, b, preferred_element_type=F32)


def _dot_nt(a, b):
    return lax.dot_general(a, b, (((1,), (1,)), ((), ())), preferred_element_type=F32)


def _rider(w, layer, axis):
    return (w, layer, axis)


def _rider_specs(riders, n_steps, step_of):
    in_specs, out_specs, out_shapes, nbytes = [], [], [], 0
    for w, layer, axis in riders:
        rows, cols = w.shape[1:]
        if axis == 0:
            blk = (rows // n_steps, cols)
            in_idx = lambda *g, layer=layer: (layer, step_of(*g), 0)
            out_idx = lambda *g: (step_of(*g), 0)
        else:
            blk = (rows, cols // n_steps)
            in_idx = lambda *g, layer=layer: (layer, 0, step_of(*g))
            out_idx = lambda *g: (0, step_of(*g))
        in_specs.append(pl.BlockSpec((None,) + blk, in_idx))
        out_specs.append(pl.BlockSpec(blk, out_idx))
        out_shapes.append(jax.ShapeDtypeStruct((rows, cols), BF16))
        nbytes += 2 * blk[0] * blk[1] * (4 + 2)
    return in_specs, out_specs, out_shapes, nbytes


def _run_riders(in_refs, out_refs):
    for src, dst in zip(in_refs, out_refs):
        dst[...] = src[...].astype(dst.dtype)


MOD_TN = 1024


def _mod_kernel(s_ref, w_ref, b_ref, o_ref):
    s = s_ref[...]
    s = s * _sigmoid(s)
    o_ref[...] = _dot(s.astype(BF16), w_ref[...].astype(BF16)) + b_ref[...]


def _mod_call(s_in, w_mod, b_mod3, layer):
    n_out = w_mod.shape[-1]
    nbytes = 2 * (D_MODEL * MOD_TN * 4) + D_MODEL * MOD_TN * 2 + 4 * MOD_ROWS * n_out
    return pl.pallas_call(
        _mod_kernel,
        grid=(n_out // MOD_TN,),
        in_specs=[
            pl.BlockSpec((MOD_ROWS, D_MODEL), lambda n: (0, 0)),
            pl.BlockSpec((None, D_MODEL, MOD_TN), lambda n: (layer, 0, n)),
            pl.BlockSpec((None, 1, MOD_TN), lambda n: (layer, 0, n)),
        ],
        out_specs=pl.BlockSpec((MOD_ROWS, MOD_TN), lambda n: (0, n)),
        out_shape=jax.ShapeDtypeStruct((MOD_ROWS, n_out), F32),
        compiler_params=_params(("arbitrary",), nbytes),
        name="modulation",
    )(s_in, w_mod, b_mod3)


MM_SUB = 256


def _row_tiles(n_rows, size):
    size = min(size, n_rows)
    return [slice(r, r + size) for r in range(0, n_rows, size)]


def _ln_mod_rows(x_ref, shift, scale1, dst_ref, rows):
    for piece in _row_tiles(rows.stop - rows.start, ROW_CHUNK):
        sl = slice(rows.start + piece.start, rows.start + piece.stop)
        dst_ref[sl, :] = (_ln(x_ref[sl, :]) * scale1 + shift).astype(dst_ref.dtype)


def _residual_ln_store(res, y, gate, gain, bias, o_ref, rows):
    for piece in _row_tiles(rows.stop - rows.start, ROW_CHUNK):
        sl = slice(rows.start + piece.start, rows.start + piece.stop)
        z = ALPHA * res[sl, :] + gate * y[piece, :]
        o_ref[sl, :] = _ln(z) * gain + bias


def _rope(a, cos, sin):
    lane = lax.broadcasted_iota(jnp.int32, a.shape, 1)
    first = (lane % (HEAD_DIM // 2)) < (HEAD_DIM // 4)
    partner = jnp.where(first, pltpu.roll(a, HEAD_DIM - HEAD_DIM // 4, 1), pltpu.roll(a, HEAD_DIM // 4, 1))
    return a * cos + partner * sin


def _inproj_lat_kernel(x_ref, mod_ref, w_ref, cos_ref, sin_ref,
                       q_ref, qr_ref, k_ref, v_ref, xo_ref, go_ref, f_ref, xn_ref):
    shift = mod_ref[0:1, :]
    scale1 = 1.0 + mod_ref[1:2, :]

    def emit_q(rows, cols, acc):
        q_ref[rows, cols] = (acc * ATTN_SCALE).astype(q_ref.dtype)
        for h in range(COL_TILE // HEAD_DIM):
            sl = slice(h * HEAD_DIM, (h + 1) * HEAD_DIM)
            dst = slice(cols.start + sl.start, cols.start + sl.stop)
            rot = _rope(acc[:, sl], cos_ref[rows, :], sin_ref[rows, :])
            qr_ref[rows, dst] = (rot * ATTN_SCALE).astype(qr_ref.dtype)

    def emit_k(rows, cols, acc):
        for h in range(COL_TILE // HEAD_DIM):
            sl = slice(h * HEAD_DIM, (h + 1) * HEAD_DIM)
            dst = slice(cols.start + sl.start, cols.start + sl.stop)
            k_ref[rows, dst] = _rope(acc[:, sl], cos_ref[rows, :], sin_ref[rows, :]).astype(k_ref.dtype)

    def emit_to(ref, fn=lambda a: a):
        def emit(rows, cols, acc):
            ref[rows, cols] = fn(acc).astype(ref.dtype)
        return emit

    half = [slice(0, COL_TILE), slice(COL_TILE, 2 * COL_TILE)]
    plan = ([(emit_q, c) for c in half] + [(emit_k, c) for c in half] + [(emit_to(v_ref), c) for c in half]
            + [(emit_to(xo_ref), half[0]), (emit_to(go_ref, _gelu_tanh), half[0]), (emit_to(f_ref), half[0])])
    for rows in _row_tiles(x_ref.shape[0], MM_SUB):
        _ln_mod_rows(x_ref, shift, scale1, xn_ref, rows)
        for n, (emit, cols) in enumerate(plan):
            emit(rows, cols, _dot(xn_ref[rows, :], w_ref[:, n * COL_TILE:(n + 1) * COL_TILE]))


def _inproj_lat_call(x2d, mod4, w_in, layer, cos_t, sin_t):
    m_rows = x2d.shape[0]
    tm = 512
    tiles_per_seq = SEQ // tm
    row = lambda m: (m, 0)
    nbytes = (2 * tm * D_MODEL * 4 + tm * D_MODEL * 2 + D_MODEL * IN_W * 2 + 4 * tm * HEAD_DIM * 4
              + 2 * tm * (4 * NA_W * 2 + 2 * LRU_W * 4 + FNET_W * 2) + 6 * MM_SUB * COL_TILE * 4
              + 4 * ROW_CHUNK * D_MODEL * 4)
    bf = lambda w: jax.ShapeDtypeStruct((m_rows, w), BF16)
    ff = lambda w: jax.ShapeDtypeStruct((m_rows, w), F32)
    widths = [NA_W, NA_W, NA_W, NA_W, LRU_W, LRU_W, FNET_W]
    return pl.pallas_call(
        _inproj_lat_kernel,
        grid=(m_rows // tm,),
        in_specs=[
            pl.BlockSpec((tm, D_MODEL), row),
            pl.BlockSpec((None, None, 6, D_MODEL), lambda m: (0, m // tiles_per_seq, 0, 0)),
            pl.BlockSpec((D_MODEL, IN_W), lambda m: (0, 0), pipeline_mode=pl.Buffered(1)),
            pl.BlockSpec((tm, HEAD_DIM), lambda m: (m % tiles_per_seq, 0)),
            pl.BlockSpec((tm, HEAD_DIM), lambda m: (m % tiles_per_seq, 0)),
        ],
        out_specs=[pl.BlockSpec((tm, w), row) for w in widths],
        out_shape=[bf(NA_W), bf(NA_W), bf(NA_W), bf(NA_W), ff(LRU_W), ff(LRU_W), bf(FNET_W)],
        scratch_shapes=[pltpu.VMEM((tm, D_MODEL), BF16)],
        compiler_params=_params(("arbitrary",), nbytes),
        name="inproj_latent",
    )(x2d, mod4, w_in, cos_t, sin_t)


def _inproj_ctx_kernel(tile_lo, with_q, with_gf, x_ref, mod_ref, w_ref, *refs):
    refs = list(refs)
    xn_ref = refs.pop()
    q_ref = refs.pop(0) if with_q else None
    k_ref, v_ref, xo_ref = refs[0], refs[1], refs[2]
    go_ref, f_ref = (refs[3], refs[4]) if with_gf else (None, None)
    n = pl.program_id(1) + tile_lo

    @pl.when(pl.program_id(1) == 0)
    def _():
        for rows in _row_tiles(x_ref.shape[0], MM_SUB):
            _ln_mod_rows(x_ref, mod_ref[0:1, :], 1.0 + mod_ref[1:2, :], xn_ref, rows)

    acc = _dot(xn_ref[...], w_ref[...])

    if with_q:
        @pl.when(n < 2)
        def _():
            q_ref[...] = (acc * ATTN_SCALE).astype(q_ref.dtype)

    @pl.when((n >= 2) & (n < 4))
    def _():
        k_ref[...] = acc.astype(k_ref.dtype)

    @pl.when((n >= 4) & (n < 6))
    def _():
        v_ref[...] = acc.astype(v_ref.dtype)

    @pl.when(n == 6)
    def _():
        xo_ref[...] = acc

    if with_gf:
        @pl.when(n == 7)
        def _():
            go_ref[...] = _gelu_tanh(acc)

        @pl.when(n == 8)
        def _():
            f_ref[...] = acc.astype(f_ref.dtype)


def _inproj_ctx_call(c2d, mod4, w_in, layer, full):
    m_rows = c2d.shape[0]
    tm = m_rows
    tile_lo, tile_hi = (0, N_COL_TILES) if full else (2, 7)

    def col(lo):
        return lambda m, n: (m, jnp.clip(n + tile_lo - lo, 0, 1))

    one = lambda m, n: (m, 0)
    bf = lambda w: jax.ShapeDtypeStruct((m_rows, w), BF16)
    ff = lambda w: jax.ShapeDtypeStruct((m_rows, w), F32)
    out_specs, out_shape = [], []
    if full:
        out_specs.append(pl.BlockSpec((tm, COL_TILE), col(0)))
        out_shape.append(bf(NA_W))
    out_specs += [pl.BlockSpec((tm, COL_TILE), col(2)), pl.BlockSpec((tm, COL_TILE), col(4)),
                  pl.BlockSpec((tm, COL_TILE), one)]
    out_shape += [bf(NA_W), bf(NA_W), ff(LRU_W)]
    if full:
        out_specs += [pl.BlockSpec((tm, COL_TILE), one), pl.BlockSpec((tm, COL_TILE), one)]
        out_shape += [ff(LRU_W), bf(FNET_W)]
    nbytes = (2 * tm * D_MODEL * 4 + tm * D_MODEL * 2 + 2 * D_MODEL * COL_TILE * 2
              + 2 * 6 * tm * COL_TILE * 4 + 3 * tm * COL_TILE * 4 + 4 * ROW_CHUNK * D_MODEL * 4)
    return pl.pallas_call(
        functools.partial(_inproj_ctx_kernel, tile_lo, full, full),
        grid=(1, tile_hi - tile_lo),
        in_specs=[
            pl.BlockSpec((tm, D_MODEL), lambda m, n: (m, 0)),
            pl.BlockSpec((None, None, 6, D_MODEL), lambda m, n: (0, CTX_MOD_ROW, 0, 0)),
            pl.BlockSpec((D_MODEL, COL_TILE), lambda m, n: (0, n + tile_lo)),
        ],
        out_specs=out_specs,
        out_shape=out_shape,
        scratch_shapes=[pltpu.VMEM((tm, D_MODEL), BF16)],
        compiler_params=_params(("arbitrary", "arbitrary"), nbytes),
        name="inproj_context",
    )(c2d, mod4, w_in)


QBLK_ROWS = 4
KBLK_ROWS = 12
N_QBLK = GRID_H // QBLK_ROWS
QBLK = QBLK_ROWS * GRID_W
KBLK = KBLK_ROWS * GRID_W
KEY_TILE = V7X_LANES
KEY_TILES = KBLK // KEY_TILE
N_DR = 2 * WIN_H - 1
BIAS_BOTH, BIAS_SECOND, BIAS_FIRST = "both", "second", "first"


def _kblk_start(first_query_row):
    return int(np.clip(first_query_row - WIN_H // 2, 0, GRID_H - KBLK_ROWS))


def _bias_entries(first_query_row):
    entries = []
    for u in range(QBLK_ROWS):
        q_row = first_query_row + u
        row_start = int(np.clip(q_row - WIN_H // 2, 0, GRID_H - WIN_H))
        row = []
        for c in range(KEY_TILES):
            k_rows = [_kblk_start(first_query_row) + 2 * c + i for i in range(2)]
            inside = [row_start <= kr < row_start + WIN_H for kr in k_rows]
            dr = [kr - q_row + (WIN_H - 1) for kr in k_rows]
            if inside[0] and inside[1]:
                row.append((BIAS_BOTH, dr[0]))
            elif inside[1]:
                row.append((BIAS_SECOND, dr[1]))
            elif inside[0]:
                row.append((BIAS_FIRST, dr[0]))
            else:
                row.append(None)
        entries.append(row)
    return entries


BIAS_TABLE = sorted({e for b in range(N_QBLK) for row in _bias_entries(b * QBLK_ROWS) for e in row if e})
BIAS_SLOT = {e: i for i, e in enumerate(BIAS_TABLE)}
N_BIAS = len(BIAS_TABLE)


def _bias_plan(first_query_row):
    return [[BIAS_SLOT[e] if e else None for e in row] for row in _bias_entries(first_query_row)]


def _attn_kernel(n_riders, q_ref, qr_ref, k_ref, v_ref, kc_ref, vc_ref, bias_ref, *rest):
    o_ref = rest[n_riders]
    vt_ref = rest[2 * n_riders + 1]
    _run_riders(rest[:n_riders], rest[n_riders + 1:2 * n_riders + 1])
    kc = kc_ref[...]
    zero_tile = jnp.zeros((GRID_W, KEY_TILE), BF16)
    for c in range(SEQ // QBLK):
        vt_ref[:, c * QBLK:(c + 1) * QBLK] = v_ref[c * QBLK:(c + 1) * QBLK, :].T
    vct = vc_ref[...].T

    def one_block(q0, k0, plan):
        s = _dot_nt(qr_ref[pl.ds(q0, QBLK), :], k_ref[pl.ds(k0, KBLK), :])
        sc = _dot_nt(q_ref[pl.ds(q0, QBLK), :], kc)
        p_rows, pc_rows, denoms = [], [], []
        for u in range(QBLK_ROWS):
            rows = slice(u * GRID_W, (u + 1) * GRID_W)
            band = {c: s[rows, c * KEY_TILE:(c + 1) * KEY_TILE] + bias_ref[idx]
                    for c, idx in enumerate(plan[u]) if idx is not None}
            ctx_tiles = [sc[rows, c * KEY_TILE:(c + 1) * KEY_TILE] for c in range(CTX_LEN // KEY_TILE)]
            tiles = list(band.values()) + ctx_tiles
            m = jnp.max(functools.reduce(jnp.maximum, tiles), axis=-1, keepdims=True)
            p_band = {c: jnp.exp(t - m) for c, t in band.items()}
            p_ctx = [jnp.exp(t - m) for t in ctx_tiles]
            total = functools.reduce(jnp.add, list(p_band.values()) + p_ctx)
            denoms.append(jnp.sum(total, axis=-1, keepdims=True))
            p_rows.append(jnp.concatenate(
                [p_band[c].astype(BF16) if c in p_band else zero_tile for c in range(KEY_TILES)], axis=1))
            pc_rows.append(jnp.concatenate([t.astype(BF16) for t in p_ctx], axis=1))
        p = jnp.concatenate(p_rows, axis=0)
        pc = jnp.concatenate(pc_rows, axis=0)
        o_t = _dot_nt(vt_ref[:, k0:k0 + KBLK], p) + _dot_nt(vct, pc)
        o_ref[pl.ds(q0, QBLK), :] = (o_t.T / jnp.concatenate(denoms, axis=0)).astype(o_ref.dtype)

    for b in range(N_QBLK):
        first_row = b * QBLK_ROWS
        one_block(b * QBLK, _kblk_start(first_row) * GRID_W, _bias_plan(first_row))


def _attn_call(q, qr, k, v, kc, vc, bias, layer, riders):
    seq_blk = lambda b, h: (b, h)
    n_steps = BATCH * NA_HEADS
    step = lambda b, h: b * NA_HEADS + h
    r_in, r_out, r_shape, r_bytes = _rider_specs(riders, n_steps, step)
    nbytes = (2 * (5 * SEQ * HEAD_DIM * 2 + 2 * CTX_LEN * HEAD_DIM * 2 + N_BIAS * GRID_W * KEY_TILE * 4)
              + 8 * QBLK * (KBLK + CTX_LEN) * 4 + r_bytes)
    res = pl.pallas_call(
        functools.partial(_attn_kernel, len(riders)),
        grid=(BATCH, NA_HEADS),
        in_specs=[
            pl.BlockSpec((SEQ, HEAD_DIM), seq_blk),
            pl.BlockSpec((SEQ, HEAD_DIM), seq_blk),
            pl.BlockSpec((SEQ, HEAD_DIM), seq_blk),
            pl.BlockSpec((SEQ, HEAD_DIM), seq_blk),
            pl.BlockSpec((CTX_LEN, HEAD_DIM), seq_blk),
            pl.BlockSpec((CTX_LEN, HEAD_DIM), seq_blk),
            pl.BlockSpec((None, None, N_BIAS, GRID_W, KEY_TILE), lambda b, h: (layer, h, 0, 0, 0)),
        ] + r_in,
        out_specs=[pl.BlockSpec((SEQ, HEAD_DIM), seq_blk)] + r_out,
        out_shape=[jax.ShapeDtypeStruct((BATCH * SEQ, NA_W), BF16)] + r_shape,
        scratch_shapes=[pltpu.VMEM((HEAD_DIM, SEQ), BF16)],
        compiler_params=_params(("arbitrary", "arbitrary"), nbytes),
        name="neighbourhood_attention",
    )(q, qr, k, v, kc, vc, bias, *[r[0] for r in riders])
    return res[0], res[1:]


def _ctx_attn_kernel(q_ref, k_ref, v_ref, o_ref):
    s = _dot_nt(q_ref[...], k_ref[...])
    m = jnp.max(s, axis=-1, keepdims=True)
    p = jnp.exp(s - m)
    denom = jnp.sum(p, axis=-1, keepdims=True)
    o_ref[...] = (_dot(p.astype(BF16), v_ref[...]) / denom).astype(o_ref.dtype)


def _ctx_attn_call(q, k, v):
    blk = pl.BlockSpec((CTX_LEN, HEAD_DIM), lambda b, h: (b, h))
    return pl.pallas_call(
        _ctx_attn_kernel,
        grid=(BATCH, NA_HEADS),
        in_specs=[blk, blk, blk],
        out_specs=blk,
        out_shape=jax.ShapeDtypeStruct((BATCH * CTX_LEN, NA_W), BF16),
        compiler_params=_params(("arbitrary", "arbitrary"), 16 << 20),
        name="context_attention",
    )(q, k, v)


def _attn_bias_table(rpb):
    col = np.arange(GRID_W)
    col_start = np.clip(col - WIN_W // 2, 0, GRID_W - WIN_W)
    in_win = (col[None, :] >= col_start[:, None]) & (col[None, :] < col_start[:, None] + WIN_W)
    dc = np.clip(col[None, :] - col[:, None] + (WIN_W - 1), 0, 2 * WIN_W - 2)
    onehot = (in_win[None] & (dc[None] == np.arange(2 * WIN_W - 1)[:, None, None])).astype(np.float32)
    t = jnp.einsum('lhdj,jqk->lhdqk', rpb, jnp.asarray(onehot), precision=lax.Precision.HIGHEST)
    t = jnp.where(in_win[None, None, None], t, NEG_INF)
    masked = jnp.full((DEPTH, NA_HEADS, GRID_W, GRID_W), NEG_INF, F32)
    tiles = []
    for kind, dr in BIAS_TABLE:
        left = masked if kind == BIAS_SECOND else t[:, :, dr]
        right = masked if kind == BIAS_FIRST else t[:, :, dr + 1 if kind == BIAS_BOTH else dr]
        tiles.append(jnp.concatenate([left, right], axis=-1))
    return jnp.stack(tiles, axis=2)


def _rope_tables():
    quarter = HEAD_DIM // 4
    inv = ROPE_THETA ** (-jnp.arange(quarter, dtype=F32) / quarter)
    t = jnp.arange(SEQ)
    ang_r = (t // GRID_W).astype(F32)[:, None] * inv
    ang_c = (t % GRID_W).astype(F32)[:, None] * inv
    cos = jnp.concatenate([jnp.cos(ang_r), jnp.cos(ang_r), jnp.cos(ang_c), jnp.cos(ang_c)], -1)
    sin = jnp.concatenate([-jnp.sin(ang_r), jnp.sin(ang_r), -jnp.sin(ang_c), jnp.sin(ang_c)], -1)
    return cos, sin


HALO = V7X_SUBLANES
N_SEG = V7X_SUBLANES
SEG_PAD = V7X_SUBLANES
LRU_SCAN_UNROLL = 8


def _lru_coeffs(xp_ref, n_rows, cw_ref, cb_ref, w4, b4_ref, sp, a_refs, u_refs):
    seg = n_rows // N_SEG
    pitch = seg + SEG_PAD
    for s in range(N_SEG):
        base = HALO + s * seg
        xc = cb_ref[...] + xp_ref[base - CONV_W // 2:base - CONV_W // 2 + seg, :] * cw_ref[0:1, :]
        for j in range(1, CONV_W):
            off = base - CONV_W // 2 + j
            xc = xc + xp_ref[off:off + seg, :] * cw_ref[j:j + 1, :]
        th = jnp.tanh(_dot(xc.astype(BF16), w4) + b4_ref[...])
        half_xc = 0.5 * xc
        for d in range(2):
            r2 = th[:, (2 * d) * LRU_BW:(2 * d + 1) * LRU_BW] + 1.0
            i2 = th[:, (2 * d + 1) * LRU_BW:(2 * d + 2) * LRU_BW] + 1.0
            log_a = r2 * sp[d:d + 1, :]
            a = jnp.exp(log_a)
            a_refs[d][s * pitch:s * pitch + seg, :] = a
            one_minus_a2 = -jnp.tanh(log_a) * (a * a + 1.0)
            root = jnp.where(one_minus_a2 == 0.0, 0.0, one_minus_a2 * lax.rsqrt(one_minus_a2))
            u_refs[d][s * pitch:s * pitch + seg, :] = root * (i2 * half_xc)


def _lru_local_scan(n_rows, coef_f, coef_b, state_f, state_b):
    seg = n_rows // N_SEG
    pitch = seg + SEG_PAD
    zero = jnp.zeros((N_SEG, LRU_BW), F32)
    one = jnp.ones((N_SEG, LRU_BW), F32)

    def step(coef, state, row, h, p):
        rows = pl.ds(row, N_SEG, stride=pitch)
        a = coef[0][rows, :]
        h = a * h + coef[1][rows, :]
        p = p * a
        state[0][rows, :] = p
        state[1][rows, :] = h
        return h, p

    def body(i, carry):
        hf, pf, hb, pb = carry
        for j in range(LRU_SCAN_UNROLL):
            t = i * LRU_SCAN_UNROLL + j
            hf, pf = step(coef_f, state_f, t, hf, pf)
            hb, pb = step(coef_b, state_b, seg - 1 - t, hb, pb)
        return hf, pf, hb, pb

    lax.fori_loop(0, seg // LRU_SCAN_UNROLL, body, (zero, one, zero, one))


def _lru_carries(n_rows, h_in_f, h_in_b, af, uf, ab, ub):
    seg = n_rows // N_SEG
    pitch = seg + SEG_PAD
    cf, cb = [h_in_f], [h_in_b]
    for s in range(N_SEG):
        last = s * pitch + seg - 1
        cf.append(uf[last:last + 1, :] + af[last:last + 1, :] * cf[-1])
        first = (N_SEG - 1 - s) * pitch
        cb.append(ub[first:first + 1, :] + ab[first:first + 1, :] * cb[-1])
    return cf[:N_SEG], cb[:N_SEG][::-1], cf[N_SEG], cb[N_SEG]


def _lru_emit(n_rows, cf, cb, af, uf, ab, ub, g_ref, o_ref):
    seg = n_rows // N_SEG
    pitch = seg + SEG_PAD
    for s in range(N_SEG):
        src = slice(s * pitch, s * pitch + seg)
        dst = slice(s * seg, (s + 1) * seg)
        y = (uf[src, :] + af[src, :] * cf[s]) + (ub[src, :] + ab[src, :] * cb[s])
        o_ref[dst, :] = (y * g_ref[dst, :]).astype(o_ref.dtype)


def _lru_kernel(ctx_out, n_riders, x_ref, g_ref, xc_ref, *refs):
    refs = list(refs)
    gc_ref = refs.pop(0) if ctx_out else None
    cw_ref, cb_ref, w4_ref, b4_ref, lam_ref = refs[:5]
    rider_in, refs = refs[5:5 + n_riders], refs[5 + n_riders:]
    o_ref = refs.pop(0)
    oc_ref = refs.pop(0) if ctx_out else None
    rider_out, refs = refs[:n_riders], refs[n_riders:]
    xp_ref, af, uf, ab, ub, pf, sf, pb, sb = refs
    _run_riders(rider_in, rider_out)

    lam = lam_ref[...]
    z = -lam
    sp = (-0.5 * LRU_C) * (jnp.maximum(z, 0.0) + jnp.log1p(jnp.exp(-jnp.abs(z))))
    w4 = w4_ref[...].astype(BF16)
    zeros_halo = jnp.zeros((HALO, LRU_BW), F32)
    h0 = jnp.zeros((1, LRU_BW), F32)

    xp_ref[0:HALO, :] = zeros_halo
    xp_ref[HALO:HALO + CTX_LEN, :] = xc_ref[...]
    xp_ref[HALO + CTX_LEN:2 * HALO + CTX_LEN, :] = zeros_halo
    _lru_coeffs(xp_ref, CTX_LEN, cw_ref, cb_ref, w4, b4_ref, sp, (af, ab), (uf, ub))
    _lru_local_scan(CTX_LEN, (af, uf), (ab, ub), (pf, sf), (pb, sb))
    cf, cb, hf, hb = _lru_carries(CTX_LEN, h0, h0, pf, sf, pb, sb)
    if ctx_out:
        _lru_emit(CTX_LEN, cf, cb, pf, sf, pb, sb, gc_ref, oc_ref)

    xp_ref[HALO:HALO + SEQ, :] = x_ref[...]
    xp_ref[HALO + SEQ:2 * HALO + SEQ, :] = zeros_halo
    _lru_coeffs(xp_ref, SEQ, cw_ref, cb_ref, w4, b4_ref, sp, (af, ab), (uf, ub))
    _lru_local_scan(SEQ, (af, uf), (ab, ub), (pf, sf), (pb, sb))
    cf, cb, _, _ = _lru_carries(SEQ, hf, hb, pf, sf, pb, sb)
    _lru_emit(SEQ, cf, cb, pf, sf, pb, sb, g_ref, o_ref)


def _lru_call(ctx_out, xl, gl, xc, gc, conv_w_l, conv_b_l, w4, b4, lam_l, riders):
    lat = pl.BlockSpec((SEQ, LRU_BW), lambda b, j: (b, j))
    cx = pl.BlockSpec((CTX_LEN, LRU_BW), lambda b, j: (b, j))
    r_in, r_out, r_shape, r_bytes = _rider_specs(riders, BATCH * LRU_BLOCKS, lambda b, j: b * LRU_BLOCKS + j)
    in_specs = [lat, lat, cx] + ([cx] if ctx_out else []) + [
        pl.BlockSpec((CONV_W, LRU_BW), lambda b, j: (0, j)),
        pl.BlockSpec((1, LRU_BW), lambda b, j: (0, j)),
        pl.BlockSpec((None, LRU_BW, 4 * LRU_BW), lambda b, j: (j, 0, 0)),
        pl.BlockSpec((None, 1, 4 * LRU_BW), lambda b, j: (j, 0, 0)),
        pl.BlockSpec((2, LRU_BW), lambda b, j: (0, j)),
    ] + r_in
    out_specs = [lat] + ([cx] if ctx_out else []) + r_out
    out_shape = [jax.ShapeDtypeStruct((BATCH * SEQ, LRU_W), BF16)]
    if ctx_out:
        out_shape.append(jax.ShapeDtypeStruct((BATCH * CTX_LEN, LRU_W), BF16))
    out_shape += r_shape
    seq_bytes = SEQ * LRU_BW * 4
    args = ([xl, gl, xc] + ([gc] if ctx_out else []) + [conv_w_l, conv_b_l, w4, b4, lam_l]
            + [r[0] for r in riders])
    res = pl.pallas_call(
        functools.partial(_lru_kernel, ctx_out, len(riders)),
        grid=(BATCH, LRU_BLOCKS),
        in_specs=in_specs,
        out_specs=out_specs,
        out_shape=out_shape,
        scratch_shapes=([pltpu.VMEM((SEQ + 2 * HALO, LRU_BW), F32)]
                        + [pltpu.VMEM((SEQ + N_SEG * SEG_PAD, LRU_BW), F32)] * 8),
        compiler_params=_params(("arbitrary", "arbitrary"), 15 * seq_bytes + (8 << 20) + r_bytes),
        name="rglru",
    )(*args)
    n_main = 2 if ctx_out else 1
    return res[0], (res[1] if ctx_out else None), res[n_main:]


def _fourier_kernel(n_pos, blk, f_ref, ch_ref, sh_ref, cc_ref, sc_ref, rev_ref, alt_ref, w_ref, b_ref, o_ref,
                    ec_ref, es_ref, mir_ref, mid_ref):
    half = n_pos // 2
    n_lo = half // blk
    step = pl.program_id(1)
    norm = 1.0 / math.sqrt(n_pos * FNET_GW)
    w = w_ref[...].astype(BF16)

    def linear(y):
        return (_dot(y.astype(BF16), w) + b_ref[...]).astype(o_ref.dtype)

    @pl.when(step == 0)
    def _():
        for i in range(n_lo):
            lo = blk * (2 * n_lo - 1 - i)
            if i == 0:
                mirrored = _dot(rev_ref[:, 0:blk], f_ref[lo:lo + blk, :])
            else:
                mirrored = _dot(rev_ref[...], f_ref[lo:lo + 2 * blk, :])
            rows = slice(i * blk, (i + 1) * blk)
            x = f_ref[rows, :].astype(F32)
            even = (x + mirrored).astype(BF16)
            odd = (x - mirrored).astype(BF16)
            for g in range(FNET_GROUPS):
                sl = slice(g * FNET_GW, (g + 1) * FNET_GW)
                ec_ref[rows, sl] = _dot(even[:, sl], cc_ref[...]).astype(BF16)
                es_ref[rows, sl] = _dot(odd[:, sl], sc_ref[...]).astype(BF16)
        for g in range(FNET_GROUPS):
            sl = slice(g * FNET_GW, (g + 1) * FNET_GW)
            mid_ref[0:BF16_ROWS, sl] = _dot(f_ref[half:half + BF16_ROWS, sl], cc_ref[...])
        mid_ref[BF16_ROWS:2 * BF16_ROWS, :] = _dot(alt_ref[...], ec_ref[...])

    mid = mid_ref[0:1, :]

    @pl.when(step < n_lo)
    def _():
        a = _dot(ch_ref[...], ec_ref[...])
        b = _dot(sh_ref[...], es_ref[...])
        row = lax.broadcasted_iota(jnp.int32, (blk, 1), 0)
        base = jnp.where(row % 2 == 0, 1.0, -1.0) * mid
        o_ref[...] = linear((a - b + base) * norm)
        r0 = pl.multiple_of(step * blk, blk)
        mir_ref[pl.ds(r0, blk), :] = ((a + b + base) * norm).astype(BF16)

    @pl.when(step == n_lo)
    def _():
        y = _dot(rev_ref[:, 0:blk], mir_ref[(n_lo - 1) * blk:n_lo * blk, :])
        nyquist = (mid_ref[BF16_ROWS:BF16_ROWS + 1, :] + mid) * norm
        row = lax.broadcasted_iota(jnp.int32, (blk, 1), 0)
        o_ref[...] = linear(jnp.where(row == 0, nyquist, y))

    if n_lo > 1:
        @pl.when(step > n_lo)
        def _():
            r0 = pl.multiple_of((2 * n_lo - 1 - step) * blk, blk)
            o_ref[...] = linear(_dot(rev_ref[...], mir_ref[pl.ds(r0, 2 * blk), :]))


DFT_SPLIT = 64


def _dft_matrices(n, size):
    t = np.arange(size, dtype=np.int64)

    def table(k):
        ang = (2.0 * np.pi / n) * ((k[:, None] * t[None, :]) % n).astype(np.float64)
        return jnp.asarray(np.cos(ang), F32), jnp.asarray(np.sin(ang), F32)

    if size <= DFT_SPLIT:
        c, s = table(t)
        return c.astype(BF16), s.astype(BF16)
    c1, s1 = table(DFT_SPLIT * np.arange(size // DFT_SPLIT, dtype=np.int64))
    c2, s2 = table(np.arange(DFT_SPLIT, dtype=np.int64))
    c = c1[:, None, :] * c2[None, :, :] - s1[:, None, :] * s2[None, :, :]
    s = s1[:, None, :] * c2[None, :, :] + c1[:, None, :] * s2[None, :, :]
    return c.reshape(size, size).astype(BF16), s.reshape(size, size).astype(BF16)


FOURIER_BLK = 256


def _fourier_call(f2d, n_pos, fno_w_l, fno_b_l):
    half = n_pos // 2
    blk = min(FOURIER_BLK, half)
    steps = n_pos // blk
    n_lo = half // blk
    ch, sh = _dft_matrices(n_pos, half)
    cc, sc = _dft_matrices(FNET_GW, FNET_GW)
    rev = np.zeros((blk, 2 * blk), np.float32)
    rev[np.arange(1, blk), blk - np.arange(1, blk)] = 1.0
    rev[0, blk] = 1.0
    alt = np.zeros((BF16_ROWS, half), np.float32)
    alt[0] = 1.0 - 2.0 * (np.arange(half) % 2)
    const = lambda b, k: (0, 0)
    dft_tile = lambda b, k: (jnp.minimum(k, n_lo - 1), 0)
    nbytes = (2 * n_pos * FNET_W * 2 + 2 * 2 * blk * half * 2 + 3 * half * FNET_W * 2 + 2 * FNET_W * FNET_W * 4
              + 8 * blk * FNET_W * 4 + 2 * blk * 2 * blk * 2)
    return pl.pallas_call(
        functools.partial(_fourier_kernel, n_pos, blk),
        grid=(BATCH, steps),
        in_specs=[
            pl.BlockSpec((n_pos, FNET_W), lambda b, k: (b, 0)),
            pl.BlockSpec((blk, half), dft_tile),
            pl.BlockSpec((blk, half), dft_tile),
            pl.BlockSpec((FNET_GW, FNET_GW), const),
            pl.BlockSpec((FNET_GW, FNET_GW), const),
            pl.BlockSpec((blk, 2 * blk), const),
            pl.BlockSpec((BF16_ROWS, half), const),
            pl.BlockSpec((FNET_W, FNET_W), const),
            pl.BlockSpec((1, FNET_W), const),
        ],
        out_specs=pl.BlockSpec((blk, FNET_W), lambda b, k: (b * steps + k, 0)),
        out_shape=jax.ShapeDtypeStruct((BATCH * n_pos, FNET_W), BF16),
        scratch_shapes=[pltpu.VMEM((half, FNET_W), BF16), pltpu.VMEM((half, FNET_W), BF16),
                        pltpu.VMEM((half, FNET_W), BF16), pltpu.VMEM((2 * BF16_ROWS, FNET_W), F32)],
        compiler_params=_params(("arbitrary", "arbitrary"), nbytes),
        name="fourier_mix",
    )(f2d, ch, sh, cc, sc, jnp.asarray(rev, BF16), jnp.asarray(alt, BF16), fno_w_l, fno_b_l)


def _outproj_kernel(carry_mod, na_ref, lru_ref, f_ref, res_ref, mod_ref, w_ref, g_ref, b_ref, *rest):
    if carry_mod:
        s_ref, wm_ref, bm_ref, o_ref, mo_ref = rest
        _mod_kernel(s_ref, wm_ref, bm_ref, mo_ref)
    else:
        (o_ref,) = rest
    gate = mod_ref[2:3, :]
    for rows in _row_tiles(res_ref.shape[0], MM_SUB):
        y = (_dot(na_ref[rows, :], w_ref[0:NA_W, :])
             + _dot(lru_ref[rows, :], w_ref[NA_W:NA_W + LRU_W, :])
             + _dot(f_ref[rows, :], w_ref[NA_W + LRU_W:D_MODEL, :]))
        _residual_ln_store(res_ref, y, gate, g_ref[...], b_ref[...], o_ref, rows)


def _outproj_call(na, lru, f, res, mod4, mod_row, w_out, layer, ln_g, ln_b, next_mod=None):
    m_rows = res.shape[0]
    tm = 512
    steps = m_rows // tm
    row = lambda m: (m, 0)
    nbytes = (2 * tm * D_MODEL * 2 + 2 * 2 * tm * D_MODEL * 4 + D_MODEL * D_MODEL * 2 + 2 * MM_SUB * D_MODEL * 4
              + 4 * ROW_CHUNK * D_MODEL * 4)
    in_specs = [
        pl.BlockSpec((tm, NA_W), row),
        pl.BlockSpec((tm, LRU_W), row),
        pl.BlockSpec((tm, FNET_W), row),
        pl.BlockSpec((tm, D_MODEL), row),
        pl.BlockSpec((None, None, 6, D_MODEL), lambda m: (0, mod_row(m * tm), 0, 0)),
        pl.BlockSpec((D_MODEL, D_MODEL), lambda m: (0, 0), pipeline_mode=pl.Buffered(1)),
        pl.BlockSpec((None, 1, D_MODEL), lambda m: (layer, 0, 0)),
        pl.BlockSpec((None, 1, D_MODEL), lambda m: (layer, 0, 0)),
    ]
    out_specs = [pl.BlockSpec((tm, D_MODEL), row)]
    out_shape = [jax.ShapeDtypeStruct((m_rows, D_MODEL), F32)]
    args = [na, lru, f, res, mod4, w_out, ln_g, ln_b]
    if next_mod is not None:
        s_in, w_mod, b_mod3, mod_layer = next_mod
        n_out = w_mod.shape[-1]
        slab = n_out // steps
        in_specs += [
            pl.BlockSpec((MOD_ROWS, D_MODEL), lambda m: (0, 0)),
            pl.BlockSpec((None, D_MODEL, slab), lambda m: (mod_layer, 0, m)),
            pl.BlockSpec((None, 1, slab), lambda m: (mod_layer, 0, m)),
        ]
        out_specs.append(pl.BlockSpec((MOD_ROWS, slab), lambda m: (0, m)))
        out_shape.append(jax.ShapeDtypeStruct((MOD_ROWS, n_out), F32))
        args += [s_in, w_mod, b_mod3]
        nbytes += 2 * D_MODEL * slab * 4 + D_MODEL * slab * 2
    res = pl.pallas_call(
        functools.partial(_outproj_kernel, next_mod is not None),
        grid=(steps,),
        in_specs=in_specs,
        out_specs=out_specs,
        out_shape=out_shape,
        compiler_params=_params(("arbitrary",), nbytes),
        name="outproj_residual",
    )(*args)
    return (res[0], res[1]) if next_mod is not None else (res[0], None)


MLP_TF = 512
MLP_TILES = D_FF // MLP_TF


def _mlp_kernel(x_ref, mod_ref, w1_ref, b1_ref, w2_ref, b2_ref, g_ref, b_ref, o_ref, v_ref, h0_ref, h1_ref):
    j = pl.program_id(1)
    tm = x_ref.shape[0]

    def up(h_out, rows=slice(None)):
        h = _dot(v_ref[rows, :], w1_ref[...]) + b1_ref[...]
        h_out[rows, :] = jnp.square(jnp.maximum(h, 0.0)).astype(BF16)

    def down(h_in, first):
        for c in range(D_MODEL // COL_TILE):
            sl = slice(c * COL_TILE, (c + 1) * COL_TILE)
            part = _dot(h_in[...], w2_ref[:, sl])
            if first:
                o_ref[:, sl] = part
            else:
                o_ref[:, sl] += part

    @pl.when(j == 0)
    def _():
        shift = mod_ref[3:4, :]
        scale1 = 1.0 + mod_ref[4:5, :]
        for rows in _row_tiles(tm, MM_SUB):
            _ln_mod_rows(x_ref, shift, scale1, v_ref, rows)
            up(h0_ref, rows)

    @pl.when(j == 1)
    def _():
        down(h0_ref, True)
        up(h1_ref)

    @pl.when((j > 1) & (j < MLP_TILES) & (j % 2 == 0))
    def _():
        down(h1_ref, False)
        up(h0_ref)

    @pl.when((j > 1) & (j < MLP_TILES) & (j % 2 == 1))
    def _():
        down(h0_ref, False)
        up(h1_ref)

    @pl.when(j == MLP_TILES)
    def _():
        h_last = h1_ref if (MLP_TILES - 1) % 2 else h0_ref
        gate = mod_ref[5:6, :]
        for rows in _row_tiles(tm, MM_SUB):
            y = o_ref[rows, :] + _dot(h_last[rows, :], w2_ref[...]) + b2_ref[...]
            _residual_ln_store(x_ref, y, gate, g_ref[...], b_ref[...], o_ref, rows)


def _mlp_call(x1, mod4, mod_row, tm, w1, b1, w2, b2, layer, ln_g, ln_b):
    m_rows = x1.shape[0]
    row = lambda m, j: (m, 0)
    vec = lambda m, j: (layer, 0, 0)
    nbytes = (4 * tm * D_MODEL * 4 + tm * D_MODEL * 2 + 2 * 2 * D_MODEL * MLP_TF * 2 + 2 * tm * MLP_TF * 2
              + tm * MLP_TF * 4 + tm * COL_TILE * 4 + 2 * MM_SUB * D_MODEL * 4 + 4 * ROW_CHUNK * D_MODEL * 4)
    return pl.pallas_call(
        _mlp_kernel,
        grid=(m_rows // tm, MLP_TILES + 1),
        in_specs=[
            pl.BlockSpec((tm, D_MODEL), row),
            pl.BlockSpec((None, None, 6, D_MODEL), lambda m, j: (0, mod_row(m * tm), 0, 0)),
            pl.BlockSpec((D_MODEL, MLP_TF), lambda m, j: (0, jnp.minimum(j, MLP_TILES - 1))),
            pl.BlockSpec((None, 1, MLP_TF), lambda m, j: (layer, 0, jnp.minimum(j, MLP_TILES - 1))),
            pl.BlockSpec((MLP_TF, D_MODEL), lambda m, j: (jnp.maximum(j - 1, 0), 0)),
            pl.BlockSpec((None, 1, D_MODEL), vec),
            pl.BlockSpec((None, 1, D_MODEL), vec),
            pl.BlockSpec((None, 1, D_MODEL), vec),
        ],
        out_specs=pl.BlockSpec((tm, D_MODEL), row),
        out_shape=jax.ShapeDtypeStruct((m_rows, D_MODEL), F32),
        scratch_shapes=[pltpu.VMEM((tm, D_MODEL), BF16), pltpu.VMEM((tm, MLP_TF), BF16),
                        pltpu.VMEM((tm, MLP_TF), BF16)],
        compiler_params=_params(("arbitrary", "arbitrary"), nbytes),
        name="mlp_residual",
    )(x1, mod4, w1, b1, w2, b2, ln_g, ln_b)


def kernel(x, c, ctx, c_ctx, w_mod, b_mod, w_in, rpb, conv_w, conv_b, lru_wa, lru_ba, lru_wx, lru_bx, lru_lambda,
           fno_w, fno_b, w_out, ln1_g, ln1_b, w_fc1, b_fc1, w_fc2, b_fc2, ln2_g, ln2_b):
    xl = x.reshape(BATCH * SEQ, D_MODEL)
    xc = ctx.reshape(BATCH * CTX_LEN, D_MODEL)
    s_in = jnp.concatenate([c, c_ctx[None], jnp.zeros((MOD_ROWS - BATCH - 1, D_MODEL), F32)], 0)
    b_mod3 = b_mod.reshape(DEPTH, 1, 6 * D_MODEL)
    as_mod4 = lambda m: m.reshape(1, MOD_ROWS, 6, D_MODEL)
    mod4 = as_mod4(_mod_call(s_in, w_mod, b_mod3, 0))
    cos_t, sin_t = _rope_tables()
    bias_tab = _attn_bias_table(rpb)
    w_in_l = w_in[0].astype(BF16)
    vec3 = lambda a: a.reshape(DEPTH, 1, a.shape[-1])
    ln1_g3, ln1_b3, ln2_g3, ln2_b3 = vec3(ln1_g), vec3(ln1_b), vec3(ln2_g), vec3(ln2_b)
    b_fc1_3, b_fc2_3 = vec3(b_fc1), vec3(b_fc2)
    lat_row = lambda r0: r0 // SEQ
    ctx_row = lambda r0: CTX_MOD_ROW

    for layer in range(DEPTH):
        ctx_out = layer < DEPTH - 1
        q, qr, k, v, xo, go, f = _inproj_lat_call(xl, mod4, w_in_l, layer, cos_t, sin_t)
        if ctx_out:
            qc, kc, vc, xoc, goc, fc = _inproj_ctx_call(xc, mod4, w_in_l, layer, True)
        else:
            kc, vc, xoc = _inproj_ctx_call(xc, mod4, w_in_l, layer, False)
            goc = None

        na, (w_out_l, w_fc1_l, w_fc2_l) = _attn_call(
            q, qr, k, v, kc, vc, bias_tab, layer,
            [_rider(w_out, layer, 0), _rider(w_fc1, layer, 1), _rider(w_fc2, layer, 0)])

        w4 = 0.5 * jnp.concatenate([lru_wa[layer, 0], lru_wx[layer, 0], lru_wa[layer, 1], lru_wx[layer, 1]], -1)
        blk = lambda a: a.reshape(LRU_BLOCKS, 1, LRU_BW)
        b4 = 0.5 * jnp.concatenate([blk(lru_ba[layer, 0]), blk(lru_bx[layer, 0]),
                                    blk(lru_ba[layer, 1]), blk(lru_bx[layer, 1])], -1)
        next_w_in = [_rider(w_in, layer + 1, 0)] if layer + 1 < DEPTH else []
        lru, lru_c, cast = _lru_call(ctx_out, xo, go, xoc, goc, conv_w[layer], conv_b[layer][None], w4, b4,
                                     lru_lambda[layer], next_w_in)
        if next_w_in:
            w_in_l = cast[0]

        fm = _fourier_call(f, SEQ, fno_w[layer], fno_b[layer][None])
        next_mod = (s_in, w_mod, b_mod3, layer + 1) if layer + 1 < DEPTH else None
        x1, mod_next = _outproj_call(na, lru, fm, xl, mod4, lat_row, w_out_l, layer, ln1_g3, ln1_b3, next_mod)
        xl = _mlp_call(x1, mod4, lat_row, 1024, w_fc1_l, b_fc1_3, w_fc2_l, b_fc2_3, layer, ln2_g3, ln2_b3)

        if ctx_out:
            na_c = _ctx_attn_call(qc, kc, vc)
            fm_c = _fourier_call(fc, CTX_LEN, fno_w[layer], fno_b[layer][None])
            c1, _ = _outproj_call(na_c, lru_c, fm_c, xc, mod4, ctx_row, w_out_l, layer, ln1_g3, ln1_b3)
            xc = _mlp_call(c1, mod4, ctx_row, 512, w_fc1_l, b_fc1_3, w_fc2_l, b_fc2_3, layer, ln2_g3, ln2_b3)
        if mod_next is not None:
            mod4 = as_mod4(mod_next)

    return xl.reshape(BATCH, SEQ, D_MODEL)
```

```python
import functools
import math

import jax
import jax.numpy as jnp
import numpy as np
from jax import lax
from jax.experimental import pallas as pl
from jax.experimental.pallas import tpu as pltpu

F32 = jnp.float32
BF16 = jnp.bfloat16

D_MODEL = 2048
BATCH = 2
SEQ = 4096
DEPTH = 2
GRID_W = 64
GRID_H = SEQ // GRID_W
CTX_LEN = 256
HEAD_DIM = 128
NA_W = D_MODEL // 2
NA_HEADS = NA_W // HEAD_DIM
WIN_H = 8
WIN_W = 16
LRU_W = D_MODEL // 4
LRU_BLOCKS = 4
LRU_BW = LRU_W // LRU_BLOCKS
CONV_W = 4
LRU_C = 8.0
FNET_W = D_MODEL // 4
FNET_GROUPS = 4
FNET_GW = FNET_W // FNET_GROUPS
IN_W = 3 * NA_W + 2 * LRU_W + FNET_W
D_FF = 4 * D_MODEL
ROPE_THETA = 10000.0
LN_EPS = 1e-5
NEG_INF = -1e30
ALPHA = (2.0 * DEPTH) ** 0.25
ATTN_SCALE = HEAD_DIM ** -0.5

V7X_LANES = 128
V7X_SUBLANES = 8
BF16_ROWS = 2 * V7X_SUBLANES
V7X_VMEM_BYTES = 64 * 1024 * 1024
VMEM_CEILING = V7X_VMEM_BYTES - 6 * 1024 * 1024

COL_TILE = 512
N_COL_TILES = IN_W // COL_TILE
ROW_CHUNK = 128
MOD_ROWS = 8
CTX_MOD_ROW = BATCH


def _vmem_limit(nbytes):
    return int(min(VMEM_CEILING, nbytes * 5 // 4 + (4 << 20)))


def _params(semantics, nbytes):
    return pltpu.CompilerParams(dimension_semantics=semantics, vmem_limit_bytes=_vmem_limit(nbytes))


def _ln(x):
    mu = jnp.mean(x, axis=-1, keepdims=True)
    xc = x - mu
    var = jnp.mean(xc * xc, axis=-1, keepdims=True)
    return xc * lax.rsqrt(var + LN_EPS)


def _sigmoid(x):
    return 1.0 / (1.0 + jnp.exp(-x))


def _gelu_tanh(x):
    return 0.5 * x * (1.0 + jnp.tanh(math.sqrt(2.0 / math.pi) * (x + 0.044715 * (x * x * x))))


def _dot(a, b):
    return jnp.dot(a, b, preferred_element_type=F32)


def _dot_nt(a, b):
    return lax.dot_general(a, b, (((1,), (1,)), ((), ())), preferred_element_type=F32)


def _rider(w, layer, axis):
    return (w, layer, axis)


def _rider_specs(riders, n_steps, step_of):
    in_specs, out_specs, out_shapes, nbytes = [], [], [], 0
    for w, layer, axis in riders:
        rows, cols = w.shape[1:]
        if axis == 0:
            blk = (rows // n_steps, cols)
            in_idx = lambda *g, layer=layer: (layer, step_of(*g), 0)
            out_idx = lambda *g: (step_of(*g), 0)
        else:
            blk = (rows, cols // n_steps)
            in_idx = lambda *g, layer=layer: (layer, 0, step_of(*g))
            out_idx = lambda *g: (0, step_of(*g))
        in_specs.append(pl.BlockSpec((None,) + blk, in_idx))
        out_specs.append(pl.BlockSpec(blk, out_idx))
        out_shapes.append(jax.ShapeDtypeStruct((rows, cols), BF16))
        nbytes += 2 * blk[0] * blk[1] * (4 + 2)
    return in_specs, out_specs, out_shapes, nbytes


def _run_riders(in_refs, out_refs):
    for src, dst in zip(in_refs, out_refs):
        dst[...] = src[...].astype(dst.dtype)


MOD_TN = 1024


def _mod_kernel(s_ref, w_ref, b_ref, o_ref):
    s = s_ref[...]
    s = s * _sigmoid(s)
    o_ref[...] = _dot(s.astype(BF16), w_ref[...].astype(BF16)) + b_ref[...]


def _mod_call(s_in, w_mod, b_mod3, layer):
    n_out = w_mod.shape[-1]
    nbytes = 2 * (D_MODEL * MOD_TN * 4) + D_MODEL * MOD_TN * 2 + 4 * MOD_ROWS * n_out
    return pl.pallas_call(
        _mod_kernel,
        grid=(n_out // MOD_TN,),
        in_specs=[
            pl.BlockSpec((MOD_ROWS, D_MODEL), lambda n: (0, 0)),
            pl.BlockSpec((None, D_MODEL, MOD_TN), lambda n: (layer, 0, n)),
            pl.BlockSpec((None, 1, MOD_TN), lambda n: (layer, 0, n)),
        ],
        out_specs=pl.BlockSpec((MOD_ROWS, MOD_TN), lambda n: (0, n)),
        out_shape=jax.ShapeDtypeStruct((MOD_ROWS, n_out), F32),
        compiler_params=_params(("arbitrary",), nbytes),
        name="modulation",
    )(s_in, w_mod, b_mod3)


MM_SUB = 256


def _row_tiles(n_rows, size):
    size = min(size, n_rows)
    return [slice(r, r + size) for r in range(0, n_rows, size)]


def _ln_mod_rows(x_ref, shift, scale1, dst_ref, rows):
    for piece in _row_tiles(rows.stop - rows.start, ROW_CHUNK):
        sl = slice(rows.start + piece.start, rows.start + piece.stop)
        dst_ref[sl, :] = (_ln(x_ref[sl, :]) * scale1 + shift).astype(dst_ref.dtype)


def _residual_ln_store(res, y, gate, gain, bias, o_ref, rows):
    for piece in _row_tiles(rows.stop - rows.start, ROW_CHUNK):
        sl = slice(rows.start + piece.start, rows.start + piece.stop)
        z = ALPHA * res[sl, :] + gate * y[piece, :]
        o_ref[sl, :] = _ln(z) * gain + bias


def _rope(a, cos, sin):
    lane = lax.broadcasted_iota(jnp.int32, a.shape, 1)
    first = (lane % (HEAD_DIM // 2)) < (HEAD_DIM // 4)
    partner = jnp.where(first, pltpu.roll(a, HEAD_DIM - HEAD_DIM // 4, 1), pltpu.roll(a, HEAD_DIM // 4, 1))
    return a * cos + partner * sin


def _inproj_lat_kernel(x_ref, mod_ref, w_ref, cos_ref, sin_ref,
                       q_ref, qr_ref, k_ref, v_ref, xo_ref, go_ref, f_ref, xn_ref):
    shift = mod_ref[0:1, :]
    scale1 = 1.0 + mod_ref[1:2, :]

    def emit_q(rows, cols, acc):
        q_ref[rows, cols] = (acc * ATTN_SCALE).astype(q_ref.dtype)
        for h in range(COL_TILE // HEAD_DIM):
            sl = slice(h * HEAD_DIM, (h + 1) * HEAD_DIM)
            dst = slice(cols.start + sl.start, cols.start + sl.stop)
            rot = _rope(acc[:, sl], cos_ref[rows, :], sin_ref[rows, :])
            qr_ref[rows, dst] = (rot * ATTN_SCALE).astype(qr_ref.dtype)

    def emit_k(rows, cols, acc):
        for h in range(COL_TILE // HEAD_DIM):
            sl = slice(h * HEAD_DIM, (h + 1) * HEAD_DIM)
            dst = slice(cols.start + sl.start, cols.start + sl.stop)
            k_ref[rows, dst] = _rope(acc[:, sl], cos_ref[rows, :], sin_ref[rows, :]).astype(k_ref.dtype)

    def emit_to(ref, fn=lambda a: a):
        def emit(rows, cols, acc):
            ref[rows, cols] = fn(acc).astype(ref.dtype)
        return emit

    half = [slice(0, COL_TILE), slice(COL_TILE, 2 * COL_TILE)]
    plan = ([(emit_q, c) for c in half] + [(emit_k, c) for c in half] + [(emit_to(v_ref), c) for c in half]
            + [(emit_to(xo_ref), half[0]), (emit_to(go_ref, _gelu_tanh), half[0]), (emit_to(f_ref), half[0])])
    for rows in _row_tiles(x_ref.shape[0], MM_SUB):
        _ln_mod_rows(x_ref, shift, scale1, xn_ref, rows)
        for n, (emit, cols) in enumerate(plan):
            emit(rows, cols, _dot(xn_ref[rows, :], w_ref[:, n * COL_TILE:(n + 1) * COL_TILE]))


def _inproj_lat_call(x2d, mod4, w_in, layer, cos_t, sin_t):
    m_rows = x2d.shape[0]
    tm = 512
    tiles_per_seq = SEQ // tm
    row = lambda m: (m, 0)
    nbytes = (2 * tm * D_MODEL * 4 + tm * D_MODEL * 2 + D_MODEL * IN_W * 2 + 4 * tm * HEAD_DIM * 4
              + 2 * tm * (4 * NA_W * 2 + 2 * LRU_W * 4 + FNET_W * 2) + 6 * MM_SUB * COL_TILE * 4
              + 4 * ROW_CHUNK * D_MODEL * 4)
    bf = lambda w: jax.ShapeDtypeStruct((m_rows, w), BF16)
    ff = lambda w: jax.ShapeDtypeStruct((m_rows, w), F32)
    widths = [NA_W, NA_W, NA_W, NA_W, LRU_W, LRU_W, FNET_W]
    return pl.pallas_call(
        _inproj_lat_kernel,
        grid=(m_rows // tm,),
        in_specs=[
            pl.BlockSpec((tm, D_MODEL), row),
            pl.BlockSpec((None, None, 6, D_MODEL), lambda m: (0, m // tiles_per_seq, 0, 0)),
            pl.BlockSpec((D_MODEL, IN_W), lambda m: (0, 0), pipeline_mode=pl.Buffered(1)),
            pl.BlockSpec((tm, HEAD_DIM), lambda m: (m % tiles_per_seq, 0)),
            pl.BlockSpec((tm, HEAD_DIM), lambda m: (m % tiles_per_seq, 0)),
        ],
        out_specs=[pl.BlockSpec((tm, w), row) for w in widths],
        out_shape=[bf(NA_W), bf(NA_W), bf(NA_W), bf(NA_W), ff(LRU_W), ff(LRU_W), bf(FNET_W)],
        scratch_shapes=[pltpu.VMEM((tm, D_MODEL), BF16)],
        compiler_params=_params(("arbitrary",), nbytes),
        name="inproj_latent",
    )(x2d, mod4, w_in, cos_t, sin_t)


def _inproj_ctx_kernel(tile_lo, with_q, with_gf, x_ref, mod_ref, w_ref, *refs):
    refs = list(refs)
    xn_ref = refs.pop()
    q_ref = refs.pop(0) if with_q else None
    k_ref, v_ref, xo_ref = refs[0], refs[1], refs[2]
    go_ref, f_ref = (refs[3], refs[4]) if with_gf else (None, None)
    n = pl.program_id(1) + tile_lo

    @pl.when(pl.program_id(1) == 0)
    def _():
        for rows in _row_tiles(x_ref.shape[0], MM_SUB):
            _ln_mod_rows(x_ref, mod_ref[0:1, :], 1.0 + mod_ref[1:2, :], xn_ref, rows)

    acc = _dot(xn_ref[...], w_ref[...])

    if with_q:
        @pl.when(n < 2)
        def _():
            q_ref[...] = (acc * ATTN_SCALE).astype(q_ref.dtype)

    @pl.when((n >= 2) & (n < 4))
    def _():
        k_ref[...] = acc.astype(k_ref.dtype)

    @pl.when((n >= 4) & (n < 6))
    def _():
        v_ref[...] = acc.astype(v_ref.dtype)

    @pl.when(n == 6)
    def _():
        xo_ref[...] = acc

    if with_gf:
        @pl.when(n == 7)
        def _():
            go_ref[...] = _gelu_tanh(acc)

        @pl.when(n == 8)
        def _():
            f_ref[...] = acc.astype(f_ref.dtype)


def _inproj_ctx_call(c2d, mod4, w_in, layer, full):
    m_rows = c2d.shape[0]
    tm = m_rows
    tile_lo, tile_hi = (0, N_COL_TILES) if full else (2, 7)

    def col(lo):
        return lambda m, n: (m, jnp.clip(n + tile_lo - lo, 0, 1))

    one = lambda m, n: (m, 0)
    bf = lambda w: jax.ShapeDtypeStruct((m_rows, w), BF16)
    ff = lambda w: jax.ShapeDtypeStruct((m_rows, w), F32)
    out_specs, out_shape = [], []
    if full:
        out_specs.append(pl.BlockSpec((tm, COL_TILE), col(0)))
        out_shape.append(bf(NA_W))
    out_specs += [pl.BlockSpec((tm, COL_TILE), col(2)), pl.BlockSpec((tm, COL_TILE), col(4)),
                  pl.BlockSpec((tm, COL_TILE), one)]
    out_shape += [bf(NA_W), bf(NA_W), ff(LRU_W)]
    if full:
        out_specs += [pl.BlockSpec((tm, COL_TILE), one), pl.BlockSpec((tm, COL_TILE), one)]
        out_shape += [ff(LRU_W), bf(FNET_W)]
    nbytes = (2 * tm * D_MODEL * 4 + tm * D_MODEL * 2 + 2 * D_MODEL * COL_TILE * 2
              + 2 * 6 * tm * COL_TILE * 4 + 3 * tm * COL_TILE * 4 + 4 * ROW_CHUNK * D_MODEL * 4)
    return pl.pallas_call(
        functools.partial(_inproj_ctx_kernel, tile_lo, full, full),
        grid=(1, tile_hi - tile_lo),
        in_specs=[
            pl.BlockSpec((tm, D_MODEL), lambda m, n: (m, 0)),
            pl.BlockSpec((None, None, 6, D_MODEL), lambda m, n: (0, CTX_MOD_ROW, 0, 0)),
            pl.BlockSpec((D_MODEL, COL_TILE), lambda m, n: (0, n + tile_lo)),
        ],
        out_specs=out_specs,
        out_shape=out_shape,
        scratch_shapes=[pltpu.VMEM((tm, D_MODEL), BF16)],
        compiler_params=_params(("arbitrary", "arbitrary"), nbytes),
        name="inproj_context",
    )(c2d, mod4, w_in)


QBLK_ROWS = 4
KBLK_ROWS = 12
N_QBLK = GRID_H // QBLK_ROWS
QBLK = QBLK_ROWS * GRID_W
KBLK = KBLK_ROWS * GRID_W
KEY_TILE = V7X_LANES
KEY_TILES = KBLK // KEY_TILE
N_DR = 2 * WIN_H - 1
BIAS_BOTH, BIAS_SECOND, BIAS_FIRST = "both", "second", "first"


def _kblk_start(first_query_row):
    return int(np.clip(first_query_row - WIN_H // 2, 0, GRID_H - KBLK_ROWS))


def _bias_entries(first_query_row):
    entries = []
    for u in range(QBLK_ROWS):
        q_row = first_query_row + u
        row_start = int(np.clip(q_row - WIN_H // 2, 0, GRID_H - WIN_H))
        row = []
        for c in range(KEY_TILES):
            k_rows = [_kblk_start(first_query_row) + 2 * c + i for i in range(2)]
            inside = [row_start <= kr < row_start + WIN_H for kr in k_rows]
            dr = [kr - q_row + (WIN_H - 1) for kr in k_rows]
            if inside[0] and inside[1]:
                row.append((BIAS_BOTH, dr[0]))
            elif inside[1]:
                row.append((BIAS_SECOND, dr[1]))
            elif inside[0]:
                row.append((BIAS_FIRST, dr[0]))
            else:
                row.append(None)
        entries.append(row)
    return entries


BIAS_TABLE = sorted({e for b in range(N_QBLK) for row in _bias_entries(b * QBLK_ROWS) for e in row if e})
BIAS_SLOT = {e: i for i, e in enumerate(BIAS_TABLE)}
N_BIAS = len(BIAS_TABLE)


def _bias_plan(first_query_row):
    return [[BIAS_SLOT[e] if e else None for e in row] for row in _bias_entries(first_query_row)]


def _attn_kernel(n_riders, q_ref, qr_ref, k_ref, v_ref, kc_ref, vc_ref, bias_ref, *rest):
    o_ref = rest[n_riders]
    vt_ref = rest[2 * n_riders + 1]
    _run_riders(rest[:n_riders], rest[n_riders + 1:2 * n_riders + 1])
    kc = kc_ref[...]
    zero_tile = jnp.zeros((GRID_W, KEY_TILE), BF16)
    for c in range(SEQ // QBLK):
        vt_ref[:, c * QBLK:(c + 1) * QBLK] = v_ref[c * QBLK:(c + 1) * QBLK, :].T
    vct = vc_ref[...].T

    def one_block(q0, k0, plan):
        s = _dot_nt(qr_ref[pl.ds(q0, QBLK), :], k_ref[pl.ds(k0, KBLK), :])
        sc = _dot_nt(q_ref[pl.ds(q0, QBLK), :], kc)
        p_rows, pc_rows, denoms = [], [], []
        for u in range(QBLK_ROWS):
            rows = slice(u * GRID_W, (u + 1) * GRID_W)
            band = {c: s[rows, c * KEY_TILE:(c + 1) * KEY_TILE] + bias_ref[idx]
                    for c, idx in enumerate(plan[u]) if idx is not None}
            ctx_tiles = [sc[rows, c * KEY_TILE:(c + 1) * KEY_TILE] for c in range(CTX_LEN // KEY_TILE)]
            tiles = list(band.values()) + ctx_tiles
            m = jnp.max(functools.reduce(jnp.maximum, tiles), axis=-1, keepdims=True)
            p_band = {c: jnp.exp(t - m) for c, t in band.items()}
            p_ctx = [jnp.exp(t - m) for t in ctx_tiles]
            total = functools.reduce(jnp.add, list(p_band.values()) + p_ctx)
            denoms.append(jnp.sum(total, axis=-1, keepdims=True))
            p_rows.append(jnp.concatenate(
                [p_band[c].astype(BF16) if c in p_band else zero_tile for c in range(KEY_TILES)], axis=1))
            pc_rows.append(jnp.concatenate([t.astype(BF16) for t in p_ctx], axis=1))
        p = jnp.concatenate(p_rows, axis=0)
        pc = jnp.concatenate(pc_rows, axis=0)
        o_t = _dot_nt(vt_ref[:, k0:k0 + KBLK], p) + _dot_nt(vct, pc)
        o_ref[pl.ds(q0, QBLK), :] = (o_t.T / jnp.concatenate(denoms, axis=0)).astype(o_ref.dtype)

    for b in range(N_QBLK):
        first_row = b * QBLK_ROWS
        one_block(b * QBLK, _kblk_start(first_row) * GRID_W, _bias_plan(first_row))


def _attn_call(q, qr, k, v, kc, vc, bias, layer, riders):
    seq_blk = lambda b, h: (b, h)
    n_steps = BATCH * NA_HEADS
    step = lambda b, h: b * NA_HEADS + h
    r_in, r_out, r_shape, r_bytes = _rider_specs(riders, n_steps, step)
    nbytes = (2 * (5 * SEQ * HEAD_DIM * 2 + 2 * CTX_LEN * HEAD_DIM * 2 + N_BIAS * GRID_W * KEY_TILE * 4)
              + 8 * QBLK * (KBLK + CTX_LEN) * 4 + r_bytes)
    res = pl.pallas_call(
        functools.partial(_attn_kernel, len(riders)),
        grid=(BATCH, NA_HEADS),
        in_specs=[
            pl.BlockSpec((SEQ, HEAD_DIM), seq_blk),
            pl.BlockSpec((SEQ, HEAD_DIM), seq_blk),
            pl.BlockSpec((SEQ, HEAD_DIM), seq_blk),
            pl.BlockSpec((SEQ, HEAD_DIM), seq_blk),
            pl.BlockSpec((CTX_LEN, HEAD_DIM), seq_blk),
            pl.BlockSpec((CTX_LEN, HEAD_DIM), seq_blk),
            pl.BlockSpec((None, None, N_BIAS, GRID_W, KEY_TILE), lambda b, h: (layer, h, 0, 0, 0)),
        ] + r_in,
        out_specs=[pl.BlockSpec((SEQ, HEAD_DIM), seq_blk)] + r_out,
        out_shape=[jax.ShapeDtypeStruct((BATCH * SEQ, NA_W), BF16)] + r_shape,
        scratch_shapes=[pltpu.VMEM((HEAD_DIM, SEQ), BF16)],
        compiler_params=_params(("arbitrary", "arbitrary"), nbytes),
        name="neighbourhood_attention",
    )(q, qr, k, v, kc, vc, bias, *[r[0] for r in riders])
    return res[0], res[1:]


def _ctx_attn_kernel(q_ref, k_ref, v_ref, o_ref):
    s = _dot_nt(q_ref[...], k_ref[...])
    m = jnp.max(s, axis=-1, keepdims=True)
    p = jnp.exp(s - m)
    denom = jnp.sum(p, axis=-1, keepdims=True)
    o_ref[...] = (_dot(p.astype(BF16), v_ref[...]) / denom).astype(o_ref.dtype)


def _ctx_attn_call(q, k, v):
    blk = pl.BlockSpec((CTX_LEN, HEAD_DIM), lambda b, h: (b, h))
    return pl.pallas_call(
        _ctx_attn_kernel,
        grid=(BATCH, NA_HEADS),
        in_specs=[blk, blk, blk],
        out_specs=blk,
        out_shape=jax.ShapeDtypeStruct((BATCH * CTX_LEN, NA_W), BF16),
        compiler_params=_params(("arbitrary", "arbitrary"), 16 << 20),
        name="context_attention",
    )(q, k, v)


def _attn_bias_table(rpb):
    col = np.arange(GRID_W)
    col_start = np.clip(col - WIN_W // 2, 0, GRID_W - WIN_W)
    in_win = (col[None, :] >= col_start[:, None]) & (col[None, :] < col_start[:, None] + WIN_W)
    dc = np.clip(col[None, :] - col[:, None] + (WIN_W - 1), 0, 2 * WIN_W - 2)
    onehot = (in_win[None] & (dc[None] == np.arange(2 * WIN_W - 1)[:, None, None])).astype(np.float32)
    t = jnp.einsum('lhdj,jqk->lhdqk', rpb, jnp.asarray(onehot), precision=lax.Precision.HIGHEST)
    t = jnp.where(in_win[None, None, None], t, NEG_INF)
    masked = jnp.full((DEPTH, NA_HEADS, GRID_W, GRID_W), NEG_INF, F32)
    tiles = []
    for kind, dr in BIAS_TABLE:
        left = masked if kind == BIAS_SECOND else t[:, :, dr]
        right = masked if kind == BIAS_FIRST else t[:, :, dr + 1 if kind == BIAS_BOTH else dr]
        tiles.append(jnp.concatenate([left, right], axis=-1))
    return jnp.stack(tiles, axis=2)


def _rope_tables():
    quarter = HEAD_DIM // 4
    inv = ROPE_THETA ** (-jnp.arange(quarter, dtype=F32) / quarter)
    t = jnp.arange(SEQ)
    ang_r = (t // GRID_W).astype(F32)[:, None] * inv
    ang_c = (t % GRID_W).astype(F32)[:, None] * inv
    cos = jnp.concatenate([jnp.cos(ang_r), jnp.cos(ang_r), jnp.cos(ang_c), jnp.cos(ang_c)], -1)
    sin = jnp.concatenate([-jnp.sin(ang_r), jnp.sin(ang_r), -jnp.sin(ang_c), jnp.sin(ang_c)], -1)
    return cos, sin


HALO = V7X_SUBLANES
N_SEG = V7X_SUBLANES
SEG_PAD = V7X_SUBLANES
LRU_SCAN_UNROLL = 8


def _lru_coeffs(xp_ref, n_rows, cw_ref, cb_ref, w4, b4_ref, sp, a_refs, u_refs):
    seg = n_rows // N_SEG
    pitch = seg + SEG_PAD
    for s in range(N_SEG):
        base = HALO + s * seg
        xc = cb_ref[...] + xp_ref[base - CONV_W // 2:base - CONV_W // 2 + seg, :] * cw_ref[0:1, :]
        for j in range(1, CONV_W):
            off = base - CONV_W // 2 + j
            xc = xc + xp_ref[off:off + seg, :] * cw_ref[j:j + 1, :]
        th = jnp.tanh(_dot(xc.astype(BF16), w4) + b4_ref[...])
        half_xc = 0.5 * xc
        for d in range(2):
            r2 = th[:, (2 * d) * LRU_BW:(2 * d + 1) * LRU_BW] + 1.0
            i2 = th[:, (2 * d + 1) * LRU_BW:(2 * d + 2) * LRU_BW] + 1.0
            log_a = r2 * sp[d:d + 1, :]
            a = jnp.exp(log_a)
            a_refs[d][s * pitch:s * pitch + seg, :] = a
            one_minus_a2 = -jnp.tanh(log_a) * (a * a + 1.0)
            root = jnp.where(one_minus_a2 == 0.0, 0.0, one_minus_a2 * lax.rsqrt(one_minus_a2))
            u_refs[d][s * pitch:s * pitch + seg, :] = root * (i2 * half_xc)


def _lru_local_scan(n_rows, coef_f, coef_b, state_f, state_b):
    seg = n_rows // N_SEG
    pitch = seg + SEG_PAD
    zero = jnp.zeros((N_SEG, LRU_BW), F32)
    one = jnp.ones((N_SEG, LRU_BW), F32)

    def step(coef, state, row, h, p):
        rows = pl.ds(row, N_SEG, stride=pitch)
        a = coef[0][rows, :]
        h = a * h + coef[1][rows, :]
        p = p * a
        state[0][rows, :] = p
        state[1][rows, :] = h
        return h, p

    def body(i, carry):
        hf, pf, hb, pb = carry
        for j in range(LRU_SCAN_UNROLL):
            t = i * LRU_SCAN_UNROLL + j
            hf, pf = step(coef_f, state_f, t, hf, pf)
            hb, pb = step(coef_b, state_b, seg - 1 - t, hb, pb)
        return hf, pf, hb, pb

    lax.fori_loop(0, seg // LRU_SCAN_UNROLL, body, (zero, one, zero, one))


def _lru_carries(n_rows, h_in_f, h_in_b, af, uf, ab, ub):
    seg = n_rows // N_SEG
    pitch = seg + SEG_PAD
    cf, cb = [h_in_f], [h_in_b]
    for s in range(N_SEG):
        last = s * pitch + seg - 1
        cf.append(uf[last:last + 1, :] + af[last:last + 1, :] * cf[-1])
        first = (N_SEG - 1 - s) * pitch
        cb.append(ub[first:first + 1, :] + ab[first:first + 1, :] * cb[-1])
    return cf[:N_SEG], cb[:N_SEG][::-1], cf[N_SEG], cb[N_SEG]


def _lru_emit(n_rows, cf, cb, af, uf, ab, ub, g_ref, o_ref):
    seg = n_rows // N_SEG
    pitch = seg + SEG_PAD
    for s in range(N_SEG):
        src = slice(s * pitch, s * pitch + seg)
        dst = slice(s * seg, (s + 1) * seg)
        y = (uf[src, :] + af[src, :] * cf[s]) + (ub[src, :] + ab[src, :] * cb[s])
        o_ref[dst, :] = (y * g_ref[dst, :]).astype(o_ref.dtype)


def _lru_kernel(ctx_out, n_riders, x_ref, g_ref, xc_ref, *refs):
    refs = list(refs)
    gc_ref = refs.pop(0) if ctx_out else None
    cw_ref, cb_ref, w4_ref, b4_ref, lam_ref = refs[:5]
    rider_in, refs = refs[5:5 + n_riders], refs[5 + n_riders:]
    o_ref = refs.pop(0)
    oc_ref = refs.pop(0) if ctx_out else None
    rider_out, refs = refs[:n_riders], refs[n_riders:]
    xp_ref, af, uf, ab, ub, pf, sf, pb, sb = refs
    _run_riders(rider_in, rider_out)

    lam = lam_ref[...]
    z = -lam
    sp = (-0.5 * LRU_C) * (jnp.maximum(z, 0.0) + jnp.log1p(jnp.exp(-jnp.abs(z))))
    w4 = w4_ref[...].astype(BF16)
    zeros_halo = jnp.zeros((HALO, LRU_BW), F32)
    h0 = jnp.zeros((1, LRU_BW), F32)

    xp_ref[0:HALO, :] = zeros_halo
    xp_ref[HALO:HALO + CTX_LEN, :] = xc_ref[...]
    xp_ref[HALO + CTX_LEN:2 * HALO + CTX_LEN, :] = zeros_halo
    _lru_coeffs(xp_ref, CTX_LEN, cw_ref, cb_ref, w4, b4_ref, sp, (af, ab), (uf, ub))
    _lru_local_scan(CTX_LEN, (af, uf), (ab, ub), (pf, sf), (pb, sb))
    cf, cb, hf, hb = _lru_carries(CTX_LEN, h0, h0, pf, sf, pb, sb)
    if ctx_out:
        _lru_emit(CTX_LEN, cf, cb, pf, sf, pb, sb, gc_ref, oc_ref)

    xp_ref[HALO:HALO + SEQ, :] = x_ref[...]
    xp_ref[HALO + SEQ:2 * HALO + SEQ, :] = zeros_halo
    _lru_coeffs(xp_ref, SEQ, cw_ref, cb_ref, w4, b4_ref, sp, (af, ab), (uf, ub))
    _lru_local_scan(SEQ, (af, uf), (ab, ub), (pf, sf), (pb, sb))
    cf, cb, _, _ = _lru_carries(SEQ, hf, hb, pf, sf, pb, sb)
    _lru_emit(SEQ, cf, cb, pf, sf, pb, sb, g_ref, o_ref)


def _lru_call(ctx_out, xl, gl, xc, gc, conv_w3, conv_b3, w4, b4, lam3, layer, riders):
    lat = pl.BlockSpec((SEQ, LRU_BW), lambda b, j: (b, j))
    cx = pl.BlockSpec((CTX_LEN, LRU_BW), lambda b, j: (b, j))
    r_in, r_out, r_shape, r_bytes = _rider_specs(riders, BATCH * LRU_BLOCKS, lambda b, j: b * LRU_BLOCKS + j)
    in_specs = [lat, lat, cx] + ([cx] if ctx_out else []) + [
        pl.BlockSpec((None, CONV_W, LRU_BW), lambda b, j: (layer, 0, j)),
        pl.BlockSpec((None, 1, LRU_BW), lambda b, j: (layer, 0, j)),
        pl.BlockSpec((None, None, LRU_BW, 4 * LRU_BW), lambda b, j: (layer, j, 0, 0)),
        pl.BlockSpec((None, None, 1, 4 * LRU_BW), lambda b, j: (layer, j, 0, 0)),
        pl.BlockSpec((None, 2, LRU_BW), lambda b, j: (layer, 0, j)),
    ] + r_in
    out_specs = [lat] + ([cx] if ctx_out else []) + r_out
    out_shape = [jax.ShapeDtypeStruct((BATCH * SEQ, LRU_W), BF16)]
    if ctx_out:
        out_shape.append(jax.ShapeDtypeStruct((BATCH * CTX_LEN, LRU_W), BF16))
    out_shape += r_shape
    seq_bytes = SEQ * LRU_BW * 4
    args = ([xl, gl, xc] + ([gc] if ctx_out else []) + [conv_w3, conv_b3, w4, b4, lam3]
            + [r[0] for r in riders])
    res = pl.pallas_call(
        functools.partial(_lru_kernel, ctx_out, len(riders)),
        grid=(BATCH, LRU_BLOCKS),
        in_specs=in_specs,
        out_specs=out_specs,
        out_shape=out_shape,
        scratch_shapes=([pltpu.VMEM((SEQ + 2 * HALO, LRU_BW), F32)]
                        + [pltpu.VMEM((SEQ + N_SEG * SEG_PAD, LRU_BW), F32)] * 8),
        compiler_params=_params(("arbitrary", "arbitrary"), 15 * seq_bytes + (8 << 20) + r_bytes),
        name="rglru",
    )(*args)
    n_main = 2 if ctx_out else 1
    return res[0], (res[1] if ctx_out else None), res[n_main:]


def _fourier_kernel(n_pos, blk, f_ref, ch_ref, sh_ref, cc_ref, sc_ref, rev_ref, alt_ref, w_ref, b_ref, o_ref,
                    ec_ref, es_ref, mir_ref, mid_ref):
    half = n_pos // 2
    n_lo = half // blk
    step = pl.program_id(1)
    norm = 1.0 / math.sqrt(n_pos * FNET_GW)
    w = w_ref[...].astype(BF16)

    def linear(y):
        return (_dot(y.astype(BF16), w) + b_ref[...]).astype(o_ref.dtype)

    @pl.when(step == 0)
    def _():
        for i in range(n_lo):
            lo = blk * (2 * n_lo - 1 - i)
            if i == 0:
                mirrored = _dot(rev_ref[:, 0:blk], f_ref[lo:lo + blk, :])
            else:
                mirrored = _dot(rev_ref[...], f_ref[lo:lo + 2 * blk, :])
            rows = slice(i * blk, (i + 1) * blk)
            x = f_ref[rows, :].astype(F32)
            even = (x + mirrored).astype(BF16)
            odd = (x - mirrored).astype(BF16)
            for g in range(FNET_GROUPS):
                sl = slice(g * FNET_GW, (g + 1) * FNET_GW)
                ec_ref[rows, sl] = _dot(even[:, sl], cc_ref[...]).astype(BF16)
                es_ref[rows, sl] = _dot(odd[:, sl], sc_ref[...]).astype(BF16)
        for g in range(FNET_GROUPS):
            sl = slice(g * FNET_GW, (g + 1) * FNET_GW)
            mid_ref[0:BF16_ROWS, sl] = _dot(f_ref[half:half + BF16_ROWS, sl], cc_ref[...])
        mid_ref[BF16_ROWS:2 * BF16_ROWS, :] = _dot(alt_ref[...], ec_ref[...])

    mid = mid_ref[0:1, :]

    @pl.when(step < n_lo)
    def _():
        a = _dot(ch_ref[...], ec_ref[...])
        b = _dot(sh_ref[...], es_ref[...])
        row = lax.broadcasted_iota(jnp.int32, (blk, 1), 0)
        base = jnp.where(row % 2 == 0, 1.0, -1.0) * mid
        o_ref[...] = linear((a - b + base) * norm)
        r0 = pl.multiple_of(step * blk, blk)
        mir_ref[pl.ds(r0, blk), :] = ((a + b + base) * norm).astype(BF16)

    @pl.when(step == n_lo)
    def _():
        y = _dot(rev_ref[:, 0:blk], mir_ref[(n_lo - 1) * blk:n_lo * blk, :])
        nyquist = (mid_ref[BF16_ROWS:BF16_ROWS + 1, :] + mid) * norm
        row = lax.broadcasted_iota(jnp.int32, (blk, 1), 0)
        o_ref[...] = linear(jnp.where(row == 0, nyquist, y))

    if n_lo > 1:
        @pl.when(step > n_lo)
        def _():
            r0 = pl.multiple_of((2 * n_lo - 1 - step) * blk, blk)
            o_ref[...] = linear(_dot(rev_ref[...], mir_ref[pl.ds(r0, 2 * blk), :]))


DFT_SPLIT = 64


def _dft_matrices(n, size):
    t = np.arange(size, dtype=np.int64)

    def table(k):
        ang = (2.0 * np.pi / n) * ((k[:, None] * t[None, :]) % n).astype(np.float64)
        return jnp.asarray(np.cos(ang), F32), jnp.asarray(np.sin(ang), F32)

    if size <= DFT_SPLIT:
        c, s = table(t)
        return c.astype(BF16), s.astype(BF16)
    c1, s1 = table(DFT_SPLIT * np.arange(size // DFT_SPLIT, dtype=np.int64))
    c2, s2 = table(np.arange(DFT_SPLIT, dtype=np.int64))
    c = c1[:, None, :] * c2[None, :, :] - s1[:, None, :] * s2[None, :, :]
    s = s1[:, None, :] * c2[None, :, :] + c1[:, None, :] * s2[None, :, :]
    return c.reshape(size, size).astype(BF16), s.reshape(size, size).astype(BF16)


FOURIER_BLK = 256


def _fourier_call(f2d, n_pos, fno_w, fno_b3, layer):
    half = n_pos // 2
    blk = min(FOURIER_BLK, half)
    steps = n_pos // blk
    n_lo = half // blk
    ch, sh = _dft_matrices(n_pos, half)
    cc, sc = _dft_matrices(FNET_GW, FNET_GW)
    rev = np.zeros((blk, 2 * blk), np.float32)
    rev[np.arange(1, blk), blk - np.arange(1, blk)] = 1.0
    rev[0, blk] = 1.0
    alt = np.zeros((BF16_ROWS, half), np.float32)
    alt[0] = 1.0 - 2.0 * (np.arange(half) % 2)
    const = lambda b, k: (0, 0)
    dft_tile = lambda b, k: (jnp.minimum(k, n_lo - 1), 0)
    nbytes = (2 * n_pos * FNET_W * 2 + 2 * 2 * blk * half * 2 + 3 * half * FNET_W * 2 + 2 * FNET_W * FNET_W * 4
              + 8 * blk * FNET_W * 4 + 2 * blk * 2 * blk * 2)
    return pl.pallas_call(
        functools.partial(_fourier_kernel, n_pos, blk),
        grid=(BATCH, steps),
        in_specs=[
            pl.BlockSpec((n_pos, FNET_W), lambda b, k: (b, 0)),
            pl.BlockSpec((blk, half), dft_tile),
            pl.BlockSpec((blk, half), dft_tile),
            pl.BlockSpec((FNET_GW, FNET_GW), const),
            pl.BlockSpec((FNET_GW, FNET_GW), const),
            pl.BlockSpec((blk, 2 * blk), const),
            pl.BlockSpec((BF16_ROWS, half), const),
            pl.BlockSpec((None, FNET_W, FNET_W), lambda b, k: (layer, 0, 0)),
            pl.BlockSpec((None, 1, FNET_W), lambda b, k: (layer, 0, 0)),
        ],
        out_specs=pl.BlockSpec((blk, FNET_W), lambda b, k: (b * steps + k, 0)),
        out_shape=jax.ShapeDtypeStruct((BATCH * n_pos, FNET_W), BF16),
        scratch_shapes=[pltpu.VMEM((half, FNET_W), BF16), pltpu.VMEM((half, FNET_W), BF16),
                        pltpu.VMEM((half, FNET_W), BF16), pltpu.VMEM((2 * BF16_ROWS, FNET_W), F32)],
        compiler_params=_params(("arbitrary", "arbitrary"), nbytes),
        name="fourier_mix",
    )(f2d, ch, sh, cc, sc, jnp.asarray(rev, BF16), jnp.asarray(alt, BF16), fno_w, fno_b3)


def _outproj_kernel(carry_mod, na_ref, lru_ref, f_ref, res_ref, mod_ref, w_ref, g_ref, b_ref, *rest):
    if carry_mod:
        s_ref, wm_ref, bm_ref, o_ref, mo_ref = rest
        _mod_kernel(s_ref, wm_ref, bm_ref, mo_ref)
    else:
        (o_ref,) = rest
    gate = mod_ref[2:3, :]
    for rows in _row_tiles(res_ref.shape[0], MM_SUB):
        y = (_dot(na_ref[rows, :], w_ref[0:NA_W, :])
             + _dot(lru_ref[rows, :], w_ref[NA_W:NA_W + LRU_W, :])
             + _dot(f_ref[rows, :], w_ref[NA_W + LRU_W:D_MODEL, :]))
        _residual_ln_store(res_ref, y, gate, g_ref[...], b_ref[...], o_ref, rows)


def _outproj_call(na, lru, f, res, mod4, mod_row, w_out, layer, ln_g, ln_b, next_mod=None):
    m_rows = res.shape[0]
    tm = 512
    steps = m_rows // tm
    row = lambda m: (m, 0)
    nbytes = (2 * tm * D_MODEL * 2 + 2 * 2 * tm * D_MODEL * 4 + D_MODEL * D_MODEL * 2 + 2 * MM_SUB * D_MODEL * 4
              + 4 * ROW_CHUNK * D_MODEL * 4)
    in_specs = [
        pl.BlockSpec((tm, NA_W), row),
        pl.BlockSpec((tm, LRU_W), row),
        pl.BlockSpec((tm, FNET_W), row),
        pl.BlockSpec((tm, D_MODEL), row),
        pl.BlockSpec((None, None, 6, D_MODEL), lambda m: (0, mod_row(m * tm), 0, 0)),
        pl.BlockSpec((D_MODEL, D_MODEL), lambda m: (0, 0), pipeline_mode=pl.Buffered(1)),
        pl.BlockSpec((None, 1, D_MODEL), lambda m: (layer, 0, 0)),
        pl.BlockSpec((None, 1, D_MODEL), lambda m: (layer, 0, 0)),
    ]
    out_specs = [pl.BlockSpec((tm, D_MODEL), row)]
    out_shape = [jax.ShapeDtypeStruct((m_rows, D_MODEL), F32)]
    args = [na, lru, f, res, mod4, w_out, ln_g, ln_b]
    if next_mod is not None:
        s_in, w_mod, b_mod3, mod_layer = next_mod
        n_out = w_mod.shape[-1]
        slab = n_out // steps
        in_specs += [
            pl.BlockSpec((MOD_ROWS, D_MODEL), lambda m: (0, 0)),
            pl.BlockSpec((None, D_MODEL, slab), lambda m: (mod_layer, 0, m)),
            pl.BlockSpec((None, 1, slab), lambda m: (mod_layer, 0, m)),
        ]
        out_specs.append(pl.BlockSpec((MOD_ROWS, slab), lambda m: (0, m)))
        out_shape.append(jax.ShapeDtypeStruct((MOD_ROWS, n_out), F32))
        args += [s_in, w_mod, b_mod3]
        nbytes += 2 * D_MODEL * slab * 4 + D_MODEL * slab * 2
    res = pl.pallas_call(
        functools.partial(_outproj_kernel, next_mod is not None),
        grid=(steps,),
        in_specs=in_specs,
        out_specs=out_specs,
        out_shape=out_shape,
        compiler_params=_params(("arbitrary",), nbytes),
        name="outproj_residual",
    )(*args)
    return (res[0], res[1]) if next_mod is not None else (res[0], None)


MLP_TF = 512
MLP_TILES = D_FF // MLP_TF


def _mlp_kernel(x_ref, mod_ref, w1_ref, b1_ref, w2_ref, b2_ref, g_ref, b_ref, o_ref, v_ref, h0_ref, h1_ref):
    j = pl.program_id(1)
    tm = x_ref.shape[0]

    def up(h_out, rows=slice(None)):
        h = _dot(v_ref[rows, :], w1_ref[...]) + b1_ref[...]
        h_out[rows, :] = jnp.square(jnp.maximum(h, 0.0)).astype(BF16)

    def down(h_in, first):
        for c in range(D_MODEL // COL_TILE):
            sl = slice(c * COL_TILE, (c + 1) * COL_TILE)
            part = _dot(h_in[...], w2_ref[:, sl])
            if first:
                o_ref[:, sl] = part
            else:
                o_ref[:, sl] += part

    @pl.when(j == 0)
    def _():
        shift = mod_ref[3:4, :]
        scale1 = 1.0 + mod_ref[4:5, :]
        for rows in _row_tiles(tm, MM_SUB):
            _ln_mod_rows(x_ref, shift, scale1, v_ref, rows)
            up(h0_ref, rows)

    @pl.when(j == 1)
    def _():
        down(h0_ref, True)
        up(h1_ref)

    @pl.when((j > 1) & (j < MLP_TILES) & (j % 2 == 0))
    def _():
        down(h1_ref, False)
        up(h0_ref)

    @pl.when((j > 1) & (j < MLP_TILES) & (j % 2 == 1))
    def _():
        down(h0_ref, False)
        up(h1_ref)

    @pl.when(j == MLP_TILES)
    def _():
        h_last = h1_ref if (MLP_TILES - 1) % 2 else h0_ref
        gate = mod_ref[5:6, :]
        for rows in _row_tiles(tm, MM_SUB):
            y = o_ref[rows, :] + _dot(h_last[rows, :], w2_ref[...]) + b2_ref[...]
            _residual_ln_store(x_ref, y, gate, g_ref[...], b_ref[...], o_ref, rows)


def _mlp_call(x1, mod4, mod_row, tm, w1, b1, w2, b2, layer, ln_g, ln_b):
    m_rows = x1.shape[0]
    row = lambda m, j: (m, 0)
    vec = lambda m, j: (layer, 0, 0)
    nbytes = (4 * tm * D_MODEL * 4 + tm * D_MODEL * 2 + 2 * 2 * D_MODEL * MLP_TF * 2 + 2 * tm * MLP_TF * 2
              + tm * MLP_TF * 4 + tm * COL_TILE * 4 + 2 * MM_SUB * D_MODEL * 4 + 4 * ROW_CHUNK * D_MODEL * 4)
    return pl.pallas_call(
        _mlp_kernel,
        grid=(m_rows // tm, MLP_TILES + 1),
        in_specs=[
            pl.BlockSpec((tm, D_MODEL), row),
            pl.BlockSpec((None, None, 6, D_MODEL), lambda m, j: (0, mod_row(m * tm), 0, 0)),
            pl.BlockSpec((D_MODEL, MLP_TF), lambda m, j: (0, jnp.minimum(j, MLP_TILES - 1))),
            pl.BlockSpec((None, 1, MLP_TF), lambda m, j: (layer, 0, jnp.minimum(j, MLP_TILES - 1))),
            pl.BlockSpec((MLP_TF, D_MODEL), lambda m, j: (jnp.maximum(j - 1, 0), 0)),
            pl.BlockSpec((None, 1, D_MODEL), vec),
            pl.BlockSpec((None, 1, D_MODEL), vec),
            pl.BlockSpec((None, 1, D_MODEL), vec),
        ],
        out_specs=pl.BlockSpec((tm, D_MODEL), row),
        out_shape=jax.ShapeDtypeStruct((m_rows, D_MODEL), F32),
        scratch_shapes=[pltpu.VMEM((tm, D_MODEL), BF16), pltpu.VMEM((tm, MLP_TF), BF16),
                        pltpu.VMEM((tm, MLP_TF), BF16)],
        compiler_params=_params(("arbitrary", "arbitrary"), nbytes),
        name="mlp_residual",
    )(x1, mod4, w1, b1, w2, b2, ln_g, ln_b)


def kernel(x, c, ctx, c_ctx, w_mod, b_mod, w_in, rpb, conv_w, conv_b, lru_wa, lru_ba, lru_wx, lru_bx, lru_lambda,
           fno_w, fno_b, w_out, ln1_g, ln1_b, w_fc1, b_fc1, w_fc2, b_fc2, ln2_g, ln2_b):
    xl = x.reshape(BATCH * SEQ, D_MODEL)
    xc = ctx.reshape(BATCH * CTX_LEN, D_MODEL)
    s_in = jnp.concatenate([c, c_ctx[None], jnp.zeros((MOD_ROWS - BATCH - 1, D_MODEL), F32)], 0)
    b_mod3 = b_mod.reshape(DEPTH, 1, 6 * D_MODEL)
    as_mod4 = lambda m: m.reshape(1, MOD_ROWS, 6, D_MODEL)
    mod4 = as_mod4(_mod_call(s_in, w_mod, b_mod3, 0))
    cos_t, sin_t = _rope_tables()
    bias_tab = _attn_bias_table(rpb)
    w_in_l = w_in[0].astype(BF16)
    vec3 = lambda a: a.reshape(DEPTH, 1, a.shape[-1])
    ln1_g3, ln1_b3, ln2_g3, ln2_b3 = vec3(ln1_g), vec3(ln1_b), vec3(ln2_g), vec3(ln2_b)
    b_fc1_3, b_fc2_3 = vec3(b_fc1), vec3(b_fc2)
    conv_b3, fno_b3 = vec3(conv_b), vec3(fno_b)
    w4 = 0.5 * jnp.concatenate([lru_wa[:, 0], lru_wx[:, 0], lru_wa[:, 1], lru_wx[:, 1]], -1)
    blk = lambda a: a.reshape(DEPTH, LRU_BLOCKS, 1, LRU_BW)
    b4 = 0.5 * jnp.concatenate([blk(lru_ba[:, 0]), blk(lru_bx[:, 0]), blk(lru_ba[:, 1]), blk(lru_bx[:, 1])], -1)
    lat_row = lambda r0: r0 // SEQ
    ctx_row = lambda r0: CTX_MOD_ROW

    for layer in range(DEPTH):
        ctx_out = layer < DEPTH - 1
        q, qr, k, v, xo, go, f = _inproj_lat_call(xl, mod4, w_in_l, layer, cos_t, sin_t)
        if ctx_out:
            qc, kc, vc, xoc, goc, fc = _inproj_ctx_call(xc, mod4, w_in_l, layer, True)
        else:
            kc, vc, xoc = _inproj_ctx_call(xc, mod4, w_in_l, layer, False)
            goc = None

        na, (w_out_l, w_fc1_l, w_fc2_l) = _attn_call(
            q, qr, k, v, kc, vc, bias_tab, layer,
            [_rider(w_out, layer, 0), _rider(w_fc1, layer, 1), _rider(w_fc2, layer, 0)])

        next_w_in = [_rider(w_in, layer + 1, 0)] if layer + 1 < DEPTH else []
        lru, lru_c, cast = _lru_call(ctx_out, xo, go, xoc, goc, conv_w, conv_b3, w4, b4, lru_lambda, layer,
                                     next_w_in)
        if next_w_in:
            w_in_l = cast[0]

        fm = _fourier_call(f, SEQ, fno_w, fno_b3, layer)
        next_mod = (s_in, w_mod, b_mod3, layer + 1) if layer + 1 < DEPTH else None
        x1, mod_next = _outproj_call(na, lru, fm, xl, mod4, lat_row, w_out_l, layer, ln1_g3, ln1_b3, next_mod)
        xl = _mlp_call(x1, mod4, lat_row, 1024, w_fc1_l, b_fc1_3, w_fc2_l, b_fc2_3, layer, ln2_g3, ln2_b3)

        if ctx_out:
            na_c = _ctx_attn_call(qc, kc, vc)
            fm_c = _fourier_call(fc, CTX_LEN, fno_w, fno_b3, layer)
            c1, _ = _outproj_call(na_c, lru_c, fm_c, xc, mod4, ctx_row, w_out_l, layer, ln1_g3, ln1_b3)
            xc = _mlp_call(c1, mod4, ctx_row, 512, w_fc1_l, b_fc1_3, w_fc2_l, b_fc2_3, layer, ln2_g3, ln2_b3)
        if mod_next is not None:
            mod4 = as_mod4(mod_next)

    return xl.reshape(BATCH, SEQ, D_MODEL)
```

```python
import functools
import math

import jax
import jax.numpy as jnp
import numpy as np
from jax import lax
from jax.experimental import pallas as pl
from jax.experimental.pallas import tpu as pltpu

F32 = jnp.float32
BF16 = jnp.bfloat16

D_MODEL = 2048
BATCH = 2
SEQ = 4096
DEPTH = 2
GRID_W = 64
GRID_H = SEQ // GRID_W
CTX_LEN = 256
HEAD_DIM = 128
NA_W = D_MODEL // 2
NA_HEADS = NA_W // HEAD_DIM
WIN_H = 8
WIN_W = 16
LRU_W = D_MODEL // 4
LRU_BLOCKS = 4
LRU_BW = LRU_W // LRU_BLOCKS
CONV_W = 4
LRU_C = 8.0
FNET_W = D_MODEL // 4
FNET_GROUPS = 4
FNET_GW = FNET_W // FNET_GROUPS
IN_W = 3 * NA_W + 2 * LRU_W + FNET_W
D_FF = 4 * D_MODEL
ROPE_THETA = 10000.0
LN_EPS = 1e-5
NEG_INF = -1e30
ALPHA = (2.0 * DEPTH) ** 0.25
ATTN_SCALE = HEAD_DIM ** -0.5

V7X_LANES = 128
V7X_SUBLANES = 8
BF16_ROWS = 2 * V7X_SUBLANES
V7X_VMEM_BYTES = 64 * 1024 * 1024
VMEM_CEILING = V7X_VMEM_BYTES - 6 * 1024 * 1024

COL_TILE = 512
N_COL_TILES = IN_W // COL_TILE
ROW_CHUNK = 128
MOD_ROWS = 8
CTX_MOD_ROW = BATCH


def _vmem_limit(nbytes):
    return int(min(VMEM_CEILING, nbytes * 5 // 4 + (4 << 20)))


def _params(semantics, nbytes):
    return pltpu.CompilerParams(dimension_semantics=semantics, vmem_limit_bytes=_vmem_limit(nbytes))


def _ln(x):
    mu = jnp.mean(x, axis=-1, keepdims=True)
    xc = x - mu
    var = jnp.mean(xc * xc, axis=-1, keepdims=True)
    return xc * lax.rsqrt(var + LN_EPS)


def _sigmoid(x):
    return 1.0 / (1.0 + jnp.exp(-x))


def _gelu_tanh(x):
    return 0.5 * x * (1.0 + jnp.tanh(math.sqrt(2.0 / math.pi) * (x + 0.044715 * (x * x * x))))


def _dot(a, b):
    return jnp.dot(a, b, preferred_element_type=F32)


def _dot_nt(a, b):
    return lax.dot_general(a, b, (((1,), (1,)), ((), ())), preferred_element_type=F32)


def _rider(w, layer, axis):
    return (w, layer, axis)


def _rider_specs(riders, n_steps, step_of):
    in_specs, out_specs, out_shapes, nbytes = [], [], [], 0
    for w, layer, axis in riders:
        rows, cols = w.shape[1:]
        if axis == 0:
            blk = (rows // n_steps, cols)
            in_idx = lambda *g, layer=layer: (layer, step_of(*g), 0)
            out_idx = lambda *g: (step_of(*g), 0)
        else:
            blk = (rows, cols // n_steps)
            in_idx = lambda *g, layer=layer: (layer, 0, step_of(*g))
            out_idx = lambda *g: (0, step_of(*g))
        in_specs.append(pl.BlockSpec((None,) + blk, in_idx))
        out_specs.append(pl.BlockSpec(blk, out_idx))
        out_shapes.append(jax.ShapeDtypeStruct((rows, cols), BF16))
        nbytes += 2 * blk[0] * blk[1] * (4 + 2)
    return in_specs, out_specs, out_shapes, nbytes


def _run_riders(in_refs, out_refs):
    for src, dst in zip(in_refs, out_refs):
        dst[...] = src[...].astype(dst.dtype)


MOD_TN = 1024


def _mod_kernel(s_ref, w_ref, b_ref, o_ref):
    s = s_ref[...]
    s = s * _sigmoid(s)
    o_ref[...] = _dot(s.astype(BF16), w_ref[...].astype(BF16)) + b_ref[...]


def _mod_call(s_in, w_mod, b_mod3, layer):
    n_out = w_mod.shape[-1]
    nbytes = 2 * (D_MODEL * MOD_TN * 4) + D_MODEL * MOD_TN * 2 + 4 * MOD_ROWS * n_out
    return pl.pallas_call(
        _mod_kernel,
        grid=(n_out // MOD_TN,),
        in_specs=[
            pl.BlockSpec((MOD_ROWS, D_MODEL), lambda n: (0, 0)),
            pl.BlockSpec((None, D_MODEL, MOD_TN), lambda n: (layer, 0, n)),
            pl.BlockSpec((None, 1, MOD_TN), lambda n: (layer, 0, n)),
        ],
        out_specs=pl.BlockSpec((MOD_ROWS, MOD_TN), lambda n: (0, n)),
        out_shape=jax.ShapeDtypeStruct((MOD_ROWS, n_out), F32),
        compiler_params=_params(("arbitrary",), nbytes),
        name="modulation",
    )(s_in, w_mod, b_mod3)


MM_SUB = 256


def _row_tiles(n_rows, size):
    size = min(size, n_rows)
    return [slice(r, r + size) for r in range(0, n_rows, size)]


def _ln_mod_rows(x_ref, shift, scale1, dst_ref, rows):
    for piece in _row_tiles(rows.stop - rows.start, ROW_CHUNK):
        sl = slice(rows.start + piece.start, rows.start + piece.stop)
        dst_ref[sl, :] = (_ln(x_ref[sl, :]) * scale1 + shift).astype(dst_ref.dtype)


def _residual_ln_store(res, y, gate, gain, bias, o_ref, rows):
    for piece in _row_tiles(rows.stop - rows.start, ROW_CHUNK):
        sl = slice(rows.start + piece.start, rows.start + piece.stop)
        z = ALPHA * res[sl, :] + gate * y[piece, :]
        o_ref[sl, :] = _ln(z) * gain + bias


def _rope(a, cos, sin):
    lane = lax.broadcasted_iota(jnp.int32, a.shape, 1)
    first = (lane % (HEAD_DIM // 2)) < (HEAD_DIM // 4)
    partner = jnp.where(first, pltpu.roll(a, HEAD_DIM - HEAD_DIM // 4, 1), pltpu.roll(a, HEAD_DIM // 4, 1))
    return a * cos + partner * sin


def _inproj_lat_kernel(x_ref, mod_ref, w_ref, cos_ref, sin_ref,
                       q_ref, qr_ref, k_ref, v_ref, xo_ref, go_ref, f_ref, xn_ref):
    shift = mod_ref[0:1, :]
    scale1 = 1.0 + mod_ref[1:2, :]

    def emit_q(rows, cols, acc):
        q_ref[rows, cols] = (acc * ATTN_SCALE).astype(q_ref.dtype)
        for h in range(COL_TILE // HEAD_DIM):
            sl = slice(h * HEAD_DIM, (h + 1) * HEAD_DIM)
            dst = slice(cols.start + sl.start, cols.start + sl.stop)
            rot = _rope(acc[:, sl], cos_ref[rows, :], sin_ref[rows, :])
            qr_ref[rows, dst] = (rot * ATTN_SCALE).astype(qr_ref.dtype)

    def emit_k(rows, cols, acc):
        for h in range(COL_TILE // HEAD_DIM):
            sl = slice(h * HEAD_DIM, (h + 1) * HEAD_DIM)
            dst = slice(cols.start + sl.start, cols.start + sl.stop)
            k_ref[rows, dst] = _rope(acc[:, sl], cos_ref[rows, :], sin_ref[rows, :]).astype(k_ref.dtype)

    def emit_to(ref, fn=lambda a: a):
        def emit(rows, cols, acc):
            ref[rows, cols] = fn(acc).astype(ref.dtype)
        return emit

    half = [slice(0, COL_TILE), slice(COL_TILE, 2 * COL_TILE)]
    plan = ([(emit_q, c) for c in half] + [(emit_k, c) for c in half] + [(emit_to(v_ref), c) for c in half]
            + [(emit_to(xo_ref), half[0]), (emit_to(go_ref, _gelu_tanh), half[0]), (emit_to(f_ref), half[0])])
    for rows in _row_tiles(x_ref.shape[0], MM_SUB):
        _ln_mod_rows(x_ref, shift, scale1, xn_ref, rows)
        for n, (emit, cols) in enumerate(plan):
            emit(rows, cols, _dot(xn_ref[rows, :], w_ref[:, n * COL_TILE:(n + 1) * COL_TILE]))


def _inproj_lat_call(x2d, mod4, w_in, layer, cos_t, sin_t):
    m_rows = x2d.shape[0]
    tm = 512
    tiles_per_seq = SEQ // tm
    row = lambda m: (m, 0)
    nbytes = (2 * tm * D_MODEL * 4 + tm * D_MODEL * 2 + D_MODEL * IN_W * 2 + 4 * tm * HEAD_DIM * 4
              + 2 * tm * (4 * NA_W * 2 + 2 * LRU_W * 4 + FNET_W * 2) + 6 * MM_SUB * COL_TILE * 4
              + 4 * ROW_CHUNK * D_MODEL * 4)
    bf = lambda w: jax.ShapeDtypeStruct((m_rows, w), BF16)
    ff = lambda w: jax.ShapeDtypeStruct((m_rows, w), F32)
    widths = [NA_W, NA_W, NA_W, NA_W, LRU_W, LRU_W, FNET_W]
    return pl.pallas_call(
        _inproj_lat_kernel,
        grid=(m_rows // tm,),
        in_specs=[
            pl.BlockSpec((tm, D_MODEL), row),
            pl.BlockSpec((None, None, 6, D_MODEL), lambda m: (0, m // tiles_per_seq, 0, 0)),
            pl.BlockSpec((D_MODEL, IN_W), lambda m: (0, 0), pipeline_mode=pl.Buffered(1)),
            pl.BlockSpec((tm, HEAD_DIM), lambda m: (m % tiles_per_seq, 0)),
            pl.BlockSpec((tm, HEAD_DIM), lambda m: (m % tiles_per_seq, 0)),
        ],
        out_specs=[pl.BlockSpec((tm, w), row) for w in widths],
        out_shape=[bf(NA_W), bf(NA_W), bf(NA_W), bf(NA_W), ff(LRU_W), ff(LRU_W), bf(FNET_W)],
        scratch_shapes=[pltpu.VMEM((tm, D_MODEL), BF16)],
        compiler_params=_params(("arbitrary",), nbytes),
        name="inproj_latent",
    )(x2d, mod4, w_in, cos_t, sin_t)


def _inproj_ctx_kernel(tile_lo, with_q, with_gf, emit_w, x_ref, mod_ref, w_ref, *refs):
    refs = list(refs)
    xn_ref = refs.pop()
    wb_ref = refs.pop() if emit_w else None
    q_ref = refs.pop(0) if with_q else None
    k_ref, v_ref, xo_ref = refs[0], refs[1], refs[2]
    go_ref, f_ref = (refs[3], refs[4]) if with_gf else (None, None)
    n = pl.program_id(1) + tile_lo

    @pl.when(pl.program_id(1) == 0)
    def _():
        for rows in _row_tiles(x_ref.shape[0], MM_SUB):
            _ln_mod_rows(x_ref, mod_ref[0:1, :], 1.0 + mod_ref[1:2, :], xn_ref, rows)

    w = w_ref[...].astype(BF16)
    if emit_w:
        wb_ref[...] = w
    acc = _dot(xn_ref[...], w)

    if with_q:
        @pl.when(n < 2)
        def _():
            q_ref[...] = (acc * ATTN_SCALE).astype(q_ref.dtype)

    @pl.when((n >= 2) & (n < 4))
    def _():
        k_ref[...] = acc.astype(k_ref.dtype)

    @pl.when((n >= 4) & (n < 6))
    def _():
        v_ref[...] = acc.astype(v_ref.dtype)

    @pl.when(n == 6)
    def _():
        xo_ref[...] = acc

    if with_gf:
        @pl.when(n == 7)
        def _():
            go_ref[...] = _gelu_tanh(acc)

        @pl.when(n == 8)
        def _():
            f_ref[...] = acc.astype(f_ref.dtype)


def _inproj_ctx_call(c2d, mod4, w_in, layer, full, cast_layer=None):
    m_rows = c2d.shape[0]
    tm = m_rows
    tile_lo, tile_hi = (0, N_COL_TILES) if full else (2, 7)
    emit_w = cast_layer is not None
    assert full or not emit_w

    def col(lo):
        return lambda m, n: (m, jnp.clip(n + tile_lo - lo, 0, 1))

    one = lambda m, n: (m, 0)
    bf = lambda w: jax.ShapeDtypeStruct((m_rows, w), BF16)
    ff = lambda w: jax.ShapeDtypeStruct((m_rows, w), F32)
    out_specs, out_shape = [], []
    if full:
        out_specs.append(pl.BlockSpec((tm, COL_TILE), col(0)))
        out_shape.append(bf(NA_W))
    out_specs += [pl.BlockSpec((tm, COL_TILE), col(2)), pl.BlockSpec((tm, COL_TILE), col(4)),
                  pl.BlockSpec((tm, COL_TILE), one)]
    out_shape += [bf(NA_W), bf(NA_W), ff(LRU_W)]
    if full:
        out_specs += [pl.BlockSpec((tm, COL_TILE), one), pl.BlockSpec((tm, COL_TILE), one)]
        out_shape += [ff(LRU_W), bf(FNET_W)]
    nbytes = (2 * tm * D_MODEL * 4 + tm * D_MODEL * 2 + 2 * D_MODEL * COL_TILE * 2
              + 2 * 6 * tm * COL_TILE * 4 + 3 * tm * COL_TILE * 4 + 4 * ROW_CHUNK * D_MODEL * 4)
    if emit_w:
        w_spec = pl.BlockSpec((None, D_MODEL, COL_TILE), lambda m, n: (cast_layer, 0, n + tile_lo))
        out_specs.append(pl.BlockSpec((D_MODEL, COL_TILE), lambda m, n: (0, n + tile_lo)))
        out_shape.append(jax.ShapeDtypeStruct((D_MODEL, IN_W), BF16))
        nbytes += 2 * D_MODEL * COL_TILE * (4 + 2)
    else:
        w_spec = pl.BlockSpec((D_MODEL, COL_TILE), lambda m, n: (0, n + tile_lo))
    return pl.pallas_call(
        functools.partial(_inproj_ctx_kernel, tile_lo, full, full, emit_w),
        grid=(1, tile_hi - tile_lo),
        in_specs=[
            pl.BlockSpec((tm, D_MODEL), lambda m, n: (m, 0)),
            pl.BlockSpec((None, None, 6, D_MODEL), lambda m, n: (0, CTX_MOD_ROW, 0, 0)),
            w_spec,
        ],
        out_specs=out_specs,
        out_shape=out_shape,
        scratch_shapes=[pltpu.VMEM((tm, D_MODEL), BF16)],
        compiler_params=_params(("arbitrary", "arbitrary"), nbytes),
        name="inproj_context",
    )(c2d, mod4, w_in)


QBLK_ROWS = 4
KBLK_ROWS = 12
N_QBLK = GRID_H // QBLK_ROWS
QBLK = QBLK_ROWS * GRID_W
KBLK = KBLK_ROWS * GRID_W
KEY_TILE = V7X_LANES
KEY_TILES = KBLK // KEY_TILE
N_DR = 2 * WIN_H - 1
BIAS_BOTH, BIAS_SECOND, BIAS_FIRST = "both", "second", "first"


def _kblk_start(first_query_row):
    return int(np.clip(first_query_row - WIN_H // 2, 0, GRID_H - KBLK_ROWS))


def _bias_entries(first_query_row):
    entries = []
    for u in range(QBLK_ROWS):
        q_row = first_query_row + u
        row_start = int(np.clip(q_row - WIN_H // 2, 0, GRID_H - WIN_H))
        row = []
        for c in range(KEY_TILES):
            k_rows = [_kblk_start(first_query_row) + 2 * c + i for i in range(2)]
            inside = [row_start <= kr < row_start + WIN_H for kr in k_rows]
            dr = [kr - q_row + (WIN_H - 1) for kr in k_rows]
            if inside[0] and inside[1]:
                row.append((BIAS_BOTH, dr[0]))
            elif inside[1]:
                row.append((BIAS_SECOND, dr[1]))
            elif inside[0]:
                row.append((BIAS_FIRST, dr[0]))
            else:
                row.append(None)
        entries.append(row)
    return entries


BIAS_TABLE = sorted({e for b in range(N_QBLK) for row in _bias_entries(b * QBLK_ROWS) for e in row if e})
BIAS_SLOT = {e: i for i, e in enumerate(BIAS_TABLE)}
N_BIAS = len(BIAS_TABLE)


def _bias_plan(first_query_row):
    return [[BIAS_SLOT[e] if e else None for e in row] for row in _bias_entries(first_query_row)]


def _attn_kernel(n_riders, q_ref, qr_ref, k_ref, v_ref, kc_ref, vc_ref, bias_ref, *rest):
    o_ref = rest[n_riders]
    vt_ref = rest[2 * n_riders + 1]
    _run_riders(rest[:n_riders], rest[n_riders + 1:2 * n_riders + 1])
    kc = kc_ref[...]
    zero_tile = jnp.zeros((GRID_W, KEY_TILE), BF16)
    for c in range(SEQ // QBLK):
        vt_ref[:, c * QBLK:(c + 1) * QBLK] = v_ref[c * QBLK:(c + 1) * QBLK, :].T
    vct = vc_ref[...].T

    def one_block(q0, k0, plan):
        s = _dot_nt(qr_ref[pl.ds(q0, QBLK), :], k_ref[pl.ds(k0, KBLK), :])
        sc = _dot_nt(q_ref[pl.ds(q0, QBLK), :], kc)
        p_rows, pc_rows, denoms = [], [], []
        for u in range(QBLK_ROWS):
            rows = slice(u * GRID_W, (u + 1) * GRID_W)
            band = {c: s[rows, c * KEY_TILE:(c + 1) * KEY_TILE] + bias_ref[idx]
                    for c, idx in enumerate(plan[u]) if idx is not None}
            ctx_tiles = [sc[rows, c * KEY_TILE:(c + 1) * KEY_TILE] for c in range(CTX_LEN // KEY_TILE)]
            tiles = list(band.values()) + ctx_tiles
            m = jnp.max(functools.reduce(jnp.maximum, tiles), axis=-1, keepdims=True)
            p_band = {c: jnp.exp(t - m) for c, t in band.items()}
            p_ctx = [jnp.exp(t - m) for t in ctx_tiles]
            total = functools.reduce(jnp.add, list(p_band.values()) + p_ctx)
            denoms.append(jnp.sum(total, axis=-1, keepdims=True))
            p_rows.append(jnp.concatenate(
                [p_band[c].astype(BF16) if c in p_band else zero_tile for c in range(KEY_TILES)], axis=1))
            pc_rows.append(jnp.concatenate([t.astype(BF16) for t in p_ctx], axis=1))
        p = jnp.concatenate(p_rows, axis=0)
        pc = jnp.concatenate(pc_rows, axis=0)
        o_t = _dot_nt(vt_ref[:, k0:k0 + KBLK], p) + _dot_nt(vct, pc)
        o_ref[pl.ds(q0, QBLK), :] = (o_t.T / jnp.concatenate(denoms, axis=0)).astype(o_ref.dtype)

    for b in range(N_QBLK):
        first_row = b * QBLK_ROWS
        one_block(b * QBLK, _kblk_start(first_row) * GRID_W, _bias_plan(first_row))


def _attn_call(q, qr, k, v, kc, vc, bias, layer, riders):
    seq_blk = lambda b, h: (b, h)
    n_steps = BATCH * NA_HEADS
    step = lambda b, h: b * NA_HEADS + h
    r_in, r_out, r_shape, r_bytes = _rider_specs(riders, n_steps, step)
    nbytes = (2 * (5 * SEQ * HEAD_DIM * 2 + 2 * CTX_LEN * HEAD_DIM * 2 + N_BIAS * GRID_W * KEY_TILE * 4)
              + 8 * QBLK * (KBLK + CTX_LEN) * 4 + r_bytes)
    res = pl.pallas_call(
        functools.partial(_attn_kernel, len(riders)),
        grid=(BATCH, NA_HEADS),
        in_specs=[
            pl.BlockSpec((SEQ, HEAD_DIM), seq_blk),
            pl.BlockSpec((SEQ, HEAD_DIM), seq_blk),
            pl.BlockSpec((SEQ, HEAD_DIM), seq_blk),
            pl.BlockSpec((SEQ, HEAD_DIM), seq_blk),
            pl.BlockSpec((CTX_LEN, HEAD_DIM), seq_blk),
            pl.BlockSpec((CTX_LEN, HEAD_DIM), seq_blk),
            pl.BlockSpec((None, None, N_BIAS, GRID_W, KEY_TILE), lambda b, h: (layer, h, 0, 0, 0)),
        ] + r_in,
        out_specs=[pl.BlockSpec((SEQ, HEAD_DIM), seq_blk)] + r_out,
        out_shape=[jax.ShapeDtypeStruct((BATCH * SEQ, NA_W), BF16)] + r_shape,
        scratch_shapes=[pltpu.VMEM((HEAD_DIM, SEQ), BF16)],
        compiler_params=_params(("arbitrary", "arbitrary"), nbytes),
        name="neighbourhood_attention",
    )(q, qr, k, v, kc, vc, bias, *[r[0] for r in riders])
    return res[0], res[1:]


def _ctx_attn_kernel(q_ref, k_ref, v_ref, o_ref):
    s = _dot_nt(q_ref[...], k_ref[...])
    m = jnp.max(s, axis=-1, keepdims=True)
    p = jnp.exp(s - m)
    denom = jnp.sum(p, axis=-1, keepdims=True)
    o_ref[...] = (_dot(p.astype(BF16), v_ref[...]) / denom).astype(o_ref.dtype)


def _ctx_attn_call(q, k, v):
    blk = pl.BlockSpec((CTX_LEN, HEAD_DIM), lambda b, h: (b, h))
    return pl.pallas_call(
        _ctx_attn_kernel,
        grid=(BATCH, NA_HEADS),
        in_specs=[blk, blk, blk],
        out_specs=blk,
        out_shape=jax.ShapeDtypeStruct((BATCH * CTX_LEN, NA_W), BF16),
        compiler_params=_params(("arbitrary", "arbitrary"), 16 << 20),
        name="context_attention",
    )(q, k, v)


def _attn_bias_table(rpb):
    col = np.arange(GRID_W)
    col_start = np.clip(col - WIN_W // 2, 0, GRID_W - WIN_W)
    in_win = (col[None, :] >= col_start[:, None]) & (col[None, :] < col_start[:, None] + WIN_W)
    dc = np.clip(col[None, :] - col[:, None] + (WIN_W - 1), 0, 2 * WIN_W - 2)
    onehot = (in_win[None] & (dc[None] == np.arange(2 * WIN_W - 1)[:, None, None])).astype(np.float32)
    t = jnp.einsum('lhdj,jqk->lhdqk', rpb, jnp.asarray(onehot), precision=lax.Precision.HIGHEST)
    t = jnp.where(in_win[None, None, None], t, NEG_INF)
    masked = jnp.full((DEPTH, NA_HEADS, GRID_W, GRID_W), NEG_INF, F32)
    tiles = []
    for kind, dr in BIAS_TABLE:
        left = masked if kind == BIAS_SECOND else t[:, :, dr]
        right = masked if kind == BIAS_FIRST else t[:, :, dr + 1 if kind == BIAS_BOTH else dr]
        tiles.append(jnp.concatenate([left, right], axis=-1))
    return jnp.stack(tiles, axis=2)


def _rope_tables():
    assert GRID_H == GRID_W
    quarter = HEAD_DIM // 4
    inv = np.float32(ROPE_THETA) ** (-np.arange(quarter, dtype=np.float32) / np.float32(quarter))
    ang = (np.arange(GRID_W, dtype=np.float32)[:, None] * inv).astype(np.float64)
    cos_g, sin_g = jnp.asarray(np.cos(ang), F32), jnp.asarray(np.sin(ang), F32)
    by_row = lambda g: jnp.broadcast_to(g[:, None, :], (GRID_H, GRID_W, quarter)).reshape(SEQ, quarter)
    by_col = lambda g: jnp.broadcast_to(g[None, :, :], (GRID_H, GRID_W, quarter)).reshape(SEQ, quarter)
    cos = jnp.concatenate([by_row(cos_g), by_row(cos_g), by_col(cos_g), by_col(cos_g)], -1)
    sin = jnp.concatenate([-by_row(sin_g), by_row(sin_g), -by_col(sin_g), by_col(sin_g)], -1)
    return cos, sin


HALO = V7X_SUBLANES
N_SEG = V7X_SUBLANES
SEG_PAD = V7X_SUBLANES
LRU_SCAN_UNROLL = 8


def _lru_coeffs(xp_ref, n_rows, cw_ref, cb_ref, w4, b4_ref, sp, a_refs, u_refs):
    seg = n_rows // N_SEG
    pitch = seg + SEG_PAD
    for s in range(N_SEG):
        base = HALO + s * seg
        xc = cb_ref[...] + xp_ref[base - CONV_W // 2:base - CONV_W // 2 + seg, :] * cw_ref[0:1, :]
        for j in range(1, CONV_W):
            off = base - CONV_W // 2 + j
            xc = xc + xp_ref[off:off + seg, :] * cw_ref[j:j + 1, :]
        th = jnp.tanh(_dot(xc.astype(BF16), w4) + b4_ref[...])
        half_xc = 0.5 * xc
        for d in range(2):
            r2 = th[:, (2 * d) * LRU_BW:(2 * d + 1) * LRU_BW] + 1.0
            i2 = th[:, (2 * d + 1) * LRU_BW:(2 * d + 2) * LRU_BW] + 1.0
            log_a = r2 * sp[d:d + 1, :]
            a = jnp.exp(log_a)
            a_refs[d][s * pitch:s * pitch + seg, :] = a
            one_minus_a2 = -jnp.tanh(log_a) * (a * a + 1.0)
            root = jnp.where(one_minus_a2 == 0.0, 0.0, one_minus_a2 * lax.rsqrt(one_minus_a2))
            u_refs[d][s * pitch:s * pitch + seg, :] = root * (i2 * half_xc)


def _lru_local_scan(n_rows, coef_f, coef_b, state_f, state_b):
    seg = n_rows // N_SEG
    pitch = seg + SEG_PAD
    zero = jnp.zeros((N_SEG, LRU_BW), F32)
    one = jnp.ones((N_SEG, LRU_BW), F32)

    def step(coef, state, row, h, p):
        rows = pl.ds(row, N_SEG, stride=pitch)
        a = coef[0][rows, :]
        h = a * h + coef[1][rows, :]
        p = p * a
        state[0][rows, :] = p
        state[1][rows, :] = h
        return h, p

    def body(i, carry):
        hf, pf, hb, pb = carry
        for j in range(LRU_SCAN_UNROLL):
            t = i * LRU_SCAN_UNROLL + j
            hf, pf = step(coef_f, state_f, t, hf, pf)
            hb, pb = step(coef_b, state_b, seg - 1 - t, hb, pb)
        return hf, pf, hb, pb

    lax.fori_loop(0, seg // LRU_SCAN_UNROLL, body, (zero, one, zero, one))


def _lru_carries(n_rows, h_in_f, h_in_b, af, uf, ab, ub):
    seg = n_rows // N_SEG
    pitch = seg + SEG_PAD
    cf, cb = [h_in_f], [h_in_b]
    for s in range(N_SEG):
        last = s * pitch + seg - 1
        cf.append(uf[last:last + 1, :] + af[last:last + 1, :] * cf[-1])
        first = (N_SEG - 1 - s) * pitch
        cb.append(ub[first:first + 1, :] + ab[first:first + 1, :] * cb[-1])
    return cf[:N_SEG], cb[:N_SEG][::-1], cf[N_SEG], cb[N_SEG]


def _lru_emit(n_rows, cf, cb, af, uf, ab, ub, g_ref, o_ref):
    seg = n_rows // N_SEG
    pitch = seg + SEG_PAD
    for s in range(N_SEG):
        src = slice(s * pitch, s * pitch + seg)
        dst = slice(s * seg, (s + 1) * seg)
        y = (uf[src, :] + af[src, :] * cf[s]) + (ub[src, :] + ab[src, :] * cb[s])
        o_ref[dst, :] = (y * g_ref[dst, :]).astype(o_ref.dtype)


def _lru_kernel(ctx_out, n_riders, x_ref, g_ref, xc_ref, *refs):
    refs = list(refs)
    gc_ref = refs.pop(0) if ctx_out else None
    cw_ref, cb_ref, w4_ref, b4_ref, lam_ref = refs[:5]
    rider_in, refs = refs[5:5 + n_riders], refs[5 + n_riders:]
    o_ref = refs.pop(0)
    oc_ref = refs.pop(0) if ctx_out else None
    rider_out, refs = refs[:n_riders], refs[n_riders:]
    xp_ref, af, uf, ab, ub, pf, sf, pb, sb = refs
    _run_riders(rider_in, rider_out)

    lam = lam_ref[...]
    z = -lam
    sp = (-0.5 * LRU_C) * (jnp.maximum(z, 0.0) + jnp.log1p(jnp.exp(-jnp.abs(z))))
    w4 = w4_ref[...].astype(BF16)
    zeros_halo = jnp.zeros((HALO, LRU_BW), F32)
    h0 = jnp.zeros((1, LRU_BW), F32)

    xp_ref[0:HALO, :] = zeros_halo
    xp_ref[HALO:HALO + CTX_LEN, :] = xc_ref[...]
    xp_ref[HALO + CTX_LEN:2 * HALO + CTX_LEN, :] = zeros_halo
    _lru_coeffs(xp_ref, CTX_LEN, cw_ref, cb_ref, w4, b4_ref, sp, (af, ab), (uf, ub))
    _lru_local_scan(CTX_LEN, (af, uf), (ab, ub), (pf, sf), (pb, sb))
    cf, cb, hf, hb = _lru_carries(CTX_LEN, h0, h0, pf, sf, pb, sb)
    if ctx_out:
        _lru_emit(CTX_LEN, cf, cb, pf, sf, pb, sb, gc_ref, oc_ref)

    xp_ref[HALO:HALO + SEQ, :] = x_ref[...]
    xp_ref[HALO + SEQ:2 * HALO + SEQ, :] = zeros_halo
    _lru_coeffs(xp_ref, SEQ, cw_ref, cb_ref, w4, b4_ref, sp, (af, ab), (uf, ub))
    _lru_local_scan(SEQ, (af, uf), (ab, ub), (pf, sf), (pb, sb))
    cf, cb, _, _ = _lru_carries(SEQ, hf, hb, pf, sf, pb, sb)
    _lru_emit(SEQ, cf, cb, pf, sf, pb, sb, g_ref, o_ref)


def _lru_call(ctx_out, xl, gl, xc, gc, conv_w3, conv_b3, w4, b4, lam3, layer, riders):
    lat = pl.BlockSpec((SEQ, LRU_BW), lambda b, j: (b, j))
    cx = pl.BlockSpec((CTX_LEN, LRU_BW), lambda b, j: (b, j))
    r_in, r_out, r_shape, r_bytes = _rider_specs(riders, BATCH * LRU_BLOCKS, lambda b, j: b * LRU_BLOCKS + j)
    in_specs = [lat, lat, cx] + ([cx] if ctx_out else []) + [
        pl.BlockSpec((None, CONV_W, LRU_BW), lambda b, j: (layer, 0, j)),
        pl.BlockSpec((None, 1, LRU_BW), lambda b, j: (layer, 0, j)),
        pl.BlockSpec((None, None, LRU_BW, 4 * LRU_BW), lambda b, j: (layer, j, 0, 0)),
        pl.BlockSpec((None, None, 1, 4 * LRU_BW), lambda b, j: (layer, j, 0, 0)),
        pl.BlockSpec((None, 2, LRU_BW), lambda b, j: (layer, 0, j)),
    ] + r_in
    out_specs = [lat] + ([cx] if ctx_out else []) + r_out
    out_shape = [jax.ShapeDtypeStruct((BATCH * SEQ, LRU_W), BF16)]
    if ctx_out:
        out_shape.append(jax.ShapeDtypeStruct((BATCH * CTX_LEN, LRU_W), BF16))
    out_shape += r_shape
    seq_bytes = SEQ * LRU_BW * 4
    args = ([xl, gl, xc] + ([gc] if ctx_out else []) + [conv_w3, conv_b3, w4, b4, lam3]
            + [r[0] for r in riders])
    res = pl.pallas_call(
        functools.partial(_lru_kernel, ctx_out, len(riders)),
        grid=(BATCH, LRU_BLOCKS),
        in_specs=in_specs,
        out_specs=out_specs,
        out_shape=out_shape,
        scratch_shapes=([pltpu.VMEM((SEQ + 2 * HALO, LRU_BW), F32)]
                        + [pltpu.VMEM((SEQ + N_SEG * SEG_PAD, LRU_BW), F32)] * 8),
        compiler_params=_params(("arbitrary", "arbitrary"), 15 * seq_bytes + (8 << 20) + r_bytes),
        name="rglru",
    )(*args)
    n_main = 2 if ctx_out else 1
    return res[0], (res[1] if ctx_out else None), res[n_main:]


def _fourier_kernel(n_pos, blk, f_ref, ch_ref, sh_ref, cc_ref, sc_ref, rev_ref, alt_ref, w_ref, b_ref, o_ref,
                    ec_ref, es_ref, mir_ref, mid_ref):
    half = n_pos // 2
    n_lo = half // blk
    step = pl.program_id(1)
    norm = 1.0 / math.sqrt(n_pos * FNET_GW)
    w = w_ref[...].astype(BF16)

    def linear(y):
        return (_dot(y.astype(BF16), w) + b_ref[...]).astype(o_ref.dtype)

    @pl.when(step == 0)
    def _():
        for i in range(n_lo):
            lo = blk * (2 * n_lo - 1 - i)
            if i == 0:
                mirrored = _dot(rev_ref[:, 0:blk], f_ref[lo:lo + blk, :])
            else:
                mirrored = _dot(rev_ref[...], f_ref[lo:lo + 2 * blk, :])
            rows = slice(i * blk, (i + 1) * blk)
            x = f_ref[rows, :].astype(F32)
            even = (x + mirrored).astype(BF16)
            odd = (x - mirrored).astype(BF16)
            for g in range(FNET_GROUPS):
                sl = slice(g * FNET_GW, (g + 1) * FNET_GW)
                ec_ref[rows, sl] = _dot(even[:, sl], cc_ref[...]).astype(BF16)
                es_ref[rows, sl] = _dot(odd[:, sl], sc_ref[...]).astype(BF16)
        for g in range(FNET_GROUPS):
            sl = slice(g * FNET_GW, (g + 1) * FNET_GW)
            mid_ref[0:BF16_ROWS, sl] = _dot(f_ref[half:half + BF16_ROWS, sl], cc_ref[...])
        mid_ref[BF16_ROWS:2 * BF16_ROWS, :] = _dot(alt_ref[...], ec_ref[...])

    mid = mid_ref[0:1, :]

    @pl.when(step < n_lo)
    def _():
        a = _dot(ch_ref[...], ec_ref[...])
        b = _dot(sh_ref[...], es_ref[...])
        row = lax.broadcasted_iota(jnp.int32, (blk, 1), 0)
        base = jnp.where(row % 2 == 0, 1.0, -1.0) * mid
        o_ref[...] = linear((a - b + base) * norm)
        r0 = pl.multiple_of(step * blk, blk)
        mir_ref[pl.ds(r0, blk), :] = ((a + b + base) * norm).astype(BF16)

    @pl.when(step == n_lo)
    def _():
        y = _dot(rev_ref[:, 0:blk], mir_ref[(n_lo - 1) * blk:n_lo * blk, :])
        nyquist = (mid_ref[BF16_ROWS:BF16_ROWS + 1, :] + mid) * norm
        row = lax.broadcasted_iota(jnp.int32, (blk, 1), 0)
        o_ref[...] = linear(jnp.where(row == 0, nyquist, y))

    if n_lo > 1:
        @pl.when(step > n_lo)
        def _():
            r0 = pl.multiple_of((2 * n_lo - 1 - step) * blk, blk)
            o_ref[...] = linear(_dot(rev_ref[...], mir_ref[pl.ds(r0, 2 * blk), :]))


DFT_SPLIT = 64


def _dft_matrices(n, size):
    t = np.arange(size, dtype=np.int64)

    def table(k):
        ang = (2.0 * np.pi / n) * ((k[:, None] * t[None, :]) % n).astype(np.float64)
        return jnp.asarray(np.cos(ang), F32), jnp.asarray(np.sin(ang), F32)

    if size <= DFT_SPLIT:
        c, s = table(t)
        return c.astype(BF16), s.astype(BF16)
    c1, s1 = table(DFT_SPLIT * np.arange(size // DFT_SPLIT, dtype=np.int64))
    c2, s2 = table(np.arange(DFT_SPLIT, dtype=np.int64))
    c = c1[:, None, :] * c2[None, :, :] - s1[:, None, :] * s2[None, :, :]
    s = s1[:, None, :] * c2[None, :, :] + c1[:, None, :] * s2[None, :, :]
    return c.reshape(size, size).astype(BF16), s.reshape(size, size).astype(BF16)


FOURIER_BLK = 256


def _fourier_call(f2d, n_pos, fno_w, fno_b3, layer):
    half = n_pos // 2
    blk = min(FOURIER_BLK, half)
    steps = n_pos // blk
    n_lo = half // blk
    ch, sh = _dft_matrices(n_pos, half)
    cc, sc = _dft_matrices(FNET_GW, FNET_GW)
    rev = np.zeros((blk, 2 * blk), np.float32)
    rev[np.arange(1, blk), blk - np.arange(1, blk)] = 1.0
    rev[0, blk] = 1.0
    alt = np.zeros((BF16_ROWS, half), np.float32)
    alt[0] = 1.0 - 2.0 * (np.arange(half) % 2)
    const = lambda b, k: (0, 0)
    dft_tile = lambda b, k: (jnp.minimum(k, n_lo - 1), 0)
    nbytes = (2 * n_pos * FNET_W * 2 + 2 * 2 * blk * half * 2 + 3 * half * FNET_W * 2 + 2 * FNET_W * FNET_W * 4
              + 8 * blk * FNET_W * 4 + 2 * blk * 2 * blk * 2)
    return pl.pallas_call(
        functools.partial(_fourier_kernel, n_pos, blk),
        grid=(BATCH, steps),
        in_specs=[
            pl.BlockSpec((n_pos, FNET_W), lambda b, k: (b, 0)),
            pl.BlockSpec((blk, half), dft_tile),
            pl.BlockSpec((blk, half), dft_tile),
            pl.BlockSpec((FNET_GW, FNET_GW), const),
            pl.BlockSpec((FNET_GW, FNET_GW), const),
            pl.BlockSpec((blk, 2 * blk), const),
            pl.BlockSpec((BF16_ROWS, half), const),
            pl.BlockSpec((None, FNET_W, FNET_W), lambda b, k: (layer, 0, 0)),
            pl.BlockSpec((None, 1, FNET_W), lambda b, k: (layer, 0, 0)),
        ],
        out_specs=pl.BlockSpec((blk, FNET_W), lambda b, k: (b * steps + k, 0)),
        out_shape=jax.ShapeDtypeStruct((BATCH * n_pos, FNET_W), BF16),
        scratch_shapes=[pltpu.VMEM((half, FNET_W), BF16), pltpu.VMEM((half, FNET_W), BF16),
                        pltpu.VMEM((half, FNET_W), BF16), pltpu.VMEM((2 * BF16_ROWS, FNET_W), F32)],
        compiler_params=_params(("arbitrary", "arbitrary"), nbytes),
        name="fourier_mix",
    )(f2d, ch, sh, cc, sc, jnp.asarray(rev, BF16), jnp.asarray(alt, BF16), fno_w, fno_b3)


def _outproj_kernel(carry_mod, na_ref, lru_ref, f_ref, res_ref, mod_ref, w_ref, g_ref, b_ref, *rest):
    if carry_mod:
        s_ref, wm_ref, bm_ref, o_ref, mo_ref = rest
        _mod_kernel(s_ref, wm_ref, bm_ref, mo_ref)
    else:
        (o_ref,) = rest
    gate = mod_ref[2:3, :]
    for rows in _row_tiles(res_ref.shape[0], MM_SUB):
        y = (_dot(na_ref[rows, :], w_ref[0:NA_W, :])
             + _dot(lru_ref[rows, :], w_ref[NA_W:NA_W + LRU_W, :])
             + _dot(f_ref[rows, :], w_ref[NA_W + LRU_W:D_MODEL, :]))
        _residual_ln_store(res_ref, y, gate, g_ref[...], b_ref[...], o_ref, rows)


def _outproj_call(na, lru, f, res, mod4, mod_row, w_out, layer, ln_g, ln_b, next_mod=None):
    m_rows = res.shape[0]
    tm = 512
    steps = m_rows // tm
    row = lambda m: (m, 0)
    nbytes = (2 * tm * D_MODEL * 2 + 2 * 2 * tm * D_MODEL * 4 + D_MODEL * D_MODEL * 2 + 2 * MM_SUB * D_MODEL * 4
              + 4 * ROW_CHUNK * D_MODEL * 4)
    in_specs = [
        pl.BlockSpec((tm, NA_W), row),
        pl.BlockSpec((tm, LRU_W), row),
        pl.BlockSpec((tm, FNET_W), row),
        pl.BlockSpec((tm, D_MODEL), row),
        pl.BlockSpec((None, None, 6, D_MODEL), lambda m: (0, mod_row(m * tm), 0, 0)),
        pl.BlockSpec((D_MODEL, D_MODEL), lambda m: (0, 0), pipeline_mode=pl.Buffered(1)),
        pl.BlockSpec((None, 1, D_MODEL), lambda m: (layer, 0, 0)),
        pl.BlockSpec((None, 1, D_MODEL), lambda m: (layer, 0, 0)),
    ]
    out_specs = [pl.BlockSpec((tm, D_MODEL), row)]
    out_shape = [jax.ShapeDtypeStruct((m_rows, D_MODEL), F32)]
    args = [na, lru, f, res, mod4, w_out, ln_g, ln_b]
    if next_mod is not None:
        s_in, w_mod, b_mod3, mod_layer = next_mod
        n_out = w_mod.shape[-1]
        slab = n_out // steps
        in_specs += [
            pl.BlockSpec((MOD_ROWS, D_MODEL), lambda m: (0, 0)),
            pl.BlockSpec((None, D_MODEL, slab), lambda m: (mod_layer, 0, m)),
            pl.BlockSpec((None, 1, slab), lambda m: (mod_layer, 0, m)),
        ]
        out_specs.append(pl.BlockSpec((MOD_ROWS, slab), lambda m: (0, m)))
        out_shape.append(jax.ShapeDtypeStruct((MOD_ROWS, n_out), F32))
        args += [s_in, w_mod, b_mod3]
        nbytes += 2 * D_MODEL * slab * 4 + D_MODEL * slab * 2
    res = pl.pallas_call(
        functools.partial(_outproj_kernel, next_mod is not None),
        grid=(steps,),
        in_specs=in_specs,
        out_specs=out_specs,
        out_shape=out_shape,
        compiler_params=_params(("arbitrary",), nbytes),
        name="outproj_residual",
    )(*args)
    return (res[0], res[1]) if next_mod is not None else (res[0], None)


MLP_TF = 512
MLP_TILES = D_FF // MLP_TF


def _mlp_kernel(x_ref, mod_ref, w1_ref, b1_ref, w2_ref, b2_ref, g_ref, b_ref, o_ref, v_ref, h0_ref, h1_ref):
    j = pl.program_id(1)
    tm = x_ref.shape[0]

    def up(h_out, rows=slice(None)):
        h = _dot(v_ref[rows, :], w1_ref[...]) + b1_ref[...]
        h_out[rows, :] = jnp.square(jnp.maximum(h, 0.0)).astype(BF16)

    def down(h_in, first):
        for c in range(D_MODEL // COL_TILE):
            sl = slice(c * COL_TILE, (c + 1) * COL_TILE)
            part = _dot(h_in[...], w2_ref[:, sl])
            if first:
                o_ref[:, sl] = part
            else:
                o_ref[:, sl] += part

    @pl.when(j == 0)
    def _():
        shift = mod_ref[3:4, :]
        scale1 = 1.0 + mod_ref[4:5, :]
        for rows in _row_tiles(tm, MM_SUB):
            _ln_mod_rows(x_ref, shift, scale1, v_ref, rows)
            up(h0_ref, rows)

    @pl.when(j == 1)
    def _():
        down(h0_ref, True)
        up(h1_ref)

    @pl.when((j > 1) & (j < MLP_TILES) & (j % 2 == 0))
    def _():
        down(h1_ref, False)
        up(h0_ref)

    @pl.when((j > 1) & (j < MLP_TILES) & (j % 2 == 1))
    def _():
        down(h0_ref, False)
        up(h1_ref)

    @pl.when(j == MLP_TILES)
    def _():
        h_last = h1_ref if (MLP_TILES - 1) % 2 else h0_ref
        gate = mod_ref[5:6, :]
        for rows in _row_tiles(tm, MM_SUB):
            y = o_ref[rows, :] + _dot(h_last[rows, :], w2_ref[...]) + b2_ref[...]
            _residual_ln_store(x_ref, y, gate, g_ref[...], b_ref[...], o_ref, rows)


def _mlp_call(x1, mod4, mod_row, tm, w1, b1, w2, b2, layer, ln_g, ln_b):
    m_rows = x1.shape[0]
    row = lambda m, j: (m, 0)
    vec = lambda m, j: (layer, 0, 0)
    nbytes = (4 * tm * D_MODEL * 4 + tm * D_MODEL * 2 + 2 * 2 * D_MODEL * MLP_TF * 2 + 2 * tm * MLP_TF * 2
              + tm * MLP_TF * 4 + tm * COL_TILE * 4 + 2 * MM_SUB * D_MODEL * 4 + 4 * ROW_CHUNK * D_MODEL * 4)
    return pl.pallas_call(
        _mlp_kernel,
        grid=(m_rows // tm, MLP_TILES + 1),
        in_specs=[
            pl.BlockSpec((tm, D_MODEL), row),
            pl.BlockSpec((None, None, 6, D_MODEL), lambda m, j: (0, mod_row(m * tm), 0, 0)),
            pl.BlockSpec((D_MODEL, MLP_TF), lambda m, j: (0, jnp.minimum(j, MLP_TILES - 1))),
            pl.BlockSpec((None, 1, MLP_TF), lambda m, j: (layer, 0, jnp.minimum(j, MLP_TILES - 1))),
            pl.BlockSpec((MLP_TF, D_MODEL), lambda m, j: (jnp.maximum(j - 1, 0), 0)),
            pl.BlockSpec((None, 1, D_MODEL), vec),
            pl.BlockSpec((None, 1, D_MODEL), vec),
            pl.BlockSpec((None, 1, D_MODEL), vec),
        ],
        out_specs=pl.BlockSpec((tm, D_MODEL), row),
        out_shape=jax.ShapeDtypeStruct((m_rows, D_MODEL), F32),
        scratch_shapes=[pltpu.VMEM((tm, D_MODEL), BF16), pltpu.VMEM((tm, MLP_TF), BF16),
                        pltpu.VMEM((tm, MLP_TF), BF16)],
        compiler_params=_params(("arbitrary", "arbitrary"), nbytes),
        name="mlp_residual",
    )(x1, mod4, w1, b1, w2, b2, ln_g, ln_b)


def kernel(x, c, ctx, c_ctx, w_mod, b_mod, w_in, rpb, conv_w, conv_b, lru_wa, lru_ba, lru_wx, lru_bx, lru_lambda,
           fno_w, fno_b, w_out, ln1_g, ln1_b, w_fc1, b_fc1, w_fc2, b_fc2, ln2_g, ln2_b):
    xl = x.reshape(BATCH * SEQ, D_MODEL)
    xc = ctx.reshape(BATCH * CTX_LEN, D_MODEL)
    s_in = jnp.concatenate([c, c_ctx[None], jnp.zeros((MOD_ROWS - BATCH - 1, D_MODEL), F32)], 0)
    b_mod3 = b_mod.reshape(DEPTH, 1, 6 * D_MODEL)
    as_mod4 = lambda m: m.reshape(1, MOD_ROWS, 6, D_MODEL)
    mod4 = as_mod4(_mod_call(s_in, w_mod, b_mod3, 0))
    cos_t, sin_t = _rope_tables()
    bias_tab = _attn_bias_table(rpb)
    w_in_l = None
    vec3 = lambda a: a.reshape(DEPTH, 1, a.shape[-1])
    ln1_g3, ln1_b3, ln2_g3, ln2_b3 = vec3(ln1_g), vec3(ln1_b), vec3(ln2_g), vec3(ln2_b)
    b_fc1_3, b_fc2_3 = vec3(b_fc1), vec3(b_fc2)
    conv_b3, fno_b3 = vec3(conv_b), vec3(fno_b)
    w4 = 0.5 * jnp.concatenate([lru_wa[:, 0], lru_wx[:, 0], lru_wa[:, 1], lru_wx[:, 1]], -1)
    blk = lambda a: a.reshape(DEPTH, LRU_BLOCKS, 1, LRU_BW)
    b4 = 0.5 * jnp.concatenate([blk(lru_ba[:, 0]), blk(lru_bx[:, 0]), blk(lru_ba[:, 1]), blk(lru_bx[:, 1])], -1)
    lat_row = lambda r0: r0 // SEQ
    ctx_row = lambda r0: CTX_MOD_ROW

    for layer in range(DEPTH):
        ctx_out = layer < DEPTH - 1
        if ctx_out and w_in_l is None:
            qc, kc, vc, xoc, goc, fc, w_in_l = _inproj_ctx_call(xc, mod4, w_in, layer, True, cast_layer=layer)
        else:
            if w_in_l is None:
                w_in_l = w_in[layer].astype(BF16)
            if ctx_out:
                qc, kc, vc, xoc, goc, fc = _inproj_ctx_call(xc, mod4, w_in_l, layer, True)
            else:
                kc, vc, xoc = _inproj_ctx_call(xc, mod4, w_in_l, layer, False)
                goc = None
        q, qr, k, v, xo, go, f = _inproj_lat_call(xl, mod4, w_in_l, layer, cos_t, sin_t)

        na, (w_out_l, w_fc1_l, w_fc2_l) = _attn_call(
            q, qr, k, v, kc, vc, bias_tab, layer,
            [_rider(w_out, layer, 0), _rider(w_fc1, layer, 1), _rider(w_fc2, layer, 0)])

        next_w_in = [_rider(w_in, layer + 1, 0)] if layer + 1 < DEPTH else []
        lru, lru_c, cast = _lru_call(ctx_out, xo, go, xoc, goc, conv_w, conv_b3, w4, b4, lru_lambda, layer,
                                     next_w_in)
        if next_w_in:
            w_in_l = cast[0]

        fm = _fourier_call(f, SEQ, fno_w, fno_b3, layer)
        next_mod = (s_in, w_mod, b_mod3, layer + 1) if layer + 1 < DEPTH else None
        x1, mod_next = _outproj_call(na, lru, fm, xl, mod4, lat_row, w_out_l, layer, ln1_g3, ln1_b3, next_mod)
        xl = _mlp_call(x1, mod4, lat_row, 1024, w_fc1_l, b_fc1_3, w_fc2_l, b_fc2_3, layer, ln2_g3, ln2_b3)

        if ctx_out:
            na_c = _ctx_attn_call(qc, kc, vc)
            fm_c = _fourier_call(fc, CTX_LEN, fno_w, fno_b3, layer)
            c1, _ = _outproj_call(na_c, lru_c, fm_c, xc, mod4, ctx_row, w_out_l, layer, ln1_g3, ln1_b3)
            xc = _mlp_call(c1, mod4, ctx_row, 512, w_fc1_l, b_fc1_3, w_fc2_l, b_fc2_3, layer, ln2_g3, ln2_b3)
        if mod_next is not None:
            mod4 = as_mod4(mod_next)

    return xl.reshape(BATCH, SEQ, D_MODEL)
```

```python
import functools
import math

import jax
import jax.numpy as jnp
import numpy as np
from jax import lax
from jax.experimental import pallas as pl
from jax.experimental.pallas import tpu as pltpu

F32 = jnp.float32
BF16 = jnp.bfloat16

D_MODEL = 2048
BATCH = 2
SEQ = 4096
DEPTH = 2
GRID_W = 64
GRID_H = SEQ // GRID_W
CTX_LEN = 256
HEAD_DIM = 128
NA_W = D_MODEL // 2
NA_HEADS = NA_W // HEAD_DIM
WIN_H = 8
WIN_W = 16
LRU_W = D_MODEL // 4
LRU_BLOCKS = 4
LRU_BW = LRU_W // LRU_BLOCKS
CONV_W = 4
LRU_C = 8.0
FNET_W = D_MODEL // 4
FNET_GROUPS = 4
FNET_GW = FNET_W // FNET_GROUPS
IN_W = 3 * NA_W + 2 * LRU_W + FNET_W
D_FF = 4 * D_MODEL
ROPE_THETA = 10000.0
LN_EPS = 1e-5
NEG_INF = -1e30
ALPHA = (2.0 * DEPTH) ** 0.25
ATTN_SCALE = HEAD_DIM ** -0.5

V7X_LANES = 128
V7X_SUBLANES = 8
BF16_ROWS = 2 * V7X_SUBLANES
V7X_VMEM_BYTES = 64 * 1024 * 1024
V7X_VMEM_RESERVED_BYTES = 6 * 1024 * 1024
VMEM_CEILING = V7X_VMEM_BYTES - V7X_VMEM_RESERVED_BYTES

COL_TILE = 512
N_COL_TILES = IN_W // COL_TILE
ROW_CHUNK = 128
MOD_ROWS = 8
CTX_MOD_ROW = BATCH


def _vmem_limit(nbytes):
    return int(min(VMEM_CEILING, nbytes * 5 // 4 + (4 << 20)))


def _params(semantics, nbytes):
    return pltpu.CompilerParams(dimension_semantics=semantics, vmem_limit_bytes=_vmem_limit(nbytes))


def _ln(x):
    mu = jnp.mean(x, axis=-1, keepdims=True)
    xc = x - mu
    var = jnp.mean(xc * xc, axis=-1, keepdims=True)
    return xc * lax.rsqrt(var + LN_EPS)


def _sigmoid(x):
    return 1.0 / (1.0 + jnp.exp(-x))


def _gelu_tanh(x):
    return 0.5 * x * (1.0 + jnp.tanh(math.sqrt(2.0 / math.pi) * (x + 0.044715 * (x * x * x))))


def _dot(a, b):
    return jnp.dot(a, b, preferred_element_type=F32)


def _dot_nt(a, b):
    return lax.dot_general(a, b, (((1,), (1,)), ((), ())), preferred_element_type=F32)


def _rider(w, layer, axis):
    return (w, layer, axis)


def _rider_specs(riders, n_steps, step_of):
    in_specs, out_specs, out_shapes, nbytes = [], [], [], 0
    for w, layer, axis in riders:
        rows, cols = w.shape[1:]
        if axis == 0:
            blk = (rows // n_steps, cols)
            in_idx = lambda *g, layer=layer: (layer, step_of(*g), 0)
            out_idx = lambda *g: (step_of(*g), 0)
        else:
            blk = (rows, cols // n_steps)
            in_idx = lambda *g, layer=layer: (layer, 0, step_of(*g))
            out_idx = lambda *g: (0, step_of(*g))
        in_specs.append(pl.BlockSpec((None,) + blk, in_idx))
        out_specs.append(pl.BlockSpec(blk, out_idx))
        out_shapes.append(jax.ShapeDtypeStruct((rows, cols), BF16))
        nbytes += 2 * blk[0] * blk[1] * (4 + 2)
    return in_specs, out_specs, out_shapes, nbytes


def _run_riders(in_refs, out_refs):
    for src, dst in zip(in_refs, out_refs):
        dst[...] = src[...].astype(dst.dtype)


MOD_TN = 1024


def _mod_kernel(s_ref, w_ref, b_ref, o_ref):
    s = s_ref[...]
    s = s * _sigmoid(s)
    o_ref[...] = _dot(s.astype(BF16), w_ref[...].astype(BF16)) + b_ref[...]


def _mod_call(s_in, w_mod, b_mod3, layer):
    n_out = w_mod.shape[-1]
    nbytes = 2 * (D_MODEL * MOD_TN * 4) + D_MODEL * MOD_TN * 2 + 4 * MOD_ROWS * n_out
    return pl.pallas_call(
        _mod_kernel,
        grid=(n_out // MOD_TN,),
        in_specs=[
            pl.BlockSpec((MOD_ROWS, D_MODEL), lambda n: (0, 0)),
            pl.BlockSpec((None, D_MODEL, MOD_TN), lambda n: (layer, 0, n)),
            pl.BlockSpec((None, 1, MOD_TN), lambda n: (layer, 0, n)),
        ],
        out_specs=pl.BlockSpec((MOD_ROWS, MOD_TN), lambda n: (0, n)),
        out_shape=jax.ShapeDtypeStruct((MOD_ROWS, n_out), F32),
        compiler_params=_params(("arbitrary",), nbytes),
        name="modulation",
    )(s_in, w_mod, b_mod3)


MM_SUB = 256


def _row_tiles(n_rows, size):
    size = min(size, n_rows)
    return [slice(r, r + size) for r in range(0, n_rows, size)]


def _ln_mod_rows(x_ref, shift, scale1, dst_ref, rows):
    for piece in _row_tiles(rows.stop - rows.start, ROW_CHUNK):
        sl = slice(rows.start + piece.start, rows.start + piece.stop)
        dst_ref[sl, :] = (_ln(x_ref[sl, :]) * scale1 + shift).astype(dst_ref.dtype)


def _residual_ln_store(res, y, gate, gain, bias, o_ref, rows):
    for piece in _row_tiles(rows.stop - rows.start, ROW_CHUNK):
        sl = slice(rows.start + piece.start, rows.start + piece.stop)
        z = ALPHA * res[sl, :] + gate * y[piece, :]
        o_ref[sl, :] = _ln(z) * gain + bias


def _rope(a, cos, sin):
    lane = lax.broadcasted_iota(jnp.int32, a.shape, 1)
    first = (lane % (HEAD_DIM // 2)) < (HEAD_DIM // 4)
    partner = jnp.where(first, pltpu.roll(a, HEAD_DIM - HEAD_DIM // 4, 1), pltpu.roll(a, HEAD_DIM // 4, 1))
    return a * cos + partner * sin


def _inproj_lat_kernel(x_ref, mod_ref, w_ref, cos_ref, sin_ref,
                       q_ref, qr_ref, k_ref, v_ref, xo_ref, go_ref, f_ref, xn_ref):
    shift = mod_ref[0:1, :]
    scale1 = 1.0 + mod_ref[1:2, :]

    def emit_q(rows, cols, acc):
        q_ref[rows, cols] = (acc * ATTN_SCALE).astype(q_ref.dtype)
        for h in range(COL_TILE // HEAD_DIM):
            sl = slice(h * HEAD_DIM, (h + 1) * HEAD_DIM)
            dst = slice(cols.start + sl.start, cols.start + sl.stop)
            rot = _rope(acc[:, sl], cos_ref[rows, :], sin_ref[rows, :])
            qr_ref[rows, dst] = (rot * ATTN_SCALE).astype(qr_ref.dtype)

    def emit_k(rows, cols, acc):
        for h in range(COL_TILE // HEAD_DIM):
            sl = slice(h * HEAD_DIM, (h + 1) * HEAD_DIM)
            dst = slice(cols.start + sl.start, cols.start + sl.stop)
            k_ref[rows, dst] = _rope(acc[:, sl], cos_ref[rows, :], sin_ref[rows, :]).astype(k_ref.dtype)

    def emit_to(ref, fn=lambda a: a):
        def emit(rows, cols, acc):
            ref[rows, cols] = fn(acc).astype(ref.dtype)
        return emit

    half = [slice(0, COL_TILE), slice(COL_TILE, 2 * COL_TILE)]
    plan = ([(emit_q, c) for c in half] + [(emit_k, c) for c in half] + [(emit_to(v_ref), c) for c in half]
            + [(emit_to(xo_ref), half[0]), (emit_to(go_ref, _gelu_tanh), half[0]), (emit_to(f_ref), half[0])])
    for rows in _row_tiles(x_ref.shape[0], MM_SUB):
        _ln_mod_rows(x_ref, shift, scale1, xn_ref, rows)
        for n, (emit, cols) in enumerate(plan):
            emit(rows, cols, _dot(xn_ref[rows, :], w_ref[:, n * COL_TILE:(n + 1) * COL_TILE]))


def _inproj_lat_call(x2d, mod4, w_in, cos_t, sin_t):
    m_rows = x2d.shape[0]
    tm = 512
    tiles_per_seq = SEQ // tm
    row = lambda m: (m, 0)
    nbytes = (2 * tm * D_MODEL * 4 + tm * D_MODEL * 2 + D_MODEL * IN_W * 2 + 4 * tm * HEAD_DIM * 4
              + 2 * tm * (4 * NA_W * 2 + 2 * LRU_W * 4 + FNET_W * 2) + 6 * MM_SUB * COL_TILE * 4
              + 4 * ROW_CHUNK * D_MODEL * 4)
    bf = lambda w: jax.ShapeDtypeStruct((m_rows, w), BF16)
    ff = lambda w: jax.ShapeDtypeStruct((m_rows, w), F32)
    widths = [NA_W, NA_W, NA_W, NA_W, LRU_W, LRU_W, FNET_W]
    return pl.pallas_call(
        _inproj_lat_kernel,
        grid=(m_rows // tm,),
        in_specs=[
            pl.BlockSpec((tm, D_MODEL), row),
            pl.BlockSpec((None, None, 6, D_MODEL), lambda m: (0, m // tiles_per_seq, 0, 0)),
            pl.BlockSpec((D_MODEL, IN_W), lambda m: (0, 0), pipeline_mode=pl.Buffered(1)),
            pl.BlockSpec((tm, HEAD_DIM), lambda m: (m % tiles_per_seq, 0)),
            pl.BlockSpec((tm, HEAD_DIM), lambda m: (m % tiles_per_seq, 0)),
        ],
        out_specs=[pl.BlockSpec((tm, w), row) for w in widths],
        out_shape=[bf(NA_W), bf(NA_W), bf(NA_W), bf(NA_W), ff(LRU_W), ff(LRU_W), bf(FNET_W)],
        scratch_shapes=[pltpu.VMEM((tm, D_MODEL), BF16)],
        compiler_params=_params(("arbitrary",), nbytes),
        name="inproj_latent",
    )(x2d, mod4, w_in, cos_t, sin_t)


def _inproj_ctx_kernel(tile_lo, with_q, with_gf, emit_w, x_ref, mod_ref, w_ref, *refs):
    refs = list(refs)
    xn_ref = refs.pop()
    wb_ref = refs.pop() if emit_w else None
    q_ref = refs.pop(0) if with_q else None
    k_ref, v_ref, xo_ref = refs[0], refs[1], refs[2]
    go_ref, f_ref = (refs[3], refs[4]) if with_gf else (None, None)
    n = pl.program_id(1) + tile_lo

    @pl.when(pl.program_id(1) == 0)
    def _():
        for rows in _row_tiles(x_ref.shape[0], MM_SUB):
            _ln_mod_rows(x_ref, mod_ref[0:1, :], 1.0 + mod_ref[1:2, :], xn_ref, rows)

    w = w_ref[...].astype(BF16)
    if emit_w:
        wb_ref[...] = w
    acc = _dot(xn_ref[...], w)

    if with_q:
        @pl.when(n < 2)
        def _():
            q_ref[...] = (acc * ATTN_SCALE).astype(q_ref.dtype)

    @pl.when((n >= 2) & (n < 4))
    def _():
        k_ref[...] = acc.astype(k_ref.dtype)

    @pl.when((n >= 4) & (n < 6))
    def _():
        v_ref[...] = acc.astype(v_ref.dtype)

    @pl.when(n == 6)
    def _():
        xo_ref[...] = acc

    if with_gf:
        @pl.when(n == 7)
        def _():
            go_ref[...] = _gelu_tanh(acc)

        @pl.when(n == 8)
        def _():
            f_ref[...] = acc.astype(f_ref.dtype)


def _inproj_ctx_call(c2d, mod4, w_in, full, cast_layer=None):
    m_rows = c2d.shape[0]
    tm = m_rows
    tile_lo, tile_hi = (0, N_COL_TILES) if full else (2, 7)
    emit_w = cast_layer is not None
    assert full or not emit_w

    def col(lo):
        return lambda m, n: (m, jnp.clip(n + tile_lo - lo, 0, 1))

    one = lambda m, n: (m, 0)
    bf = lambda w: jax.ShapeDtypeStruct((m_rows, w), BF16)
    ff = lambda w: jax.ShapeDtypeStruct((m_rows, w), F32)
    out_specs, out_shape = [], []
    if full:
        out_specs.append(pl.BlockSpec((tm, COL_TILE), col(0)))
        out_shape.append(bf(NA_W))
    out_specs += [pl.BlockSpec((tm, COL_TILE), col(2)), pl.BlockSpec((tm, COL_TILE), col(4)),
                  pl.BlockSpec((tm, COL_TILE), one)]
    out_shape += [bf(NA_W), bf(NA_W), ff(LRU_W)]
    if full:
        out_specs += [pl.BlockSpec((tm, COL_TILE), one), pl.BlockSpec((tm, COL_TILE), one)]
        out_shape += [ff(LRU_W), bf(FNET_W)]
    nbytes = (2 * tm * D_MODEL * 4 + tm * D_MODEL * 2 + 2 * D_MODEL * COL_TILE * 2
              + 2 * 6 * tm * COL_TILE * 4 + 3 * tm * COL_TILE * 4 + 4 * ROW_CHUNK * D_MODEL * 4)
    if emit_w:
        w_spec = pl.BlockSpec((None, D_MODEL, COL_TILE), lambda m, n: (cast_layer, 0, n + tile_lo))
        out_specs.append(pl.BlockSpec((D_MODEL, COL_TILE), lambda m, n: (0, n + tile_lo)))
        out_shape.append(jax.ShapeDtypeStruct((D_MODEL, IN_W), BF16))
        nbytes += 2 * D_MODEL * COL_TILE * (4 + 2)
    else:
        w_spec = pl.BlockSpec((D_MODEL, COL_TILE), lambda m, n: (0, n + tile_lo))
    return pl.pallas_call(
        functools.partial(_inproj_ctx_kernel, tile_lo, full, full, emit_w),
        grid=(1, tile_hi - tile_lo),
        in_specs=[
            pl.BlockSpec((tm, D_MODEL), lambda m, n: (m, 0)),
            pl.BlockSpec((None, None, 6, D_MODEL), lambda m, n: (0, CTX_MOD_ROW, 0, 0)),
            w_spec,
        ],
        out_specs=out_specs,
        out_shape=out_shape,
        scratch_shapes=[pltpu.VMEM((tm, D_MODEL), BF16)],
        compiler_params=_params(("arbitrary", "arbitrary"), nbytes),
        name="inproj_context",
    )(c2d, mod4, w_in)


QBLK_ROWS = 4
KBLK_ROWS = 12
N_QBLK = GRID_H // QBLK_ROWS
QBLK = QBLK_ROWS * GRID_W
KBLK = KBLK_ROWS * GRID_W
KEY_TILE = V7X_LANES
KEY_TILES = KBLK // KEY_TILE
N_DR = 2 * WIN_H - 1
BIAS_BOTH, BIAS_SECOND, BIAS_FIRST = "both", "second", "first"


def _kblk_start(first_query_row):
    return int(np.clip(first_query_row - WIN_H // 2, 0, GRID_H - KBLK_ROWS))


def _bias_entries(first_query_row):
    entries = []
    for u in range(QBLK_ROWS):
        q_row = first_query_row + u
        row_start = int(np.clip(q_row - WIN_H // 2, 0, GRID_H - WIN_H))
        row = []
        for c in range(KEY_TILES):
            k_rows = [_kblk_start(first_query_row) + 2 * c + i for i in range(2)]
            inside = [row_start <= kr < row_start + WIN_H for kr in k_rows]
            dr = [kr - q_row + (WIN_H - 1) for kr in k_rows]
            if inside[0] and inside[1]:
                row.append((BIAS_BOTH, dr[0]))
            elif inside[1]:
                row.append((BIAS_SECOND, dr[1]))
            elif inside[0]:
                row.append((BIAS_FIRST, dr[0]))
            else:
                row.append(None)
        entries.append(row)
    return entries


BIAS_TABLE = sorted({e for b in range(N_QBLK) for row in _bias_entries(b * QBLK_ROWS) for e in row if e})
BIAS_SLOT = {e: i for i, e in enumerate(BIAS_TABLE)}
N_BIAS = len(BIAS_TABLE)


def _bias_plan(first_query_row):
    return [[BIAS_SLOT[e] if e else None for e in row] for row in _bias_entries(first_query_row)]


def _attn_kernel(n_riders, q_ref, qr_ref, k_ref, v_ref, kc_ref, vc_ref, bias_ref, *rest):
    o_ref = rest[n_riders]
    vt_ref = rest[2 * n_riders + 1]
    _run_riders(rest[:n_riders], rest[n_riders + 1:2 * n_riders + 1])
    kc = kc_ref[...]
    zero_tile = jnp.zeros((GRID_W, KEY_TILE), BF16)
    for c in range(SEQ // QBLK):
        vt_ref[:, c * QBLK:(c + 1) * QBLK] = v_ref[c * QBLK:(c + 1) * QBLK, :].T
    vct = vc_ref[...].T

    def one_block(q0, k0, plan):
        s = _dot_nt(qr_ref[pl.ds(q0, QBLK), :], k_ref[pl.ds(k0, KBLK), :])
        sc = _dot_nt(q_ref[pl.ds(q0, QBLK), :], kc)
        p_rows, pc_rows, denoms = [], [], []
        for u in range(QBLK_ROWS):
            rows = slice(u * GRID_W, (u + 1) * GRID_W)
            band = {c: s[rows, c * KEY_TILE:(c + 1) * KEY_TILE] + bias_ref[idx]
                    for c, idx in enumerate(plan[u]) if idx is not None}
            ctx_tiles = [sc[rows, c * KEY_TILE:(c + 1) * KEY_TILE] for c in range(CTX_LEN // KEY_TILE)]
            tiles = list(band.values()) + ctx_tiles
            m = jnp.max(functools.reduce(jnp.maximum, tiles), axis=-1, keepdims=True)
            p_band = {c: jnp.exp(t - m) for c, t in band.items()}
            p_ctx = [jnp.exp(t - m) for t in ctx_tiles]
            total = functools.reduce(jnp.add, list(p_band.values()) + p_ctx)
            denoms.append(jnp.sum(total, axis=-1, keepdims=True))
            p_rows.append(jnp.concatenate(
                [p_band[c].astype(BF16) if c in p_band else zero_tile for c in range(KEY_TILES)], axis=1))
            pc_rows.append(jnp.concatenate([t.astype(BF16) for t in p_ctx], axis=1))
        p = jnp.concatenate(p_rows, axis=0)
        pc = jnp.concatenate(pc_rows, axis=0)
        o_t = _dot_nt(vt_ref[:, k0:k0 + KBLK], p) + _dot_nt(vct, pc)
        o_ref[pl.ds(q0, QBLK), :] = (o_t.T / jnp.concatenate(denoms, axis=0)).astype(o_ref.dtype)

    for b in range(N_QBLK):
        first_row = b * QBLK_ROWS
        one_block(b * QBLK, _kblk_start(first_row) * GRID_W, _bias_plan(first_row))


def _attn_call(q, qr, k, v, kc, vc, bias, layer, riders):
    seq_blk = lambda b, h: (b, h)
    n_steps = BATCH * NA_HEADS
    step = lambda b, h: b * NA_HEADS + h
    r_in, r_out, r_shape, r_bytes = _rider_specs(riders, n_steps, step)
    nbytes = (2 * (5 * SEQ * HEAD_DIM * 2 + 2 * CTX_LEN * HEAD_DIM * 2 + N_BIAS * GRID_W * KEY_TILE * 4)
              + 8 * QBLK * (KBLK + CTX_LEN) * 4 + r_bytes)
    res = pl.pallas_call(
        functools.partial(_attn_kernel, len(riders)),
        grid=(BATCH, NA_HEADS),
        in_specs=[
            pl.BlockSpec((SEQ, HEAD_DIM), seq_blk),
            pl.BlockSpec((SEQ, HEAD_DIM), seq_blk),
            pl.BlockSpec((SEQ, HEAD_DIM), seq_blk),
            pl.BlockSpec((SEQ, HEAD_DIM), seq_blk),
            pl.BlockSpec((CTX_LEN, HEAD_DIM), seq_blk),
            pl.BlockSpec((CTX_LEN, HEAD_DIM), seq_blk),
            pl.BlockSpec((None, None, N_BIAS, GRID_W, KEY_TILE), lambda b, h: (layer, h, 0, 0, 0)),
        ] + r_in,
        out_specs=[pl.BlockSpec((SEQ, HEAD_DIM), seq_blk)] + r_out,
        out_shape=[jax.ShapeDtypeStruct((BATCH * SEQ, NA_W), BF16)] + r_shape,
        scratch_shapes=[pltpu.VMEM((HEAD_DIM, SEQ), BF16)],
        compiler_params=_params(("arbitrary", "arbitrary"), nbytes),
        name="neighbourhood_attention",
    )(q, qr, k, v, kc, vc, bias, *[r[0] for r in riders])
    return res[0], res[1:]


def _ctx_attn_kernel(q_ref, k_ref, v_ref, o_ref):
    s = _dot_nt(q_ref[...], k_ref[...])
    m = jnp.max(s, axis=-1, keepdims=True)
    p = jnp.exp(s - m)
    denom = jnp.sum(p, axis=-1, keepdims=True)
    o_ref[...] = (_dot(p.astype(BF16), v_ref[...]) / denom).astype(o_ref.dtype)


def _ctx_attn_call(q, k, v):
    blk = pl.BlockSpec((CTX_LEN, HEAD_DIM), lambda b, h: (b, h))
    return pl.pallas_call(
        _ctx_attn_kernel,
        grid=(BATCH, NA_HEADS),
        in_specs=[blk, blk, blk],
        out_specs=blk,
        out_shape=jax.ShapeDtypeStruct((BATCH * CTX_LEN, NA_W), BF16),
        compiler_params=_params(("arbitrary", "arbitrary"), 16 << 20),
        name="context_attention",
    )(q, k, v)


def _attn_bias_table(rpb):
    col = np.arange(GRID_W)
    col_start = np.clip(col - WIN_W // 2, 0, GRID_W - WIN_W)
    in_win = (col[None, :] >= col_start[:, None]) & (col[None, :] < col_start[:, None] + WIN_W)
    dc = np.clip(col[None, :] - col[:, None] + (WIN_W - 1), 0, 2 * WIN_W - 2)
    onehot = (in_win[None] & (dc[None] == np.arange(2 * WIN_W - 1)[:, None, None])).astype(np.float32)
    t = jnp.einsum('lhdj,jqk->lhdqk', rpb, jnp.asarray(onehot), precision=lax.Precision.HIGHEST)
    t = jnp.where(in_win[None, None, None], t, NEG_INF)
    masked = jnp.full((DEPTH, NA_HEADS, GRID_W, GRID_W), NEG_INF, F32)
    tiles = []
    for kind, dr in BIAS_TABLE:
        left = masked if kind == BIAS_SECOND else t[:, :, dr]
        right = masked if kind == BIAS_FIRST else t[:, :, dr + 1 if kind == BIAS_BOTH else dr]
        tiles.append(jnp.concatenate([left, right], axis=-1))
    return jnp.stack(tiles, axis=2)


def _rope_tables():
    assert GRID_H == GRID_W
    quarter = HEAD_DIM // 4
    inv = np.float32(ROPE_THETA) ** (-np.arange(quarter, dtype=np.float32) / np.float32(quarter))
    ang = (np.arange(GRID_W, dtype=np.float32)[:, None] * inv).astype(np.float64)
    cos_g, sin_g = jnp.asarray(np.cos(ang), F32), jnp.asarray(np.sin(ang), F32)
    by_row = lambda g: jnp.broadcast_to(g[:, None, :], (GRID_H, GRID_W, quarter)).reshape(SEQ, quarter)
    by_col = lambda g: jnp.broadcast_to(g[None, :, :], (GRID_H, GRID_W, quarter)).reshape(SEQ, quarter)
    cos = jnp.concatenate([by_row(cos_g), by_row(cos_g), by_col(cos_g), by_col(cos_g)], -1)
    sin = jnp.concatenate([-by_row(sin_g), by_row(sin_g), -by_col(sin_g), by_col(sin_g)], -1)
    return cos, sin


HALO = V7X_SUBLANES
N_SEG = V7X_SUBLANES
SEG_PAD = V7X_SUBLANES
LRU_SCAN_UNROLL = 8


def _lru_coeffs(xp_ref, n_rows, cw_ref, cb_ref, w4, b4_ref, sp, a_refs, u_refs):
    seg = n_rows // N_SEG
    pitch = seg + SEG_PAD
    for s in range(N_SEG):
        base = HALO + s * seg
        xc = cb_ref[...] + xp_ref[base - CONV_W // 2:base - CONV_W // 2 + seg, :] * cw_ref[0:1, :]
        for j in range(1, CONV_W):
            off = base - CONV_W // 2 + j
            xc = xc + xp_ref[off:off + seg, :] * cw_ref[j:j + 1, :]
        th = jnp.tanh(_dot(xc.astype(BF16), w4) + b4_ref[...])
        half_xc = 0.5 * xc
        for d in range(2):
            r2 = th[:, (2 * d) * LRU_BW:(2 * d + 1) * LRU_BW] + 1.0
            i2 = th[:, (2 * d + 1) * LRU_BW:(2 * d + 2) * LRU_BW] + 1.0
            log_a = r2 * sp[d:d + 1, :]
            a = jnp.exp(log_a)
            a_refs[d][s * pitch:s * pitch + seg, :] = a
            one_minus_a2 = -jnp.tanh(log_a) * (a * a + 1.0)
            root = jnp.where(one_minus_a2 == 0.0, 0.0, one_minus_a2 * lax.rsqrt(one_minus_a2))
            u_refs[d][s * pitch:s * pitch + seg, :] = root * (i2 * half_xc)


def _lru_local_scan(n_rows, coef_f, coef_b, state_f, state_b):
    seg = n_rows // N_SEG
    pitch = seg + SEG_PAD
    zero = jnp.zeros((N_SEG, LRU_BW), F32)
    one = jnp.ones((N_SEG, LRU_BW), F32)

    def step(coef, state, row, h, p):
        rows = pl.ds(row, N_SEG, stride=pitch)
        a = coef[0][rows, :]
        h = a * h + coef[1][rows, :]
        p = p * a
        state[0][rows, :] = p
        state[1][rows, :] = h
        return h, p

    def body(i, carry):
        hf, pf, hb, pb = carry
        for j in range(LRU_SCAN_UNROLL):
            t = i * LRU_SCAN_UNROLL + j
            hf, pf = step(coef_f, state_f, t, hf, pf)
            hb, pb = step(coef_b, state_b, seg - 1 - t, hb, pb)
        return hf, pf, hb, pb

    lax.fori_loop(0, seg // LRU_SCAN_UNROLL, body, (zero, one, zero, one))


def _lru_carries(n_rows, h_in_f, h_in_b, af, uf, ab, ub):
    seg = n_rows // N_SEG
    pitch = seg + SEG_PAD
    cf, cb = [h_in_f], [h_in_b]
    for s in range(N_SEG):
        last = s * pitch + seg - 1
        cf.append(uf[last:last + 1, :] + af[last:last + 1, :] * cf[-1])
        first = (N_SEG - 1 - s) * pitch
        cb.append(ub[first:first + 1, :] + ab[first:first + 1, :] * cb[-1])
    return cf[:N_SEG], cb[:N_SEG][::-1], cf[N_SEG], cb[N_SEG]


def _lru_emit(n_rows, cf, cb, af, uf, ab, ub, g_ref, o_ref):
    seg = n_rows // N_SEG
    pitch = seg + SEG_PAD
    for s in range(N_SEG):
        src = slice(s * pitch, s * pitch + seg)
        dst = slice(s * seg, (s + 1) * seg)
        y = (uf[src, :] + af[src, :] * cf[s]) + (ub[src, :] + ab[src, :] * cb[s])
        o_ref[dst, :] = (y * g_ref[dst, :]).astype(o_ref.dtype)


def _lru_kernel(ctx_out, n_riders, x_ref, g_ref, xc_ref, *refs):
    refs = list(refs)
    gc_ref = refs.pop(0) if ctx_out else None
    cw_ref, cb_ref, w4_ref, b4_ref, lam_ref = refs[:5]
    rider_in, refs = refs[5:5 + n_riders], refs[5 + n_riders:]
    o_ref = refs.pop(0)
    oc_ref = refs.pop(0) if ctx_out else None
    rider_out, refs = refs[:n_riders], refs[n_riders:]
    xp_ref, af, uf, ab, ub, pf, sf, pb, sb = refs
    _run_riders(rider_in, rider_out)

    lam = lam_ref[...]
    z = -lam
    sp = (-0.5 * LRU_C) * (jnp.maximum(z, 0.0) + jnp.log1p(jnp.exp(-jnp.abs(z))))
    w4 = w4_ref[...].astype(BF16)
    zeros_halo = jnp.zeros((HALO, LRU_BW), F32)
    h0 = jnp.zeros((1, LRU_BW), F32)

    xp_ref[0:HALO, :] = zeros_halo
    xp_ref[HALO:HALO + CTX_LEN, :] = xc_ref[...]
    xp_ref[HALO + CTX_LEN:2 * HALO + CTX_LEN, :] = zeros_halo
    _lru_coeffs(xp_ref, CTX_LEN, cw_ref, cb_ref, w4, b4_ref, sp, (af, ab), (uf, ub))
    _lru_local_scan(CTX_LEN, (af, uf), (ab, ub), (pf, sf), (pb, sb))
    cf, cb, hf, hb = _lru_carries(CTX_LEN, h0, h0, pf, sf, pb, sb)
    if ctx_out:
        _lru_emit(CTX_LEN, cf, cb, pf, sf, pb, sb, gc_ref, oc_ref)

    xp_ref[HALO:HALO + SEQ, :] = x_ref[...]
    xp_ref[HALO + SEQ:2 * HALO + SEQ, :] = zeros_halo
    _lru_coeffs(xp_ref, SEQ, cw_ref, cb_ref, w4, b4_ref, sp, (af, ab), (uf, ub))
    _lru_local_scan(SEQ, (af, uf), (ab, ub), (pf, sf), (pb, sb))
    cf, cb, _, _ = _lru_carries(SEQ, hf, hb, pf, sf, pb, sb)
    _lru_emit(SEQ, cf, cb, pf, sf, pb, sb, g_ref, o_ref)


def _lru_call(ctx_out, xl, gl, xc, gc, conv_w3, conv_b3, w4, b4, lam3, layer, riders):
    lat = pl.BlockSpec((SEQ, LRU_BW), lambda b, j: (b, j))
    cx = pl.BlockSpec((CTX_LEN, LRU_BW), lambda b, j: (b, j))
    r_in, r_out, r_shape, r_bytes = _rider_specs(riders, BATCH * LRU_BLOCKS, lambda b, j: b * LRU_BLOCKS + j)
    in_specs = [lat, lat, cx] + ([cx] if ctx_out else []) + [
        pl.BlockSpec((None, CONV_W, LRU_BW), lambda b, j: (layer, 0, j)),
        pl.BlockSpec((None, 1, LRU_BW), lambda b, j: (layer, 0, j)),
        pl.BlockSpec((None, None, LRU_BW, 4 * LRU_BW), lambda b, j: (layer, j, 0, 0)),
        pl.BlockSpec((None, None, 1, 4 * LRU_BW), lambda b, j: (layer, j, 0, 0)),
        pl.BlockSpec((None, 2, LRU_BW), lambda b, j: (layer, 0, j)),
    ] + r_in
    out_specs = [lat] + ([cx] if ctx_out else []) + r_out
    out_shape = [jax.ShapeDtypeStruct((BATCH * SEQ, LRU_W), BF16)]
    if ctx_out:
        out_shape.append(jax.ShapeDtypeStruct((BATCH * CTX_LEN, LRU_W), BF16))
    out_shape += r_shape
    seq_bytes = SEQ * LRU_BW * 4
    args = ([xl, gl, xc] + ([gc] if ctx_out else []) + [conv_w3, conv_b3, w4, b4, lam3]
            + [r[0] for r in riders])
    res = pl.pallas_call(
        functools.partial(_lru_kernel, ctx_out, len(riders)),
        grid=(BATCH, LRU_BLOCKS),
        in_specs=in_specs,
        out_specs=out_specs,
        out_shape=out_shape,
        scratch_shapes=([pltpu.VMEM((SEQ + 2 * HALO, LRU_BW), F32)]
                        + [pltpu.VMEM((SEQ + N_SEG * SEG_PAD, LRU_BW), F32)] * 8),
        compiler_params=_params(("arbitrary", "arbitrary"), 15 * seq_bytes + (8 << 20) + r_bytes),
        name="rglru",
    )(*args)
    n_main = 2 if ctx_out else 1
    return res[0], (res[1] if ctx_out else None), res[n_main:]


def _fourier_kernel(n_pos, blk, f_ref, ch_ref, sh_ref, cc_ref, sc_ref, rev_ref, alt_ref, w_ref, b_ref, o_ref,
                    ec_ref, es_ref, mir_ref, mid_ref):
    half = n_pos // 2
    n_lo = half // blk
    step = pl.program_id(1)
    norm = 1.0 / math.sqrt(n_pos * FNET_GW)
    w = w_ref[...].astype(BF16)

    def linear(y):
        return (_dot(y.astype(BF16), w) + b_ref[...]).astype(o_ref.dtype)

    @pl.when(step == 0)
    def _():
        for i in range(n_lo):
            lo = blk * (2 * n_lo - 1 - i)
            if i == 0:
                mirrored = _dot(rev_ref[:, 0:blk], f_ref[lo:lo + blk, :])
            else:
                mirrored = _dot(rev_ref[...], f_ref[lo:lo + 2 * blk, :])
            rows = slice(i * blk, (i + 1) * blk)
            x = f_ref[rows, :].astype(F32)
            even = (x + mirrored).astype(BF16)
            odd = (x - mirrored).astype(BF16)
            for g in range(FNET_GROUPS):
                sl = slice(g * FNET_GW, (g + 1) * FNET_GW)
                ec_ref[rows, sl] = _dot(even[:, sl], cc_ref[...]).astype(BF16)
                es_ref[rows, sl] = _dot(odd[:, sl], sc_ref[...]).astype(BF16)
        for g in range(FNET_GROUPS):
            sl = slice(g * FNET_GW, (g + 1) * FNET_GW)
            mid_ref[0:BF16_ROWS, sl] = _dot(f_ref[half:half + BF16_ROWS, sl], cc_ref[...])
        mid_ref[BF16_ROWS:2 * BF16_ROWS, :] = _dot(alt_ref[...], ec_ref[...])

    mid = mid_ref[0:1, :]

    @pl.when(step < n_lo)
    def _():
        a = _dot(ch_ref[...], ec_ref[...])
        b = _dot(sh_ref[...], es_ref[...])
        row = lax.broadcasted_iota(jnp.int32, (blk, 1), 0)
        base = jnp.where(row % 2 == 0, 1.0, -1.0) * mid
        o_ref[...] = linear((a - b + base) * norm)
        r0 = pl.multiple_of(step * blk, blk)
        mir_ref[pl.ds(r0, blk), :] = ((a + b + base) * norm).astype(BF16)

    @pl.when(step == n_lo)
    def _():
        y = _dot(rev_ref[:, 0:blk], mir_ref[(n_lo - 1) * blk:n_lo * blk, :])
        nyquist = (mid_ref[BF16_ROWS:BF16_ROWS + 1, :] + mid) * norm
        row = lax.broadcasted_iota(jnp.int32, (blk, 1), 0)
        o_ref[...] = linear(jnp.where(row == 0, nyquist, y))

    if n_lo > 1:
        @pl.when(step > n_lo)
        def _():
            r0 = pl.multiple_of((2 * n_lo - 1 - step) * blk, blk)
            o_ref[...] = linear(_dot(rev_ref[...], mir_ref[pl.ds(r0, 2 * blk), :]))


DFT_SPLIT = 64


def _dft_matrices(n, size):
    t = np.arange(size, dtype=np.int64)

    def table(k):
        ang = (2.0 * np.pi / n) * ((k[:, None] * t[None, :]) % n).astype(np.float64)
        return jnp.asarray(np.cos(ang), F32), jnp.asarray(np.sin(ang), F32)

    if size <= DFT_SPLIT:
        c, s = table(t)
        return c.astype(BF16), s.astype(BF16)
    c1, s1 = table(DFT_SPLIT * np.arange(size // DFT_SPLIT, dtype=np.int64))
    c2, s2 = table(np.arange(DFT_SPLIT, dtype=np.int64))
    c = c1[:, None, :] * c2[None, :, :] - s1[:, None, :] * s2[None, :, :]
    s = s1[:, None, :] * c2[None, :, :] + c1[:, None, :] * s2[None, :, :]
    return c.reshape(size, size).astype(BF16), s.reshape(size, size).astype(BF16)


FOURIER_BLK = 256


def _fourier_call(f2d, n_pos, fno_w, fno_b3, layer):
    half = n_pos // 2
    blk = min(FOURIER_BLK, half)
    steps = n_pos // blk
    n_lo = half // blk
    ch, sh = _dft_matrices(n_pos, half)
    cc, sc = _dft_matrices(FNET_GW, FNET_GW)
    rev = np.zeros((blk, 2 * blk), np.float32)
    rev[np.arange(1, blk), blk - np.arange(1, blk)] = 1.0
    rev[0, blk] = 1.0
    alt = np.zeros((BF16_ROWS, half), np.float32)
    alt[0] = 1.0 - 2.0 * (np.arange(half) % 2)
    const = lambda b, k: (0, 0)
    dft_tile = lambda b, k: (jnp.minimum(k, n_lo - 1), 0)
    nbytes = (2 * n_pos * FNET_W * 2 + 2 * 2 * blk * half * 2 + 3 * half * FNET_W * 2 + 2 * FNET_W * FNET_W * 4
              + 8 * blk * FNET_W * 4 + 2 * blk * 2 * blk * 2)
    return pl.pallas_call(
        functools.partial(_fourier_kernel, n_pos, blk),
        grid=(BATCH, steps),
        in_specs=[
            pl.BlockSpec((n_pos, FNET_W), lambda b, k: (b, 0)),
            pl.BlockSpec((blk, half), dft_tile),
            pl.BlockSpec((blk, half), dft_tile),
            pl.BlockSpec((FNET_GW, FNET_GW), const),
            pl.BlockSpec((FNET_GW, FNET_GW), const),
            pl.BlockSpec((blk, 2 * blk), const),
            pl.BlockSpec((BF16_ROWS, half), const),
            pl.BlockSpec((None, FNET_W, FNET_W), lambda b, k: (layer, 0, 0)),
            pl.BlockSpec((None, 1, FNET_W), lambda b, k: (layer, 0, 0)),
        ],
        out_specs=pl.BlockSpec((blk, FNET_W), lambda b, k: (b * steps + k, 0)),
        out_shape=jax.ShapeDtypeStruct((BATCH * n_pos, FNET_W), BF16),
        scratch_shapes=[pltpu.VMEM((half, FNET_W), BF16), pltpu.VMEM((half, FNET_W), BF16),
                        pltpu.VMEM((half, FNET_W), BF16), pltpu.VMEM((2 * BF16_ROWS, FNET_W), F32)],
        compiler_params=_params(("arbitrary", "arbitrary"), nbytes),
        name="fourier_mix",
    )(f2d, ch, sh, cc, sc, jnp.asarray(rev, BF16), jnp.asarray(alt, BF16), fno_w, fno_b3)


def _outproj_kernel(carry_mod, na_ref, lru_ref, f_ref, res_ref, mod_ref, w_ref, g_ref, b_ref, *rest):
    if carry_mod:
        s_ref, wm_ref, bm_ref, o_ref, mo_ref = rest
        _mod_kernel(s_ref, wm_ref, bm_ref, mo_ref)
    else:
        (o_ref,) = rest
    gate = mod_ref[2:3, :]
    for rows in _row_tiles(res_ref.shape[0], MM_SUB):
        y = (_dot(na_ref[rows, :], w_ref[0:NA_W, :])
             + _dot(lru_ref[rows, :], w_ref[NA_W:NA_W + LRU_W, :])
             + _dot(f_ref[rows, :], w_ref[NA_W + LRU_W:D_MODEL, :]))
        _residual_ln_store(res_ref, y, gate, g_ref[...], b_ref[...], o_ref, rows)


def _outproj_call(na, lru, f, res, mod4, mod_row, w_out, layer, ln_g, ln_b, next_mod=None):
    m_rows = res.shape[0]
    tm = 512
    steps = m_rows // tm
    row = lambda m: (m, 0)
    nbytes = (2 * tm * D_MODEL * 2 + 2 * 2 * tm * D_MODEL * 4 + D_MODEL * D_MODEL * 2 + 2 * MM_SUB * D_MODEL * 4
              + 4 * ROW_CHUNK * D_MODEL * 4)
    in_specs = [
        pl.BlockSpec((tm, NA_W), row),
        pl.BlockSpec((tm, LRU_W), row),
        pl.BlockSpec((tm, FNET_W), row),
        pl.BlockSpec((tm, D_MODEL), row),
        pl.BlockSpec((None, None, 6, D_MODEL), lambda m: (0, mod_row(m * tm), 0, 0)),
        pl.BlockSpec((D_MODEL, D_MODEL), lambda m: (0, 0), pipeline_mode=pl.Buffered(1)),
        pl.BlockSpec((None, 1, D_MODEL), lambda m: (layer, 0, 0)),
        pl.BlockSpec((None, 1, D_MODEL), lambda m: (layer, 0, 0)),
    ]
    out_specs = [pl.BlockSpec((tm, D_MODEL), row)]
    out_shape = [jax.ShapeDtypeStruct((m_rows, D_MODEL), F32)]
    args = [na, lru, f, res, mod4, w_out, ln_g, ln_b]
    if next_mod is not None:
        s_in, w_mod, b_mod3, mod_layer = next_mod
        n_out = w_mod.shape[-1]
        slab = n_out // steps
        in_specs += [
            pl.BlockSpec((MOD_ROWS, D_MODEL), lambda m: (0, 0)),
            pl.BlockSpec((None, D_MODEL, slab), lambda m: (mod_layer, 0, m)),
            pl.BlockSpec((None, 1, slab), lambda m: (mod_layer, 0, m)),
        ]
        out_specs.append(pl.BlockSpec((MOD_ROWS, slab), lambda m: (0, m)))
        out_shape.append(jax.ShapeDtypeStruct((MOD_ROWS, n_out), F32))
        args += [s_in, w_mod, b_mod3]
        nbytes += 2 * D_MODEL * slab * 4 + D_MODEL * slab * 2
    res = pl.pallas_call(
        functools.partial(_outproj_kernel, next_mod is not None),
        grid=(steps,),
        in_specs=in_specs,
        out_specs=out_specs,
        out_shape=out_shape,
        compiler_params=_params(("arbitrary",), nbytes),
        name="outproj_residual",
    )(*args)
    return (res[0], res[1]) if next_mod is not None else (res[0], None)


MLP_TF = 512
MLP_TILES = D_FF // MLP_TF


def _mlp_kernel(x_ref, mod_ref, w1_ref, b1_ref, w2_ref, b2_ref, g_ref, b_ref, o_ref, v_ref, h0_ref, h1_ref):
    j = pl.program_id(1)
    tm = x_ref.shape[0]

    def up(h_out, rows=slice(None)):
        h = _dot(v_ref[rows, :], w1_ref[...]) + b1_ref[...]
        h_out[rows, :] = jnp.square(jnp.maximum(h, 0.0)).astype(BF16)

    def down(h_in, first):
        for c in range(D_MODEL // COL_TILE):
            sl = slice(c * COL_TILE, (c + 1) * COL_TILE)
            part = _dot(h_in[...], w2_ref[:, sl])
            if first:
                o_ref[:, sl] = part
            else:
                o_ref[:, sl] += part

    @pl.when(j == 0)
    def _():
        shift = mod_ref[3:4, :]
        scale1 = 1.0 + mod_ref[4:5, :]
        for rows in _row_tiles(tm, MM_SUB):
            _ln_mod_rows(x_ref, shift, scale1, v_ref, rows)
            up(h0_ref, rows)

    @pl.when(j == 1)
    def _():
        down(h0_ref, True)
        up(h1_ref)

    @pl.when((j > 1) & (j < MLP_TILES) & (j % 2 == 0))
    def _():
        down(h1_ref, False)
        up(h0_ref)

    @pl.when((j > 1) & (j < MLP_TILES) & (j % 2 == 1))
    def _():
        down(h0_ref, False)
        up(h1_ref)

    @pl.when(j == MLP_TILES)
    def _():
        h_last = h1_ref if (MLP_TILES - 1) % 2 else h0_ref
        gate = mod_ref[5:6, :]
        for rows in _row_tiles(tm, MM_SUB):
            y = o_ref[rows, :] + _dot(h_last[rows, :], w2_ref[...]) + b2_ref[...]
            _residual_ln_store(x_ref, y, gate, g_ref[...], b_ref[...], o_ref, rows)


def _mlp_call(x1, mod4, mod_row, tm, w1, b1, w2, b2, layer, ln_g, ln_b):
    m_rows = x1.shape[0]
    row = lambda m, j: (m, 0)
    vec = lambda m, j: (layer, 0, 0)
    nbytes = (4 * tm * D_MODEL * 4 + tm * D_MODEL * 2 + 2 * 2 * D_MODEL * MLP_TF * 2 + 2 * tm * MLP_TF * 2
              + tm * MLP_TF * 4 + tm * COL_TILE * 4 + 2 * MM_SUB * D_MODEL * 4 + 4 * ROW_CHUNK * D_MODEL * 4)
    return pl.pallas_call(
        _mlp_kernel,
        grid=(m_rows // tm, MLP_TILES + 1),
        in_specs=[
            pl.BlockSpec((tm, D_MODEL), row),
            pl.BlockSpec((None, None, 6, D_MODEL), lambda m, j: (0, mod_row(m * tm), 0, 0)),
            pl.BlockSpec((D_MODEL, MLP_TF), lambda m, j: (0, jnp.minimum(j, MLP_TILES - 1))),
            pl.BlockSpec((None, 1, MLP_TF), lambda m, j: (layer, 0, jnp.minimum(j, MLP_TILES - 1))),
            pl.BlockSpec((MLP_TF, D_MODEL), lambda m, j: (jnp.maximum(j - 1, 0), 0)),
            pl.BlockSpec((None, 1, D_MODEL), vec),
            pl.BlockSpec((None, 1, D_MODEL), vec),
            pl.BlockSpec((None, 1, D_MODEL), vec),
        ],
        out_specs=pl.BlockSpec((tm, D_MODEL), row),
        out_shape=jax.ShapeDtypeStruct((m_rows, D_MODEL), F32),
        scratch_shapes=[pltpu.VMEM((tm, D_MODEL), BF16), pltpu.VMEM((tm, MLP_TF), BF16),
                        pltpu.VMEM((tm, MLP_TF), BF16)],
        compiler_params=_params(("arbitrary", "arbitrary"), nbytes),
        name="mlp_residual",
    )(x1, mod4, w1, b1, w2, b2, ln_g, ln_b)


def kernel(x, c, ctx, c_ctx, w_mod, b_mod, w_in, rpb, conv_w, conv_b, lru_wa, lru_ba, lru_wx, lru_bx, lru_lambda,
           fno_w, fno_b, w_out, ln1_g, ln1_b, w_fc1, b_fc1, w_fc2, b_fc2, ln2_g, ln2_b):
    xl = x.reshape(BATCH * SEQ, D_MODEL)
    xc = ctx.reshape(BATCH * CTX_LEN, D_MODEL)
    s_in = jnp.concatenate([c, c_ctx[None], jnp.zeros((MOD_ROWS - BATCH - 1, D_MODEL), F32)], 0)
    b_mod3 = b_mod.reshape(DEPTH, 1, 6 * D_MODEL)
    as_mod4 = lambda m: m.reshape(1, MOD_ROWS, 6, D_MODEL)
    mod4 = as_mod4(_mod_call(s_in, w_mod, b_mod3, 0))
    cos_t, sin_t = _rope_tables()
    bias_tab = _attn_bias_table(rpb)
    w_in_l = None
    vec3 = lambda a: a.reshape(DEPTH, 1, a.shape[-1])
    ln1_g3, ln1_b3, ln2_g3, ln2_b3 = vec3(ln1_g), vec3(ln1_b), vec3(ln2_g), vec3(ln2_b)
    b_fc1_3, b_fc2_3 = vec3(b_fc1), vec3(b_fc2)
    conv_b3, fno_b3 = vec3(conv_b), vec3(fno_b)
    w4 = 0.5 * jnp.concatenate([lru_wa[:, 0], lru_wx[:, 0], lru_wa[:, 1], lru_wx[:, 1]], -1)
    blk = lambda a: a.reshape(DEPTH, LRU_BLOCKS, 1, LRU_BW)
    b4 = 0.5 * jnp.concatenate([blk(lru_ba[:, 0]), blk(lru_bx[:, 0]), blk(lru_ba[:, 1]), blk(lru_bx[:, 1])], -1)
    lat_row = lambda r0: r0 // SEQ
    ctx_row = lambda r0: CTX_MOD_ROW

    for layer in range(DEPTH):
        ctx_out = layer < DEPTH - 1
        if ctx_out and w_in_l is None:
            qc, kc, vc, xoc, goc, fc, w_in_l = _inproj_ctx_call(xc, mod4, w_in, True, cast_layer=layer)
        else:
            if w_in_l is None:
                w_in_l = w_in[layer].astype(BF16)
            if ctx_out:
                qc, kc, vc, xoc, goc, fc = _inproj_ctx_call(xc, mod4, w_in_l, True)
            else:
                kc, vc, xoc = _inproj_ctx_call(xc, mod4, w_in_l, False)
                goc = None
        q, qr, k, v, xo, go, f = _inproj_lat_call(xl, mod4, w_in_l, cos_t, sin_t)

        na, (w_out_l, w_fc1_l, w_fc2_l) = _attn_call(
            q, qr, k, v, kc, vc, bias_tab, layer,
            [_rider(w_out, layer, 0), _rider(w_fc1, layer, 1), _rider(w_fc2, layer, 0)])

        next_w_in = [_rider(w_in, layer + 1, 0)] if layer + 1 < DEPTH else []
        lru, lru_c, cast = _lru_call(ctx_out, xo, go, xoc, goc, conv_w, conv_b3, w4, b4, lru_lambda, layer,
                                     next_w_in)
        if next_w_in:
            w_in_l = cast[0]

        fm = _fourier_call(f, SEQ, fno_w, fno_b3, layer)
        next_mod = (s_in, w_mod, b_mod3, layer + 1) if layer + 1 < DEPTH else None
        x1, mod_next = _outproj_call(na, lru, fm, xl, mod4, lat_row, w_out_l, layer, ln1_g3, ln1_b3, next_mod)
        xl = _mlp_call(x1, mod4, lat_row, 1024, w_fc1_l, b_fc1_3, w_fc2_l, b_fc2_3, layer, ln2_g3, ln2_b3)

        if ctx_out:
            na_c = _ctx_attn_call(qc, kc, vc)
            fm_c = _fourier_call(fc, CTX_LEN, fno_w, fno_b3, layer)
            c1, _ = _outproj_call(na_c, lru_c, fm_c, xc, mod4, ctx_row, w_out_l, layer, ln1_g3, ln1_b3)
            xc = _mlp_call(c1, mod4, ctx_row, 512, w_fc1_l, b_fc1_3, w_fc2_l, b_fc2_3, layer, ln2_g3, ln2_b3)
        if mod_next is not None:
            mod4 = as_mod4(mod_next)

    return xl.reshape(BATCH, SEQ, D_MODEL)
```

```python
import functools
import math

import jax
import jax.numpy as jnp
import numpy as np
from jax import lax
from jax.experimental import pallas as pl
from jax.experimental.pallas import tpu as pltpu

F32 = jnp.float32
BF16 = jnp.bfloat16

D_MODEL = 2048
BATCH = 2
SEQ = 4096
DEPTH = 2
GRID_W = 64
GRID_H = SEQ // GRID_W
CTX_LEN = 256
HEAD_DIM = 128
NA_W = D_MODEL // 2
NA_HEADS = NA_W // HEAD_DIM
WIN_H = 8
WIN_W = 16
LRU_W = D_MODEL // 4
LRU_BLOCKS = 4
LRU_BW = LRU_W // LRU_BLOCKS
CONV_W = 4
LRU_C = 8.0
FNET_W = D_MODEL // 4
FNET_GROUPS = 4
FNET_GW = FNET_W // FNET_GROUPS
IN_W = 3 * NA_W + 2 * LRU_W + FNET_W
D_FF = 4 * D_MODEL
ROPE_THETA = 10000.0
LN_EPS = 1e-5
NEG_INF = -1e30
ALPHA = (2.0 * DEPTH) ** 0.25
ATTN_SCALE = HEAD_DIM ** -0.5

V7X_LANES = 128
V7X_SUBLANES = 8
BF16_ROWS = 2 * V7X_SUBLANES
V7X_VMEM_BYTES = 64 * 1024 * 1024
V7X_VMEM_RESERVED_BYTES = 6 * 1024 * 1024
VMEM_CEILING = V7X_VMEM_BYTES - V7X_VMEM_RESERVED_BYTES

COL_TILE = 512
N_COL_TILES = IN_W // COL_TILE
ROW_CHUNK = 128
MOD_ROWS = 8
CTX_MOD_ROW = BATCH


def _vmem_limit(nbytes):
    return int(min(VMEM_CEILING, nbytes * 5 // 4 + (4 << 20)))


def _params(semantics, nbytes):
    return pltpu.CompilerParams(dimension_semantics=semantics, vmem_limit_bytes=_vmem_limit(nbytes))


def _ln(x):
    mu = jnp.mean(x, axis=-1, keepdims=True)
    xc = x - mu
    var = jnp.mean(xc * xc, axis=-1, keepdims=True)
    return xc * lax.rsqrt(var + LN_EPS)


def _sigmoid(x):
    return 1.0 / (1.0 + jnp.exp(-x))


def _gelu_tanh(x):
    return 0.5 * x * (1.0 + jnp.tanh(math.sqrt(2.0 / math.pi) * (x + 0.044715 * (x * x * x))))


def _dot(a, b):
    return jnp.dot(a, b, preferred_element_type=F32)


def _dot_nt(a, b):
    return lax.dot_general(a, b, (((1,), (1,)), ((), ())), preferred_element_type=F32)


def _rider(w, layer, axis):
    return (w, layer, axis)


def _rider_specs(riders, n_steps, step_of):
    in_specs, out_specs, out_shapes, nbytes = [], [], [], 0
    for w, layer, axis in riders:
        rows, cols = w.shape[1:]
        if axis == 0:
            blk = (rows // n_steps, cols)
            in_idx = lambda *g, layer=layer: (layer, step_of(*g), 0)
            out_idx = lambda *g: (step_of(*g), 0)
        else:
            blk = (rows, cols // n_steps)
            in_idx = lambda *g, layer=layer: (layer, 0, step_of(*g))
            out_idx = lambda *g: (0, step_of(*g))
        in_specs.append(pl.BlockSpec((None,) + blk, in_idx))
        out_specs.append(pl.BlockSpec(blk, out_idx))
        out_shapes.append(jax.ShapeDtypeStruct((rows, cols), BF16))
        nbytes += 2 * blk[0] * blk[1] * (4 + 2)
    return in_specs, out_specs, out_shapes, nbytes


def _run_riders(in_refs, out_refs):
    for src, dst in zip(in_refs, out_refs):
        dst[...] = src[...].astype(dst.dtype)


MOD_TN = 1024


def _mod_kernel(s_ref, w_ref, b_ref, o_ref):
    s = s_ref[...]
    s = s * _sigmoid(s)
    o_ref[...] = _dot(s.astype(BF16), w_ref[...].astype(BF16)) + b_ref[...]


def _mod_call(s_in, w_mod, b_mod3, layer):
    n_out = w_mod.shape[-1]
    nbytes = 2 * (D_MODEL * MOD_TN * 4) + D_MODEL * MOD_TN * 2 + 4 * MOD_ROWS * n_out
    return pl.pallas_call(
        _mod_kernel,
        grid=(n_out // MOD_TN,),
        in_specs=[
            pl.BlockSpec((MOD_ROWS, D_MODEL), lambda n: (0, 0)),
            pl.BlockSpec((None, D_MODEL, MOD_TN), lambda n: (layer, 0, n)),
            pl.BlockSpec((None, 1, MOD_TN), lambda n: (layer, 0, n)),
        ],
        out_specs=pl.BlockSpec((MOD_ROWS, MOD_TN), lambda n: (0, n)),
        out_shape=jax.ShapeDtypeStruct((MOD_ROWS, n_out), F32),
        compiler_params=_params(("arbitrary",), nbytes),
        name="modulation",
    )(s_in, w_mod, b_mod3)


MM_SUB = 256


def _row_tiles(n_rows, size):
    size = min(size, n_rows)
    return [slice(r, r + size) for r in range(0, n_rows, size)]


def _ln_mod_rows(x_ref, shift, scale1, dst_ref, rows):
    for piece in _row_tiles(rows.stop - rows.start, ROW_CHUNK):
        sl = slice(rows.start + piece.start, rows.start + piece.stop)
        dst_ref[sl, :] = (_ln(x_ref[sl, :]) * scale1 + shift).astype(dst_ref.dtype)


def _residual_ln_store(res, y, gate, gain, bias, o_ref, rows):
    for piece in _row_tiles(rows.stop - rows.start, ROW_CHUNK):
        sl = slice(rows.start + piece.start, rows.start + piece.stop)
        z = ALPHA * res[sl, :] + gate * y[piece, :]
        o_ref[sl, :] = _ln(z) * gain + bias


def _rope(a, cos, sin):
    lane = lax.broadcasted_iota(jnp.int32, a.shape, 1)
    first = (lane % (HEAD_DIM // 2)) < (HEAD_DIM // 4)
    partner = jnp.where(first, pltpu.roll(a, HEAD_DIM - HEAD_DIM // 4, 1), pltpu.roll(a, HEAD_DIM // 4, 1))
    return a * cos + partner * sin


def _inproj_lat_kernel(x_ref, mod_ref, w_ref, cos_ref, sin_ref,
                       q_ref, qr_ref, k_ref, v_ref, xo_ref, go_ref, f_ref, xn_ref):
    shift = mod_ref[0:1, :]
    scale1 = 1.0 + mod_ref[1:2, :]

    def emit_q(rows, cols, acc):
        q_ref[rows, cols] = (acc * ATTN_SCALE).astype(q_ref.dtype)
        for h in range(COL_TILE // HEAD_DIM):
            sl = slice(h * HEAD_DIM, (h + 1) * HEAD_DIM)
            dst = slice(cols.start + sl.start, cols.start + sl.stop)
            rot = _rope(acc[:, sl], cos_ref[rows, :], sin_ref[rows, :])
            qr_ref[rows, dst] = (rot * ATTN_SCALE).astype(qr_ref.dtype)

    def emit_k(rows, cols, acc):
        for h in range(COL_TILE // HEAD_DIM):
            sl = slice(h * HEAD_DIM, (h + 1) * HEAD_DIM)
            dst = slice(cols.start + sl.start, cols.start + sl.stop)
            k_ref[rows, dst] = _rope(acc[:, sl], cos_ref[rows, :], sin_ref[rows, :]).astype(k_ref.dtype)

    def emit_to(ref, fn=lambda a: a):
        def emit(rows, cols, acc):
            ref[rows, cols] = fn(acc).astype(ref.dtype)
        return emit

    half = [slice(0, COL_TILE), slice(COL_TILE, 2 * COL_TILE)]
    plan = ([(emit_q, c) for c in half] + [(emit_k, c) for c in half] + [(emit_to(v_ref), c) for c in half]
            + [(emit_to(xo_ref), half[0]), (emit_to(go_ref, _gelu_tanh), half[0]), (emit_to(f_ref), half[0])])
    for rows in _row_tiles(x_ref.shape[0], MM_SUB):
        _ln_mod_rows(x_ref, shift, scale1, xn_ref, rows)
        for n, (emit, cols) in enumerate(plan):
            emit(rows, cols, _dot(xn_ref[rows, :], w_ref[:, n * COL_TILE:(n + 1) * COL_TILE]))


def _inproj_lat_call(x2d, mod4, w_in, cos_t, sin_t):
    m_rows = x2d.shape[0]
    tm = 512
    tiles_per_seq = SEQ // tm
    row = lambda m: (m, 0)
    nbytes = (2 * tm * D_MODEL * 4 + tm * D_MODEL * 2 + D_MODEL * IN_W * 2 + 4 * tm * HEAD_DIM * 4
              + 2 * tm * (4 * NA_W * 2 + 2 * LRU_W * 4 + FNET_W * 2) + 6 * MM_SUB * COL_TILE * 4
              + 4 * ROW_CHUNK * D_MODEL * 4)
    bf = lambda w: jax.ShapeDtypeStruct((m_rows, w), BF16)
    ff = lambda w: jax.ShapeDtypeStruct((m_rows, w), F32)
    widths = [NA_W, NA_W, NA_W, NA_W, LRU_W, LRU_W, FNET_W]
    return pl.pallas_call(
        _inproj_lat_kernel,
        grid=(m_rows // tm,),
        in_specs=[
            pl.BlockSpec((tm, D_MODEL), row),
            pl.BlockSpec((None, None, 6, D_MODEL), lambda m: (0, m // tiles_per_seq, 0, 0)),
            pl.BlockSpec((D_MODEL, IN_W), lambda m: (0, 0), pipeline_mode=pl.Buffered(1)),
            pl.BlockSpec((tm, HEAD_DIM), lambda m: (m % tiles_per_seq, 0)),
            pl.BlockSpec((tm, HEAD_DIM), lambda m: (m % tiles_per_seq, 0)),
        ],
        out_specs=[pl.BlockSpec((tm, w), row) for w in widths],
        out_shape=[bf(NA_W), bf(NA_W), bf(NA_W), bf(NA_W), ff(LRU_W), ff(LRU_W), bf(FNET_W)],
        scratch_shapes=[pltpu.VMEM((tm, D_MODEL), BF16)],
        compiler_params=_params(("arbitrary",), nbytes),
        name="inproj_latent",
    )(x2d, mod4, w_in, cos_t, sin_t)


def _inproj_ctx_kernel(tile_lo, with_q, with_gf, emit_w, x_ref, mod_ref, w_ref, *refs):
    refs = list(refs)
    xn_ref = refs.pop()
    wb_ref = refs.pop() if emit_w else None
    q_ref = refs.pop(0) if with_q else None
    k_ref, v_ref, xo_ref = refs[0], refs[1], refs[2]
    go_ref, f_ref = (refs[3], refs[4]) if with_gf else (None, None)
    n = pl.program_id(1) + tile_lo

    @pl.when(pl.program_id(1) == 0)
    def _():
        for rows in _row_tiles(x_ref.shape[0], MM_SUB):
            _ln_mod_rows(x_ref, mod_ref[0:1, :], 1.0 + mod_ref[1:2, :], xn_ref, rows)

    w = w_ref[...].astype(BF16)
    if emit_w:
        wb_ref[...] = w
    acc = _dot(xn_ref[...], w)

    if with_q:
        @pl.when(n < 2)
        def _():
            q_ref[...] = (acc * ATTN_SCALE).astype(q_ref.dtype)

    @pl.when((n >= 2) & (n < 4))
    def _():
        k_ref[...] = acc.astype(k_ref.dtype)

    @pl.when((n >= 4) & (n < 6))
    def _():
        v_ref[...] = acc.astype(v_ref.dtype)

    @pl.when(n == 6)
    def _():
        xo_ref[...] = acc

    if with_gf:
        @pl.when(n == 7)
        def _():
            go_ref[...] = _gelu_tanh(acc)

        @pl.when(n == 8)
        def _():
            f_ref[...] = acc.astype(f_ref.dtype)


def _inproj_ctx_call(c2d, mod4, w_in, full, cast_layer=None):
    m_rows = c2d.shape[0]
    tm = m_rows
    tile_lo, tile_hi = (0, N_COL_TILES) if full else (2, 7)
    emit_w = cast_layer is not None
    assert full or not emit_w

    def col(lo):
        return lambda m, n: (m, jnp.clip(n + tile_lo - lo, 0, 1))

    one = lambda m, n: (m, 0)
    bf = lambda w: jax.ShapeDtypeStruct((m_rows, w), BF16)
    ff = lambda w: jax.ShapeDtypeStruct((m_rows, w), F32)
    out_specs, out_shape = [], []
    if full:
        out_specs.append(pl.BlockSpec((tm, COL_TILE), col(0)))
        out_shape.append(bf(NA_W))
    out_specs += [pl.BlockSpec((tm, COL_TILE), col(2)), pl.BlockSpec((tm, COL_TILE), col(4)),
                  pl.BlockSpec((tm, COL_TILE), one)]
    out_shape += [bf(NA_W), bf(NA_W), ff(LRU_W)]
    if full:
        out_specs += [pl.BlockSpec((tm, COL_TILE), one), pl.BlockSpec((tm, COL_TILE), one)]
        out_shape += [ff(LRU_W), bf(FNET_W)]
    nbytes = (2 * tm * D_MODEL * 4 + tm * D_MODEL * 2 + 2 * D_MODEL * COL_TILE * 2
              + 2 * 6 * tm * COL_TILE * 4 + 3 * tm * COL_TILE * 4 + 4 * ROW_CHUNK * D_MODEL * 4)
    if emit_w:
        w_spec = pl.BlockSpec((None, D_MODEL, COL_TILE), lambda m, n: (cast_layer, 0, n + tile_lo))
        out_specs.append(pl.BlockSpec((D_MODEL, COL_TILE), lambda m, n: (0, n + tile_lo)))
        out_shape.append(jax.ShapeDtypeStruct((D_MODEL, IN_W), BF16))
        nbytes += 2 * D_MODEL * COL_TILE * (4 + 2)
    else:
        w_spec = pl.BlockSpec((D_MODEL, COL_TILE), lambda m, n: (0, n + tile_lo))
    return pl.pallas_call(
        functools.partial(_inproj_ctx_kernel, tile_lo, full, full, emit_w),
        grid=(1, tile_hi - tile_lo),
        in_specs=[
            pl.BlockSpec((tm, D_MODEL), lambda m, n: (m, 0)),
            pl.BlockSpec((None, None, 6, D_MODEL), lambda m, n: (0, CTX_MOD_ROW, 0, 0)),
            w_spec,
        ],
        out_specs=out_specs,
        out_shape=out_shape,
        scratch_shapes=[pltpu.VMEM((tm, D_MODEL), BF16)],
        compiler_params=_params(("arbitrary", "arbitrary"), nbytes),
        name="inproj_context",
    )(c2d, mod4, w_in)


QBLK_ROWS = 4
KBLK_ROWS = 12
N_QBLK = GRID_H // QBLK_ROWS
QBLK = QBLK_ROWS * GRID_W
KBLK = KBLK_ROWS * GRID_W
KEY_TILE = V7X_LANES
KEY_TILES = KBLK // KEY_TILE
N_DR = 2 * WIN_H - 1
BIAS_BOTH, BIAS_SECOND, BIAS_FIRST = "both", "second", "first"


def _kblk_start(first_query_row):
    return int(np.clip(first_query_row - WIN_H // 2, 0, GRID_H - KBLK_ROWS))


def _bias_entries(first_query_row):
    entries = []
    for u in range(QBLK_ROWS):
        q_row = first_query_row + u
        row_start = int(np.clip(q_row - WIN_H // 2, 0, GRID_H - WIN_H))
        row = []
        for c in range(KEY_TILES):
            k_rows = [_kblk_start(first_query_row) + 2 * c + i for i in range(2)]
            inside = [row_start <= kr < row_start + WIN_H for kr in k_rows]
            dr = [kr - q_row + (WIN_H - 1) for kr in k_rows]
            if inside[0] and inside[1]:
                row.append((BIAS_BOTH, dr[0]))
            elif inside[1]:
                row.append((BIAS_SECOND, dr[1]))
            elif inside[0]:
                row.append((BIAS_FIRST, dr[0]))
            else:
                row.append(None)
        entries.append(row)
    return entries


BIAS_TABLE = sorted({e for b in range(N_QBLK) for row in _bias_entries(b * QBLK_ROWS) for e in row if e})
BIAS_SLOT = {e: i for i, e in enumerate(BIAS_TABLE)}
N_BIAS = len(BIAS_TABLE)


def _bias_plan(first_query_row):
    return [[BIAS_SLOT[e] if e else None for e in row] for row in _bias_entries(first_query_row)]


def _attn_kernel(n_riders, q_ref, qr_ref, k_ref, v_ref, kc_ref, vc_ref, bias_ref, *rest):
    o_ref = rest[n_riders]
    vt_ref = rest[2 * n_riders + 1]
    _run_riders(rest[:n_riders], rest[n_riders + 1:2 * n_riders + 1])
    kc = kc_ref[...]
    zero_tile = jnp.zeros((GRID_W, KEY_TILE), BF16)
    for c in range(SEQ // QBLK):
        vt_ref[:, c * QBLK:(c + 1) * QBLK] = v_ref[c * QBLK:(c + 1) * QBLK, :].T
    vct = vc_ref[...].T

    def one_block(q0, k0, plan):
        s = _dot_nt(qr_ref[pl.ds(q0, QBLK), :], k_ref[pl.ds(k0, KBLK), :])
        sc = _dot_nt(q_ref[pl.ds(q0, QBLK), :], kc)
        p_rows, pc_rows, denoms = [], [], []
        for u in range(QBLK_ROWS):
            rows = slice(u * GRID_W, (u + 1) * GRID_W)
            band = {c: s[rows, c * KEY_TILE:(c + 1) * KEY_TILE] + bias_ref[idx]
                    for c, idx in enumerate(plan[u]) if idx is not None}
            ctx_tiles = [sc[rows, c * KEY_TILE:(c + 1) * KEY_TILE] for c in range(CTX_LEN // KEY_TILE)]
            tiles = list(band.values()) + ctx_tiles
            m = jnp.max(functools.reduce(jnp.maximum, tiles), axis=-1, keepdims=True)
            p_band = {c: jnp.exp(t - m) for c, t in band.items()}
            p_ctx = [jnp.exp(t - m) for t in ctx_tiles]
            total = functools.reduce(jnp.add, list(p_band.values()) + p_ctx)
            denoms.append(jnp.sum(total, axis=-1, keepdims=True))
            p_rows.append(jnp.concatenate(
                [p_band[c].astype(BF16) if c in p_band else zero_tile for c in range(KEY_TILES)], axis=1))
            pc_rows.append(jnp.concatenate([t.astype(BF16) for t in p_ctx], axis=1))
        p = jnp.concatenate(p_rows, axis=0)
        pc = jnp.concatenate(pc_rows, axis=0)
        o_t = _dot_nt(jnp.concatenate([vt_ref[:, k0:k0 + KBLK], vct], axis=1),
                      jnp.concatenate([p, pc], axis=1))
        o_ref[pl.ds(q0, QBLK), :] = (o_t.T / jnp.concatenate(denoms, axis=0)).astype(o_ref.dtype)

    for b in range(N_QBLK):
        first_row = b * QBLK_ROWS
        one_block(b * QBLK, _kblk_start(first_row) * GRID_W, _bias_plan(first_row))


def _attn_call(q, qr, k, v, kc, vc, bias, layer, riders):
    seq_blk = lambda b, h: (b, h)
    n_steps = BATCH * NA_HEADS
    step = lambda b, h: b * NA_HEADS + h
    r_in, r_out, r_shape, r_bytes = _rider_specs(riders, n_steps, step)
    nbytes = (2 * (5 * SEQ * HEAD_DIM * 2 + 2 * CTX_LEN * HEAD_DIM * 2 + N_BIAS * GRID_W * KEY_TILE * 4)
              + 8 * QBLK * (KBLK + CTX_LEN) * 4 + r_bytes)
    res = pl.pallas_call(
        functools.partial(_attn_kernel, len(riders)),
        grid=(BATCH, NA_HEADS),
        in_specs=[
            pl.BlockSpec((SEQ, HEAD_DIM), seq_blk),
            pl.BlockSpec((SEQ, HEAD_DIM), seq_blk),
            pl.BlockSpec((SEQ, HEAD_DIM), seq_blk),
            pl.BlockSpec((SEQ, HEAD_DIM), seq_blk),
            pl.BlockSpec((CTX_LEN, HEAD_DIM), seq_blk),
            pl.BlockSpec((CTX_LEN, HEAD_DIM), seq_blk),
            pl.BlockSpec((None, None, N_BIAS, GRID_W, KEY_TILE), lambda b, h: (layer, h, 0, 0, 0)),
        ] + r_in,
        out_specs=[pl.BlockSpec((SEQ, HEAD_DIM), seq_blk)] + r_out,
        out_shape=[jax.ShapeDtypeStruct((BATCH * SEQ, NA_W), BF16)] + r_shape,
        scratch_shapes=[pltpu.VMEM((HEAD_DIM, SEQ), BF16)],
        compiler_params=_params(("arbitrary", "arbitrary"), nbytes),
        name="neighbourhood_attention",
    )(q, qr, k, v, kc, vc, bias, *[r[0] for r in riders])
    return res[0], res[1:]


def _ctx_attn_kernel(q_ref, k_ref, v_ref, o_ref):
    s = _dot_nt(q_ref[...], k_ref[...])
    m = jnp.max(s, axis=-1, keepdims=True)
    p = jnp.exp(s - m)
    denom = jnp.sum(p, axis=-1, keepdims=True)
    o_ref[...] = (_dot(p.astype(BF16), v_ref[...]) / denom).astype(o_ref.dtype)


def _ctx_attn_call(q, k, v):
    blk = pl.BlockSpec((CTX_LEN, HEAD_DIM), lambda b, h: (b, h))
    return pl.pallas_call(
        _ctx_attn_kernel,
        grid=(BATCH, NA_HEADS),
        in_specs=[blk, blk, blk],
        out_specs=blk,
        out_shape=jax.ShapeDtypeStruct((BATCH * CTX_LEN, NA_W), BF16),
        compiler_params=_params(("arbitrary", "arbitrary"), 16 << 20),
        name="context_attention",
    )(q, k, v)


def _attn_bias_table(rpb):
    col = np.arange(GRID_W)
    col_start = np.clip(col - WIN_W // 2, 0, GRID_W - WIN_W)
    in_win = (col[None, :] >= col_start[:, None]) & (col[None, :] < col_start[:, None] + WIN_W)
    dc = np.clip(col[None, :] - col[:, None] + (WIN_W - 1), 0, 2 * WIN_W - 2)
    onehot = (in_win[None] & (dc[None] == np.arange(2 * WIN_W - 1)[:, None, None])).astype(np.float32)
    t = jnp.einsum('lhdj,jqk->lhdqk', rpb, jnp.asarray(onehot), precision=lax.Precision.HIGHEST)
    t = jnp.where(in_win[None, None, None], t, NEG_INF)
    masked = jnp.full((DEPTH, NA_HEADS, GRID_W, GRID_W), NEG_INF, F32)
    tiles = []
    for kind, dr in BIAS_TABLE:
        left = masked if kind == BIAS_SECOND else t[:, :, dr]
        right = masked if kind == BIAS_FIRST else t[:, :, dr + 1 if kind == BIAS_BOTH else dr]
        tiles.append(jnp.concatenate([left, right], axis=-1))
    return jnp.stack(tiles, axis=2)


def _rope_tables():
    assert GRID_H == GRID_W
    quarter = HEAD_DIM // 4
    inv = np.float32(ROPE_THETA) ** (-np.arange(quarter, dtype=np.float32) / np.float32(quarter))
    ang = (np.arange(GRID_W, dtype=np.float32)[:, None] * inv).astype(np.float64)
    cos_g, sin_g = jnp.asarray(np.cos(ang), F32), jnp.asarray(np.sin(ang), F32)
    by_row = lambda g: jnp.broadcast_to(g[:, None, :], (GRID_H, GRID_W, quarter)).reshape(SEQ, quarter)
    by_col = lambda g: jnp.broadcast_to(g[None, :, :], (GRID_H, GRID_W, quarter)).reshape(SEQ, quarter)
    cos = jnp.concatenate([by_row(cos_g), by_row(cos_g), by_col(cos_g), by_col(cos_g)], -1)
    sin = jnp.concatenate([-by_row(sin_g), by_row(sin_g), -by_col(sin_g), by_col(sin_g)], -1)
    return cos, sin


HALO = V7X_SUBLANES
N_SEG = V7X_SUBLANES
SEG_PAD = V7X_SUBLANES
LRU_SCAN_UNROLL = 8


def _lru_coeffs(xp_ref, n_rows, cw_ref, cb_ref, w4, b4_ref, sp, a_refs, u_refs):
    seg = n_rows // N_SEG
    pitch = seg + SEG_PAD
    for s in range(N_SEG):
        base = HALO + s * seg
        xc = cb_ref[...] + xp_ref[base - CONV_W // 2:base - CONV_W // 2 + seg, :] * cw_ref[0:1, :]
        for j in range(1, CONV_W):
            off = base - CONV_W // 2 + j
            xc = xc + xp_ref[off:off + seg, :] * cw_ref[j:j + 1, :]
        th = jnp.tanh(_dot(xc.astype(BF16), w4) + b4_ref[...])
        half_xc = 0.5 * xc
        for d in range(2):
            r2 = th[:, (2 * d) * LRU_BW:(2 * d + 1) * LRU_BW] + 1.0
            i2 = th[:, (2 * d + 1) * LRU_BW:(2 * d + 2) * LRU_BW] + 1.0
            log_a = r2 * sp[d:d + 1, :]
            a = jnp.exp(log_a)
            a_refs[d][s * pitch:s * pitch + seg, :] = a
            one_minus_a2 = -jnp.tanh(log_a) * (a * a + 1.0)
            root = jnp.where(one_minus_a2 == 0.0, 0.0, one_minus_a2 * lax.rsqrt(one_minus_a2))
            u_refs[d][s * pitch:s * pitch + seg, :] = root * (i2 * half_xc)


def _lru_local_scan(n_rows, coef_f, coef_b, state_f, state_b):
    seg = n_rows // N_SEG
    pitch = seg + SEG_PAD
    zero = jnp.zeros((N_SEG, LRU_BW), F32)
    one = jnp.ones((N_SEG, LRU_BW), F32)

    def step(coef, state, row, h, p):
        rows = pl.ds(row, N_SEG, stride=pitch)
        a = coef[0][rows, :]
        h = a * h + coef[1][rows, :]
        p = p * a
        state[0][rows, :] = p
        state[1][rows, :] = h
        return h, p

    def body(i, carry):
        hf, pf, hb, pb = carry
        for j in range(LRU_SCAN_UNROLL):
            t = i * LRU_SCAN_UNROLL + j
            hf, pf = step(coef_f, state_f, t, hf, pf)
            hb, pb = step(coef_b, state_b, seg - 1 - t, hb, pb)
        return hf, pf, hb, pb

    lax.fori_loop(0, seg // LRU_SCAN_UNROLL, body, (zero, one, zero, one))


def _lru_carries(n_rows, h_in_f, h_in_b, af, uf, ab, ub):
    seg = n_rows // N_SEG
    pitch = seg + SEG_PAD
    cf, cb = [h_in_f], [h_in_b]
    for s in range(N_SEG):
        last = s * pitch + seg - 1
        cf.append(uf[last:last + 1, :] + af[last:last + 1, :] * cf[-1])
        first = (N_SEG - 1 - s) * pitch
        cb.append(ub[first:first + 1, :] + ab[first:first + 1, :] * cb[-1])
    return cf[:N_SEG], cb[:N_SEG][::-1], cf[N_SEG], cb[N_SEG]


def _lru_emit(n_rows, cf, cb, af, uf, ab, ub, g_ref, o_ref):
    seg = n_rows // N_SEG
    pitch = seg + SEG_PAD
    for s in range(N_SEG):
        src = slice(s * pitch, s * pitch + seg)
        dst = slice(s * seg, (s + 1) * seg)
        y = (uf[src, :] + af[src, :] * cf[s]) + (ub[src, :] + ab[src, :] * cb[s])
        o_ref[dst, :] = (y * g_ref[dst, :]).astype(o_ref.dtype)


def _lru_kernel(ctx_out, n_riders, x_ref, g_ref, xc_ref, *refs):
    refs = list(refs)
    gc_ref = refs.pop(0) if ctx_out else None
    cw_ref, cb_ref, w4_ref, b4_ref, lam_ref = refs[:5]
    rider_in, refs = refs[5:5 + n_riders], refs[5 + n_riders:]
    o_ref = refs.pop(0)
    oc_ref = refs.pop(0) if ctx_out else None
    rider_out, refs = refs[:n_riders], refs[n_riders:]
    xp_ref, af, uf, ab, ub, pf, sf, pb, sb = refs
    _run_riders(rider_in, rider_out)

    lam = lam_ref[...]
    z = -lam
    sp = (-0.5 * LRU_C) * (jnp.maximum(z, 0.0) + jnp.log1p(jnp.exp(-jnp.abs(z))))
    w4 = w4_ref[...].astype(BF16)
    zeros_halo = jnp.zeros((HALO, LRU_BW), F32)
    h0 = jnp.zeros((1, LRU_BW), F32)

    xp_ref[0:HALO, :] = zeros_halo
    xp_ref[HALO:HALO + CTX_LEN, :] = xc_ref[...]
    xp_ref[HALO + CTX_LEN:2 * HALO + CTX_LEN, :] = zeros_halo
    _lru_coeffs(xp_ref, CTX_LEN, cw_ref, cb_ref, w4, b4_ref, sp, (af, ab), (uf, ub))
    _lru_local_scan(CTX_LEN, (af, uf), (ab, ub), (pf, sf), (pb, sb))
    cf, cb, hf, hb = _lru_carries(CTX_LEN, h0, h0, pf, sf, pb, sb)
    if ctx_out:
        _lru_emit(CTX_LEN, cf, cb, pf, sf, pb, sb, gc_ref, oc_ref)

    xp_ref[HALO:HALO + SEQ, :] = x_ref[...]
    xp_ref[HALO + SEQ:2 * HALO + SEQ, :] = zeros_halo
    _lru_coeffs(xp_ref, SEQ, cw_ref, cb_ref, w4, b4_ref, sp, (af, ab), (uf, ub))
    _lru_local_scan(SEQ, (af, uf), (ab, ub), (pf, sf), (pb, sb))
    cf, cb, _, _ = _lru_carries(SEQ, hf, hb, pf, sf, pb, sb)
    _lru_emit(SEQ, cf, cb, pf, sf, pb, sb, g_ref, o_ref)


def _lru_call(ctx_out, xl, gl, xc, gc, conv_w3, conv_b3, w4, b4, lam3, layer, riders):
    lat = pl.BlockSpec((SEQ, LRU_BW), lambda b, j: (b, j))
    cx = pl.BlockSpec((CTX_LEN, LRU_BW), lambda b, j: (b, j))
    r_in, r_out, r_shape, r_bytes = _rider_specs(riders, BATCH * LRU_BLOCKS, lambda b, j: b * LRU_BLOCKS + j)
    in_specs = [lat, lat, cx] + ([cx] if ctx_out else []) + [
        pl.BlockSpec((None, CONV_W, LRU_BW), lambda b, j: (layer, 0, j)),
        pl.BlockSpec((None, 1, LRU_BW), lambda b, j: (layer, 0, j)),
        pl.BlockSpec((None, None, LRU_BW, 4 * LRU_BW), lambda b, j: (layer, j, 0, 0)),
        pl.BlockSpec((None, None, 1, 4 * LRU_BW), lambda b, j: (layer, j, 0, 0)),
        pl.BlockSpec((None, 2, LRU_BW), lambda b, j: (layer, 0, j)),
    ] + r_in
    out_specs = [lat] + ([cx] if ctx_out else []) + r_out
    out_shape = [jax.ShapeDtypeStruct((BATCH * SEQ, LRU_W), BF16)]
    if ctx_out:
        out_shape.append(jax.ShapeDtypeStruct((BATCH * CTX_LEN, LRU_W), BF16))
    out_shape += r_shape
    seq_bytes = SEQ * LRU_BW * 4
    args = ([xl, gl, xc] + ([gc] if ctx_out else []) + [conv_w3, conv_b3, w4, b4, lam3]
            + [r[0] for r in riders])
    res = pl.pallas_call(
        functools.partial(_lru_kernel, ctx_out, len(riders)),
        grid=(BATCH, LRU_BLOCKS),
        in_specs=in_specs,
        out_specs=out_specs,
        out_shape=out_shape,
        scratch_shapes=([pltpu.VMEM((SEQ + 2 * HALO, LRU_BW), F32)]
                        + [pltpu.VMEM((SEQ + N_SEG * SEG_PAD, LRU_BW), F32)] * 8),
        compiler_params=_params(("arbitrary", "arbitrary"), 15 * seq_bytes + (8 << 20) + r_bytes),
        name="rglru",
    )(*args)
    n_main = 2 if ctx_out else 1
    return res[0], (res[1] if ctx_out else None), res[n_main:]


def _fourier_kernel(n_pos, blk, f_ref, ch_ref, sh_ref, cc_ref, sc_ref, rev_ref, alt_ref, w_ref, b_ref, o_ref,
                    ec_ref, es_ref, mir_ref, mid_ref):
    half = n_pos // 2
    n_lo = half // blk
    step = pl.program_id(1)
    norm = 1.0 / math.sqrt(n_pos * FNET_GW)
    w = w_ref[...].astype(BF16)

    def linear(y):
        return (_dot(y.astype(BF16), w) + b_ref[...]).astype(o_ref.dtype)

    @pl.when(step == 0)
    def _():
        for i in range(n_lo):
            lo = blk * (2 * n_lo - 1 - i)
            if i == 0:
                mirrored = _dot(rev_ref[:, 0:blk], f_ref[lo:lo + blk, :])
            else:
                mirrored = _dot(rev_ref[...], f_ref[lo:lo + 2 * blk, :])
            rows = slice(i * blk, (i + 1) * blk)
            x = f_ref[rows, :].astype(F32)
            even = (x + mirrored).astype(BF16)
            odd = (x - mirrored).astype(BF16)
            for g in range(FNET_GROUPS):
                sl = slice(g * FNET_GW, (g + 1) * FNET_GW)
                ec_ref[rows, sl] = _dot(even[:, sl], cc_ref[...]).astype(BF16)
                es_ref[rows, sl] = _dot(odd[:, sl], sc_ref[...]).astype(BF16)
        for g in range(FNET_GROUPS):
            sl = slice(g * FNET_GW, (g + 1) * FNET_GW)
            mid_ref[0:BF16_ROWS, sl] = _dot(f_ref[half:half + BF16_ROWS, sl], cc_ref[...])
        mid_ref[BF16_ROWS:2 * BF16_ROWS, :] = _dot(alt_ref[...], ec_ref[...])

    mid = mid_ref[0:1, :]

    @pl.when(step < n_lo)
    def _():
        a = _dot(ch_ref[...], ec_ref[...])
        b = _dot(sh_ref[...], es_ref[...])
        row = lax.broadcasted_iota(jnp.int32, (blk, 1), 0)
        base = jnp.where(row % 2 == 0, 1.0, -1.0) * mid
        o_ref[...] = linear((a - b + base) * norm)
        r0 = pl.multiple_of(step * blk, blk)
        mir_ref[pl.ds(r0, blk), :] = ((a + b + base) * norm).astype(BF16)

    @pl.when(step == n_lo)
    def _():
        y = _dot(rev_ref[:, 0:blk], mir_ref[(n_lo - 1) * blk:n_lo * blk, :])
        nyquist = (mid_ref[BF16_ROWS:BF16_ROWS + 1, :] + mid) * norm
        row = lax.broadcasted_iota(jnp.int32, (blk, 1), 0)
        o_ref[...] = linear(jnp.where(row == 0, nyquist, y))

    if n_lo > 1:
        @pl.when(step > n_lo)
        def _():
            r0 = pl.multiple_of((2 * n_lo - 1 - step) * blk, blk)
            o_ref[...] = linear(_dot(rev_ref[...], mir_ref[pl.ds(r0, 2 * blk), :]))


DFT_SPLIT = 64


def _dft_matrices(n, size):
    t = np.arange(size, dtype=np.int64)

    def table(k):
        ang = (2.0 * np.pi / n) * ((k[:, None] * t[None, :]) % n).astype(np.float64)
        return jnp.asarray(np.cos(ang), F32), jnp.asarray(np.sin(ang), F32)

    if size <= DFT_SPLIT:
        c, s = table(t)
        return c.astype(BF16), s.astype(BF16)
    c1, s1 = table(DFT_SPLIT * np.arange(size // DFT_SPLIT, dtype=np.int64))
    c2, s2 = table(np.arange(DFT_SPLIT, dtype=np.int64))
    c = c1[:, None, :] * c2[None, :, :] - s1[:, None, :] * s2[None, :, :]
    s = s1[:, None, :] * c2[None, :, :] + c1[:, None, :] * s2[None, :, :]
    return c.reshape(size, size).astype(BF16), s.reshape(size, size).astype(BF16)


FOURIER_BLK = 256


def _fourier_call(f2d, n_pos, fno_w, fno_b3, layer):
    half = n_pos // 2
    blk = min(FOURIER_BLK, half)
    steps = n_pos // blk
    n_lo = half // blk
    ch, sh = _dft_matrices(n_pos, half)
    cc, sc = _dft_matrices(FNET_GW, FNET_GW)
    rev = np.zeros((blk, 2 * blk), np.float32)
    rev[np.arange(1, blk), blk - np.arange(1, blk)] = 1.0
    rev[0, blk] = 1.0
    alt = np.zeros((BF16_ROWS, half), np.float32)
    alt[0] = 1.0 - 2.0 * (np.arange(half) % 2)
    const = lambda b, k: (0, 0)
    dft_tile = lambda b, k: (jnp.minimum(k, n_lo - 1), 0)
    nbytes = (2 * n_pos * FNET_W * 2 + 2 * 2 * blk * half * 2 + 3 * half * FNET_W * 2 + 2 * FNET_W * FNET_W * 4
              + 8 * blk * FNET_W * 4 + 2 * blk * 2 * blk * 2)
    return pl.pallas_call(
        functools.partial(_fourier_kernel, n_pos, blk),
        grid=(BATCH, steps),
        in_specs=[
            pl.BlockSpec((n_pos, FNET_W), lambda b, k: (b, 0)),
            pl.BlockSpec((blk, half), dft_tile),
            pl.BlockSpec((blk, half), dft_tile),
            pl.BlockSpec((FNET_GW, FNET_GW), const),
            pl.BlockSpec((FNET_GW, FNET_GW), const),
            pl.BlockSpec((blk, 2 * blk), const),
            pl.BlockSpec((BF16_ROWS, half), const),
            pl.BlockSpec((None, FNET_W, FNET_W), lambda b, k: (layer, 0, 0)),
            pl.BlockSpec((None, 1, FNET_W), lambda b, k: (layer, 0, 0)),
        ],
        out_specs=pl.BlockSpec((blk, FNET_W), lambda b, k: (b * steps + k, 0)),
        out_shape=jax.ShapeDtypeStruct((BATCH * n_pos, FNET_W), BF16),
        scratch_shapes=[pltpu.VMEM((half, FNET_W), BF16), pltpu.VMEM((half, FNET_W), BF16),
                        pltpu.VMEM((half, FNET_W), BF16), pltpu.VMEM((2 * BF16_ROWS, FNET_W), F32)],
        compiler_params=_params(("arbitrary", "arbitrary"), nbytes),
        name="fourier_mix",
    )(f2d, ch, sh, cc, sc, jnp.asarray(rev, BF16), jnp.asarray(alt, BF16), fno_w, fno_b3)


def _outproj_kernel(carry_mod, na_ref, lru_ref, f_ref, res_ref, mod_ref, w_ref, g_ref, b_ref, *rest):
    if carry_mod:
        s_ref, wm_ref, bm_ref, o_ref, mo_ref = rest
        _mod_kernel(s_ref, wm_ref, bm_ref, mo_ref)
    else:
        (o_ref,) = rest
    gate = mod_ref[2:3, :]
    for rows in _row_tiles(res_ref.shape[0], MM_SUB):
        y = (_dot(na_ref[rows, :], w_ref[0:NA_W, :])
             + _dot(lru_ref[rows, :], w_ref[NA_W:NA_W + LRU_W, :])
             + _dot(f_ref[rows, :], w_ref[NA_W + LRU_W:D_MODEL, :]))
        _residual_ln_store(res_ref, y, gate, g_ref[...], b_ref[...], o_ref, rows)


def _outproj_call(na, lru, f, res, mod4, mod_row, w_out, layer, ln_g, ln_b, next_mod=None):
    m_rows = res.shape[0]
    tm = 512
    steps = m_rows // tm
    row = lambda m: (m, 0)
    nbytes = (2 * tm * D_MODEL * 2 + 2 * 2 * tm * D_MODEL * 4 + D_MODEL * D_MODEL * 2 + 2 * MM_SUB * D_MODEL * 4
              + 4 * ROW_CHUNK * D_MODEL * 4)
    in_specs = [
        pl.BlockSpec((tm, NA_W), row),
        pl.BlockSpec((tm, LRU_W), row),
        pl.BlockSpec((tm, FNET_W), row),
        pl.BlockSpec((tm, D_MODEL), row),
        pl.BlockSpec((None, None, 6, D_MODEL), lambda m: (0, mod_row(m * tm), 0, 0)),
        pl.BlockSpec((D_MODEL, D_MODEL), lambda m: (0, 0), pipeline_mode=pl.Buffered(1)),
        pl.BlockSpec((None, 1, D_MODEL), lambda m: (layer, 0, 0)),
        pl.BlockSpec((None, 1, D_MODEL), lambda m: (layer, 0, 0)),
    ]
    out_specs = [pl.BlockSpec((tm, D_MODEL), row)]
    out_shape = [jax.ShapeDtypeStruct((m_rows, D_MODEL), F32)]
    args = [na, lru, f, res, mod4, w_out, ln_g, ln_b]
    if next_mod is not None:
        s_in, w_mod, b_mod3, mod_layer = next_mod
        n_out = w_mod.shape[-1]
        slab = n_out // steps
        in_specs += [
            pl.BlockSpec((MOD_ROWS, D_MODEL), lambda m: (0, 0)),
            pl.BlockSpec((None, D_MODEL, slab), lambda m: (mod_layer, 0, m)),
            pl.BlockSpec((None, 1, slab), lambda m: (mod_layer, 0, m)),
        ]
        out_specs.append(pl.BlockSpec((MOD_ROWS, slab), lambda m: (0, m)))
        out_shape.append(jax.ShapeDtypeStruct((MOD_ROWS, n_out), F32))
        args += [s_in, w_mod, b_mod3]
        nbytes += 2 * D_MODEL * slab * 4 + D_MODEL * slab * 2
    res = pl.pallas_call(
        functools.partial(_outproj_kernel, next_mod is not None),
        grid=(steps,),
        in_specs=in_specs,
        out_specs=out_specs,
        out_shape=out_shape,
        compiler_params=_params(("arbitrary",), nbytes),
        name="outproj_residual",
    )(*args)
    return (res[0], res[1]) if next_mod is not None else (res[0], None)


MLP_TF = 512
MLP_TILES = D_FF // MLP_TF


def _mlp_kernel(x_ref, mod_ref, w1_ref, b1_ref, w2_ref, b2_ref, g_ref, b_ref, o_ref, v_ref, h0_ref, h1_ref):
    j = pl.program_id(1)
    tm = x_ref.shape[0]

    def up(h_out, rows=slice(None)):
        h = _dot(v_ref[rows, :], w1_ref[...]) + b1_ref[...]
        h_out[rows, :] = jnp.square(jnp.maximum(h, 0.0)).astype(BF16)

    def down(h_in, first):
        for c in range(D_MODEL // COL_TILE):
            sl = slice(c * COL_TILE, (c + 1) * COL_TILE)
            part = _dot(h_in[...], w2_ref[:, sl])
            if first:
                o_ref[:, sl] = part
            else:
                o_ref[:, sl] += part

    @pl.when(j == 0)
    def _():
        shift = mod_ref[3:4, :]
        scale1 = 1.0 + mod_ref[4:5, :]
        for rows in _row_tiles(tm, MM_SUB):
            _ln_mod_rows(x_ref, shift, scale1, v_ref, rows)
            up(h0_ref, rows)

    @pl.when(j == 1)
    def _():
        down(h0_ref, True)
        up(h1_ref)

    @pl.when((j > 1) & (j < MLP_TILES) & (j % 2 == 0))
    def _():
        down(h1_ref, False)
        up(h0_ref)

    @pl.when((j > 1) & (j < MLP_TILES) & (j % 2 == 1))
    def _():
        down(h0_ref, False)
        up(h1_ref)

    @pl.when(j == MLP_TILES)
    def _():
        h_last = h1_ref if (MLP_TILES - 1) % 2 else h0_ref
        gate = mod_ref[5:6, :]
        for rows in _row_tiles(tm, MM_SUB):
            y = o_ref[rows, :] + _dot(h_last[rows, :], w2_ref[...]) + b2_ref[...]
            _residual_ln_store(x_ref, y, gate, g_ref[...], b_ref[...], o_ref, rows)


def _mlp_call(x1, mod4, mod_row, tm, w1, b1, w2, b2, layer, ln_g, ln_b):
    m_rows = x1.shape[0]
    row = lambda m, j: (m, 0)
    vec = lambda m, j: (layer, 0, 0)
    nbytes = (4 * tm * D_MODEL * 4 + tm * D_MODEL * 2 + 2 * 2 * D_MODEL * MLP_TF * 2 + 2 * tm * MLP_TF * 2
              + tm * MLP_TF * 4 + tm * COL_TILE * 4 + 2 * MM_SUB * D_MODEL * 4 + 4 * ROW_CHUNK * D_MODEL * 4)
    return pl.pallas_call(
        _mlp_kernel,
        grid=(m_rows // tm, MLP_TILES + 1),
        in_specs=[
            pl.BlockSpec((tm, D_MODEL), row),
            pl.BlockSpec((None, None, 6, D_MODEL), lambda m, j: (0, mod_row(m * tm), 0, 0)),
            pl.BlockSpec((D_MODEL, MLP_TF), lambda m, j: (0, jnp.minimum(j, MLP_TILES - 1))),
            pl.BlockSpec((None, 1, MLP_TF), lambda m, j: (layer, 0, jnp.minimum(j, MLP_TILES - 1))),
            pl.BlockSpec((MLP_TF, D_MODEL), lambda m, j: (jnp.maximum(j - 1, 0), 0)),
            pl.BlockSpec((None, 1, D_MODEL), vec),
            pl.BlockSpec((None, 1, D_MODEL), vec),
            pl.BlockSpec((None, 1, D_MODEL), vec),
        ],
        out_specs=pl.BlockSpec((tm, D_MODEL), row),
        out_shape=jax.ShapeDtypeStruct((m_rows, D_MODEL), F32),
        scratch_shapes=[pltpu.VMEM((tm, D_MODEL), BF16), pltpu.VMEM((tm, MLP_TF), BF16),
                        pltpu.VMEM((tm, MLP_TF), BF16)],
        compiler_params=_params(("arbitrary", "arbitrary"), nbytes),
        name="mlp_residual",
    )(x1, mod4, w1, b1, w2, b2, ln_g, ln_b)


def kernel(x, c, ctx, c_ctx, w_mod, b_mod, w_in, rpb, conv_w, conv_b, lru_wa, lru_ba, lru_wx, lru_bx, lru_lambda,
           fno_w, fno_b, w_out, ln1_g, ln1_b, w_fc1, b_fc1, w_fc2, b_fc2, ln2_g, ln2_b):
    xl = x.reshape(BATCH * SEQ, D_MODEL)
    xc = ctx.reshape(BATCH * CTX_LEN, D_MODEL)
    s_in = jnp.concatenate([c, c_ctx[None], jnp.zeros((MOD_ROWS - BATCH - 1, D_MODEL), F32)], 0)
    b_mod3 = b_mod.reshape(DEPTH, 1, 6 * D_MODEL)
    as_mod4 = lambda m: m.reshape(1, MOD_ROWS, 6, D_MODEL)
    mod4 = as_mod4(_mod_call(s_in, w_mod, b_mod3, 0))
    cos_t, sin_t = _rope_tables()
    bias_tab = _attn_bias_table(rpb)
    w_in_l = None
    vec3 = lambda a: a.reshape(DEPTH, 1, a.shape[-1])
    ln1_g3, ln1_b3, ln2_g3, ln2_b3 = vec3(ln1_g), vec3(ln1_b), vec3(ln2_g), vec3(ln2_b)
    b_fc1_3, b_fc2_3 = vec3(b_fc1), vec3(b_fc2)
    conv_b3, fno_b3 = vec3(conv_b), vec3(fno_b)
    w4 = 0.5 * jnp.concatenate([lru_wa[:, 0], lru_wx[:, 0], lru_wa[:, 1], lru_wx[:, 1]], -1)
    blk = lambda a: a.reshape(DEPTH, LRU_BLOCKS, 1, LRU_BW)
    b4 = 0.5 * jnp.concatenate([blk(lru_ba[:, 0]), blk(lru_bx[:, 0]), blk(lru_ba[:, 1]), blk(lru_bx[:, 1])], -1)
    lat_row = lambda r0: r0 // SEQ
    ctx_row = lambda r0: CTX_MOD_ROW

    for layer in range(DEPTH):
        ctx_out = layer < DEPTH - 1
        if ctx_out and w_in_l is None:
            qc, kc, vc, xoc, goc, fc, w_in_l = _inproj_ctx_call(xc, mod4, w_in, True, cast_layer=layer)
        else:
            if w_in_l is None:
                w_in_l = w_in[layer].astype(BF16)
            if ctx_out:
                qc, kc, vc, xoc, goc, fc = _inproj_ctx_call(xc, mod4, w_in_l, True)
            else:
                kc, vc, xoc = _inproj_ctx_call(xc, mod4, w_in_l, False)
                goc = None
        q, qr, k, v, xo, go, f = _inproj_lat_call(xl, mod4, w_in_l, cos_t, sin_t)

        na, (w_out_l, w_fc1_l, w_fc2_l) = _attn_call(
            q, qr, k, v, kc, vc, bias_tab, layer,
            [_rider(w_out, layer, 0), _rider(w_fc1, layer, 1), _rider(w_fc2, layer, 0)])

        next_w_in = [_rider(w_in, layer + 1, 0)] if layer + 1 < DEPTH else []
        lru, lru_c, cast = _lru_call(ctx_out, xo, go, xoc, goc, conv_w, conv_b3, w4, b4, lru_lambda, layer,
                                     next_w_in)
        if next_w_in:
            w_in_l = cast[0]

        fm = _fourier_call(f, SEQ, fno_w, fno_b3, layer)
        next_mod = (s_in, w_mod, b_mod3, layer + 1) if layer + 1 < DEPTH else None
        x1, mod_next = _outproj_call(na, lru, fm, xl, mod4, lat_row, w_out_l, layer, ln1_g3, ln1_b3, next_mod)
        xl = _mlp_call(x1, mod4, lat_row, 1024, w_fc1_l, b_fc1_3, w_fc2_l, b_fc2_3, layer, ln2_g3, ln2_b3)

        if ctx_out:
            na_c = _ctx_attn_call(qc, kc, vc)
            fm_c = _fourier_call(fc, CTX_LEN, fno_w, fno_b3, layer)
            c1, _ = _outproj_call(na_c, lru_c, fm_c, xc, mod4, ctx_row, w_out_l, layer, ln1_g3, ln1_b3)
            xc = _mlp_call(c1, mod4, ctx_row, 512, w_fc1_l, b_fc1_3, w_fc2_l, b_fc2_3, layer, ln2_g3, ln2_b3)
        if mod_next is not None:
            mod4 = as_mod4(mod_next)

    return xl.reshape(BATCH, SEQ, D_MODEL)
```

```python
import functools
import math

import jax
import jax.numpy as jnp
import numpy as np
from jax import lax
from jax.experimental import pallas as pl
from jax.experimental.pallas import tpu as pltpu

F32 = jnp.float32
BF16 = jnp.bfloat16

D_MODEL = 2048
BATCH = 2
SEQ = 4096
DEPTH = 2
GRID_W = 64
GRID_H = SEQ // GRID_W
CTX_LEN = 256
HEAD_DIM = 128
NA_W = D_MODEL // 2
NA_HEADS = NA_W // HEAD_DIM
WIN_H = 8
WIN_W = 16
LRU_W = D_MODEL // 4
LRU_BLOCKS = 4
LRU_BW = LRU_W // LRU_BLOCKS
CONV_W = 4
LRU_C = 8.0
FNET_W = D_MODEL // 4
FNET_GROUPS = 4
FNET_GW = FNET_W // FNET_GROUPS
IN_W = 3 * NA_W + 2 * LRU_W + FNET_W
D_FF = 4 * D_MODEL
ROPE_THETA = 10000.0
LN_EPS = 1e-5
NEG_INF = -1e30
ALPHA = (2.0 * DEPTH) ** 0.25
ATTN_SCALE = HEAD_DIM ** -0.5

V7X_LANES = 128
V7X_SUBLANES = 8
BF16_ROWS = 2 * V7X_SUBLANES
V7X_VMEM_BYTES = 64 * 1024 * 1024
V7X_VMEM_RESERVED_BYTES = 6 * 1024 * 1024
VMEM_CEILING = V7X_VMEM_BYTES - V7X_VMEM_RESERVED_BYTES

COL_TILE = 512
N_COL_TILES = IN_W // COL_TILE
ROW_CHUNK = 128
MOD_ROWS = 8
CTX_MOD_ROW = BATCH


def _vmem_limit(nbytes):
    return int(min(VMEM_CEILING, nbytes * 5 // 4 + (4 << 20)))


def _params(semantics, nbytes):
    return pltpu.CompilerParams(dimension_semantics=semantics, vmem_limit_bytes=_vmem_limit(nbytes))


def _ln(x):
    mu = jnp.mean(x, axis=-1, keepdims=True)
    xc = x - mu
    var = jnp.mean(xc * xc, axis=-1, keepdims=True)
    return xc * lax.rsqrt(var + LN_EPS)


def _sigmoid(x):
    return 1.0 / (1.0 + jnp.exp(-x))


def _gelu_tanh(x):
    return 0.5 * x * (1.0 + jnp.tanh(math.sqrt(2.0 / math.pi) * (x + 0.044715 * (x * x * x))))


def _dot(a, b):
    return jnp.dot(a, b, preferred_element_type=F32)


def _dot_nt(a, b):
    return lax.dot_general(a, b, (((1,), (1,)), ((), ())), preferred_element_type=F32)


def _rider(w, layer, axis):
    return (w, layer, axis)


def _rider_specs(riders, n_steps, step_of):
    in_specs, out_specs, out_shapes, nbytes = [], [], [], 0
    for w, layer, axis in riders:
        rows, cols = w.shape[1:]
        if axis == 0:
            blk = (rows // n_steps, cols)
            in_idx = lambda *g, layer=layer: (layer, step_of(*g), 0)
            out_idx = lambda *g: (step_of(*g), 0)
        else:
            blk = (rows, cols // n_steps)
            in_idx = lambda *g, layer=layer: (layer, 0, step_of(*g))
            out_idx = lambda *g: (0, step_of(*g))
        in_specs.append(pl.BlockSpec((None,) + blk, in_idx))
        out_specs.append(pl.BlockSpec(blk, out_idx))
        out_shapes.append(jax.ShapeDtypeStruct((rows, cols), BF16))
        nbytes += 2 * blk[0] * blk[1] * (4 + 2)
    return in_specs, out_specs, out_shapes, nbytes


def _run_riders(in_refs, out_refs):
    for src, dst in zip(in_refs, out_refs):
        dst[...] = src[...].astype(dst.dtype)


MOD_TN = 1024


def _mod_kernel(s_ref, w_ref, b_ref, o_ref):
    s = s_ref[...]
    s = s * _sigmoid(s)
    o_ref[...] = _dot(s.astype(BF16), w_ref[...].astype(BF16)) + b_ref[...]


def _mod_call(s_in, w_mod, b_mod3, layer):
    n_out = w_mod.shape[-1]
    nbytes = 2 * (D_MODEL * MOD_TN * 4) + D_MODEL * MOD_TN * 2 + 4 * MOD_ROWS * n_out
    return pl.pallas_call(
        _mod_kernel,
        grid=(n_out // MOD_TN,),
        in_specs=[
            pl.BlockSpec((MOD_ROWS, D_MODEL), lambda n: (0, 0)),
            pl.BlockSpec((None, D_MODEL, MOD_TN), lambda n: (layer, 0, n)),
            pl.BlockSpec((None, 1, MOD_TN), lambda n: (layer, 0, n)),
        ],
        out_specs=pl.BlockSpec((MOD_ROWS, MOD_TN), lambda n: (0, n)),
        out_shape=jax.ShapeDtypeStruct((MOD_ROWS, n_out), F32),
        compiler_params=_params(("arbitrary",), nbytes),
        name="modulation",
    )(s_in, w_mod, b_mod3)


MM_SUB = 256


def _row_tiles(n_rows, size):
    size = min(size, n_rows)
    return [slice(r, r + size) for r in range(0, n_rows, size)]


def _ln_mod_rows(x_ref, shift, scale1, dst_ref, rows):
    for piece in _row_tiles(rows.stop - rows.start, ROW_CHUNK):
        sl = slice(rows.start + piece.start, rows.start + piece.stop)
        dst_ref[sl, :] = (_ln(x_ref[sl, :]) * scale1 + shift).astype(dst_ref.dtype)


def _residual_ln_store(res, y, gate, gain, bias, o_ref, rows):
    for piece in _row_tiles(rows.stop - rows.start, ROW_CHUNK):
        sl = slice(rows.start + piece.start, rows.start + piece.stop)
        z = ALPHA * res[sl, :] + gate * y[piece, :]
        o_ref[sl, :] = _ln(z) * gain + bias


def _rope(a, cos, sin):
    lane = lax.broadcasted_iota(jnp.int32, a.shape, 1)
    first = (lane % (HEAD_DIM // 2)) < (HEAD_DIM // 4)
    partner = jnp.where(first, pltpu.roll(a, HEAD_DIM - HEAD_DIM // 4, 1), pltpu.roll(a, HEAD_DIM // 4, 1))
    return a * cos + partner * sin


def _inproj_lat_kernel(x_ref, mod_ref, w_ref, cos_ref, sin_ref,
                       q_ref, qr_ref, k_ref, v_ref, xo_ref, go_ref, f_ref, xn_ref):
    shift = mod_ref[0:1, :]
    scale1 = 1.0 + mod_ref[1:2, :]

    def emit_q(rows, cols, acc):
        q_ref[rows, cols] = (acc * ATTN_SCALE).astype(q_ref.dtype)
        for h in range(COL_TILE // HEAD_DIM):
            sl = slice(h * HEAD_DIM, (h + 1) * HEAD_DIM)
            dst = slice(cols.start + sl.start, cols.start + sl.stop)
            rot = _rope(acc[:, sl], cos_ref[rows, :], sin_ref[rows, :])
            qr_ref[rows, dst] = (rot * ATTN_SCALE).astype(qr_ref.dtype)

    def emit_k(rows, cols, acc):
        for h in range(COL_TILE // HEAD_DIM):
            sl = slice(h * HEAD_DIM, (h + 1) * HEAD_DIM)
            dst = slice(cols.start + sl.start, cols.start + sl.stop)
            k_ref[rows, dst] = _rope(acc[:, sl], cos_ref[rows, :], sin_ref[rows, :]).astype(k_ref.dtype)

    def emit_to(ref, fn=lambda a: a):
        def emit(rows, cols, acc):
            ref[rows, cols] = fn(acc).astype(ref.dtype)
        return emit

    half = [slice(0, COL_TILE), slice(COL_TILE, 2 * COL_TILE)]
    plan = ([(emit_q, c) for c in half] + [(emit_k, c) for c in half] + [(emit_to(v_ref), c) for c in half]
            + [(emit_to(xo_ref), half[0]), (emit_to(go_ref, _gelu_tanh), half[0]), (emit_to(f_ref), half[0])])
    for rows in _row_tiles(x_ref.shape[0], MM_SUB):
        _ln_mod_rows(x_ref, shift, scale1, xn_ref, rows)
        for n, (emit, cols) in enumerate(plan):
            emit(rows, cols, _dot(xn_ref[rows, :], w_ref[:, n * COL_TILE:(n + 1) * COL_TILE]))


def _inproj_lat_call(x2d, mod4, w_in, cos_t, sin_t):
    m_rows = x2d.shape[0]
    tm = 512
    tiles_per_seq = SEQ // tm
    row = lambda m: (m, 0)
    nbytes = (2 * tm * D_MODEL * 4 + tm * D_MODEL * 2 + D_MODEL * IN_W * 2 + 4 * tm * HEAD_DIM * 4
              + 2 * tm * (4 * NA_W * 2 + 2 * LRU_W * 4 + FNET_W * 2) + 6 * MM_SUB * COL_TILE * 4
              + 4 * ROW_CHUNK * D_MODEL * 4)
    bf = lambda w: jax.ShapeDtypeStruct((m_rows, w), BF16)
    ff = lambda w: jax.ShapeDtypeStruct((m_rows, w), F32)
    widths = [NA_W, NA_W, NA_W, NA_W, LRU_W, LRU_W, FNET_W]
    return pl.pallas_call(
        _inproj_lat_kernel,
        grid=(m_rows // tm,),
        in_specs=[
            pl.BlockSpec((tm, D_MODEL), row),
            pl.BlockSpec((None, None, 6, D_MODEL), lambda m: (0, m // tiles_per_seq, 0, 0)),
            pl.BlockSpec((D_MODEL, IN_W), lambda m: (0, 0), pipeline_mode=pl.Buffered(1)),
            pl.BlockSpec((tm, HEAD_DIM), lambda m: (m % tiles_per_seq, 0)),
            pl.BlockSpec((tm, HEAD_DIM), lambda m: (m % tiles_per_seq, 0)),
        ],
        out_specs=[pl.BlockSpec((tm, w), row) for w in widths],
        out_shape=[bf(NA_W), bf(NA_W), bf(NA_W), bf(NA_W), ff(LRU_W), ff(LRU_W), bf(FNET_W)],
        scratch_shapes=[pltpu.VMEM((tm, D_MODEL), BF16)],
        compiler_params=_params(("arbitrary",), nbytes),
        name="inproj_latent",
    )(x2d, mod4, w_in, cos_t, sin_t)


def _inproj_ctx_kernel(tile_lo, with_q, with_gf, emit_w, x_ref, mod_ref, w_ref, *refs):
    refs = list(refs)
    xn_ref = refs.pop()
    wb_ref = refs.pop() if emit_w else None
    q_ref = refs.pop(0) if with_q else None
    k_ref, v_ref, xo_ref = refs[0], refs[1], refs[2]
    go_ref, f_ref = (refs[3], refs[4]) if with_gf else (None, None)
    n = pl.program_id(1) + tile_lo

    @pl.when(pl.program_id(1) == 0)
    def _():
        for rows in _row_tiles(x_ref.shape[0], MM_SUB):
            _ln_mod_rows(x_ref, mod_ref[0:1, :], 1.0 + mod_ref[1:2, :], xn_ref, rows)

    w = w_ref[...].astype(BF16)
    if emit_w:
        wb_ref[...] = w
    acc = _dot(xn_ref[...], w)

    if with_q:
        @pl.when(n < 2)
        def _():
            q_ref[...] = (acc * ATTN_SCALE).astype(q_ref.dtype)

    @pl.when((n >= 2) & (n < 4))
    def _():
        k_ref[...] = acc.astype(k_ref.dtype)

    @pl.when((n >= 4) & (n < 6))
    def _():
        v_ref[...] = acc.astype(v_ref.dtype)

    @pl.when(n == 6)
    def _():
        xo_ref[...] = acc

    if with_gf:
        @pl.when(n == 7)
        def _():
            go_ref[...] = _gelu_tanh(acc)

        @pl.when(n == 8)
        def _():
            f_ref[...] = acc.astype(f_ref.dtype)


def _inproj_ctx_call(c2d, mod4, w_in, full, cast_layer=None):
    m_rows = c2d.shape[0]
    tm = m_rows
    tile_lo, tile_hi = (0, N_COL_TILES) if full else (2, 7)
    emit_w = cast_layer is not None
    assert full or not emit_w

    def col(lo):
        return lambda m, n: (m, jnp.clip(n + tile_lo - lo, 0, 1))

    one = lambda m, n: (m, 0)
    bf = lambda w: jax.ShapeDtypeStruct((m_rows, w), BF16)
    ff = lambda w: jax.ShapeDtypeStruct((m_rows, w), F32)
    out_specs, out_shape = [], []
    if full:
        out_specs.append(pl.BlockSpec((tm, COL_TILE), col(0)))
        out_shape.append(bf(NA_W))
    out_specs += [pl.BlockSpec((tm, COL_TILE), col(2)), pl.BlockSpec((tm, COL_TILE), col(4)),
                  pl.BlockSpec((tm, COL_TILE), one)]
    out_shape += [bf(NA_W), bf(NA_W), ff(LRU_W)]
    if full:
        out_specs += [pl.BlockSpec((tm, COL_TILE), one), pl.BlockSpec((tm, COL_TILE), one)]
        out_shape += [ff(LRU_W), bf(FNET_W)]
    nbytes = (2 * tm * D_MODEL * 4 + tm * D_MODEL * 2 + 2 * D_MODEL * COL_TILE * 2
              + 2 * 6 * tm * COL_TILE * 4 + 3 * tm * COL_TILE * 4 + 4 * ROW_CHUNK * D_MODEL * 4)
    if emit_w:
        w_spec = pl.BlockSpec((None, D_MODEL, COL_TILE), lambda m, n: (cast_layer, 0, n + tile_lo))
        out_specs.append(pl.BlockSpec((D_MODEL, COL_TILE), lambda m, n: (0, n + tile_lo)))
        out_shape.append(jax.ShapeDtypeStruct((D_MODEL, IN_W), BF16))
        nbytes += 2 * D_MODEL * COL_TILE * (4 + 2)
    else:
        w_spec = pl.BlockSpec((D_MODEL, COL_TILE), lambda m, n: (0, n + tile_lo))
    return pl.pallas_call(
        functools.partial(_inproj_ctx_kernel, tile_lo, full, full, emit_w),
        grid=(1, tile_hi - tile_lo),
        in_specs=[
            pl.BlockSpec((tm, D_MODEL), lambda m, n: (m, 0)),
            pl.BlockSpec((None, None, 6, D_MODEL), lambda m, n: (0, CTX_MOD_ROW, 0, 0)),
            w_spec,
        ],
        out_specs=out_specs,
        out_shape=out_shape,
        scratch_shapes=[pltpu.VMEM((tm, D_MODEL), BF16)],
        compiler_params=_params(("arbitrary", "arbitrary"), nbytes),
        name="inproj_context",
    )(c2d, mod4, w_in)


QBLK_ROWS = 4
KBLK_ROWS = 12
N_QBLK = GRID_H // QBLK_ROWS
QBLK = QBLK_ROWS * GRID_W
KBLK = KBLK_ROWS * GRID_W
KEY_TILE = V7X_LANES
KEY_TILES = KBLK // KEY_TILE
N_DR = 2 * WIN_H - 1
BIAS_BOTH, BIAS_SECOND, BIAS_FIRST = "both", "second", "first"


def _kblk_start(first_query_row):
    return int(np.clip(first_query_row - WIN_H // 2, 0, GRID_H - KBLK_ROWS))


def _bias_entries(first_query_row):
    entries = []
    for u in range(QBLK_ROWS):
        q_row = first_query_row + u
        row_start = int(np.clip(q_row - WIN_H // 2, 0, GRID_H - WIN_H))
        row = []
        for c in range(KEY_TILES):
            k_rows = [_kblk_start(first_query_row) + 2 * c + i for i in range(2)]
            inside = [row_start <= kr < row_start + WIN_H for kr in k_rows]
            dr = [kr - q_row + (WIN_H - 1) for kr in k_rows]
            if inside[0] and inside[1]:
                row.append((BIAS_BOTH, dr[0]))
            elif inside[1]:
                row.append((BIAS_SECOND, dr[1]))
            elif inside[0]:
                row.append((BIAS_FIRST, dr[0]))
            else:
                row.append(None)
        entries.append(row)
    return entries


BIAS_TABLE = sorted({e for b in range(N_QBLK) for row in _bias_entries(b * QBLK_ROWS) for e in row if e})
BIAS_SLOT = {e: i for i, e in enumerate(BIAS_TABLE)}
N_BIAS = len(BIAS_TABLE)


def _bias_plan(first_query_row):
    return [[BIAS_SLOT[e] if e else None for e in row] for row in _bias_entries(first_query_row)]


def _attn_kernel(n_riders, q_ref, qr_ref, k_ref, v_ref, kc_ref, vc_ref, bias_ref, *rest):
    o_ref = rest[n_riders]
    vt_ref = rest[2 * n_riders + 1]
    _run_riders(rest[:n_riders], rest[n_riders + 1:2 * n_riders + 1])
    kc = kc_ref[...]
    zero_tile = jnp.zeros((GRID_W, KEY_TILE), BF16)
    for c in range(SEQ // QBLK):
        vt_ref[:, c * QBLK:(c + 1) * QBLK] = v_ref[c * QBLK:(c + 1) * QBLK, :].T
    vct = vc_ref[...].T

    def one_block(q0, k0, plan):
        s = _dot_nt(qr_ref[pl.ds(q0, QBLK), :], k_ref[pl.ds(k0, KBLK), :])
        sc = _dot_nt(q_ref[pl.ds(q0, QBLK), :], kc)
        p_rows, pc_rows, denoms = [], [], []
        for u in range(QBLK_ROWS):
            rows = slice(u * GRID_W, (u + 1) * GRID_W)
            band = {c: s[rows, c * KEY_TILE:(c + 1) * KEY_TILE] + bias_ref[idx]
                    for c, idx in enumerate(plan[u]) if idx is not None}
            ctx_tiles = [sc[rows, c * KEY_TILE:(c + 1) * KEY_TILE] for c in range(CTX_LEN // KEY_TILE)]
            tiles = list(band.values()) + ctx_tiles
            m = jnp.max(functools.reduce(jnp.maximum, tiles), axis=-1, keepdims=True)
            p_band = {c: jnp.exp(t - m) for c, t in band.items()}
            p_ctx = [jnp.exp(t - m) for t in ctx_tiles]
            total = functools.reduce(jnp.add, list(p_band.values()) + p_ctx)
            denoms.append(jnp.sum(total, axis=-1, keepdims=True))
            p_rows.append(jnp.concatenate(
                [p_band[c].astype(BF16) if c in p_band else zero_tile for c in range(KEY_TILES)], axis=1))
            pc_rows.append(jnp.concatenate([t.astype(BF16) for t in p_ctx], axis=1))
        p = jnp.concatenate(p_rows, axis=0)
        pc = jnp.concatenate(pc_rows, axis=0)
        o_t = _dot_nt(jnp.concatenate([vt_ref[:, k0:k0 + KBLK], vct], axis=1),
                      jnp.concatenate([p, pc], axis=1))
        o_ref[pl.ds(q0, QBLK), :] = (o_t.T / jnp.concatenate(denoms, axis=0)).astype(o_ref.dtype)

    for b in range(N_QBLK):
        first_row = b * QBLK_ROWS
        one_block(b * QBLK, _kblk_start(first_row) * GRID_W, _bias_plan(first_row))


def _attn_call(q, qr, k, v, kc, vc, bias, layer, riders):
    seq_blk = lambda b, h: (b, h)
    n_steps = BATCH * NA_HEADS
    step = lambda b, h: b * NA_HEADS + h
    r_in, r_out, r_shape, r_bytes = _rider_specs(riders, n_steps, step)
    nbytes = (2 * (5 * SEQ * HEAD_DIM * 2 + 2 * CTX_LEN * HEAD_DIM * 2 + N_BIAS * GRID_W * KEY_TILE * 4)
              + 8 * QBLK * (KBLK + CTX_LEN) * 4 + r_bytes)
    res = pl.pallas_call(
        functools.partial(_attn_kernel, len(riders)),
        grid=(BATCH, NA_HEADS),
        in_specs=[
            pl.BlockSpec((SEQ, HEAD_DIM), seq_blk),
            pl.BlockSpec((SEQ, HEAD_DIM), seq_blk),
            pl.BlockSpec((SEQ, HEAD_DIM), seq_blk),
            pl.BlockSpec((SEQ, HEAD_DIM), seq_blk),
            pl.BlockSpec((CTX_LEN, HEAD_DIM), seq_blk),
            pl.BlockSpec((CTX_LEN, HEAD_DIM), seq_blk),
            pl.BlockSpec((None, None, N_BIAS, GRID_W, KEY_TILE), lambda b, h: (layer, h, 0, 0, 0)),
        ] + r_in,
        out_specs=[pl.BlockSpec((SEQ, HEAD_DIM), seq_blk)] + r_out,
        out_shape=[jax.ShapeDtypeStruct((BATCH * SEQ, NA_W), BF16)] + r_shape,
        scratch_shapes=[pltpu.VMEM((HEAD_DIM, SEQ), BF16)],
        compiler_params=_params(("arbitrary", "arbitrary"), nbytes),
        name="neighbourhood_attention",
    )(q, qr, k, v, kc, vc, bias, *[r[0] for r in riders])
    return res[0], res[1:]


def _ctx_attn_kernel(q_ref, k_ref, v_ref, o_ref):
    s = _dot_nt(q_ref[...], k_ref[...])
    m = jnp.max(s, axis=-1, keepdims=True)
    p = jnp.exp(s - m)
    denom = jnp.sum(p, axis=-1, keepdims=True)
    o_ref[...] = (_dot(p.astype(BF16), v_ref[...]) / denom).astype(o_ref.dtype)


def _ctx_attn_call(q, k, v):
    blk = pl.BlockSpec((CTX_LEN, HEAD_DIM), lambda b, h: (b, h))
    return pl.pallas_call(
        _ctx_attn_kernel,
        grid=(BATCH, NA_HEADS),
        in_specs=[blk, blk, blk],
        out_specs=blk,
        out_shape=jax.ShapeDtypeStruct((BATCH * CTX_LEN, NA_W), BF16),
        compiler_params=_params(("arbitrary", "arbitrary"), 16 << 20),
        name="context_attention",
    )(q, k, v)


def _attn_bias_table(rpb):
    col = np.arange(GRID_W)
    col_start = np.clip(col - WIN_W // 2, 0, GRID_W - WIN_W)
    in_win = (col[None, :] >= col_start[:, None]) & (col[None, :] < col_start[:, None] + WIN_W)
    dc = np.clip(col[None, :] - col[:, None] + (WIN_W - 1), 0, 2 * WIN_W - 2)
    onehot = (in_win[None] & (dc[None] == np.arange(2 * WIN_W - 1)[:, None, None])).astype(np.float32)
    t = jnp.einsum('lhdj,jqk->lhdqk', rpb, jnp.asarray(onehot), precision=lax.Precision.HIGHEST)
    t = jnp.where(in_win[None, None, None], t, NEG_INF)
    masked = jnp.full((DEPTH, NA_HEADS, GRID_W, GRID_W), NEG_INF, F32)
    tiles = []
    for kind, dr in BIAS_TABLE:
        left = masked if kind == BIAS_SECOND else t[:, :, dr]
        right = masked if kind == BIAS_FIRST else t[:, :, dr + 1 if kind == BIAS_BOTH else dr]
        tiles.append(jnp.concatenate([left, right], axis=-1))
    return jnp.stack(tiles, axis=2)


def _rope_tables():
    assert GRID_H == GRID_W
    quarter = HEAD_DIM // 4
    inv = np.float32(ROPE_THETA) ** (-np.arange(quarter, dtype=np.float32) / np.float32(quarter))
    ang = (np.arange(GRID_W, dtype=np.float32)[:, None] * inv).astype(np.float64)
    cos_g, sin_g = jnp.asarray(np.cos(ang), F32), jnp.asarray(np.sin(ang), F32)
    by_row = lambda g: jnp.broadcast_to(g[:, None, :], (GRID_H, GRID_W, quarter)).reshape(SEQ, quarter)
    by_col = lambda g: jnp.broadcast_to(g[None, :, :], (GRID_H, GRID_W, quarter)).reshape(SEQ, quarter)
    cos = jnp.concatenate([by_row(cos_g), by_row(cos_g), by_col(cos_g), by_col(cos_g)], -1)
    sin = jnp.concatenate([-by_row(sin_g), by_row(sin_g), -by_col(sin_g), by_col(sin_g)], -1)
    return cos, sin


HALO = V7X_SUBLANES
N_SEG = V7X_SUBLANES
SEG_PAD = V7X_SUBLANES
LRU_SCAN_UNROLL = 8


def _lru_coeffs(xp_ref, n_rows, cw_ref, cb_ref, w4, b4_ref, sp, a_refs, u_refs):
    seg = n_rows // N_SEG
    pitch = seg + SEG_PAD
    for s in range(N_SEG):
        base = HALO + s * seg
        xc = cb_ref[...] + xp_ref[base - CONV_W // 2:base - CONV_W // 2 + seg, :] * cw_ref[0:1, :]
        for j in range(1, CONV_W):
            off = base - CONV_W // 2 + j
            xc = xc + xp_ref[off:off + seg, :] * cw_ref[j:j + 1, :]
        th = jnp.tanh(_dot(xc.astype(BF16), w4) + b4_ref[...])
        half_xc = 0.5 * xc
        for d in range(2):
            r2 = th[:, (2 * d) * LRU_BW:(2 * d + 1) * LRU_BW] + 1.0
            i2 = th[:, (2 * d + 1) * LRU_BW:(2 * d + 2) * LRU_BW] + 1.0
            log_a = r2 * sp[d:d + 1, :]
            a = jnp.exp(log_a)
            a_refs[d][s * pitch:s * pitch + seg, :] = a
            one_minus_a2 = -jnp.tanh(log_a) * (a * a + 1.0)
            root = jnp.where(one_minus_a2 == 0.0, 0.0, one_minus_a2 * lax.rsqrt(one_minus_a2))
            u_refs[d][s * pitch:s * pitch + seg, :] = root * (i2 * half_xc)


def _lru_local_scan(n_rows, coef_f, coef_b, state_f, state_b):
    seg = n_rows // N_SEG
    pitch = seg + SEG_PAD
    zero = jnp.zeros((N_SEG, LRU_BW), F32)
    one = jnp.ones((N_SEG, LRU_BW), F32)

    def step(coef, state, row, h, p):
        rows = pl.ds(row, N_SEG, stride=pitch)
        a = coef[0][rows, :]
        h = a * h + coef[1][rows, :]
        p = p * a
        state[0][rows, :] = p
        state[1][rows, :] = h
        return h, p

    def body(i, carry):
        hf, pf, hb, pb = carry
        for j in range(LRU_SCAN_UNROLL):
            t = i * LRU_SCAN_UNROLL + j
            hf, pf = step(coef_f, state_f, t, hf, pf)
            hb, pb = step(coef_b, state_b, seg - 1 - t, hb, pb)
        return hf, pf, hb, pb

    lax.fori_loop(0, seg // LRU_SCAN_UNROLL, body, (zero, one, zero, one))


def _lru_carries(n_rows, h_in_f, h_in_b, af, uf, ab, ub):
    seg = n_rows // N_SEG
    pitch = seg + SEG_PAD
    cf, cb = [h_in_f], [h_in_b]
    for s in range(N_SEG):
        last = s * pitch + seg - 1
        cf.append(uf[last:last + 1, :] + af[last:last + 1, :] * cf[-1])
        first = (N_SEG - 1 - s) * pitch
        cb.append(ub[first:first + 1, :] + ab[first:first + 1, :] * cb[-1])
    return cf[:N_SEG], cb[:N_SEG][::-1], cf[N_SEG], cb[N_SEG]


def _lru_emit(n_rows, cf, cb, af, uf, ab, ub, g_ref, o_ref):
    seg = n_rows // N_SEG
    pitch = seg + SEG_PAD
    for s in range(N_SEG):
        src = slice(s * pitch, s * pitch + seg)
        dst = slice(s * seg, (s + 1) * seg)
        y = (uf[src, :] + af[src, :] * cf[s]) + (ub[src, :] + ab[src, :] * cb[s])
        o_ref[dst, :] = (y * g_ref[dst, :]).astype(o_ref.dtype)


def _lru_kernel(ctx_out, n_riders, x_ref, g_ref, xc_ref, *refs):
    refs = list(refs)
    gc_ref = refs.pop(0) if ctx_out else None
    cw_ref, cb_ref, w4_ref, b4_ref, lam_ref = refs[:5]
    rider_in, refs = refs[5:5 + n_riders], refs[5 + n_riders:]
    o_ref = refs.pop(0)
    oc_ref = refs.pop(0) if ctx_out else None
    rider_out, refs = refs[:n_riders], refs[n_riders:]
    xp_ref, af, uf, ab, ub, pf, sf, pb, sb = refs
    _run_riders(rider_in, rider_out)

    lam = lam_ref[...]
    z = -lam
    sp = (-0.5 * LRU_C) * (jnp.maximum(z, 0.0) + jnp.log1p(jnp.exp(-jnp.abs(z))))
    w4 = w4_ref[...].astype(BF16)
    zeros_halo = jnp.zeros((HALO, LRU_BW), F32)
    h0 = jnp.zeros((1, LRU_BW), F32)

    xp_ref[0:HALO, :] = zeros_halo
    xp_ref[HALO:HALO + CTX_LEN, :] = xc_ref[...]
    xp_ref[HALO + CTX_LEN:2 * HALO + CTX_LEN, :] = zeros_halo
    _lru_coeffs(xp_ref, CTX_LEN, cw_ref, cb_ref, w4, b4_ref, sp, (af, ab), (uf, ub))
    _lru_local_scan(CTX_LEN, (af, uf), (ab, ub), (pf, sf), (pb, sb))
    cf, cb, hf, hb = _lru_carries(CTX_LEN, h0, h0, pf, sf, pb, sb)
    if ctx_out:
        _lru_emit(CTX_LEN, cf, cb, pf, sf, pb, sb, gc_ref, oc_ref)

    xp_ref[HALO:HALO + SEQ, :] = x_ref[...]
    xp_ref[HALO + SEQ:2 * HALO + SEQ, :] = zeros_halo
    _lru_coeffs(xp_ref, SEQ, cw_ref, cb_ref, w4, b4_ref, sp, (af, ab), (uf, ub))
    _lru_local_scan(SEQ, (af, uf), (ab, ub), (pf, sf), (pb, sb))
    cf, cb, _, _ = _lru_carries(SEQ, hf, hb, pf, sf, pb, sb)
    _lru_emit(SEQ, cf, cb, pf, sf, pb, sb, g_ref, o_ref)


def _lru_call(ctx_out, xl, gl, xc, gc, conv_w3, conv_b3, w4, b4, lam3, layer, riders):
    lat = pl.BlockSpec((SEQ, LRU_BW), lambda b, j: (b, j))
    cx = pl.BlockSpec((CTX_LEN, LRU_BW), lambda b, j: (b, j))
    r_in, r_out, r_shape, r_bytes = _rider_specs(riders, BATCH * LRU_BLOCKS, lambda b, j: b * LRU_BLOCKS + j)
    in_specs = [lat, lat, cx] + ([cx] if ctx_out else []) + [
        pl.BlockSpec((None, CONV_W, LRU_BW), lambda b, j: (layer, 0, j)),
        pl.BlockSpec((None, 1, LRU_BW), lambda b, j: (layer, 0, j)),
        pl.BlockSpec((None, None, LRU_BW, 4 * LRU_BW), lambda b, j: (layer, j, 0, 0)),
        pl.BlockSpec((None, None, 1, 4 * LRU_BW), lambda b, j: (layer, j, 0, 0)),
        pl.BlockSpec((None, 2, LRU_BW), lambda b, j: (layer, 0, j)),
    ] + r_in
    out_specs = [lat] + ([cx] if ctx_out else []) + r_out
    out_shape = [jax.ShapeDtypeStruct((BATCH * SEQ, LRU_W), BF16)]
    if ctx_out:
        out_shape.append(jax.ShapeDtypeStruct((BATCH * CTX_LEN, LRU_W), BF16))
    out_shape += r_shape
    seq_bytes = SEQ * LRU_BW * 4
    args = ([xl, gl, xc] + ([gc] if ctx_out else []) + [conv_w3, conv_b3, w4, b4, lam3]
            + [r[0] for r in riders])
    res = pl.pallas_call(
        functools.partial(_lru_kernel, ctx_out, len(riders)),
        grid=(BATCH, LRU_BLOCKS),
        in_specs=in_specs,
        out_specs=out_specs,
        out_shape=out_shape,
        scratch_shapes=([pltpu.VMEM((SEQ + 2 * HALO, LRU_BW), F32)]
                        + [pltpu.VMEM((SEQ + N_SEG * SEG_PAD, LRU_BW), F32)] * 8),
        compiler_params=_params(("arbitrary", "arbitrary"), 15 * seq_bytes + (8 << 20) + r_bytes),
        name="rglru",
    )(*args)
    n_main = 2 if ctx_out else 1
    return res[0], (res[1] if ctx_out else None), res[n_main:]


def _fourier_kernel(n_pos, blk, f_ref, ch_ref, sh_ref, cc_ref, sc_ref, rev_ref, alt_ref, w_ref, b_ref, o_ref,
                    ec_ref, es_ref, mir_ref, mid_ref):
    half = n_pos // 2
    n_lo = half // blk
    step = pl.program_id(1)
    norm = 1.0 / math.sqrt(n_pos * FNET_GW)
    w = w_ref[...].astype(BF16)

    def linear(y):
        return (_dot(y.astype(BF16), w) + b_ref[...]).astype(o_ref.dtype)

    @pl.when(step == 0)
    def _():
        for i in range(n_lo):
            lo = blk * (2 * n_lo - 1 - i)
            if i == 0:
                mirrored = _dot(rev_ref[:, 0:blk], f_ref[lo:lo + blk, :])
            else:
                mirrored = _dot(rev_ref[...], f_ref[lo:lo + 2 * blk, :])
            rows = slice(i * blk, (i + 1) * blk)
            x = f_ref[rows, :].astype(F32)
            even = (x + mirrored).astype(BF16)
            odd = (x - mirrored).astype(BF16)
            for g in range(FNET_GROUPS):
                sl = slice(g * FNET_GW, (g + 1) * FNET_GW)
                ec_ref[rows, sl] = _dot(even[:, sl], cc_ref[...]).astype(BF16)
                es_ref[rows, sl] = _dot(odd[:, sl], sc_ref[...]).astype(BF16)
        for g in range(FNET_GROUPS):
            sl = slice(g * FNET_GW, (g + 1) * FNET_GW)
            mid_ref[0:BF16_ROWS, sl] = _dot(f_ref[half:half + BF16_ROWS, sl], cc_ref[...])
        mid_ref[BF16_ROWS:2 * BF16_ROWS, :] = _dot(alt_ref[...], ec_ref[...])

    mid = mid_ref[0:1, :]

    @pl.when(step < n_lo)
    def _():
        a = _dot(ch_ref[...], ec_ref[...])
        b = _dot(sh_ref[...], es_ref[...])
        row = lax.broadcasted_iota(jnp.int32, (blk, 1), 0)
        base = jnp.where(row % 2 == 0, 1.0, -1.0) * mid
        o_ref[...] = linear((a - b + base) * norm)
        r0 = pl.multiple_of(step * blk, blk)
        mir_ref[pl.ds(r0, blk), :] = ((a + b + base) * norm).astype(BF16)

    @pl.when(step == n_lo)
    def _():
        y = _dot(rev_ref[:, 0:blk], mir_ref[(n_lo - 1) * blk:n_lo * blk, :])
        nyquist = (mid_ref[BF16_ROWS:BF16_ROWS + 1, :] + mid) * norm
        row = lax.broadcasted_iota(jnp.int32, (blk, 1), 0)
        o_ref[...] = linear(jnp.where(row == 0, nyquist, y))

    if n_lo > 1:
        @pl.when(step > n_lo)
        def _():
            r0 = pl.multiple_of((2 * n_lo - 1 - step) * blk, blk)
            o_ref[...] = linear(_dot(rev_ref[...], mir_ref[pl.ds(r0, 2 * blk), :]))


DFT_SPLIT = 64


def _dft_matrices(n, size):
    t = np.arange(size, dtype=np.int64)

    def table(k):
        ang = (2.0 * np.pi / n) * ((k[:, None] * t[None, :]) % n).astype(np.float64)
        return jnp.asarray(np.cos(ang), F32), jnp.asarray(np.sin(ang), F32)

    if size <= DFT_SPLIT:
        c, s = table(t)
        return c.astype(BF16), s.astype(BF16)
    c1, s1 = table(DFT_SPLIT * np.arange(size // DFT_SPLIT, dtype=np.int64))
    c2, s2 = table(np.arange(DFT_SPLIT, dtype=np.int64))
    c = c1[:, None, :] * c2[None, :, :] - s1[:, None, :] * s2[None, :, :]
    s = s1[:, None, :] * c2[None, :, :] + c1[:, None, :] * s2[None, :, :]
    return c.reshape(size, size).astype(BF16), s.reshape(size, size).astype(BF16)


FOURIER_BLK = 256


def _fourier_call(f2d, n_pos, fno_w, fno_b3, layer):
    half = n_pos // 2
    blk = min(FOURIER_BLK, half)
    steps = n_pos // blk
    n_lo = half // blk
    ch, sh = _dft_matrices(n_pos, half)
    cc, sc = _dft_matrices(FNET_GW, FNET_GW)
    rev = np.zeros((blk, 2 * blk), np.float32)
    rev[np.arange(1, blk), blk - np.arange(1, blk)] = 1.0
    rev[0, blk] = 1.0
    alt = np.zeros((BF16_ROWS, half), np.float32)
    alt[0] = 1.0 - 2.0 * (np.arange(half) % 2)
    const = lambda b, k: (0, 0)
    dft_tile = lambda b, k: (jnp.minimum(k, n_lo - 1), 0)
    nbytes = (2 * n_pos * FNET_W * 2 + 2 * 2 * blk * half * 2 + 3 * half * FNET_W * 2 + 2 * FNET_W * FNET_W * 4
              + 8 * blk * FNET_W * 4 + 2 * blk * 2 * blk * 2)
    return pl.pallas_call(
        functools.partial(_fourier_kernel, n_pos, blk),
        grid=(BATCH, steps),
        in_specs=[
            pl.BlockSpec((n_pos, FNET_W), lambda b, k: (b, 0)),
            pl.BlockSpec((blk, half), dft_tile),
            pl.BlockSpec((blk, half), dft_tile),
            pl.BlockSpec((FNET_GW, FNET_GW), const),
            pl.BlockSpec((FNET_GW, FNET_GW), const),
            pl.BlockSpec((blk, 2 * blk), const),
            pl.BlockSpec((BF16_ROWS, half), const),
            pl.BlockSpec((None, FNET_W, FNET_W), lambda b, k: (layer, 0, 0)),
            pl.BlockSpec((None, 1, FNET_W), lambda b, k: (layer, 0, 0)),
        ],
        out_specs=pl.BlockSpec((blk, FNET_W), lambda b, k: (b * steps + k, 0)),
        out_shape=jax.ShapeDtypeStruct((BATCH * n_pos, FNET_W), BF16),
        scratch_shapes=[pltpu.VMEM((half, FNET_W), BF16), pltpu.VMEM((half, FNET_W), BF16),
                        pltpu.VMEM((half, FNET_W), BF16), pltpu.VMEM((2 * BF16_ROWS, FNET_W), F32)],
        compiler_params=_params(("arbitrary", "arbitrary"), nbytes),
        name="fourier_mix",
    )(f2d, ch, sh, cc, sc, jnp.asarray(rev, BF16), jnp.asarray(alt, BF16), fno_w, fno_b3)


def _outproj_kernel(carry_mod, na_ref, lru_ref, f_ref, res_ref, mod_ref, w_ref, g_ref, b_ref, *rest):
    if carry_mod:
        s_ref, wm_ref, bm_ref, o_ref, mo_ref = rest
        _mod_kernel(s_ref, wm_ref, bm_ref, mo_ref)
    else:
        (o_ref,) = rest
    gate = mod_ref[2:3, :]
    for rows in _row_tiles(res_ref.shape[0], MM_SUB):
        y = (_dot(na_ref[rows, :], w_ref[0:NA_W, :])
             + _dot(lru_ref[rows, :], w_ref[NA_W:NA_W + LRU_W, :])
             + _dot(f_ref[rows, :], w_ref[NA_W + LRU_W:D_MODEL, :]))
        _residual_ln_store(res_ref, y, gate, g_ref[...], b_ref[...], o_ref, rows)


def _outproj_call(na, lru, f, res, mod4, mod_row, w_out, layer, ln_g, ln_b, next_mod=None):
    m_rows = res.shape[0]
    tm = 512
    steps = m_rows // tm
    row = lambda m: (m, 0)
    nbytes = (2 * tm * D_MODEL * 2 + 2 * 2 * tm * D_MODEL * 4 + D_MODEL * D_MODEL * 2 + 2 * MM_SUB * D_MODEL * 4
              + 4 * ROW_CHUNK * D_MODEL * 4)
    in_specs = [
        pl.BlockSpec((tm, NA_W), row),
        pl.BlockSpec((tm, LRU_W), row),
        pl.BlockSpec((tm, FNET_W), row),
        pl.BlockSpec((tm, D_MODEL), row),
        pl.BlockSpec((None, None, 6, D_MODEL), lambda m: (0, mod_row(m * tm), 0, 0)),
        pl.BlockSpec((D_MODEL, D_MODEL), lambda m: (0, 0), pipeline_mode=pl.Buffered(1)),
        pl.BlockSpec((None, 1, D_MODEL), lambda m: (layer, 0, 0)),
        pl.BlockSpec((None, 1, D_MODEL), lambda m: (layer, 0, 0)),
    ]
    out_specs = [pl.BlockSpec((tm, D_MODEL), row)]
    out_shape = [jax.ShapeDtypeStruct((m_rows, D_MODEL), F32)]
    args = [na, lru, f, res, mod4, w_out, ln_g, ln_b]
    if next_mod is not None:
        s_in, w_mod, b_mod3, mod_layer = next_mod
        n_out = w_mod.shape[-1]
        slab = n_out // steps
        in_specs += [
            pl.BlockSpec((MOD_ROWS, D_MODEL), lambda m: (0, 0)),
            pl.BlockSpec((None, D_MODEL, slab), lambda m: (mod_layer, 0, m)),
            pl.BlockSpec((None, 1, slab), lambda m: (mod_layer, 0, m)),
        ]
        out_specs.append(pl.BlockSpec((MOD_ROWS, slab), lambda m: (0, m)))
        out_shape.append(jax.ShapeDtypeStruct((MOD_ROWS, n_out), F32))
        args += [s_in, w_mod, b_mod3]
        nbytes += 2 * D_MODEL * slab * 4 + D_MODEL * slab * 2
    res = pl.pallas_call(
        functools.partial(_outproj_kernel, next_mod is not None),
        grid=(steps,),
        in_specs=in_specs,
        out_specs=out_specs,
        out_shape=out_shape,
        compiler_params=_params(("arbitrary",), nbytes),
        name="outproj_residual",
    )(*args)
    return (res[0], res[1]) if next_mod is not None else (res[0], None)


MLP_TF = 512
MLP_TILES = D_FF // MLP_TF


def _mlp_kernel(x_ref, mod_ref, w1_ref, b1_ref, w2_ref, b2_ref, g_ref, b_ref, o_ref, v_ref, h0_ref, h1_ref):
    j = pl.program_id(1)
    tm = x_ref.shape[0]

    def up(h_out, rows=slice(None)):
        h = _dot(v_ref[rows, :], w1_ref[...]) + b1_ref[...]
        h_out[rows, :] = jnp.square(jnp.maximum(h, 0.0)).astype(BF16)

    def down(h_in, first):
        for c in range(D_MODEL // COL_TILE):
            sl = slice(c * COL_TILE, (c + 1) * COL_TILE)
            part = _dot(h_in[...], w2_ref[:, sl])
            if first:
                o_ref[:, sl] = part
            else:
                o_ref[:, sl] += part

    @pl.when(j == 0)
    def _():
        shift = mod_ref[3:4, :]
        scale1 = 1.0 + mod_ref[4:5, :]
        for rows in _row_tiles(tm, MM_SUB):
            _ln_mod_rows(x_ref, shift, scale1, v_ref, rows)
            up(h0_ref, rows)

    @pl.when(j == 1)
    def _():
        down(h0_ref, True)
        up(h1_ref)

    @pl.when((j > 1) & (j < MLP_TILES) & (j % 2 == 0))
    def _():
        down(h1_ref, False)
        up(h0_ref)

    @pl.when((j > 1) & (j < MLP_TILES) & (j % 2 == 1))
    def _():
        down(h0_ref, False)
        up(h1_ref)

    @pl.when(j == MLP_TILES)
    def _():
        h_last = h1_ref if (MLP_TILES - 1) % 2 else h0_ref
        gate = mod_ref[5:6, :]
        for rows in _row_tiles(tm, MM_SUB):
            y = o_ref[rows, :] + _dot(h_last[rows, :], w2_ref[...]) + b2_ref[...]
            _residual_ln_store(x_ref, y, gate, g_ref[...], b_ref[...], o_ref, rows)


def _mlp_call(x1, mod4, mod_row, tm, w1, b1, w2, b2, layer, ln_g, ln_b):
    m_rows = x1.shape[0]
    row = lambda m, j: (m, 0)
    vec = lambda m, j: (layer, 0, 0)
    nbytes = (4 * tm * D_MODEL * 4 + tm * D_MODEL * 2 + 2 * 2 * D_MODEL * MLP_TF * 2 + 2 * tm * MLP_TF * 2
              + tm * MLP_TF * 4 + tm * COL_TILE * 4 + 2 * MM_SUB * D_MODEL * 4 + 4 * ROW_CHUNK * D_MODEL * 4)
    grid = (m_rows // tm, MLP_TILES + 1)
    in_specs = [
        pl.BlockSpec((tm, D_MODEL), row),
        pl.BlockSpec((None, None, 6, D_MODEL), lambda m, j: (0, mod_row(m * tm), 0, 0)),
        pl.BlockSpec((D_MODEL, MLP_TF), lambda m, j: (0, jnp.minimum(j, MLP_TILES - 1))),
        pl.BlockSpec((None, 1, MLP_TF), lambda m, j: (layer, 0, jnp.minimum(j, MLP_TILES - 1))),
        pl.BlockSpec((MLP_TF, D_MODEL), lambda m, j: (jnp.maximum(j - 1, 0), 0)),
        pl.BlockSpec((None, 1, D_MODEL), vec),
        pl.BlockSpec((None, 1, D_MODEL), vec),
        pl.BlockSpec((None, 1, D_MODEL), vec),
    ]
    out_specs = pl.BlockSpec((tm, D_MODEL), row)

    def nested(*refs):
        operands, scratch = refs[:len(in_specs) + 1], refs[len(in_specs) + 1:]
        body = lambda *blocks: _mlp_kernel(*blocks, *scratch)
        pltpu.emit_pipeline(body, grid=grid, in_specs=in_specs, out_specs=out_specs)(*operands)

    whole = pl.BlockSpec(memory_space=pl.ANY)
    return pl.pallas_call(
        nested,
        in_specs=[whole] * len(in_specs),
        out_specs=whole,
        out_shape=jax.ShapeDtypeStruct((m_rows, D_MODEL), F32),
        scratch_shapes=[pltpu.VMEM((tm, D_MODEL), BF16), pltpu.VMEM((tm, MLP_TF), BF16),
                        pltpu.VMEM((tm, MLP_TF), BF16)],
        compiler_params=pltpu.CompilerParams(vmem_limit_bytes=_vmem_limit(nbytes)),
        name="mlp_residual",
    )(x1, mod4, w1, b1, w2, b2, ln_g, ln_b)


def kernel(x, c, ctx, c_ctx, w_mod, b_mod, w_in, rpb, conv_w, conv_b, lru_wa, lru_ba, lru_wx, lru_bx, lru_lambda,
           fno_w, fno_b, w_out, ln1_g, ln1_b, w_fc1, b_fc1, w_fc2, b_fc2, ln2_g, ln2_b):
    xl = x.reshape(BATCH * SEQ, D_MODEL)
    xc = ctx.reshape(BATCH * CTX_LEN, D_MODEL)
    s_in = jnp.concatenate([c, c_ctx[None], jnp.zeros((MOD_ROWS - BATCH - 1, D_MODEL), F32)], 0)
    b_mod3 = b_mod.reshape(DEPTH, 1, 6 * D_MODEL)
    as_mod4 = lambda m: m.reshape(1, MOD_ROWS, 6, D_MODEL)
    mod4 = as_mod4(_mod_call(s_in, w_mod, b_mod3, 0))
    cos_t, sin_t = _rope_tables()
    bias_tab = _attn_bias_table(rpb)
    w_in_l = None
    vec3 = lambda a: a.reshape(DEPTH, 1, a.shape[-1])
    ln1_g3, ln1_b3, ln2_g3, ln2_b3 = vec3(ln1_g), vec3(ln1_b), vec3(ln2_g), vec3(ln2_b)
    b_fc1_3, b_fc2_3 = vec3(b_fc1), vec3(b_fc2)
    conv_b3, fno_b3 = vec3(conv_b), vec3(fno_b)
    w4 = 0.5 * jnp.concatenate([lru_wa[:, 0], lru_wx[:, 0], lru_wa[:, 1], lru_wx[:, 1]], -1)
    blk = lambda a: a.reshape(DEPTH, LRU_BLOCKS, 1, LRU_BW)
    b4 = 0.5 * jnp.concatenate([blk(lru_ba[:, 0]), blk(lru_bx[:, 0]), blk(lru_ba[:, 1]), blk(lru_bx[:, 1])], -1)
    lat_row = lambda r0: r0 // SEQ
    ctx_row = lambda r0: CTX_MOD_ROW

    for layer in range(DEPTH):
        ctx_out = layer < DEPTH - 1
        if ctx_out and w_in_l is None:
            qc, kc, vc, xoc, goc, fc, w_in_l = _inproj_ctx_call(xc, mod4, w_in, True, cast_layer=layer)
        else:
            if w_in_l is None:
                w_in_l = w_in[layer].astype(BF16)
            if ctx_out:
                qc, kc, vc, xoc, goc, fc = _inproj_ctx_call(xc, mod4, w_in_l, True)
            else:
                kc, vc, xoc = _inproj_ctx_call(xc, mod4, w_in_l, False)
                goc = None
        q, qr, k, v, xo, go, f = _inproj_lat_call(xl, mod4, w_in_l, cos_t, sin_t)

        na, (w_out_l, w_fc1_l, w_fc2_l) = _attn_call(
            q, qr, k, v, kc, vc, bias_tab, layer,
            [_rider(w_out, layer, 0), _rider(w_fc1, layer, 1), _rider(w_fc2, layer, 0)])

        next_w_in = [_rider(w_in, layer + 1, 0)] if layer + 1 < DEPTH else []
        lru, lru_c, cast = _lru_call(ctx_out, xo, go, xoc, goc, conv_w, conv_b3, w4, b4, lru_lambda, layer,
                                     next_w_in)
        if next_w_in:
            w_in_l = cast[0]

        fm = _fourier_call(f, SEQ, fno_w, fno_b3, layer)
        next_mod = (s_in, w_mod, b_mod3, layer + 1) if layer + 1 < DEPTH else None
        x1, mod_next = _outproj_call(na, lru, fm, xl, mod4, lat_row, w_out_l, layer, ln1_g3, ln1_b3, next_mod)
        xl = _mlp_call(x1, mod4, lat_row, 1024, w_fc1_l, b_fc1_3, w_fc2_l, b_fc2_3, layer, ln2_g3, ln2_b3)

        if ctx_out:
            na_c = _ctx_attn_call(qc, kc, vc)
            fm_c = _fourier_call(fc, CTX_LEN, fno_w, fno_b3, layer)
            c1, _ = _outproj_call(na_c, lru_c, fm_c, xc, mod4, ctx_row, w_out_l, layer, ln1_g3, ln1_b3)
            xc = _mlp_call(c1, mod4, ctx_row, 512, w_fc1_l, b_fc1_3, w_fc2_l, b_fc2_3, layer, ln2_g3, ln2_b3)
        if mod_next is not None:
            mod4 = as_mod4(mod_next)

    return xl.reshape(BATCH, SEQ, D_MODEL)
```
